```python
import math
import jax, jax.numpy as jnp
from jax import lax
import numpy as np

D_MODEL = 2048
BATCH = 32
SEQ = 256
DEPTH = 2
DEC_BATCH = 2
DEC_SEQ = 1024
PAST_LEN = 512

GRID_W = 64
N_MIXERS = 2
N_HEADS = 16
HEAD_DIM = D_MODEL // N_HEADS
NA_ROWS_MAX = 8
NA_COLS = 16
RPB_R = 2 * NA_ROWS_MAX - 1
RPB_C = 2 * NA_COLS - 1
S5_CH = 16
S5_GROUPS = D_MODEL // S5_CH
S5_P = 64
Q_BLOCK = 128
EPS = 1e-6
DT_MIN = 0.001
DT_MAX = 0.1

kernel_name = "hybrid_na_s5_prefix_denoise_step"


def _rmsnorm(x, g):
    xf = x.astype(jnp.float32)
    y = xf * lax.rsqrt(jnp.mean(xf * xf, axis=-1, keepdims=True) + EPS)
    return (y * g.astype(jnp.float32)).astype(x.dtype)


def _pre(x, cond, g, w_ada, b_ada):
    shift, scale, gate = jnp.split(jax.nn.silu(cond) @ w_ada + b_ada, 3, axis=-1)
    h = _rmsnorm(x, g) * (1 + scale) + shift
    return h, gate


def _na_proj(h, w_in):
    b, l, _ = h.shape
    q, k, v, z = jnp.split(h @ w_in, 4, axis=-1)
    shp = (b, l, N_HEADS, HEAD_DIM)
    return q.reshape(shp), k.reshape(shp), v.reshape(shp), z


def _ctx_attention(q, k, v):
    b, l, h, d = q.shape
    nblk = l // Q_BLOCK
    qb = jnp.moveaxis(q.reshape(b, nblk, Q_BLOCK, h, d), 1, 0)
    scale = HEAD_DIM ** -0.5

    def blk(qi):
        s = jnp.einsum("bqhd,bkhd->bhqk", qi, k).astype(jnp.float32) * scale
        p = jax.nn.softmax(s, axis=-1).astype(v.dtype)
        return jnp.einsum("bhqk,bkhd->bqhd", p, v)

    o = lax.map(blk, qb)
    return jnp.moveaxis(o, 0, 1).reshape(b, l, h, d)


def _latent_na(q, k, v, k_ctx, v_ctx, rpb):
    b, n, h, d = q.shape
    rows = n // GRID_W
    kr = min(NA_ROWS_MAX, rows)
    qg = q.reshape(b, rows, GRID_W, h, d)
    kg = k.reshape(b, rows, GRID_W, h, d)
    vg = v.reshape(b, rows, GRID_W, h, d)
    col = jnp.arange(GRID_W)
    c0 = jnp.clip(col - NA_COLS // 2, 0, GRID_W - NA_COLS)
    col_ok = (col[None, :] >= c0[:, None]) & (col[None, :] < c0[:, None] + NA_COLS)
    dc_idx = jnp.clip(col[None, :] - col[:, None] + NA_COLS - 1, 0, RPB_C - 1)
    scale = HEAD_DIM ** -0.5
    neg = jnp.float32(-1e30)

    def row_block(r):
        r0 = jnp.clip(r - kr // 2, 0, rows - kr)
        qr = lax.dynamic_index_in_dim(qg, r, axis=1, keepdims=False)
        kw = lax.dynamic_slice_in_dim(kg, r0, kr, axis=1)
        vw = lax.dynamic_slice_in_dim(vg, r0, kr, axis=1)
        dr_idx = r0 + jnp.arange(kr) - r + NA_ROWS_MAX - 1
        bias = rpb[:, dr_idx[:, None, None], dc_idx[None, :, :]]
        bias = jnp.transpose(bias, (0, 2, 1, 3)).astype(jnp.float32)
        s_loc = jnp.einsum("bqhd,bikhd->bhqik", qr, kw).astype(jnp.float32) * scale + bias[None]
        s_loc = jnp.where(col_ok[None, None, :, None, :], s_loc, neg)
        s_ctx = jnp.einsum("bqhd,bjhd->bhqj", qr, k_ctx).astype(jnp.float32) * scale
        s = jnp.concatenate([s_loc.reshape(b, h, GRID_W, kr * GRID_W), s_ctx], axis=-1)
        p = jax.nn.softmax(s, axis=-1).astype(v.dtype)
        p_loc = p[..., : kr * GRID_W].reshape(b, h, GRID_W, kr, GRID_W)
        p_ctx = p[..., kr * GRID_W:]
        return (jnp.einsum("bhqik,bikhd->bqhd", p_loc, vw)
                + jnp.einsum("bhqj,bjhd->bqhd", p_ctx, v_ctx))

    o = lax.map(row_block, jnp.arange(rows))
    return jnp.moveaxis(o, 0, 1).reshape(b, n, h, d)


def _na_out(x, o, z, gate, w_out):
    b, l = x.shape[:2]
    out = (o.reshape(b, l, D_MODEL) * jax.nn.silu(z)) @ w_out
    return x + gate * out


def _na_layer_ctx(x, c_ctx, p):
    g, w_ada, b_ada, w_in, rpb, w_out = p
    h, gate = _pre(x, c_ctx, g, w_ada, b_ada)
    q, k, v, z = _na_proj(h, w_in)
    o = _ctx_attention(q, k, v)
    return _na_out(x, o, z, gate, w_out), k, v


def _na_layer_lat(x, cond, k_ctx, v_ctx, p):
    g, w_ada, b_ada, w_in, rpb, w_out = p
    h, gate = _pre(x, cond, g, w_ada, b_ada)
    q, k, v, z = _na_proj(h, w_in)
    o = _latent_na(q, k, v, k_ctx, v_ctx, rpb)
    return _na_out(x, o, z, gate, w_out)


def _s5_discretise(a_re, a_im, log_dt, b_re, b_im):
    a_re = a_re.astype(jnp.float32)
    a_im = a_im.astype(jnp.float32)
    b_re = b_re.astype(jnp.float32)
    b_im = b_im.astype(jnp.float32)
    dt = jnp.exp(log_dt.astype(jnp.float32))[:, None]
    mag = jnp.exp(a_re * dt)
    ab_re = mag * jnp.cos(a_im * dt)
    ab_im = mag * jnp.sin(a_im * dt)
    den = a_re * a_re + a_im * a_im
    nr = ab_re - 1.0
    f_re = (nr * a_re + ab_im * a_im) / den
    f_im = (ab_im * a_re - nr * a_im) / den
    bb_re = f_re[..., None] * b_re - f_im[..., None] * b_im
    bb_im = f_re[..., None] * b_im + f_im[..., None] * b_re
    return ab_re, ab_im, bb_re, bb_im


def _complex_combine(e1, e2):
    a1r, a1i, b1r, b1i = e1
    a2r, a2i, b2r, b2i = e2
    return (a2r * a1r - a2i * a1i, a2r * a1i + a2i * a1r,
            a2r * b1r - a2i * b1i + b2r, a2r * b1i + a2i * b1r + b2i)


def _s5_scan(ug, a_re, a_im, log_dt, b_re, b_im, h0):
    ab_re, ab_im, bb_re, bb_im = _s5_discretise(a_re, a_im, log_dt, b_re, b_im)
    bu_re = jnp.einsum("blgc,gpc->blgp", ug, bb_re)
    bu_im = jnp.einsum("blgc,gpc->blgp", ug, bb_im)
    if h0 is not None:
        h0r, h0i = h0
        bu_re = bu_re.at[:, 0].add(ab_re * h0r - ab_im * h0i)
        bu_im = bu_im.at[:, 0].add(ab_re * h0i + ab_im * h0r)
    ar = jnp.broadcast_to(ab_re, bu_re.shape)
    ai = jnp.broadcast_to(ab_im, bu_re.shape)
    _, _, hr, hi = lax.associative_scan(_complex_combine, (ar, ai, bu_re, bu_im), axis=1)
    return hr, hi


def _s5_readout(hr, hi, c_re, c_im):
    return (jnp.einsum("blgp,gcp->blgc", hr, c_re.astype(jnp.float32))
            - jnp.einsum("blgp,gcp->blgc", hi, c_im.astype(jnp.float32)))


def _s5_layer(x, cond, p, init, return_state):
    (g, w_ada, b_ada, w_in, a_re, a_im, log_dt, b_re, b_im, c_re, c_im,
     d_skip, w_glu, b_glu, w_out) = p
    bsz, l, _ = x.shape
    h, gate = _pre(x, cond, g, w_ada, b_ada)
    u, z = jnp.split(h @ w_in, 2, axis=-1)
    uf = u.astype(jnp.float32)
    ug = uf.reshape(bsz, l, S5_GROUPS, S5_CH)
    if init is None:
        h0f, h0b = None, None
    else:
        st = init.astype(jnp.float32)
        h0f, h0b = (st[:, 0, 0], st[:, 0, 1]), (st[:, 1, 0], st[:, 1, 1])
    hfr, hfi = _s5_scan(ug, a_re[0], a_im[0], log_dt[0], b_re[0], b_im[0], h0f)
    hbr, hbi = _s5_scan(ug[:, ::-1], a_re[1], a_im[1], log_dt[1], b_re[1], b_im[1], h0b)
    y = _s5_readout(hfr, hfi, c_re[0], c_im[0]) + _s5_readout(hbr, hbi, c_re[1], c_im[1])[:, ::-1]
    y = y.reshape(bsz, l, D_MODEL) + d_skip.astype(jnp.float32) * uf
    y = jax.nn.gelu(y).astype(x.dtype)
    y = y * jax.nn.sigmoid(y @ w_glu + b_glu)
    out = (y * jax.nn.silu(z)) @ w_out
    x_new = x + gate * out
    if return_state:
        final = jnp.stack([jnp.stack([hfr[:, -1], hfi[:, -1]], axis=1),
                           jnp.stack([hbr[:, -1], hbi[:, -1]], axis=1)], axis=1)
        return x_new, final
    return x_new


def setup_inputs(seed: int = 0) -> dict:
    key = jax.random.key(seed)
    ks = jax.random.split(key, 32)
    f32 = jnp.float32
    d = D_MODEL
    sd = d ** -0.5

    def nrm(k, shape, s=1.0):
        return jax.random.normal(k, shape, f32) * s

    n_idx = jnp.arange(S5_P, dtype=f32)
    return {
        "x_prompt": nrm(ks[0], (BATCH, SEQ, d)),
        "x_sample": nrm(ks[1], (DEC_BATCH, DEC_SEQ, d)),
        "cache_l0_k": nrm(ks[2], (DEC_BATCH, PAST_LEN, N_HEADS, HEAD_DIM)),
        "cache_l0_v": nrm(ks[3], (DEC_BATCH, PAST_LEN, N_HEADS, HEAD_DIM)),
        "state_l1_s5": nrm(ks[4], (DEC_BATCH, 2, 2, S5_GROUPS, S5_P), 0.5),
        "c": nrm(ks[5], (DEC_BATCH, d)),
        "c_ctx": nrm(ks[6], (d,)),
        "l0_norm_g": 1.0 + nrm(ks[7], (d,), 0.1),
        "l0_w_ada": nrm(ks[8], (d, 3 * d), 0.5 * sd),
        "l0_b_ada": nrm(ks[9], (3 * d,), 0.01),
        "l0_w_in": nrm(ks[10], (d, 4 * d), sd),
        "l0_rpb": nrm(ks[11], (N_HEADS, RPB_R, RPB_C), 0.1),
        "l0_w_out": nrm(ks[12], (d, d), sd),
        "l1_norm_g": 1.0 + nrm(ks[13], (d,), 0.1),
        "l1_w_ada": nrm(ks[14], (d, 3 * d), 0.5 * sd),
        "l1_b_ada": nrm(ks[15], (3 * d,), 0.01),
        "l1_w_in": nrm(ks[16], (d, 2 * d), sd),
        "l1_a_re": -0.5 + nrm(ks[17], (2, S5_GROUPS, S5_P), 0.01),
        "l1_a_im": jnp.pi * n_idx + nrm(ks[18], (2, S5_GROUPS, S5_P), 0.01),
        "l1_log_dt": jax.random.uniform(ks[19], (2, S5_GROUPS), f32,
                                        minval=math.log(DT_MIN), maxval=math.log(DT_MAX)),
        "l1_b_re": nrm(ks[20], (2, S5_GROUPS, S5_P, S5_CH), (2.0 * S5_CH) ** -0.5),
        "l1_b_im": nrm(ks[21], (2, S5_GROUPS, S5_P, S5_CH), (2.0 * S5_CH) ** -0.5),
        "l1_c_re": nrm(ks[22], (2, S5_GROUPS, S5_CH, S5_P), (2.0 * S5_P) ** -0.5),
        "l1_c_im": nrm(ks[23], (2, S5_GROUPS, S5_CH, S5_P), (2.0 * S5_P) ** -0.5),
        "l1_d": nrm(ks[24], (d,)),
        "l1_w_glu": nrm(ks[25], (d, d), sd),
        "l1_b_glu": nrm(ks[26], (d,), 0.01),
        "l1_w_out": nrm(ks[27], (d, d), sd),
        "final_norm_g": 1.0 + nrm(ks[28], (d,), 0.1),
    }


def reference(x_prompt, x_sample, cache_l0_k, cache_l0_v, state_l1_s5, c, c_ctx,
              l0_norm_g, l0_w_ada, l0_b_ada, l0_w_in, l0_rpb, l0_w_out,
              l1_norm_g, l1_w_ada, l1_b_ada, l1_w_in, l1_a_re, l1_a_im, l1_log_dt,
              l1_b_re, l1_b_im, l1_c_re, l1_c_im, l1_d, l1_w_glu, l1_b_glu, l1_w_out,
              final_norm_g):
    params = [
        (l0_norm_g, l0_w_ada, l0_b_ada, l0_w_in, l0_rpb, l0_w_out),
        (l1_norm_g, l1_w_ada, l1_b_ada, l1_w_in, l1_a_re, l1_a_im, l1_log_dt,
         l1_b_re, l1_b_im, l1_c_re, l1_c_im, l1_d, l1_w_glu, l1_b_glu, l1_w_out),
    ]
    caches = [(cache_l0_k, cache_l0_v), (state_l1_s5,)]
    cond = c[:, None, :]
    yp, ys = x_prompt, x_sample
    new_state = []
    for i in range(DEPTH):
        if i % N_MIXERS == 0:
            yp, k_new, v_new = _na_layer_ctx(yp, c_ctx, params[i])
            ys = _na_layer_lat(ys, cond, caches[i][0], caches[i][1], params[i])
            new_state += [k_new, v_new]
        else:
            yp, st_new = _s5_layer(yp, c_ctx, params[i], None, True)
            ys = _s5_layer(ys, cond, params[i], caches[i][0], False)
            new_state.append(st_new)
    y_prompt = _rmsnorm(yp, final_norm_g)
    y_sample = _rmsnorm(ys, final_norm_g)
    new_k_l0, new_v_l0, new_state_l1 = new_state
    return (y_prompt, y_sample, new_k_l0, new_v_l0, new_state_l1)
```

```python
import functools

import jax
import jax.numpy as jnp
from jax import lax
from jax.experimental import pallas as pl
from jax.experimental.pallas import tpu as pltpu

F32 = jnp.float32
BF16 = jnp.bfloat16

EPS = 1e-6
N_HEADS = 16
HEAD_DIM = 128
GRID_W = 64
NA_ROWS = 8
NA_COLS = 16
RPB_R = 2 * NA_ROWS - 1
RPB_C = 2 * NA_COLS - 1
S5_CH = 16
S5_P = 64
S5_CHUNK = 16
NEG = -1e30
LANES = 128
VMEM_LIMIT = 56 * 1024 * 1024


def _cparams(*sem):
    return pltpu.CompilerParams(dimension_semantics=sem, vmem_limit_bytes=VMEM_LIMIT)


def _dot(a, b):
    return jnp.dot(a, b, preferred_element_type=F32)


def _dot_nt(a, b, precision=None):
    return lax.dot_general(a, b, (((1,), (1,)), ((), ())), preferred_element_type=F32,
                           precision=precision)


def _silu(x):
    return x * jax.nn.sigmoid(x)


def _ada_kernel(cond_ref, w_ref, b_ref, o_ref):
    s = _silu(cond_ref[...])
    o_ref[...] = _dot(s.astype(BF16), w_ref[...].astype(BF16)) + b_ref[...]


def _ada(cond8, w_ada, b_ada):
    d, n = w_ada.shape
    tn = 1024
    return pl.pallas_call(
        _ada_kernel,
        grid=(n // tn,),
        in_specs=[pl.BlockSpec((8, d), lambda j: (0, 0)),
                  pl.BlockSpec((d, tn), lambda j: (0, j)),
                  pl.BlockSpec((1, tn), lambda j: (0, j))],
        out_specs=pl.BlockSpec((8, tn), lambda j: (0, j)),
        out_shape=jax.ShapeDtypeStruct((8, n), F32),
        compiler_params=_cparams("parallel"),
        name="ada",
    )(cond8, w_ada, b_ada.reshape(1, n))


def _inproj_kernel(x_ref, g_ref, shift_ref, scale_ref, *rest, n_split):
    w_refs = rest[:n_split]
    o_refs = rest[n_split:2 * n_split]
    h_ref = rest[2 * n_split]

    @pl.when(pl.program_id(1) == 0)
    def _():
        x = x_ref[...]
        y = x * lax.rsqrt(jnp.mean(x * x, axis=-1, keepdims=True) + EPS) * g_ref[...]
        h_ref[...] = (y * (1.0 + scale_ref[...]) + shift_ref[...]).astype(BF16)

    h = h_ref[...]
    for w_ref, o_ref in zip(w_refs, o_refs):
        o_ref[...] = _dot(h, w_ref[...]).astype(o_ref.dtype)


def _inproj(x, g, mod3, cond0, rows_per_cond, w, out_dtypes, tm, tn):
    m, d = x.shape
    n_split = len(out_dtypes)
    nj = d // tn

    def cond_of(i):
        return cond0 + (i * tm) // rows_per_cond

    in_specs = [pl.BlockSpec((tm, d), lambda i, j: (i, 0)),
                pl.BlockSpec((1, d), lambda i, j: (0, 0)),
                pl.BlockSpec((None, 1, d), lambda i, j: (cond_of(i), 0, 0)),
                pl.BlockSpec((None, 1, d), lambda i, j: (cond_of(i), 0, 1))]
    for s in range(n_split):
        in_specs.append(pl.BlockSpec((d, tn), lambda i, j, s=s: (0, s * nj + j)))
    return pl.pallas_call(
        functools.partial(_inproj_kernel, n_split=n_split),
        grid=(m // tm, nj),
        in_specs=in_specs,
        out_specs=[pl.BlockSpec((tm, tn), lambda i, j: (i, j)) for _ in range(n_split)],
        out_shape=[jax.ShapeDtypeStruct((m, d), dt) for dt in out_dtypes],
        scratch_shapes=[pltpu.VMEM((tm, d), BF16)],
        compiler_params=_cparams("parallel", "arbitrary"),
        name="inproj",
    )(x, g.reshape(1, d), mod3, mod3, *([w] * n_split))


def _softmax_pv(parts):
    m = parts[0][0].max(axis=-1, keepdims=True)
    for s, _ in parts[1:]:
        m = jnp.maximum(m, s.max(axis=-1, keepdims=True))
    l = None
    o = None
    for s, v in parts:
        e = jnp.exp(s - m)
        li = e.sum(axis=-1, keepdims=True)
        oi = _dot(e.astype(BF16), v)
        l = li if l is None else l + li
        o = oi if o is None else o + oi
    return o / l


def _ctx_attn_kernel(q_ref, k_ref, v_ref, o_ref):
    scale = HEAD_DIM ** -0.5
    for h in range(N_HEADS):
        sl = slice(h * HEAD_DIM, (h + 1) * HEAD_DIM)
        q = q_ref[:, sl]
        k = k_ref[:, sl].astype(BF16)
        v = v_ref[:, sl].astype(BF16)
        s = _dot_nt(q, k) * scale
        o_ref[:, sl] = _softmax_pv([(s, v)]).astype(o_ref.dtype)


def _ctx_attn(q, k, v, seq):
    m, d = q.shape
    spec = pl.BlockSpec((seq, d), lambda b: (b, 0))
    return pl.pallas_call(
        _ctx_attn_kernel,
        grid=(m // seq,),
        in_specs=[spec, spec, spec],
        out_specs=spec,
        out_shape=jax.ShapeDtypeStruct((m, d), BF16),
        compiler_params=_cparams("parallel"),
        name="ctx_attn",
    )(q, k, v)


def _na_window_start(r, rows):
    return min(max(r - NA_ROWS // 2, 0), rows - NA_ROWS)


def _na_kernel(q_ref, k_ref, v_ref, kc_ref, vc_ref, rpb_ref, o_ref, *, rows, q_rows):
    scale = HEAD_DIM ** -0.5
    w = GRID_W
    qc = lax.broadcasted_iota(jnp.int32, (w, LANES), 0)
    lane = lax.broadcasted_iota(jnp.int32, (w, LANES), 1)
    c0 = jnp.clip(qc - NA_COLS // 2, 0, w - NA_COLS)
    ok_l = (lane >= c0) & (lane < c0 + NA_COLS)
    ok_r = (lane - w >= c0) & (lane - w < c0 + NA_COLS)
    neg_tile = jnp.full((w, LANES), NEG, F32)
    tile_l, tile_r = [], []
    for d in range(RPB_R):
        base = jnp.broadcast_to(rpb_ref[d:d + 1, :], (w, LANES))
        left = pltpu.roll(base, LANES - (NA_COLS - 1), 1, stride=1, stride_axis=0)
        right = pltpu.roll(base, w - (NA_COLS - 1), 1, stride=1, stride_axis=0)
        tile_l.append(jnp.where(ok_l, left, NEG))
        tile_r.append(jnp.where(ok_r, right, NEG))

    kc = kc_ref[...].astype(BF16)
    vc = vc_ref[...].astype(BF16)
    n_groups = rows // q_rows
    for gi in range(n_groups):
        rs = list(range(gi * q_rows, (gi + 1) * q_rows))
        klo = min(_na_window_start(r, rows) for r in rs) // 2 * 2
        khi = -(-(max(_na_window_start(r, rows) for r in rs) + NA_ROWS) // 2) * 2
        bias_rows = []
        for r in rs:
            r0 = _na_window_start(r, rows)
            tiles = []
            for kr in range(klo, khi, 2):
                ok0 = r0 <= kr < r0 + NA_ROWS
                ok1 = r0 <= kr + 1 < r0 + NA_ROWS
                t0 = tile_l[kr - r + NA_ROWS - 1] if ok0 else neg_tile
                t1 = tile_r[kr + 1 - r + NA_ROWS - 1] if ok1 else neg_tile
                tiles.append(jnp.maximum(t0, t1) if (ok0 or ok1) else neg_tile)
            bias_rows.append(jnp.concatenate(tiles, axis=1))
        bias = jnp.concatenate(bias_rows, axis=0)
        q = q_ref[gi * q_rows * w:(gi + 1) * q_rows * w, :]
        kl = k_ref[klo * w:khi * w, :]
        vl = v_ref[klo * w:khi * w, :]
        s_loc = _dot_nt(q, kl) * scale + bias
        s_ctx = _dot_nt(q, kc) * scale
        o = _softmax_pv([(s_loc, vl), (s_ctx, vc)])
        o_ref[gi * q_rows * w:(gi + 1) * q_rows * w, :] = o.astype(o_ref.dtype)


def _latent_na(q, k, v, k_ctx, v_ctx, rpb_pad, n_tok):
    m, d = q.shape
    nb = m // n_tok
    lc = k_ctx.shape[1]
    rows = n_tok // GRID_W
    spec = pl.BlockSpec((n_tok, HEAD_DIM), lambda b, h: (b, h))
    cspec = pl.BlockSpec((None, lc, HEAD_DIM), lambda b, h: (b, 0, h))
    return pl.pallas_call(
        functools.partial(_na_kernel, rows=rows, q_rows=4),
        grid=(nb, N_HEADS),
        in_specs=[spec, spec, spec, cspec, cspec,
                  pl.BlockSpec((None, 16, LANES), lambda b, h: (h, 0, 0))],
        out_specs=spec,
        out_shape=jax.ShapeDtypeStruct((m, d), BF16),
        compiler_params=_cparams("parallel", "parallel"),
        name="latent_na",
    )(q, k, v, k_ctx, v_ctx, rpb_pad)


def _gated_out_kernel(o_ref, z_ref, x_ref, gate_ref, w_ref, y_ref):
    a = o_ref[...].astype(F32) * _silu(z_ref[...].astype(F32))
    y_ref[...] = x_ref[...] + gate_ref[...] * _dot(a.astype(BF16), w_ref[...])


def _gated_out(o, z, x, mod3, cond0, rows_per_cond, w_out, tm):
    m, d = x.shape
    row = pl.BlockSpec((tm, d), lambda i: (i, 0))
    return pl.pallas_call(
        _gated_out_kernel,
        grid=(m // tm,),
        in_specs=[row, row, row,
                  pl.BlockSpec((None, 1, d), lambda i: (cond0 + (i * tm) // rows_per_cond, 0, 2)),
                  pl.BlockSpec((d, d), lambda i: (0, 0), pipeline_mode=pl.Buffered(1))],
        out_specs=row,
        out_shape=jax.ShapeDtypeStruct((m, d), F32),
        compiler_params=_cparams("parallel"),
        name="gated_out",
    )(o, z, x, mod3, w_out)


def _cmul(ar, ai, br, bi):
    return ar * br - ai * bi, ar * bi + ai * br


def _s5_powers(ar, ai, exps):
    shape = exps[0].shape
    one = jnp.ones(shape, F32)
    zero = jnp.zeros(shape, F32)
    outs = [None] * len(exps)
    cr, ci = ar, ai
    bit = 1
    while bit < S5_CHUNK:
        crb = jnp.broadcast_to(cr, shape)
        cib = jnp.broadcast_to(ci, shape)
        for n, e in enumerate(exps):
            sel = (e & bit) != 0
            fr = jnp.where(sel, crb, one)
            fi = jnp.where(sel, cib, zero)
            outs[n] = (fr, fi) if outs[n] is None else _cmul(outs[n][0], outs[n][1], fr, fi)
        cr, ci = _cmul(cr, ci, cr, ci)
        bit *= 2
    return outs, (cr, ci)


def _s5_kernel(xp_ref, xs_ref, h0_ref, are_ref, aim_ref, ldt_ref, bre_ref, bim_ref, cre_ref, cim_ref,
               yp_ref, ys_ref, st_ref, w_ref, ws_ref, wc_ref, s_ref, ha_ref, hb_ref, *, nb_p, nb_s):
    p = S5_P
    kc = S5_CHUNK * S5_CH
    hi = lax.Precision.HIGHEST
    a_re = are_ref[...]
    a_im = aim_ref[...]
    dt = jnp.exp(ldt_ref[...])
    mag = jnp.exp(a_re * dt)
    ab_re = mag * jnp.cos(a_im * dt)
    ab_im = mag * jnp.sin(a_im * dt)
    den = a_re * a_re + a_im * a_im
    nr = ab_re - 1.0
    f_re = (nr * a_re + ab_im * a_im) / den
    f_im = (ab_im * a_re - nr * a_im) / den
    bbt_re, bbt_im = _cmul(f_re, f_im, bre_ref[...], bim_ref[...])

    blk = lax.broadcasted_iota(jnp.int32, (kc, 2 * p), 0) // S5_CH
    is_fwd = lax.broadcasted_iota(jnp.int32, (kc, 2 * p), 1) < p
    e_g = jnp.where(is_fwd, blk, S5_CHUNK - 1 - blk)
    ((pg_r, pg_i), (ps_r, ps_i)), (ac_r, ac_i) = _s5_powers(ab_re, ab_im, [e_g, S5_CHUNK - 1 - e_g])
    c_re = jnp.concatenate([cre_ref[...]] * S5_CHUNK, axis=0)
    c_im = jnp.concatenate([cim_ref[...]] * S5_CHUNK, axis=0)
    bt_re = jnp.concatenate([bbt_re] * S5_CHUNK, axis=0)
    bt_im = jnp.concatenate([bbt_im] * S5_CHUNK, axis=0)

    g_re, g_im = _cmul(pg_r, pg_i, c_re, c_im)
    fwd16 = lax.broadcasted_iota(jnp.int32, (S5_CH, 2 * p), 1) < p
    r0 = []
    for d in range(2):
        msk = fwd16 if d == 0 else jnp.logical_not(fwd16)
        r0.append(_dot_nt(jnp.where(msk, bbt_re, 0.0), g_re, hi)
                  - _dot_nt(jnp.where(msk, bbt_im, 0.0), g_im, hi))
    lane = lax.broadcasted_iota(jnp.int32, (S5_CH, kc), 1)
    for s in range(S5_CHUNK):
        f = jnp.where(lane >= S5_CH * s, pltpu.roll(r0[0], S5_CH * s, 1), 0.0) if s else r0[0]
        b = jnp.where(lane < S5_CH * (s + 1), pltpu.roll(r0[1], (S5_CH * (s + 1)) % kc, 1), 0.0)
        w_ref[s * S5_CH:(s + 1) * S5_CH, :] = (f + b).astype(BF16)
    e_re, e_im = _cmul(ps_r, ps_i, bt_re, bt_im)
    ws_ref[:, 0:2 * p] = e_re.astype(BF16)
    ws_ref[:, 2 * p:4 * p] = e_im.astype(BF16)
    g1_re, g1_im = _cmul(g_re, g_im, ab_re, ab_im)
    wc_ref[:, 0:2 * p] = g1_re.astype(BF16)
    wc_ref[:, 2 * p:4 * p] = (-g1_im).astype(BF16)

    def run(x_ref, y_ref, nb, hr, hm):
        rows = x_ref.shape[0]
        n = rows // nb
        x = x_ref[...]
        y = _dot(x, w_ref[...])
        s_ref[0:rows, :] = _dot(x, ws_ref[...])
        arb = jnp.broadcast_to(ac_r, (nb, 2 * p))
        aib = jnp.broadcast_to(ac_i, (nb, 2 * p))
        isf = lax.broadcasted_iota(jnp.int32, (nb, 2 * p), 1) < p
        for k in range(n):
            rf = slice(k * nb, (k + 1) * nb)
            rb = slice((n - 1 - k) * nb, (n - k) * nb)
            ha_ref[rf, 0:2 * p] = hr
            ha_ref[rf, 2 * p:4 * p] = hm
            hb_ref[rb, 0:2 * p] = hr
            hb_ref[rb, 2 * p:4 * p] = hm
            sr = jnp.where(isf, s_ref[rf, 0:2 * p], s_ref[rb, 0:2 * p])
            sm = jnp.where(isf, s_ref[rf, 2 * p:4 * p], s_ref[rb, 2 * p:4 * p])
            hr, hm = arb * hr - aib * hm + sr, arb * hm + aib * hr + sm
        isf4 = (lax.broadcasted_iota(jnp.int32, (rows, 4 * p), 1) // p) % 2 == 0
        h_prev = jnp.where(isf4, ha_ref[0:rows, :], hb_ref[0:rows, :])
        y_ref[...] = y + _dot_nt(h_prev.astype(BF16), wc_ref[...])
        return hr, hm

    zeros = jnp.zeros((nb_p, 2 * p), F32)
    hr, hm = run(xp_ref, yp_ref, nb_p, zeros, zeros)
    st_ref[:, 0:2 * p] = hr
    st_ref[:, 2 * p:4 * p] = hm
    run(xs_ref, ys_ref, nb_s, h0_ref[:, 0:2 * p], h0_ref[:, 2 * p:4 * p])


def _s5(xp, xs, h0, a_re, a_im, log_dt, bt_re, bt_im, c_re, c_im, nb_p, nb_s):
    g, rows_p, kc = xp.shape
    rows_s = xs.shape[1]
    p = S5_P

    def gspec(shape):
        return pl.BlockSpec((None,) + shape, lambda i: (i, 0, 0))

    return pl.pallas_call(
        functools.partial(_s5_kernel, nb_p=nb_p, nb_s=nb_s),
        grid=(g,),
        in_specs=[gspec((rows_p, kc)), gspec((rows_s, kc)), gspec((nb_s, 4 * p)),
                  gspec((1, 2 * p)), gspec((1, 2 * p)), gspec((1, 2 * p)),
                  gspec((S5_CH, 2 * p)), gspec((S5_CH, 2 * p)), gspec((S5_CH, 2 * p)), gspec((S5_CH, 2 * p))],
        out_specs=[gspec((rows_p, kc)), gspec((rows_s, kc)), gspec((nb_p, 4 * p))],
        out_shape=[jax.ShapeDtypeStruct((g, rows_p, kc), F32),
                   jax.ShapeDtypeStruct((g, rows_s, kc), F32),
                   jax.ShapeDtypeStruct((g, nb_p, 4 * p), F32)],
        scratch_shapes=[pltpu.VMEM((kc, kc), BF16), pltpu.VMEM((kc, 4 * p), BF16), pltpu.VMEM((kc, 4 * p), BF16),
                        pltpu.VMEM((max(rows_p, rows_s), 4 * p), F32),
                        pltpu.VMEM((max(rows_p, rows_s), 4 * p), F32),
                        pltpu.VMEM((max(rows_p, rows_s), 4 * p), F32)],
        compiler_params=_cparams("parallel"),
        name="s5",
    )(xp, xs, h0, a_re, a_im, log_dt, bt_re, bt_im, c_re, c_im)


def _glu_out_kernel(y_ref, u_ref, z_ref, x_ref, d_ref, bglu_ref, gate_ref, fg_ref, wglu_ref, wout_ref, o_ref):
    y = y_ref[...] + d_ref[...] * u_ref[...].astype(F32)
    y = jax.nn.gelu(y)
    y = y * jax.nn.sigmoid(_dot(y.astype(BF16), wglu_ref[...]) + bglu_ref[...])
    a = y * _silu(z_ref[...].astype(F32))
    xn = x_ref[...] + gate_ref[...] * _dot(a.astype(BF16), wout_ref[...])
    o_ref[...] = xn * lax.rsqrt(jnp.mean(xn * xn, axis=-1, keepdims=True) + EPS) * fg_ref[...]


def _glu_out(y, u, z, x, d_skip, b_glu, mod3, cond0, rows_per_cond, final_g, w_glu, w_out, tm):
    m, d = x.shape
    row = pl.BlockSpec((tm, d), lambda i: (i, 0))
    vec = pl.BlockSpec((1, d), lambda i: (0, 0))
    wspec = pl.BlockSpec((d, d), lambda i: (0, 0), pipeline_mode=pl.Buffered(1))
    return pl.pallas_call(
        _glu_out_kernel,
        grid=(m // tm,),
        in_specs=[row, row, row, row, vec, vec,
                  pl.BlockSpec((None, 1, d), lambda i: (cond0 + (i * tm) // rows_per_cond, 0, 2)),
                  vec, wspec, wspec],
        out_specs=row,
        out_shape=jax.ShapeDtypeStruct((m, d), F32),
        compiler_params=_cparams("parallel"),
        name="glu_out",
    )(y, u, z, x, d_skip.reshape(1, d), b_glu.reshape(1, d), mod3, final_g.reshape(1, d), w_glu, w_out)


def _to_chunks(u, nb):
    m, d = u.shape
    n_chunks = m // nb // S5_CHUNK
    g = d // S5_CH
    u = u.reshape(nb, n_chunks, S5_CHUNK, g, S5_CH)
    return u.transpose(3, 1, 0, 2, 4).reshape(g, n_chunks * nb, S5_CHUNK * S5_CH)


def _from_chunks(y, nb):
    g, rows, _ = y.shape
    n_chunks = rows // nb
    y = y.reshape(g, n_chunks, nb, S5_CHUNK, S5_CH)
    return y.transpose(2, 1, 3, 0, 4).reshape(nb * n_chunks * S5_CHUNK, g * S5_CH)


def _both_dirs(a):
    _, g, r, p = a.shape
    return a.transpose(1, 2, 0, 3).reshape(g, r, 2 * p)


def kernel(x_prompt, x_sample, cache_l0_k, cache_l0_v, state_l1_s5, c, c_ctx, l0_norm_g, l0_w_ada, l0_b_ada, l0_w_in, l0_rpb, l0_w_out, l1_norm_g, l1_w_ada, l1_b_ada, l1_w_in, l1_a_re, l1_a_im, l1_log_dt, l1_b_re, l1_b_im, l1_c_re, l1_c_im, l1_d, l1_w_glu, l1_b_glu, l1_w_out, final_norm_g):
    bp, seq, d = x_prompt.shape
    bs, n_tok, _ = x_sample.shape
    lc = cache_l0_k.shape[1]
    g = d // S5_CH
    p = S5_P
    xp = x_prompt.reshape(bp * seq, d)
    xs = x_sample.reshape(bs * n_tok, d)

    cond8 = jnp.zeros((8, d), F32).at[0].set(c_ctx).at[1:1 + bs].set(c)
    mod0 = _ada(cond8, l0_w_ada, l0_b_ada).reshape(8, 1, 3 * d)
    mod1 = _ada(cond8, l1_w_ada, l1_b_ada).reshape(8, 1, 3 * d)

    w_in0 = l0_w_in.astype(BF16)
    w_out0 = l0_w_out.astype(BF16)
    qp, kp, vp, zp = _inproj(xp, l0_norm_g, mod0, 0, bp * seq, w_in0, (BF16, F32, F32, BF16), 512, 512)
    qs, ks, vs, zs = _inproj(xs, l0_norm_g, mod0, 1, n_tok, w_in0, (BF16, BF16, BF16, BF16), 512, 512)
    op = _ctx_attn(qp, kp, vp, seq)
    rpb_pad = jnp.zeros((N_HEADS, 16, LANES), F32).at[:, :RPB_R, :RPB_C].set(l0_rpb)
    os_ = _latent_na(qs, ks, vs, cache_l0_k.reshape(bs, lc, d), cache_l0_v.reshape(bs, lc, d), rpb_pad, n_tok)
    x1p = _gated_out(op, zp, xp, mod0, 0, bp * seq, w_out0, 512)
    x1s = _gated_out(os_, zs, xs, mod0, 1, n_tok, w_out0, 512)

    w_in1 = l1_w_in.astype(BF16)
    up, zp1 = _inproj(x1p, l1_norm_g, mod1, 0, bp * seq, w_in1, (BF16, BF16), 512, 512)
    us, zs1 = _inproj(x1s, l1_norm_g, mod1, 1, n_tok, w_in1, (BF16, BF16), 512, 512)
    h0 = state_l1_s5.transpose(3, 0, 2, 1, 4).reshape(g, bs, 4 * p)
    log_dt = jnp.broadcast_to(l1_log_dt[:, :, None, None], (2, g, 1, p))
    yp, ys, st = _s5(
        _to_chunks(up, bp), _to_chunks(us, bs), h0,
        _both_dirs(l1_a_re[:, :, None, :]), _both_dirs(l1_a_im[:, :, None, :]), _both_dirs(log_dt),
        _both_dirs(l1_b_re.transpose(0, 1, 3, 2)), _both_dirs(l1_b_im.transpose(0, 1, 3, 2)),
        _both_dirs(l1_c_re), _both_dirs(l1_c_im), bp, bs)
    w_glu = l1_w_glu.astype(BF16)
    w_out1 = l1_w_out.astype(BF16)
    y_prompt = _glu_out(_from_chunks(yp, bp), up, zp1, x1p, l1_d, l1_b_glu, mod1, 0, bp * seq,
                        final_norm_g, w_glu, w_out1, 256)
    y_sample = _glu_out(_from_chunks(ys, bs), us, zs1, x1s, l1_d, l1_b_glu, mod1, 1, n_tok,
                        final_norm_g, w_glu, w_out1, 256)
    new_state = st.reshape(g, bp, 2, 2, p).transpose(1, 3, 2, 0, 4)
    return (y_prompt.reshape(bp, seq, d), y_sample.reshape(bs, n_tok, d),
            kp.reshape(bp, seq, N_HEADS, HEAD_DIM), vp.reshape(bp, seq, N_HEADS, HEAD_DIM), new_state)
```

```python
import functools

import jax
import jax.numpy as jnp
from jax import lax
from jax.experimental import pallas as pl
from jax.experimental.pallas import tpu as pltpu

F32 = jnp.float32
BF16 = jnp.bfloat16

EPS = 1e-6
N_HEADS = 16
HEAD_DIM = 128
GRID_W = 64
NA_ROWS = 8
NA_COLS = 16
RPB_R = 2 * NA_ROWS - 1
RPB_C = 2 * NA_COLS - 1
S5_CH = 16
S5_P = 64
S5_CHUNK = 16
NEG = -1e30
LANES = 128
VMEM_LIMIT = 56 * 1024 * 1024


def _cparams(*sem):
    return pltpu.CompilerParams(dimension_semantics=sem, vmem_limit_bytes=VMEM_LIMIT)


def _dot(a, b):
    return jnp.dot(a, b, preferred_element_type=F32)


def _dot_nt(a, b, precision=None):
    return lax.dot_general(a, b, (((1,), (1,)), ((), ())), preferred_element_type=F32,
                           precision=precision)


def _silu(x):
    return x * jax.nn.sigmoid(x)


def _ada_kernel(cond_ref, w_ref, b_ref, o_ref):
    s = _silu(cond_ref[...])
    o_ref[...] = _dot(s.astype(BF16), w_ref[...].astype(BF16)) + b_ref[...]


def _ada(cond8, w_ada, b_ada):
    d, n = w_ada.shape
    tn = 1024
    return pl.pallas_call(
        _ada_kernel,
        grid=(n // tn,),
        in_specs=[pl.BlockSpec((8, d), lambda j: (0, 0)),
                  pl.BlockSpec((d, tn), lambda j: (0, j)),
                  pl.BlockSpec((1, tn), lambda j: (0, j))],
        out_specs=pl.BlockSpec((8, tn), lambda j: (0, j)),
        out_shape=jax.ShapeDtypeStruct((8, n), F32),
        compiler_params=_cparams("parallel"),
        name="ada",
    )(cond8, w_ada, b_ada.reshape(1, n))


def _inproj_kernel(x_ref, g_ref, shift_ref, scale_ref, *rest, n_split):
    w_refs = rest[:n_split]
    o_refs = rest[n_split:2 * n_split]
    h_ref = rest[2 * n_split]

    @pl.when(pl.program_id(1) == 0)
    def _():
        x = x_ref[...]
        y = x * lax.rsqrt(jnp.mean(x * x, axis=-1, keepdims=True) + EPS) * g_ref[...]
        h_ref[...] = (y * (1.0 + scale_ref[...]) + shift_ref[...]).astype(BF16)

    h = h_ref[...]
    for w_ref, o_ref in zip(w_refs, o_refs):
        o_ref[...] = _dot(h, w_ref[...]).astype(o_ref.dtype)


def _inproj(x, g, mod3, cond0, rows_per_cond, w, out_dtypes, tm, tn):
    m, d = x.shape
    n_split = len(out_dtypes)
    nj = d // tn

    def cond_of(i):
        return cond0 + (i * tm) // rows_per_cond

    in_specs = [pl.BlockSpec((tm, d), lambda i, j: (i, 0)),
                pl.BlockSpec((1, d), lambda i, j: (0, 0)),
                pl.BlockSpec((None, 1, d), lambda i, j: (cond_of(i), 0, 0)),
                pl.BlockSpec((None, 1, d), lambda i, j: (cond_of(i), 0, 1))]
    for s in range(n_split):
        in_specs.append(pl.BlockSpec((d, tn), lambda i, j, s=s: (0, s * nj + j)))
    return pl.pallas_call(
        functools.partial(_inproj_kernel, n_split=n_split),
        grid=(m // tm, nj),
        in_specs=in_specs,
        out_specs=[pl.BlockSpec((tm, tn), lambda i, j: (i, j)) for _ in range(n_split)],
        out_shape=[jax.ShapeDtypeStruct((m, d), dt) for dt in out_dtypes],
        scratch_shapes=[pltpu.VMEM((tm, d), BF16)],
        compiler_params=_cparams("parallel", "arbitrary"),
        name="inproj",
    )(x, g.reshape(1, d), mod3, mod3, *([w] * n_split))


def _softmax_pv(parts):
    m = parts[0][0].max(axis=-1, keepdims=True)
    for s, _ in parts[1:]:
        m = jnp.maximum(m, s.max(axis=-1, keepdims=True))
    l = None
    o = None
    for s, v in parts:
        e = jnp.exp(s - m)
        li = e.sum(axis=-1, keepdims=True)
        oi = _dot(e.astype(BF16), v)
        l = li if l is None else l + li
        o = oi if o is None else o + oi
    return o / l


def _ctx_attn_kernel(q_ref, k_ref, v_ref, o_ref):
    scale = HEAD_DIM ** -0.5
    for h in range(N_HEADS):
        sl = slice(h * HEAD_DIM, (h + 1) * HEAD_DIM)
        q = q_ref[:, sl]
        k = k_ref[:, sl].astype(BF16)
        v = v_ref[:, sl].astype(BF16)
        s = _dot_nt(q, k) * scale
        o_ref[:, sl] = _softmax_pv([(s, v)]).astype(o_ref.dtype)


def _ctx_attn(q, k, v, seq):
    m, d = q.shape
    spec = pl.BlockSpec((seq, d), lambda b: (b, 0))
    return pl.pallas_call(
        _ctx_attn_kernel,
        grid=(m // seq,),
        in_specs=[spec, spec, spec],
        out_specs=spec,
        out_shape=jax.ShapeDtypeStruct((m, d), BF16),
        compiler_params=_cparams("parallel"),
        name="ctx_attn",
    )(q, k, v)


def _na_window_start(r, rows):
    return min(max(r - NA_ROWS // 2, 0), rows - NA_ROWS)


def _na_kernel(q_ref, k_ref, v_ref, kc_ref, vc_ref, rpb_ref, o_ref, *, rows, q_rows):
    scale = HEAD_DIM ** -0.5
    w = GRID_W
    qc = lax.broadcasted_iota(jnp.int32, (w, LANES), 0)
    lane = lax.broadcasted_iota(jnp.int32, (w, LANES), 1)
    c0 = jnp.clip(qc - NA_COLS // 2, 0, w - NA_COLS)
    ok_l = (lane >= c0) & (lane < c0 + NA_COLS)
    ok_r = (lane - w >= c0) & (lane - w < c0 + NA_COLS)
    neg_tile = jnp.full((w, LANES), NEG, F32)
    tile_l, tile_r = [], []
    for d in range(RPB_R):
        base = jnp.broadcast_to(rpb_ref[d:d + 1, :], (w, LANES))
        left = pltpu.roll(base, LANES - (NA_COLS - 1), 1, stride=1, stride_axis=0)
        right = pltpu.roll(base, w - (NA_COLS - 1), 1, stride=1, stride_axis=0)
        tile_l.append(jnp.where(ok_l, left, NEG))
        tile_r.append(jnp.where(ok_r, right, NEG))

    kc = kc_ref[...].astype(BF16)
    vc = vc_ref[...].astype(BF16)
    n_groups = rows // q_rows
    for gi in range(n_groups):
        rs = list(range(gi * q_rows, (gi + 1) * q_rows))
        klo = min(_na_window_start(r, rows) for r in rs) // 2 * 2
        khi = -(-(max(_na_window_start(r, rows) for r in rs) + NA_ROWS) // 2) * 2
        bias_rows = []
        for r in rs:
            r0 = _na_window_start(r, rows)
            tiles = []
            for kr in range(klo, khi, 2):
                ok0 = r0 <= kr < r0 + NA_ROWS
                ok1 = r0 <= kr + 1 < r0 + NA_ROWS
                t0 = tile_l[kr - r + NA_ROWS - 1] if ok0 else neg_tile
                t1 = tile_r[kr + 1 - r + NA_ROWS - 1] if ok1 else neg_tile
                tiles.append(jnp.maximum(t0, t1) if (ok0 or ok1) else neg_tile)
            bias_rows.append(jnp.concatenate(tiles, axis=1))
        bias = jnp.concatenate(bias_rows, axis=0)
        q = q_ref[gi * q_rows * w:(gi + 1) * q_rows * w, :]
        kl = k_ref[klo * w:khi * w, :]
        vl = v_ref[klo * w:khi * w, :]
        s_loc = _dot_nt(q, kl) * scale + bias
        s_ctx = _dot_nt(q, kc) * scale
        o = _softmax_pv([(s_loc, vl), (s_ctx, vc)])
        o_ref[gi * q_rows * w:(gi + 1) * q_rows * w, :] = o.astype(o_ref.dtype)


def _latent_na(q, k, v, k_ctx, v_ctx, rpb_pad, n_tok):
    m, d = q.shape
    nb = m // n_tok
    lc = k_ctx.shape[1]
    rows = n_tok // GRID_W
    spec = pl.BlockSpec((n_tok, HEAD_DIM), lambda b, h: (b, h))
    cspec = pl.BlockSpec((None, lc, HEAD_DIM), lambda b, h: (b, 0, h))
    return pl.pallas_call(
        functools.partial(_na_kernel, rows=rows, q_rows=4),
        grid=(nb, N_HEADS),
        in_specs=[spec, spec, spec, cspec, cspec,
                  pl.BlockSpec((None, 16, LANES), lambda b, h: (h, 0, 0))],
        out_specs=spec,
        out_shape=jax.ShapeDtypeStruct((m, d), BF16),
        compiler_params=_cparams("parallel", "parallel"),
        name="latent_na",
    )(q, k, v, k_ctx, v_ctx, rpb_pad)


def _gated_out_kernel(o_ref, z_ref, x_ref, gate_ref, w_ref, y_ref):
    a = o_ref[...].astype(F32) * _silu(z_ref[...].astype(F32))
    y_ref[...] = x_ref[...] + gate_ref[...] * _dot(a.astype(BF16), w_ref[...])


def _gated_out(o, z, x, mod3, cond0, rows_per_cond, w_out, tm):
    m, d = x.shape
    row = pl.BlockSpec((tm, d), lambda i: (i, 0))
    return pl.pallas_call(
        _gated_out_kernel,
        grid=(m // tm,),
        in_specs=[row, row, row,
                  pl.BlockSpec((None, 1, d), lambda i: (cond0 + (i * tm) // rows_per_cond, 0, 2)),
                  pl.BlockSpec((d, d), lambda i: (0, 0), pipeline_mode=pl.Buffered(1))],
        out_specs=row,
        out_shape=jax.ShapeDtypeStruct((m, d), F32),
        compiler_params=_cparams("parallel"),
        name="gated_out",
    )(o, z, x, mod3, w_out)


LANE_TILE = 256


def _tile_geometry(nc):
    n_x = max(1, LANE_TILE // nc)
    return n_x, max(1, nc // LANE_TILE), LANE_TILE // n_x


def _x_tile_specs(n_x, tps, rx, d):
    return [pl.BlockSpec((rx, d), lambda l, xi=xi: (l % tps, (l // tps) * n_x + xi)) for xi in range(n_x)]


def _cat_rows(refs):
    return refs[0][...] if len(refs) == 1 else jnp.concatenate([r[...] for r in refs], axis=0)


def _cond_rows(ref, n_rows, rx, n_cond):
    out = ref[n_cond - 1]
    if n_cond > 1:
        b = (lax.broadcasted_iota(jnp.int32, (n_rows, 1), 0) % rx) // (rx // n_cond)
        for i in range(n_cond - 2, -1, -1):
            out = jnp.where(b == i, ref[i], out)
    return out


def _inproj_t_kernel(*refs, n_x, n_cond):
    x_refs = refs[:n_x]
    g_ref, shift_ref, scale_ref, wt_ref, u_ref, z_ref = refs[n_x:]
    x = _cat_rows(x_refs)
    n_rows, d = x.shape
    y = x * lax.rsqrt(jnp.mean(x * x, axis=-1, keepdims=True) + EPS) * g_ref[...]
    rx = n_rows // n_x
    h = (y * (1.0 + _cond_rows(scale_ref, n_rows, rx, n_cond))
         + _cond_rows(shift_ref, n_rows, rx, n_cond)).astype(BF16)
    u_ref[...] = _dot_nt(wt_ref[0:d, :], h).astype(u_ref.dtype)
    z_ref[...] = _dot_nt(wt_ref[d:2 * d, :], h).astype(z_ref.dtype)


def _inproj_t(x2d, g, mod3, cond0, n_cond, wt):
    nc, sd = x2d.shape
    d = sd // S5_CHUNK
    n_x, tps, rx = _tile_geometry(nc)
    assert n_cond == 1 or rx == nc

    def mspec(col):
        return pl.BlockSpec((n_cond, 1, d), lambda l: (cond0 // n_cond, 0, col))

    tile = pl.BlockSpec((d, LANE_TILE), lambda l: (0, l))
    return pl.pallas_call(
        functools.partial(_inproj_t_kernel, n_x=n_x, n_cond=n_cond),
        grid=(S5_CHUNK * nc // LANE_TILE,),
        in_specs=_x_tile_specs(n_x, tps, rx, d) + [
            pl.BlockSpec((1, d), lambda l: (0, 0)), mspec(0), mspec(1),
            pl.BlockSpec((2 * d, d), lambda l: (0, 0), pipeline_mode=pl.Buffered(1))],
        out_specs=[tile, tile],
        out_shape=[jax.ShapeDtypeStruct((d, S5_CHUNK * nc), BF16)] * 2,
        compiler_params=_cparams("parallel"),
        name="inproj_t",
    )(*([x2d] * n_x), g.reshape(1, d), mod3, mod3, wt)


def _cmul(ar, ai, br, bi):
    return ar * br - ai * bi, ar * bi + ai * br


def _s5_powers(ar, ai, exps):
    shape = exps[0].shape
    one = jnp.ones(shape, F32)
    zero = jnp.zeros(shape, F32)
    outs = [None] * len(exps)
    cr, ci = ar, ai
    bit = 1
    while bit < S5_CHUNK:
        crb = jnp.broadcast_to(cr, shape)
        cib = jnp.broadcast_to(ci, shape)
        for n, e in enumerate(exps):
            sel = (e & bit) != 0
            fr = jnp.where(sel, crb, one)
            fi = jnp.where(sel, cib, zero)
            outs[n] = (fr, fi) if outs[n] is None else _cmul(outs[n][0], outs[n][1], fr, fi)
        cr, ci = _cmul(cr, ci, cr, ci)
        bit *= 2
    return outs, (cr, ci)


def _s5_kernel(xp_ref, xs_ref, h0_ref, are_ref, aim_ref, ldt_ref, bre_ref, bim_ref, cre_ref, cim_ref,
               yp_ref, ys_ref, st_ref, w_ref, ws_ref, wc_ref, s_ref, *, nb_p, nb_s):
    p = S5_P
    kc = S5_CHUNK * S5_CH
    hi = lax.Precision.HIGHEST
    a_re = are_ref[...]
    a_im = aim_ref[...]
    dt = jnp.exp(ldt_ref[...])
    mag = jnp.exp(a_re * dt)
    ab_re = mag * jnp.cos(a_im * dt)
    ab_im = mag * jnp.sin(a_im * dt)
    den = a_re * a_re + a_im * a_im
    nr = ab_re - 1.0
    f_re = (nr * a_re + ab_im * a_im) / den
    f_im = (ab_im * a_re - nr * a_im) / den
    bbt_re, bbt_im = _cmul(f_re, f_im, bre_ref[...], bim_ref[...])

    blk = lax.broadcasted_iota(jnp.int32, (kc, 2 * p), 0) // S5_CH
    is_fwd = lax.broadcasted_iota(jnp.int32, (kc, 2 * p), 1) < p
    e_g = jnp.where(is_fwd, blk, S5_CHUNK - 1 - blk)
    ((pg_r, pg_i), (ps_r, ps_i)), (ac_r, ac_i) = _s5_powers(ab_re, ab_im, [e_g, S5_CHUNK - 1 - e_g])
    c_re = jnp.concatenate([cre_ref[...]] * S5_CHUNK, axis=0)
    c_im = jnp.concatenate([cim_ref[...]] * S5_CHUNK, axis=0)
    bt_re = jnp.concatenate([bbt_re] * S5_CHUNK, axis=0)
    bt_im = jnp.concatenate([bbt_im] * S5_CHUNK, axis=0)

    g_re, g_im = _cmul(pg_r, pg_i, c_re, c_im)
    fwd16 = lax.broadcasted_iota(jnp.int32, (S5_CH, 2 * p), 1) < p
    r0 = []
    for d in range(2):
        msk = fwd16 if d == 0 else jnp.logical_not(fwd16)
        r0.append(_dot_nt(jnp.where(msk, bbt_re, 0.0), g_re, hi)
                  - _dot_nt(jnp.where(msk, bbt_im, 0.0), g_im, hi))
    lane = lax.broadcasted_iota(jnp.int32, (S5_CH, kc), 1)
    for s in range(S5_CHUNK):
        f = jnp.where(lane >= S5_CH * s, pltpu.roll(r0[0], S5_CH * s, 1), 0.0) if s else r0[0]
        b = jnp.where(lane < S5_CH * (s + 1), pltpu.roll(r0[1], (S5_CH * (s + 1)) % kc, 1), 0.0)
        w_ref[s * S5_CH:(s + 1) * S5_CH, :] = (f + b).astype(BF16)
    e_re, e_im = _cmul(ps_r, ps_i, bt_re, bt_im)
    ws_ref[:, 0:2 * p] = e_re.astype(BF16)
    ws_ref[:, 2 * p:4 * p] = e_im.astype(BF16)
    g1_re, g1_im = _cmul(g_re, g_im, ab_re, ab_im)
    wc_ref[:, 0:2 * p] = g1_re.astype(BF16)
    wc_ref[:, 2 * p:4 * p] = (-g1_im).astype(BF16)

    def run(xt_ref, yt_ref, nb, hr, hm):
        rows = xt_ref.shape[1] // S5_CHUNK
        n = rows // nb
        xt = jnp.concatenate([xt_ref[:, s * rows:(s + 1) * rows] for s in range(S5_CHUNK)], axis=0)
        x = xt.astype(F32).T.astype(BF16)
        y = _dot(x, w_ref[...])
        s_all = _dot(x, ws_ref[...])
        s_ref[0, 0:rows, :] = s_all[:, 0:2 * p]
        s_ref[1, 0:rows, :] = s_all[:, 2 * p:4 * p]
        arb = jnp.broadcast_to(ac_r, (nb, 2 * p))
        aib = jnp.broadcast_to(ac_i, (nb, 2 * p))
        isf = lax.broadcasted_iota(jnp.int32, (nb, 2 * p), 1) < p
        for k in range(n):
            rf = pl.ds(k, nb, stride=n)
            rb = pl.ds(n - 1 - k, nb, stride=n)
            s_ref[2, rf, :] = hr
            s_ref[3, rf, :] = hm
            s_ref[4, rb, :] = hr
            s_ref[5, rb, :] = hm
            sr = jnp.where(isf, s_ref[0, rf, :], s_ref[0, rb, :])
            sm = jnp.where(isf, s_ref[1, rf, :], s_ref[1, rb, :])
            hr, hm = arb * hr - aib * hm + sr, arb * hm + aib * hr + sm
        isf_rows = lax.broadcasted_iota(jnp.int32, (rows, 2 * p), 1) < p
        h_prev = jnp.concatenate([jnp.where(isf_rows, s_ref[2, 0:rows, :], s_ref[4, 0:rows, :]),
                                  jnp.where(isf_rows, s_ref[3, 0:rows, :], s_ref[5, 0:rows, :])], axis=1)
        yt = (y + _dot_nt(h_prev.astype(BF16), wc_ref[...])).T
        for t in range(S5_CHUNK):
            yt_ref[:, t * rows:(t + 1) * rows] = yt[t * S5_CH:(t + 1) * S5_CH, :]
        return hr, hm

    zeros = jnp.zeros((nb_p, 2 * p), F32)
    hr, hm = run(xp_ref, yp_ref, nb_p, zeros, zeros)
    st_ref[:, 0:2 * p] = hr
    st_ref[:, 2 * p:4 * p] = hm
    run(xs_ref, ys_ref, nb_s, h0_ref[:, 0:2 * p], h0_ref[:, 2 * p:4 * p])


def _s5(utp, uts, h0, a_re, a_im, log_dt, bt_re, bt_im, c_re, c_im, nb_p, nb_s):
    d, lanes_p = utp.shape
    lanes_s = uts.shape[1]
    g = d // S5_CH
    rows_p = lanes_p // S5_CHUNK
    rows_s = lanes_s // S5_CHUNK
    kc = S5_CHUNK * S5_CH
    p = S5_P

    def gspec(shape):
        return pl.BlockSpec((None,) + shape, lambda i: (i, 0, 0))

    def tspec(lanes):
        return pl.BlockSpec((S5_CH, lanes), lambda i: (i, 0))

    return pl.pallas_call(
        functools.partial(_s5_kernel, nb_p=nb_p, nb_s=nb_s),
        grid=(g,),
        in_specs=[tspec(lanes_p), tspec(lanes_s), gspec((nb_s, 4 * p)),
                  gspec((1, 2 * p)), gspec((1, 2 * p)), gspec((1, 2 * p)),
                  gspec((S5_CH, 2 * p)), gspec((S5_CH, 2 * p)), gspec((S5_CH, 2 * p)), gspec((S5_CH, 2 * p))],
        out_specs=[tspec(lanes_p), tspec(lanes_s), gspec((nb_p, 4 * p))],
        out_shape=[jax.ShapeDtypeStruct((d, lanes_p), F32),
                   jax.ShapeDtypeStruct((d, lanes_s), F32),
                   jax.ShapeDtypeStruct((g, nb_p, 4 * p), F32)],
        scratch_shapes=[pltpu.VMEM((kc, kc), BF16), pltpu.VMEM((kc, 4 * p), BF16), pltpu.VMEM((kc, 4 * p), BF16),
                        pltpu.VMEM((6, max(rows_p, rows_s), 2 * p), F32)],
        compiler_params=_cparams("parallel"),
        name="s5",
    )(utp, uts, h0, a_re, a_im, log_dt, bt_re, bt_im, c_re, c_im)


def _glu_out_t_kernel(*refs, n_x, n_cond):
    x_refs = refs[:n_x]
    yt_ref, ut_ref, zt_ref, d_ref, bglu_ref, gate_ref, fg_ref, wglut_ref, wout_ref, o_ref = refs[n_x:]
    y = yt_ref[...] + d_ref[...] * ut_ref[...].astype(F32)
    y = jax.nn.gelu(y)
    y = y * jax.nn.sigmoid(_dot(wglut_ref[...], y.astype(BF16)) + bglu_ref[...])
    a = (y * _silu(zt_ref[...].astype(F32))).T.astype(BF16)
    x = _cat_rows(x_refs)
    n_rows, d = x.shape
    rx = n_rows // n_x
    xn = x + _cond_rows(gate_ref, n_rows, rx, n_cond) * _dot(a, wout_ref[...])
    res = xn * lax.rsqrt(jnp.mean(xn * xn, axis=-1, keepdims=True) + EPS) * fg_ref[...]
    for xi in range(n_x):
        o_ref[:, xi * d:(xi + 1) * d] = res[xi * rx:(xi + 1) * rx, :]


def _glu_out_t(yt, ut, zt, x2d, d_skip, b_glu, mod3, cond0, n_cond, final_g, wglut, w_out):
    nc, sd = x2d.shape
    d = sd // S5_CHUNK
    n_x, tps, rx = _tile_geometry(nc)
    assert n_cond == 1 or rx == nc
    tile = pl.BlockSpec((d, LANE_TILE), lambda l: (0, l))
    col = pl.BlockSpec((d, 1), lambda l: (0, 0))
    wspec = pl.BlockSpec((d, d), lambda l: (0, 0), pipeline_mode=pl.Buffered(1))
    return pl.pallas_call(
        functools.partial(_glu_out_t_kernel, n_x=n_x, n_cond=n_cond),
        grid=(S5_CHUNK * nc // LANE_TILE,),
        in_specs=_x_tile_specs(n_x, tps, rx, d) + [
            tile, tile, tile, col, col,
            pl.BlockSpec((n_cond, 1, d), lambda l: (cond0 // n_cond, 0, 2)),
            pl.BlockSpec((1, d), lambda l: (0, 0)), wspec, wspec],
        out_specs=pl.BlockSpec((rx, n_x * d), lambda l: (l % tps, l // tps)),
        out_shape=jax.ShapeDtypeStruct((nc, sd), F32),
        compiler_params=_cparams("parallel"),
        name="glu_out_t",
    )(*([x2d] * n_x), yt, ut, zt, d_skip.reshape(d, 1), b_glu.reshape(d, 1), mod3, final_g.reshape(1, d),
      wglut, w_out)


def _both_dirs(a):
    _, g, r, p = a.shape
    return a.transpose(1, 2, 0, 3).reshape(g, r, 2 * p)


def kernel(x_prompt, x_sample, cache_l0_k, cache_l0_v, state_l1_s5, c, c_ctx, l0_norm_g, l0_w_ada, l0_b_ada, l0_w_in, l0_rpb, l0_w_out, l1_norm_g, l1_w_ada, l1_b_ada, l1_w_in, l1_a_re, l1_a_im, l1_log_dt, l1_b_re, l1_b_im, l1_c_re, l1_c_im, l1_d, l1_w_glu, l1_b_glu, l1_w_out, final_norm_g):
    bp, seq, d = x_prompt.shape
    bs, n_tok, _ = x_sample.shape
    lc = cache_l0_k.shape[1]
    g = d // S5_CH
    p = S5_P
    xp = x_prompt.reshape(bp * seq, d)
    xs = x_sample.reshape(bs * n_tok, d)

    ctx = bs
    cond8 = jnp.zeros((8, d), F32).at[0:bs].set(c).at[ctx].set(c_ctx)
    mod0 = _ada(cond8, l0_w_ada, l0_b_ada).reshape(8, 1, 3 * d)
    mod1 = _ada(cond8, l1_w_ada, l1_b_ada).reshape(8, 1, 3 * d)

    w_in0 = l0_w_in.astype(BF16)
    w_out0 = l0_w_out.astype(BF16)
    qp, kp, vp, zp = _inproj(xp, l0_norm_g, mod0, ctx, bp * seq, w_in0, (BF16, F32, F32, BF16), 512, 512)
    qs, ks, vs, zs = _inproj(xs, l0_norm_g, mod0, 0, n_tok, w_in0, (BF16, BF16, BF16, BF16), 512, 512)
    op = _ctx_attn(qp, kp, vp, seq)
    rpb_pad = jnp.zeros((N_HEADS, 16, LANES), F32).at[:, :RPB_R, :RPB_C].set(l0_rpb)
    os_ = _latent_na(qs, ks, vs, cache_l0_k.reshape(bs, lc, d), cache_l0_v.reshape(bs, lc, d), rpb_pad, n_tok)
    x1p = _gated_out(op, zp, xp, mod0, ctx, bp * seq, w_out0, 512).reshape(-1, S5_CHUNK * d)
    x1s = _gated_out(os_, zs, xs, mod0, 0, n_tok, w_out0, 512).reshape(-1, S5_CHUNK * d)

    wt_in1 = l1_w_in.T.astype(BF16)
    utp, ztp = _inproj_t(x1p, l1_norm_g, mod1, ctx, 1, wt_in1)
    uts, zts = _inproj_t(x1s, l1_norm_g, mod1, 0, bs, wt_in1)
    h0 = state_l1_s5.transpose(3, 0, 2, 1, 4).reshape(g, bs, 4 * p)
    log_dt = jnp.broadcast_to(l1_log_dt[:, :, None, None], (2, g, 1, p))
    ytp, yts, st = _s5(
        utp, uts, h0,
        _both_dirs(l1_a_re[:, :, None, :]), _both_dirs(l1_a_im[:, :, None, :]), _both_dirs(log_dt),
        _both_dirs(l1_b_re.transpose(0, 1, 3, 2)), _both_dirs(l1_b_im.transpose(0, 1, 3, 2)),
        _both_dirs(l1_c_re), _both_dirs(l1_c_im), bp, bs)
    wt_glu = l1_w_glu.T.astype(BF16)
    w_out1 = l1_w_out.astype(BF16)
    y_prompt = _glu_out_t(ytp, utp, ztp, x1p, l1_d, l1_b_glu, mod1, ctx, 1, final_norm_g, wt_glu, w_out1)
    y_sample = _glu_out_t(yts, uts, zts, x1s, l1_d, l1_b_glu, mod1, 0, bs, final_norm_g, wt_glu, w_out1)
    new_state = st.reshape(g, bp, 2, 2, p).transpose(1, 3, 2, 0, 4)
    return (y_prompt.reshape(bp, seq, d), y_sample.reshape(bs, n_tok, d),
            kp.reshape(bp, seq, N_HEADS, HEAD_DIM), vp.reshape(bp, seq, N_HEADS, HEAD_DIM), new_state)
```

```python
import functools

import jax
import jax.numpy as jnp
from jax import lax
from jax.experimental import pallas as pl
from jax.experimental.pallas import tpu as pltpu

F32 = jnp.float32
BF16 = jnp.bfloat16

EPS = 1e-6
N_HEADS = 16
HEAD_DIM = 128
GRID_W = 64
NA_ROWS = 8
NA_COLS = 16
RPB_R = 2 * NA_ROWS - 1
RPB_C = 2 * NA_COLS - 1
S5_CH = 16
S5_P = 64
S5_CHUNK = 16
NEG = -1e30
LANES = 128
VMEM_LIMIT = 56 * 1024 * 1024


def _cparams(*sem):
    return pltpu.CompilerParams(dimension_semantics=sem, vmem_limit_bytes=VMEM_LIMIT)


def _dot(a, b):
    return jnp.dot(a, b, preferred_element_type=F32)


def _dot_nt(a, b, precision=None):
    return lax.dot_general(a, b, (((1,), (1,)), ((), ())), preferred_element_type=F32,
                           precision=precision)


def _silu(x):
    return x * jax.nn.sigmoid(x)


def _ada_kernel(cond_ref, w_ref, b_ref, o_ref):
    s = _silu(cond_ref[...])
    o_ref[...] = _dot(s.astype(BF16), w_ref[...].astype(BF16)) + b_ref[...]


def _ada(cond8, w_ada, b_ada):
    d, n = w_ada.shape
    tn = 1024
    return pl.pallas_call(
        _ada_kernel,
        grid=(n // tn,),
        in_specs=[pl.BlockSpec((8, d), lambda j: (0, 0)),
                  pl.BlockSpec((d, tn), lambda j: (0, j)),
                  pl.BlockSpec((1, tn), lambda j: (0, j))],
        out_specs=pl.BlockSpec((8, tn), lambda j: (0, j)),
        out_shape=jax.ShapeDtypeStruct((8, n), F32),
        compiler_params=_cparams("parallel"),
        name="ada",
    )(cond8, w_ada, b_ada.reshape(1, n))


def _inproj_kernel(x_ref, g_ref, shift_ref, scale_ref, *rest, n_split):
    w_refs = rest[:n_split]
    o_refs = rest[n_split:2 * n_split]
    h_ref = rest[2 * n_split]

    @pl.when(pl.program_id(1) == 0)
    def _():
        x = x_ref[...]
        y = x * lax.rsqrt(jnp.mean(x * x, axis=-1, keepdims=True) + EPS) * g_ref[...]
        h_ref[...] = (y * (1.0 + scale_ref[...]) + shift_ref[...]).astype(BF16)

    h = h_ref[...]
    for w_ref, o_ref in zip(w_refs, o_refs):
        o_ref[...] = _dot(h, w_ref[...]).astype(o_ref.dtype)


def _inproj(x, g, mod3, cond0, rows_per_cond, w, out_dtypes, tm, tn):
    m, d = x.shape
    n_split = len(out_dtypes)
    nj = d // tn

    def cond_of(i):
        return cond0 + (i * tm) // rows_per_cond

    in_specs = [pl.BlockSpec((tm, d), lambda i, j: (i, 0)),
                pl.BlockSpec((1, d), lambda i, j: (0, 0)),
                pl.BlockSpec((None, 1, d), lambda i, j: (cond_of(i), 0, 0)),
                pl.BlockSpec((None, 1, d), lambda i, j: (cond_of(i), 0, 1))]
    for s in range(n_split):
        in_specs.append(pl.BlockSpec((d, tn), lambda i, j, s=s: (0, s * nj + j)))
    return pl.pallas_call(
        functools.partial(_inproj_kernel, n_split=n_split),
        grid=(m // tm, nj),
        in_specs=in_specs,
        out_specs=[pl.BlockSpec((tm, tn), lambda i, j: (i, j)) for _ in range(n_split)],
        out_shape=[jax.ShapeDtypeStruct((m, d), dt) for dt in out_dtypes],
        scratch_shapes=[pltpu.VMEM((tm, d), BF16)],
        compiler_params=_cparams("parallel", "arbitrary"),
        name="inproj",
    )(x, g.reshape(1, d), mod3, mod3, *([w] * n_split))


def _softmax_pv(parts):
    m = parts[0][0].max(axis=-1, keepdims=True)
    for s, _ in parts[1:]:
        m = jnp.maximum(m, s.max(axis=-1, keepdims=True))
    l = None
    o = None
    for s, v in parts:
        e = jnp.exp(s - m)
        li = e.sum(axis=-1, keepdims=True)
        oi = _dot(e.astype(BF16), v)
        l = li if l is None else l + li
        o = oi if o is None else o + oi
    return o / l


def _ctx_attn_kernel(q_ref, k_ref, v_ref, o_ref):
    scale = HEAD_DIM ** -0.5
    for h in range(N_HEADS):
        sl = slice(h * HEAD_DIM, (h + 1) * HEAD_DIM)
        q = q_ref[:, sl]
        k = k_ref[:, sl].astype(BF16)
        v = v_ref[:, sl].astype(BF16)
        s = _dot_nt(q, k) * scale
        o_ref[:, sl] = _softmax_pv([(s, v)]).astype(o_ref.dtype)


def _ctx_attn(q, k, v, seq):
    m, d = q.shape
    spec = pl.BlockSpec((seq, d), lambda b: (b, 0))
    return pl.pallas_call(
        _ctx_attn_kernel,
        grid=(m // seq,),
        in_specs=[spec, spec, spec],
        out_specs=spec,
        out_shape=jax.ShapeDtypeStruct((m, d), BF16),
        compiler_params=_cparams("parallel"),
        name="ctx_attn",
    )(q, k, v)


def _na_window_start(r, rows):
    return min(max(r - NA_ROWS // 2, 0), rows - NA_ROWS)


def _na_kernel(q_ref, k_ref, v_ref, kc_ref, vc_ref, rpb_ref, o_ref, *, rows, q_rows):
    scale = HEAD_DIM ** -0.5
    w = GRID_W
    qc = lax.broadcasted_iota(jnp.int32, (w, LANES), 0)
    lane = lax.broadcasted_iota(jnp.int32, (w, LANES), 1)
    c0 = jnp.clip(qc - NA_COLS // 2, 0, w - NA_COLS)
    ok_l = (lane >= c0) & (lane < c0 + NA_COLS)
    ok_r = (lane - w >= c0) & (lane - w < c0 + NA_COLS)
    neg_tile = jnp.full((w, LANES), NEG, F32)
    tile_l, tile_r = [], []
    for d in range(RPB_R):
        base = jnp.broadcast_to(rpb_ref[d:d + 1, :], (w, LANES))
        left = pltpu.roll(base, LANES - (NA_COLS - 1), 1, stride=1, stride_axis=0)
        right = pltpu.roll(base, w - (NA_COLS - 1), 1, stride=1, stride_axis=0)
        tile_l.append(jnp.where(ok_l, left, NEG))
        tile_r.append(jnp.where(ok_r, right, NEG))

    kc = kc_ref[...].astype(BF16)
    vc = vc_ref[...].astype(BF16)
    n_groups = rows // q_rows
    for gi in range(n_groups):
        rs = list(range(gi * q_rows, (gi + 1) * q_rows))
        klo = min(_na_window_start(r, rows) for r in rs) // 2 * 2
        khi = -(-(max(_na_window_start(r, rows) for r in rs) + NA_ROWS) // 2) * 2
        bias_rows = []
        for r in rs:
            r0 = _na_window_start(r, rows)
            tiles = []
            for kr in range(klo, khi, 2):
                ok0 = r0 <= kr < r0 + NA_ROWS
                ok1 = r0 <= kr + 1 < r0 + NA_ROWS
                t0 = tile_l[kr - r + NA_ROWS - 1] if ok0 else neg_tile
                t1 = tile_r[kr + 1 - r + NA_ROWS - 1] if ok1 else neg_tile
                tiles.append(jnp.maximum(t0, t1) if (ok0 or ok1) else neg_tile)
            bias_rows.append(jnp.concatenate(tiles, axis=1))
        bias = jnp.concatenate(bias_rows, axis=0)
        q = q_ref[gi * q_rows * w:(gi + 1) * q_rows * w, :]
        kl = k_ref[klo * w:khi * w, :]
        vl = v_ref[klo * w:khi * w, :]
        s_loc = _dot_nt(q, kl) * scale + bias
        s_ctx = _dot_nt(q, kc) * scale
        o = _softmax_pv([(s_loc, vl), (s_ctx, vc)])
        o_ref[gi * q_rows * w:(gi + 1) * q_rows * w, :] = o.astype(o_ref.dtype)


def _latent_na(q, k, v, k_ctx, v_ctx, rpb_pad, n_tok):
    m, d = q.shape
    nb = m // n_tok
    lc = k_ctx.shape[1]
    rows = n_tok // GRID_W
    spec = pl.BlockSpec((n_tok, HEAD_DIM), lambda b, h: (b, h))
    cspec = pl.BlockSpec((None, lc, HEAD_DIM), lambda b, h: (b, 0, h))
    return pl.pallas_call(
        functools.partial(_na_kernel, rows=rows, q_rows=4),
        grid=(nb, N_HEADS),
        in_specs=[spec, spec, spec, cspec, cspec,
                  pl.BlockSpec((None, 16, LANES), lambda b, h: (h, 0, 0))],
        out_specs=spec,
        out_shape=jax.ShapeDtypeStruct((m, d), BF16),
        compiler_params=_cparams("parallel", "parallel"),
        name="latent_na",
    )(q, k, v, k_ctx, v_ctx, rpb_pad)


def _gated_out_kernel(o_ref, z_ref, x_ref, gate_ref, w_ref, y_ref):
    a = o_ref[...].astype(F32) * _silu(z_ref[...].astype(F32))
    y_ref[...] = x_ref[...] + gate_ref[...] * _dot(a.astype(BF16), w_ref[...])


def _gated_out(o, z, x, mod3, cond0, rows_per_cond, w_out, tm):
    m, d = x.shape
    row = pl.BlockSpec((tm, d), lambda i: (i, 0))
    return pl.pallas_call(
        _gated_out_kernel,
        grid=(m // tm,),
        in_specs=[row, row, row,
                  pl.BlockSpec((None, 1, d), lambda i: (cond0 + (i * tm) // rows_per_cond, 0, 2)),
                  pl.BlockSpec((d, d), lambda i: (0, 0), pipeline_mode=pl.Buffered(1))],
        out_specs=row,
        out_shape=jax.ShapeDtypeStruct((m, d), F32),
        compiler_params=_cparams("parallel"),
        name="gated_out",
    )(o, z, x, mod3, w_out)


LANE_TILE = 512


def _tile_geometry(nc):
    n_x = max(1, LANE_TILE // nc)
    return n_x, max(1, nc // LANE_TILE), LANE_TILE // n_x


def _x_tile_specs(n_x, tps, rx, d):
    return [pl.BlockSpec((rx, d), lambda l, xi=xi: (l % tps, (l // tps) * n_x + xi)) for xi in range(n_x)]


def _cat_rows(refs):
    return refs[0][...] if len(refs) == 1 else jnp.concatenate([r[...] for r in refs], axis=0)


def _cond_rows(ref, n_rows, rx, n_cond):
    out = ref[n_cond - 1]
    if n_cond > 1:
        b = (lax.broadcasted_iota(jnp.int32, (n_rows, 1), 0) % rx) // (rx // n_cond)
        for i in range(n_cond - 2, -1, -1):
            out = jnp.where(b == i, ref[i], out)
    return out


def _inproj_t_kernel(*refs, n_x, n_cond):
    x_refs = refs[:n_x]
    g_ref, shift_ref, scale_ref, wtu_ref, wz_ref, u_ref, z_ref = refs[n_x:]
    x = _cat_rows(x_refs)
    n_rows, d = x.shape
    y = x * lax.rsqrt(jnp.mean(x * x, axis=-1, keepdims=True) + EPS) * g_ref[...]
    rx = n_rows // n_x
    h = (y * (1.0 + _cond_rows(scale_ref, n_rows, rx, n_cond))
         + _cond_rows(shift_ref, n_rows, rx, n_cond)).astype(BF16)
    u_ref[...] = _dot_nt(wtu_ref[...], h).astype(u_ref.dtype)
    z_ref[...] = _dot(h, wz_ref[...]).astype(z_ref.dtype)


def _inproj_t(x2d, g, mod3, cond0, n_cond, wt_u, w_z):
    nc, sd = x2d.shape
    d = sd // S5_CHUNK
    n_x, tps, rx = _tile_geometry(nc)
    assert n_cond == 1 or rx == nc

    def mspec(col):
        return pl.BlockSpec((n_cond, 1, d), lambda l: (cond0 // n_cond, 0, col))

    wspec = pl.BlockSpec((d, d), lambda l: (0, 0), pipeline_mode=pl.Buffered(1))
    return pl.pallas_call(
        functools.partial(_inproj_t_kernel, n_x=n_x, n_cond=n_cond),
        grid=(S5_CHUNK * nc // LANE_TILE,),
        in_specs=_x_tile_specs(n_x, tps, rx, d) + [
            pl.BlockSpec((1, d), lambda l: (0, 0)), mspec(0), mspec(1), wspec, wspec],
        out_specs=[pl.BlockSpec((d, LANE_TILE), lambda l: (0, l)), pl.BlockSpec((LANE_TILE, d), lambda l: (l, 0))],
        out_shape=[jax.ShapeDtypeStruct((d, S5_CHUNK * nc), BF16),
                   jax.ShapeDtypeStruct((S5_CHUNK * nc, d), BF16)],
        compiler_params=_cparams("parallel"),
        name="inproj_t",
    )(*([x2d] * n_x), g.reshape(1, d), mod3, mod3, wt_u, w_z)


def _cmul(ar, ai, br, bi):
    return ar * br - ai * bi, ar * bi + ai * br


def _s5_powers(ar, ai):
    width = ar.shape[-1]
    pw = [(jnp.ones_like(ar), jnp.zeros_like(ar))]
    for _ in range(S5_CHUNK):
        pw.append(_cmul(pw[-1][0], pw[-1][1], ar, ai))
    is_fwd = lax.broadcasted_iota(jnp.int32, (1, width), 1) < width // 2

    def pattern(fwd_ascending):
        out = []
        for part in range(2):
            blocks = []
            for j in range(S5_CHUNK):
                ef, eb = (j, S5_CHUNK - 1 - j) if fwd_ascending else (S5_CHUNK - 1 - j, j)
                blocks.append(jnp.broadcast_to(jnp.where(is_fwd, pw[ef][part], pw[eb][part]), (S5_CH, width)))
            out.append(jnp.concatenate(blocks, axis=0))
        return out

    return pattern(True), pattern(False), pw[S5_CHUNK]


def _s5_kernel(xp_ref, xs_ref, h0_ref, are_ref, aim_ref, ldt_ref, bre_ref, bim_ref, cre_ref, cim_ref, dsk_ref,
               yp_ref, ys_ref, st_ref, w_ref, ws_ref, wc_ref, s_ref, f_ref, *, nb_p, nb_s, n_seg_s):
    p = S5_P
    kc = S5_CHUNK * S5_CH
    hi = lax.Precision.HIGHEST
    a_re = are_ref[...]
    a_im = aim_ref[...]
    dt = jnp.exp(ldt_ref[...])
    mag = jnp.exp(a_re * dt)
    ab_re = mag * jnp.cos(a_im * dt)
    ab_im = mag * jnp.sin(a_im * dt)
    den = a_re * a_re + a_im * a_im
    nr = ab_re - 1.0
    f_re = (nr * a_re + ab_im * a_im) / den
    f_im = (ab_im * a_re - nr * a_im) / den
    bbt_re, bbt_im = _cmul(f_re, f_im, bre_ref[...], bim_ref[...])

    (pg_r, pg_i), (ps_r, ps_i), (ac_r, ac_i) = _s5_powers(ab_re, ab_im)
    c_re = jnp.concatenate([cre_ref[...]] * S5_CHUNK, axis=0)
    c_im = jnp.concatenate([cim_ref[...]] * S5_CHUNK, axis=0)
    bt_re = jnp.concatenate([bbt_re] * S5_CHUNK, axis=0)
    bt_im = jnp.concatenate([bbt_im] * S5_CHUNK, axis=0)

    g_re, g_im = _cmul(pg_r, pg_i, c_re, c_im)
    fwd16 = lax.broadcasted_iota(jnp.int32, (S5_CH, 2 * p), 1) < p
    r0 = []
    for d in range(2):
        msk = fwd16 if d == 0 else jnp.logical_not(fwd16)
        r0.append(_dot_nt(jnp.where(msk, bbt_re, 0.0), g_re, hi)
                  - _dot_nt(jnp.where(msk, bbt_im, 0.0), g_im, hi))
    lane = lax.broadcasted_iota(jnp.int32, (S5_CH, kc), 1)
    for s in range(S5_CHUNK):
        f = jnp.where(lane >= S5_CH * s, pltpu.roll(r0[0], S5_CH * s, 1), 0.0) if s else r0[0]
        b = jnp.where(lane < S5_CH * (s + 1), pltpu.roll(r0[1], (S5_CH * (s + 1)) % kc, 1), 0.0)
        w_ref[s * S5_CH:(s + 1) * S5_CH, :] = (f + b).astype(BF16)
    e_re, e_im = _cmul(ps_r, ps_i, bt_re, bt_im)
    ws_ref[:, 0:2 * p] = e_re.astype(BF16)
    ws_ref[:, 2 * p:4 * p] = e_im.astype(BF16)
    g1_re, g1_im = _cmul(g_re, g_im, ab_re, ab_im)
    wc_ref[:, 0:2 * p] = g1_re.astype(BF16)
    wc_ref[:, 2 * p:4 * p] = (-g1_im).astype(BF16)

    def run(xt_ref, yt_ref, nb, n_seg, hr, hm):
        rows = xt_ref.shape[1] // S5_CHUNK
        nbx = nb * n_seg
        m = rows // nbx
        xt = jnp.concatenate([xt_ref[:, s * rows:(s + 1) * rows] for s in range(S5_CHUNK)], axis=0)
        xf = xt.astype(F32).T
        x = xf.astype(BF16)
        y = _dot(x, w_ref[...]) + dsk_ref[...] * xf
        s_all = _dot(x, ws_ref[...])
        s_ref[0, 0:rows, :] = s_all[:, 0:2 * p]
        s_ref[1, 0:rows, :] = s_all[:, 2 * p:4 * p]

        def step(ar, ai, hr, hm, plane_r, plane_i, rf, rb):
            isf = lax.broadcasted_iota(jnp.int32, hr.shape, 1) < p
            sr = jnp.where(isf, plane_r[rf, :], plane_r[rb, :])
            sm = jnp.where(isf, plane_i[rf, :], plane_i[rb, :])
            return ar * hr - ai * hm + sr, ar * hm + ai * hr + sm

        def scan(hr, hm, store):
            arb = jnp.broadcast_to(ac_r, (nbx, 2 * p))
            aib = jnp.broadcast_to(ac_i, (nbx, 2 * p))
            for k in range(m):
                rf = pl.ds(k, nbx, stride=m)
                rb = pl.ds(m - 1 - k, nbx, stride=m)
                if store:
                    s_ref[2, rf, :] = hr
                    s_ref[3, rf, :] = hm
                    s_ref[4, rb, :] = hr
                    s_ref[5, rb, :] = hm
                hr, hm = step(arb, aib, hr, hm, s_ref.at[0], s_ref.at[1], rf, rb)
            return hr, hm

        if n_seg > 1:
            zeros = jnp.zeros((nbx, 2 * p), F32)
            f_ref[0], f_ref[1] = scan(zeros, zeros, False)
            sr_, si_ = ac_r, ac_i
            for _ in range(m - 1):
                sr_, si_ = _cmul(sr_, si_, ac_r, ac_i)
            sr_ = jnp.broadcast_to(sr_, (nb, 2 * p))
            si_ = jnp.broadcast_to(si_, (nb, 2 * p))
            for j in range(n_seg):
                rf = pl.ds(j, nb, stride=n_seg)
                rb = pl.ds(n_seg - 1 - j, nb, stride=n_seg)
                f_ref[2, rf, :] = hr
                f_ref[3, rf, :] = hm
                f_ref[4, rb, :] = hr
                f_ref[5, rb, :] = hm
                hr, hm = step(sr_, si_, hr, hm, f_ref.at[0], f_ref.at[1], rf, rb)
            isf = lax.broadcasted_iota(jnp.int32, (nbx, 2 * p), 1) < p
            hr = jnp.where(isf, f_ref[2], f_ref[4])
            hm = jnp.where(isf, f_ref[3], f_ref[5])
        hr, hm = scan(hr, hm, True)
        isf_rows = lax.broadcasted_iota(jnp.int32, (rows, 2 * p), 1) < p
        h_prev = jnp.concatenate([jnp.where(isf_rows, s_ref[2, 0:rows, :], s_ref[4, 0:rows, :]),
                                  jnp.where(isf_rows, s_ref[3, 0:rows, :], s_ref[5, 0:rows, :])], axis=1)
        yt = (y + _dot_nt(h_prev.astype(BF16), wc_ref[...])).T
        for t in range(S5_CHUNK):
            yt_ref[:, t * rows:(t + 1) * rows] = yt[t * S5_CH:(t + 1) * S5_CH, :]
        return hr, hm

    zeros = jnp.zeros((nb_p, 2 * p), F32)
    hr, hm = run(xp_ref, yp_ref, nb_p, 1, zeros, zeros)
    st_ref[:, 0:2 * p] = hr
    st_ref[:, 2 * p:4 * p] = hm
    run(xs_ref, ys_ref, nb_s, n_seg_s, h0_ref[:, 0:2 * p], h0_ref[:, 2 * p:4 * p])


def _s5(utp, uts, h0, a_re, a_im, log_dt, bt_re, bt_im, c_re, c_im, d_skip, nb_p, nb_s):
    d, lanes_p = utp.shape
    lanes_s = uts.shape[1]
    g = d // S5_CH
    rows_p = lanes_p // S5_CHUNK
    rows_s = lanes_s // S5_CHUNK
    n_seg_s = max(1, rows_s // nb_s // 8)
    kc = S5_CHUNK * S5_CH
    p = S5_P

    def gspec(shape):
        return pl.BlockSpec((None,) + shape, lambda i: (i, 0, 0))

    def tspec(lanes):
        return pl.BlockSpec((S5_CH, lanes), lambda i: (i, 0))

    return pl.pallas_call(
        functools.partial(_s5_kernel, nb_p=nb_p, nb_s=nb_s, n_seg_s=n_seg_s),
        grid=(g,),
        in_specs=[tspec(lanes_p), tspec(lanes_s), gspec((nb_s, 4 * p)),
                  gspec((1, 2 * p)), gspec((1, 2 * p)), gspec((1, 2 * p)),
                  gspec((S5_CH, 2 * p)), gspec((S5_CH, 2 * p)), gspec((S5_CH, 2 * p)), gspec((S5_CH, 2 * p)),
                  gspec((1, kc))],
        out_specs=[tspec(lanes_p), tspec(lanes_s), gspec((nb_p, 4 * p))],
        out_shape=[jax.ShapeDtypeStruct((d, lanes_p), F32),
                   jax.ShapeDtypeStruct((d, lanes_s), F32),
                   jax.ShapeDtypeStruct((g, nb_p, 4 * p), F32)],
        scratch_shapes=[pltpu.VMEM((kc, kc), BF16), pltpu.VMEM((kc, 4 * p), BF16), pltpu.VMEM((kc, 4 * p), BF16),
                        pltpu.VMEM((6, max(rows_p, rows_s), 2 * p), F32),
                        pltpu.VMEM((6, nb_s * n_seg_s, 2 * p), F32)],
        compiler_params=_cparams("parallel"),
        name="s5",
    )(utp, uts, h0, a_re, a_im, log_dt, bt_re, bt_im, c_re, c_im,
      jnp.tile(d_skip.reshape(g, 1, S5_CH), (1, 1, S5_CHUNK)))


def _glu_kernel(yt_ref, z_ref, w_ref, b_ref, a_ref, yf_ref, yb_ref):
    nj, _, tn = yf_ref.shape
    j = pl.program_id(1)

    @pl.when(j == 0)
    def _():
        y = jax.nn.gelu(yt_ref[...].T)
        yb_ref[...] = y.astype(BF16)
        for jj in range(nj):
            yf_ref[jj] = y[:, jj * tn:(jj + 1) * tn]

    gl = _dot(yb_ref[...], w_ref[...]) + b_ref[...]
    a_ref[...] = (yf_ref[j] * jax.nn.sigmoid(gl) * _silu(z_ref[...].astype(F32))).astype(a_ref.dtype)


def _glu(yt, z, w_glu, b_glu, tn):
    d, m = yt.shape
    tm = LANE_TILE
    nj = d // tn
    return pl.pallas_call(
        _glu_kernel,
        grid=(m // tm, nj),
        in_specs=[pl.BlockSpec((d, tm), lambda i, j: (0, i)),
                  pl.BlockSpec((tm, tn), lambda i, j: (i, j)),
                  pl.BlockSpec((d, tn), lambda i, j: (0, j)),
                  pl.BlockSpec((1, tn), lambda i, j: (0, j))],
        out_specs=pl.BlockSpec((tm, tn), lambda i, j: (i, j)),
        out_shape=jax.ShapeDtypeStruct((m, d), BF16),
        scratch_shapes=[pltpu.VMEM((nj, tm, tn), F32), pltpu.VMEM((tm, d), BF16)],
        compiler_params=_cparams("parallel", "arbitrary"),
        name="glu",
    )(yt, z, w_glu, b_glu.reshape(1, d))


def _out_norm_kernel(*refs, n_x, n_cond):
    x_refs = refs[:n_x]
    a_ref, gate_ref, fg_ref, w_ref, o_ref = refs[n_x:]
    x = _cat_rows(x_refs)
    n_rows, d = x.shape
    rx = n_rows // n_x
    xn = x + _cond_rows(gate_ref, n_rows, rx, n_cond) * _dot(a_ref[...], w_ref[...])
    res = xn * lax.rsqrt(jnp.mean(xn * xn, axis=-1, keepdims=True) + EPS) * fg_ref[...]
    for xi in range(n_x):
        o_ref[:, xi * d:(xi + 1) * d] = res[xi * rx:(xi + 1) * rx, :]


def _out_norm(a, x2d, mod3, cond0, n_cond, final_g, w_out):
    nc, sd = x2d.shape
    d = sd // S5_CHUNK
    n_x, tps, rx = _tile_geometry(nc)
    assert n_cond == 1 or rx == nc
    return pl.pallas_call(
        functools.partial(_out_norm_kernel, n_x=n_x, n_cond=n_cond),
        grid=(S5_CHUNK * nc // LANE_TILE,),
        in_specs=_x_tile_specs(n_x, tps, rx, d) + [
            pl.BlockSpec((LANE_TILE, d), lambda l: (l, 0)),
            pl.BlockSpec((n_cond, 1, d), lambda l: (cond0 // n_cond, 0, 2)),
            pl.BlockSpec((1, d), lambda l: (0, 0)),
            pl.BlockSpec((d, d), lambda l: (0, 0), pipeline_mode=pl.Buffered(1))],
        out_specs=pl.BlockSpec((rx, n_x * d), lambda l: (l % tps, l // tps)),
        out_shape=jax.ShapeDtypeStruct((nc, sd), F32),
        compiler_params=_cparams("parallel"),
        name="out_norm",
    )(*([x2d] * n_x), a, mod3, final_g.reshape(1, d), w_out)


def _both_dirs(a):
    _, g, r, p = a.shape
    return a.transpose(1, 2, 0, 3).reshape(g, r, 2 * p)


def kernel(x_prompt, x_sample, cache_l0_k, cache_l0_v, state_l1_s5, c, c_ctx, l0_norm_g, l0_w_ada, l0_b_ada, l0_w_in, l0_rpb, l0_w_out, l1_norm_g, l1_w_ada, l1_b_ada, l1_w_in, l1_a_re, l1_a_im, l1_log_dt, l1_b_re, l1_b_im, l1_c_re, l1_c_im, l1_d, l1_w_glu, l1_b_glu, l1_w_out, final_norm_g):
    bp, seq, d = x_prompt.shape
    bs, n_tok, _ = x_sample.shape
    lc = cache_l0_k.shape[1]
    g = d // S5_CH
    p = S5_P
    xp = x_prompt.reshape(bp * seq, d)
    xs = x_sample.reshape(bs * n_tok, d)

    ctx = bs
    cond8 = jnp.zeros((8, d), F32).at[0:bs].set(c).at[ctx].set(c_ctx)
    mod0 = _ada(cond8, l0_w_ada, l0_b_ada).reshape(8, 1, 3 * d)
    mod1 = _ada(cond8, l1_w_ada, l1_b_ada).reshape(8, 1, 3 * d)

    w_in0 = l0_w_in.astype(BF16)
    w_out0 = l0_w_out.astype(BF16)
    qp, kp, vp, zp = _inproj(xp, l0_norm_g, mod0, ctx, bp * seq, w_in0, (BF16, F32, F32, BF16), 512, 512)
    qs, ks, vs, zs = _inproj(xs, l0_norm_g, mod0, 0, n_tok, w_in0, (BF16, BF16, BF16, BF16), 512, 512)
    op = _ctx_attn(qp, kp, vp, seq)
    rpb_pad = jnp.zeros((N_HEADS, 16, LANES), F32).at[:, :RPB_R, :RPB_C].set(l0_rpb)
    os_ = _latent_na(qs, ks, vs, cache_l0_k.reshape(bs, lc, d), cache_l0_v.reshape(bs, lc, d), rpb_pad, n_tok)
    x1p = _gated_out(op, zp, xp, mod0, ctx, bp * seq, w_out0, 512).reshape(-1, S5_CHUNK * d)
    x1s = _gated_out(os_, zs, xs, mod0, 0, n_tok, w_out0, 512).reshape(-1, S5_CHUNK * d)

    wt_u = l1_w_in[:, :d].T.astype(BF16)
    w_z = l1_w_in[:, d:].astype(BF16)
    utp, zp1 = _inproj_t(x1p, l1_norm_g, mod1, ctx, 1, wt_u, w_z)
    uts, zs1 = _inproj_t(x1s, l1_norm_g, mod1, 0, bs, wt_u, w_z)
    h0 = state_l1_s5.transpose(3, 0, 2, 1, 4).reshape(g, bs, 4 * p)
    log_dt = jnp.broadcast_to(l1_log_dt[:, :, None, None], (2, g, 1, p))
    ytp, yts, st = _s5(
        utp, uts, h0,
        _both_dirs(l1_a_re[:, :, None, :]), _both_dirs(l1_a_im[:, :, None, :]), _both_dirs(log_dt),
        _both_dirs(l1_b_re.transpose(0, 1, 3, 2)), _both_dirs(l1_b_im.transpose(0, 1, 3, 2)),
        _both_dirs(l1_c_re), _both_dirs(l1_c_im), l1_d, bp, bs)
    w_glu = l1_w_glu.astype(BF16)
    w_out1 = l1_w_out.astype(BF16)
    y_prompt = _out_norm(_glu(ytp, zp1, w_glu, l1_b_glu, 512), x1p, mod1, ctx, 1, final_norm_g, w_out1)
    y_sample = _out_norm(_glu(yts, zs1, w_glu, l1_b_glu, 512), x1s, mod1, 0, bs, final_norm_g, w_out1)
    new_state = st.reshape(g, bp, 2, 2, p).transpose(1, 3, 2, 0, 4)
    return (y_prompt.reshape(bp, seq, d), y_sample.reshape(bs, n_tok, d),
            kp.reshape(bp, seq, N_HEADS, HEAD_DIM), vp.reshape(bp, seq, N_HEADS, HEAD_DIM), new_state)
```

```python
import functools

import jax
import jax.numpy as jnp
from jax import lax
from jax.experimental import pallas as pl
from jax.experimental.pallas import tpu as pltpu

F32 = jnp.float32
BF16 = jnp.bfloat16

EPS = 1e-6
N_HEADS = 16
HEAD_DIM = 128
GRID_W = 64
NA_ROWS = 8
NA_COLS = 16
RPB_R = 2 * NA_ROWS - 1
RPB_C = 2 * NA_COLS - 1
S5_CH = 16
S5_P = 64
S5_CHUNK = 16
NEG = -1e30
LANES = 128
VMEM_LIMIT = 56 * 1024 * 1024


def _cparams(*sem):
    return pltpu.CompilerParams(dimension_semantics=sem, vmem_limit_bytes=VMEM_LIMIT)


def _dot(a, b):
    return jnp.dot(a, b, preferred_element_type=F32)


def _dot_nt(a, b, precision=None):
    return lax.dot_general(a, b, (((1,), (1,)), ((), ())), preferred_element_type=F32,
                           precision=precision)


def _silu(x):
    return x * jax.nn.sigmoid(x)


def _ada_kernel(cond_ref, w_ref, b_ref, o_ref):
    s = _silu(cond_ref[...])
    o_ref[...] = _dot(s.astype(BF16), w_ref[...].astype(BF16)) + b_ref[...]


def _ada(cond8, w_ada, b_ada):
    d, n = w_ada.shape
    tn = 1024
    return pl.pallas_call(
        _ada_kernel,
        grid=(n // tn,),
        in_specs=[pl.BlockSpec((8, d), lambda j: (0, 0)),
                  pl.BlockSpec((d, tn), lambda j: (0, j)),
                  pl.BlockSpec((1, tn), lambda j: (0, j))],
        out_specs=pl.BlockSpec((8, tn), lambda j: (0, j)),
        out_shape=jax.ShapeDtypeStruct((8, n), F32),
        compiler_params=_cparams("parallel"),
        name="ada",
    )(cond8, w_ada, b_ada.reshape(1, n))


def _inproj_kernel(x_ref, g_ref, shift_ref, scale_ref, *rest, out_split):
    n_split = max(out_split) + 1
    w_refs = rest[:n_split]
    o_refs = rest[n_split:-1]
    h_ref = rest[-1]

    @pl.when(pl.program_id(1) == 0)
    def _():
        x = x_ref[...]
        y = x * lax.rsqrt(jnp.mean(x * x, axis=-1, keepdims=True) + EPS) * g_ref[...]
        h_ref[...] = (y * (1.0 + scale_ref[...]) + shift_ref[...]).astype(BF16)

    h = h_ref[...]
    for s, w_ref in enumerate(w_refs):
        r = _dot(h, w_ref[...])
        for o_ref, o_s in zip(o_refs, out_split):
            if o_s == s:
                o_ref[...] = r.astype(o_ref.dtype)


def _inproj(x, g, mod3, cond0, rows_per_cond, w, outs, tm, tn):
    m, d = x.shape
    out_split = tuple(s for s, _ in outs)
    out_dtypes = [dt for _, dt in outs]
    n_split = max(out_split) + 1
    nj = d // tn

    def cond_of(i):
        return cond0 + (i * tm) // rows_per_cond

    in_specs = [pl.BlockSpec((tm, d), lambda i, j: (i, 0)),
                pl.BlockSpec((1, d), lambda i, j: (0, 0)),
                pl.BlockSpec((None, 1, d), lambda i, j: (cond_of(i), 0, 0)),
                pl.BlockSpec((None, 1, d), lambda i, j: (cond_of(i), 0, 1))]
    for s in range(n_split):
        in_specs.append(pl.BlockSpec((d, tn), lambda i, j, s=s: (0, s * nj + j)))
    return pl.pallas_call(
        functools.partial(_inproj_kernel, out_split=out_split),
        grid=(m // tm, nj),
        in_specs=in_specs,
        out_specs=[pl.BlockSpec((tm, tn), lambda i, j: (i, j)) for _ in outs],
        out_shape=[jax.ShapeDtypeStruct((m, d), dt) for dt in out_dtypes],
        scratch_shapes=[pltpu.VMEM((tm, d), BF16)],
        compiler_params=_cparams("parallel", "arbitrary"),
        name="inproj",
    )(x, g.reshape(1, d), mod3, mod3, *([w] * n_split))


def _softmax_pv(parts):
    m = parts[0][0].max(axis=-1, keepdims=True)
    for s, _ in parts[1:]:
        m = jnp.maximum(m, s.max(axis=-1, keepdims=True))
    l = None
    o = None
    for s, v in parts:
        e = jnp.exp(s - m)
        li = e.sum(axis=-1, keepdims=True)
        oi = _dot(e.astype(BF16), v)
        l = li if l is None else l + li
        o = oi if o is None else o + oi
    return o / l


def _ctx_attn_kernel(q_ref, k_ref, v_ref, o_ref):
    scale = HEAD_DIM ** -0.5
    for h in range(N_HEADS):
        sl = slice(h * HEAD_DIM, (h + 1) * HEAD_DIM)
        s = _dot_nt(q_ref[:, sl], k_ref[:, sl]) * scale
        o_ref[:, sl] = _softmax_pv([(s, v_ref[:, sl])]).astype(o_ref.dtype)


def _ctx_attn(q, k, v, seq):
    m, d = q.shape
    spec = pl.BlockSpec((seq, d), lambda b: (b, 0))
    return pl.pallas_call(
        _ctx_attn_kernel,
        grid=(m // seq,),
        in_specs=[spec, spec, spec],
        out_specs=spec,
        out_shape=jax.ShapeDtypeStruct((m, d), BF16),
        compiler_params=_cparams("parallel"),
        name="ctx_attn",
    )(q, k, v)


def _na_window_start(r, rows):
    return min(max(r - NA_ROWS // 2, 0), rows - NA_ROWS)


def _na_kernel(q_ref, k_ref, v_ref, kc_ref, vc_ref, rpb_ref, o_ref, *, rows, q_rows):
    scale = HEAD_DIM ** -0.5
    w = GRID_W
    qc = lax.broadcasted_iota(jnp.int32, (w, LANES), 0)
    lane = lax.broadcasted_iota(jnp.int32, (w, LANES), 1)
    c0 = jnp.clip(qc - NA_COLS // 2, 0, w - NA_COLS)
    ok_l = (lane >= c0) & (lane < c0 + NA_COLS)
    ok_r = (lane - w >= c0) & (lane - w < c0 + NA_COLS)
    neg_tile = jnp.full((w, LANES), NEG, F32)
    tile_l, tile_r = [], []
    for d in range(RPB_R):
        base = jnp.broadcast_to(rpb_ref[d:d + 1, :], (w, LANES))
        left = pltpu.roll(base, LANES - (NA_COLS - 1), 1, stride=1, stride_axis=0)
        right = pltpu.roll(base, w - (NA_COLS - 1), 1, stride=1, stride_axis=0)
        tile_l.append(jnp.where(ok_l, left, NEG))
        tile_r.append(jnp.where(ok_r, right, NEG))

    kc = kc_ref[...].astype(BF16)
    vc = vc_ref[...].astype(BF16)
    n_groups = rows // q_rows
    for gi in range(n_groups):
        rs = list(range(gi * q_rows, (gi + 1) * q_rows))
        klo = min(_na_window_start(r, rows) for r in rs) // 2 * 2
        khi = -(-(max(_na_window_start(r, rows) for r in rs) + NA_ROWS) // 2) * 2
        bias_rows = []
        for r in rs:
            r0 = _na_window_start(r, rows)
            tiles = []
            for kr in range(klo, khi, 2):
                ok0 = r0 <= kr < r0 + NA_ROWS
                ok1 = r0 <= kr + 1 < r0 + NA_ROWS
                t0 = tile_l[kr - r + NA_ROWS - 1] if ok0 else neg_tile
                t1 = tile_r[kr + 1 - r + NA_ROWS - 1] if ok1 else neg_tile
                tiles.append(jnp.maximum(t0, t1) if (ok0 or ok1) else neg_tile)
            bias_rows.append(jnp.concatenate(tiles, axis=1))
        bias = jnp.concatenate(bias_rows, axis=0)
        q = q_ref[gi * q_rows * w:(gi + 1) * q_rows * w, :]
        kl = k_ref[klo * w:khi * w, :]
        vl = v_ref[klo * w:khi * w, :]
        s_loc = _dot_nt(q, kl) * scale + bias
        s_ctx = _dot_nt(q, kc) * scale
        o = _softmax_pv([(s_loc, vl), (s_ctx, vc)])
        o_ref[gi * q_rows * w:(gi + 1) * q_rows * w, :] = o.astype(o_ref.dtype)


def _latent_na(q, k, v, k_ctx, v_ctx, rpb_pad, n_tok):
    m, d = q.shape
    nb = m // n_tok
    lc = k_ctx.shape[1]
    rows = n_tok // GRID_W
    spec = pl.BlockSpec((n_tok, HEAD_DIM), lambda b, h: (b, h))
    cspec = pl.BlockSpec((None, lc, HEAD_DIM), lambda b, h: (b, 0, h))
    return pl.pallas_call(
        functools.partial(_na_kernel, rows=rows, q_rows=4),
        grid=(nb, N_HEADS),
        in_specs=[spec, spec, spec, cspec, cspec,
                  pl.BlockSpec((None, 16, LANES), lambda b, h: (h, 0, 0))],
        out_specs=spec,
        out_shape=jax.ShapeDtypeStruct((m, d), BF16),
        compiler_params=_cparams("parallel", "parallel"),
        name="latent_na",
    )(q, k, v, k_ctx, v_ctx, rpb_pad)


def _gated_out_kernel(o_ref, z_ref, x_ref, gate_ref, w_ref, y_ref):
    a = o_ref[...].astype(F32) * _silu(z_ref[...].astype(F32))
    y_ref[...] = x_ref[...] + gate_ref[...] * _dot(a.astype(BF16), w_ref[...])


def _gated_out(o, z, x, mod3, cond0, rows_per_cond, w_out, tm):
    m, d = x.shape
    row = pl.BlockSpec((tm, d), lambda i: (i, 0))
    return pl.pallas_call(
        _gated_out_kernel,
        grid=(m // tm,),
        in_specs=[row, row, row,
                  pl.BlockSpec((None, 1, d), lambda i: (cond0 + (i * tm) // rows_per_cond, 0, 2)),
                  pl.BlockSpec((d, d), lambda i: (0, 0), pipeline_mode=pl.Buffered(1))],
        out_specs=row,
        out_shape=jax.ShapeDtypeStruct((m, d), F32),
        compiler_params=_cparams("parallel"),
        name="gated_out",
    )(o, z, x, mod3, w_out)


LANE_TILE = 512


def _tile_geometry(nc):
    n_x = max(1, LANE_TILE // nc)
    return n_x, max(1, nc // LANE_TILE), LANE_TILE // n_x


def _x_tile_specs(n_x, tps, rx, d):
    if n_x == 1:
        return [pl.BlockSpec(memory_space=pl.ANY)]
    return [pl.BlockSpec((rx, d), lambda l, xi=xi: (l % tps, (l // tps) * n_x + xi)) for xi in range(n_x)]


def _x_tile_scratch(n_x, rx, d):
    return [pltpu.VMEM((2, rx, d), F32), pltpu.SemaphoreType.DMA((2,))] if n_x == 1 else []


def _x_tile_operands(x3, n_x):
    nc, s, d = x3.shape
    return [x3] if n_x == 1 else [x3.reshape(nc, s * d)] * n_x


def _cat_rows(refs):
    return refs[0][...] if len(refs) == 1 else jnp.concatenate([r[...] for r in refs], axis=0)


def _cond_rows(ref, n_rows, rx, n_cond):
    out = ref[n_cond - 1]
    if n_cond > 1:
        b = (lax.broadcasted_iota(jnp.int32, (n_rows, 1), 0) % rx) // (rx // n_cond)
        for i in range(n_cond - 2, -1, -1):
            out = jnp.where(b == i, ref[i], out)
    return out


def _slot_copy(x_hbm, buf, sem, l, tps):
    rx = buf.shape[1]
    return pltpu.make_async_copy(x_hbm.at[pl.ds((l % tps) * rx, rx), l // tps, :], buf.at[l % 2], sem.at[l % 2])


def _fetch_slot(x_hbm, buf, sem, tps):
    l = pl.program_id(0)

    @pl.when(l == 0)
    def _():
        _slot_copy(x_hbm, buf, sem, l, tps).start()

    @pl.when(l + 1 < pl.num_programs(0))
    def _():
        _slot_copy(x_hbm, buf, sem, l + 1, tps).start()

    _slot_copy(x_hbm, buf, sem, l, tps).wait()
    return buf[l % 2]


def _inproj_t_kernel(*refs, n_x, n_cond, tps):
    if n_x == 1:
        x_hbm, g_ref, shift_ref, scale_ref, wtu_ref, wz_ref, u_ref, z_ref, buf, sem = refs
        x = _fetch_slot(x_hbm, buf, sem, tps)
    else:
        g_ref, shift_ref, scale_ref, wtu_ref, wz_ref, u_ref, z_ref = refs[n_x:]
        x = _cat_rows(refs[:n_x])
    n_rows, d = x.shape
    y = x * lax.rsqrt(jnp.mean(x * x, axis=-1, keepdims=True) + EPS) * g_ref[...]
    rx = n_rows // n_x
    h = (y * (1.0 + _cond_rows(scale_ref, n_rows, rx, n_cond))
         + _cond_rows(shift_ref, n_rows, rx, n_cond)).astype(BF16)
    u_ref[...] = _dot_nt(wtu_ref[...], h).astype(u_ref.dtype)
    z_ref[...] = _dot(h, wz_ref[...]).astype(z_ref.dtype)


def _inproj_t(x3, g, mod3, cond0, n_cond, wt_u, w_z):
    nc, _, d = x3.shape
    n_x, tps, rx = _tile_geometry(nc)
    assert n_cond == 1 or rx == nc

    def mspec(col):
        return pl.BlockSpec((n_cond, 1, d), lambda l: (cond0 // n_cond, 0, col))

    wspec = pl.BlockSpec((d, d), lambda l: (0, 0), pipeline_mode=pl.Buffered(1))
    return pl.pallas_call(
        functools.partial(_inproj_t_kernel, n_x=n_x, n_cond=n_cond, tps=tps),
        grid=(S5_CHUNK * nc // LANE_TILE,),
        in_specs=_x_tile_specs(n_x, tps, rx, d) + [
            pl.BlockSpec((1, d), lambda l: (0, 0)), mspec(0), mspec(1), wspec, wspec],
        out_specs=[pl.BlockSpec((d, LANE_TILE), lambda l: (0, l)), pl.BlockSpec((LANE_TILE, d), lambda l: (l, 0))],
        out_shape=[jax.ShapeDtypeStruct((d, S5_CHUNK * nc), BF16),
                   jax.ShapeDtypeStruct((S5_CHUNK * nc, d), BF16)],
        scratch_shapes=_x_tile_scratch(n_x, rx, d),
        compiler_params=_cparams("arbitrary"),
        name="inproj_t",
    )(*_x_tile_operands(x3, n_x), g.reshape(1, d), mod3, mod3, wt_u, w_z)


def _cmul(ar, ai, br, bi):
    return ar * br - ai * bi, ar * bi + ai * br


def _s5_powers(ar, ai):
    width = ar.shape[-1]
    pw = [(jnp.ones_like(ar), jnp.zeros_like(ar))]
    for _ in range(S5_CHUNK):
        pw.append(_cmul(pw[-1][0], pw[-1][1], ar, ai))
    is_fwd = lax.broadcasted_iota(jnp.int32, (1, width), 1) < width // 2

    def pattern(fwd_ascending):
        out = []
        for part in range(2):
            blocks = []
            for j in range(S5_CHUNK):
                ef, eb = (j, S5_CHUNK - 1 - j) if fwd_ascending else (S5_CHUNK - 1 - j, j)
                blocks.append(jnp.broadcast_to(jnp.where(is_fwd, pw[ef][part], pw[eb][part]), (S5_CH, width)))
            out.append(jnp.concatenate(blocks, axis=0))
        return out

    return pattern(True), pattern(False), pw[S5_CHUNK]


def _s5_kernel(xp_ref, xs_ref, h0_ref, are_ref, aim_ref, ldt_ref, bre_ref, bim_ref, cre_ref, cim_ref, dsk_ref,
               yp_ref, ys_ref, st_ref, w_ref, ws_ref, wc_ref, s_ref, f_ref, *, nb_p, nb_s, n_seg_s):
    p = S5_P
    kc = S5_CHUNK * S5_CH
    hi = lax.Precision.HIGHEST
    a_re = are_ref[...]
    a_im = aim_ref[...]
    dt = jnp.exp(ldt_ref[...])
    mag = jnp.exp(a_re * dt)
    ab_re = mag * jnp.cos(a_im * dt)
    ab_im = mag * jnp.sin(a_im * dt)
    den = a_re * a_re + a_im * a_im
    nr = ab_re - 1.0
    f_re = (nr * a_re + ab_im * a_im) / den
    f_im = (ab_im * a_re - nr * a_im) / den
    bbt_re, bbt_im = _cmul(f_re, f_im, bre_ref[...], bim_ref[...])

    (pg_r, pg_i), (ps_r, ps_i), (ac_r, ac_i) = _s5_powers(ab_re, ab_im)
    c_re = jnp.concatenate([cre_ref[...]] * S5_CHUNK, axis=0)
    c_im = jnp.concatenate([cim_ref[...]] * S5_CHUNK, axis=0)
    bt_re = jnp.concatenate([bbt_re] * S5_CHUNK, axis=0)
    bt_im = jnp.concatenate([bbt_im] * S5_CHUNK, axis=0)

    g_re, g_im = _cmul(pg_r, pg_i, c_re, c_im)
    fwd16 = lax.broadcasted_iota(jnp.int32, (S5_CH, 2 * p), 1) < p
    r0 = []
    for d in range(2):
        msk = fwd16 if d == 0 else jnp.logical_not(fwd16)
        r0.append(_dot_nt(jnp.where(msk, bbt_re, 0.0), g_re, hi)
                  - _dot_nt(jnp.where(msk, bbt_im, 0.0), g_im, hi))
    lane = lax.broadcasted_iota(jnp.int32, (S5_CH, kc), 1)
    for s in range(S5_CHUNK):
        f = jnp.where(lane >= S5_CH * s, pltpu.roll(r0[0], S5_CH * s, 1), 0.0) if s else r0[0]
        b = jnp.where(lane < S5_CH * (s + 1), pltpu.roll(r0[1], (S5_CH * (s + 1)) % kc, 1), 0.0)
        w_ref[s * S5_CH:(s + 1) * S5_CH, :] = (f + b).astype(BF16)
    e_re, e_im = _cmul(ps_r, ps_i, bt_re, bt_im)
    ws_ref[:, 0:2 * p] = e_re.astype(BF16)
    ws_ref[:, 2 * p:4 * p] = e_im.astype(BF16)
    g1_re, g1_im = _cmul(g_re, g_im, ab_re, ab_im)
    wc_ref[:, 0:2 * p] = g1_re.astype(BF16)
    wc_ref[:, 2 * p:4 * p] = (-g1_im).astype(BF16)

    def run(xt_ref, yt_ref, nb, n_seg, hr, hm):
        rows = xt_ref.shape[1] // S5_CHUNK
        nbx = nb * n_seg
        m = rows // nbx
        xt = jnp.concatenate([xt_ref[:, s * rows:(s + 1) * rows] for s in range(S5_CHUNK)], axis=0)
        xf = xt.astype(F32).T
        x = xf.astype(BF16)
        y = _dot(x, w_ref[...]) + dsk_ref[...] * xf
        s_all = _dot(x, ws_ref[...])
        s_ref[0, 0:rows, :] = s_all[:, 0:2 * p]
        s_ref[1, 0:rows, :] = s_all[:, 2 * p:4 * p]

        def step(ar, ai, hr, hm, plane_r, plane_i, rf, rb):
            isf = lax.broadcasted_iota(jnp.int32, hr.shape, 1) < p
            sr = jnp.where(isf, plane_r[rf, :], plane_r[rb, :])
            sm = jnp.where(isf, plane_i[rf, :], plane_i[rb, :])
            return ar * hr - ai * hm + sr, ar * hm + ai * hr + sm

        def scan(hr, hm, store):
            arb = jnp.broadcast_to(ac_r, (nbx, 2 * p))
            aib = jnp.broadcast_to(ac_i, (nbx, 2 * p))
            for k in range(m):
                rf = pl.ds(k, nbx, stride=m)
                rb = pl.ds(m - 1 - k, nbx, stride=m)
                if store:
                    s_ref[2, rf, :] = hr
                    s_ref[3, rf, :] = hm
                    s_ref[4, rb, :] = hr
                    s_ref[5, rb, :] = hm
                hr, hm = step(arb, aib, hr, hm, s_ref.at[0], s_ref.at[1], rf, rb)
            return hr, hm

        if n_seg > 1:
            zeros = jnp.zeros((nbx, 2 * p), F32)
            f_ref[0], f_ref[1] = scan(zeros, zeros, False)
            sr_, si_ = ac_r, ac_i
            for _ in range(m - 1):
                sr_, si_ = _cmul(sr_, si_, ac_r, ac_i)
            sr_ = jnp.broadcast_to(sr_, (nb, 2 * p))
            si_ = jnp.broadcast_to(si_, (nb, 2 * p))
            for j in range(n_seg):
                rf = pl.ds(j, nb, stride=n_seg)
                rb = pl.ds(n_seg - 1 - j, nb, stride=n_seg)
                f_ref[2, rf, :] = hr
                f_ref[3, rf, :] = hm
                f_ref[4, rb, :] = hr
                f_ref[5, rb, :] = hm
                hr, hm = step(sr_, si_, hr, hm, f_ref.at[0], f_ref.at[1], rf, rb)
            isf = lax.broadcasted_iota(jnp.int32, (nbx, 2 * p), 1) < p
            hr = jnp.where(isf, f_ref[2], f_ref[4])
            hm = jnp.where(isf, f_ref[3], f_ref[5])
        hr, hm = scan(hr, hm, True)
        isf_rows = lax.broadcasted_iota(jnp.int32, (rows, 2 * p), 1) < p
        h_prev = jnp.concatenate([jnp.where(isf_rows, s_ref[2, 0:rows, :], s_ref[4, 0:rows, :]),
                                  jnp.where(isf_rows, s_ref[3, 0:rows, :], s_ref[5, 0:rows, :])], axis=1)
        yt = (y + _dot_nt(h_prev.astype(BF16), wc_ref[...])).T
        for t in range(S5_CHUNK):
            yt_ref[:, t * rows:(t + 1) * rows] = yt[t * S5_CH:(t + 1) * S5_CH, :]
        return hr, hm

    zeros = jnp.zeros((nb_p, 2 * p), F32)
    hr, hm = run(xp_ref, yp_ref, nb_p, 1, zeros, zeros)
    st_ref[:, 0:2 * p] = hr
    st_ref[:, 2 * p:4 * p] = hm
    run(xs_ref, ys_ref, nb_s, n_seg_s, h0_ref[:, 0:2 * p], h0_ref[:, 2 * p:4 * p])


def _s5(utp, uts, h0, a_re, a_im, log_dt, bt_re, bt_im, c_re, c_im, d_skip, nb_p, nb_s):
    d, lanes_p = utp.shape
    lanes_s = uts.shape[1]
    g = d // S5_CH
    rows_p = lanes_p // S5_CHUNK
    rows_s = lanes_s // S5_CHUNK
    n_seg_s = max(1, rows_s // nb_s // 8)
    kc = S5_CHUNK * S5_CH
    p = S5_P

    def gspec(shape):
        return pl.BlockSpec((None,) + shape, lambda i: (i, 0, 0))

    def tspec(lanes):
        return pl.BlockSpec((S5_CH, lanes), lambda i: (i, 0))

    return pl.pallas_call(
        functools.partial(_s5_kernel, nb_p=nb_p, nb_s=nb_s, n_seg_s=n_seg_s),
        grid=(g,),
        in_specs=[tspec(lanes_p), tspec(lanes_s), gspec((nb_s, 4 * p)),
                  gspec((1, 2 * p)), gspec((1, 2 * p)), gspec((1, 2 * p)),
                  gspec((S5_CH, 2 * p)), gspec((S5_CH, 2 * p)), gspec((S5_CH, 2 * p)), gspec((S5_CH, 2 * p)),
                  gspec((1, kc))],
        out_specs=[tspec(lanes_p), tspec(lanes_s), gspec((nb_p, 4 * p))],
        out_shape=[jax.ShapeDtypeStruct((d, lanes_p), F32),
                   jax.ShapeDtypeStruct((d, lanes_s), F32),
                   jax.ShapeDtypeStruct((g, nb_p, 4 * p), F32)],
        scratch_shapes=[pltpu.VMEM((kc, kc), BF16), pltpu.VMEM((kc, 4 * p), BF16), pltpu.VMEM((kc, 4 * p), BF16),
                        pltpu.VMEM((6, max(rows_p, rows_s), 2 * p), F32),
                        pltpu.VMEM((6, nb_s * n_seg_s, 2 * p), F32)],
        compiler_params=_cparams("parallel"),
        name="s5",
    )(utp, uts, h0, a_re, a_im, log_dt, bt_re, bt_im, c_re, c_im,
      jnp.tile(d_skip.reshape(g, 1, S5_CH), (1, 1, S5_CHUNK)))


def _glu_kernel(yt_ref, z_ref, w_ref, b_ref, a_ref, yf_ref, yb_ref):
    nj, _, tn = yf_ref.shape
    j = pl.program_id(1)

    @pl.when(j == 0)
    def _():
        y = jax.nn.gelu(yt_ref[...].T)
        yb_ref[...] = y.astype(BF16)
        for jj in range(nj):
            yf_ref[jj] = y[:, jj * tn:(jj + 1) * tn]

    gl = _dot(yb_ref[...], w_ref[...]) + b_ref[...]
    a_ref[...] = (yf_ref[j] * jax.nn.sigmoid(gl) * _silu(z_ref[...].astype(F32))).astype(a_ref.dtype)


def _glu(yt, z, w_glu, b_glu, tn):
    d, m = yt.shape
    tm = LANE_TILE
    nj = d // tn
    return pl.pallas_call(
        _glu_kernel,
        grid=(m // tm, nj),
        in_specs=[pl.BlockSpec((d, tm), lambda i, j: (0, i)),
                  pl.BlockSpec((tm, tn), lambda i, j: (i, j)),
                  pl.BlockSpec((d, tn), lambda i, j: (0, j)),
                  pl.BlockSpec((1, tn), lambda i, j: (0, j))],
        out_specs=pl.BlockSpec((tm, tn), lambda i, j: (i, j)),
        out_shape=jax.ShapeDtypeStruct((m, d), BF16),
        scratch_shapes=[pltpu.VMEM((nj, tm, tn), F32), pltpu.VMEM((tm, d), BF16)],
        compiler_params=_cparams("parallel", "arbitrary"),
        name="glu",
    )(yt, z, w_glu, b_glu.reshape(1, d))


def _store_slot(res, o_hbm, buf, sem, tps):
    l = pl.program_id(0)
    n = pl.num_programs(0)
    rx = buf.shape[1]

    def copy(ll):
        return pltpu.make_async_copy(buf.at[ll % 2], o_hbm.at[pl.ds((ll % tps) * rx, rx), ll // tps, :],
                                     sem.at[ll % 2])

    @pl.when(l >= 2)
    def _():
        copy(l - 2).wait()

    buf[l % 2] = res
    copy(l).start()

    @pl.when(l == n - 1)
    def _():
        @pl.when(l >= 1)
        def _():
            copy(l - 1).wait()

        copy(l).wait()


def _out_norm_kernel(*refs, n_x, n_cond, tps):
    if n_x == 1:
        x_hbm, a_ref, gate_ref, fg_ref, w_ref, o_ref, xbuf, xsem, obuf, osem = refs
        x = _fetch_slot(x_hbm, xbuf, xsem, tps)
    else:
        a_ref, gate_ref, fg_ref, w_ref, o_ref = refs[n_x:]
        x = _cat_rows(refs[:n_x])
    n_rows, d = x.shape
    rx = n_rows // n_x
    xn = x + _cond_rows(gate_ref, n_rows, rx, n_cond) * _dot(a_ref[...], w_ref[...])
    res = xn * lax.rsqrt(jnp.mean(xn * xn, axis=-1, keepdims=True) + EPS) * fg_ref[...]
    if n_x == 1:
        _store_slot(res, o_ref, obuf, osem, tps)
    else:
        for xi in range(n_x):
            o_ref[:, xi * d:(xi + 1) * d] = res[xi * rx:(xi + 1) * rx, :]


def _out_norm(a, x3, mod3, cond0, n_cond, final_g, w_out):
    nc, s, d = x3.shape
    n_x, tps, rx = _tile_geometry(nc)
    assert n_cond == 1 or rx == nc
    if n_x == 1:
        out_spec = pl.BlockSpec(memory_space=pl.ANY)
        out_shape = jax.ShapeDtypeStruct((nc, s, d), F32)
    else:
        out_spec = pl.BlockSpec((rx, n_x * d), lambda l: (l % tps, l // tps))
        out_shape = jax.ShapeDtypeStruct((nc, s * d), F32)
    return pl.pallas_call(
        functools.partial(_out_norm_kernel, n_x=n_x, n_cond=n_cond, tps=tps),
        grid=(S5_CHUNK * nc // LANE_TILE,),
        in_specs=_x_tile_specs(n_x, tps, rx, d) + [
            pl.BlockSpec((LANE_TILE, d), lambda l: (l, 0)),
            pl.BlockSpec((n_cond, 1, d), lambda l: (cond0 // n_cond, 0, 2)),
            pl.BlockSpec((1, d), lambda l: (0, 0)),
            pl.BlockSpec((d, d), lambda l: (0, 0), pipeline_mode=pl.Buffered(1))],
        out_specs=out_spec,
        out_shape=out_shape,
        scratch_shapes=2 * _x_tile_scratch(n_x, rx, d),
        compiler_params=_cparams("arbitrary"),
        name="out_norm",
    )(*_x_tile_operands(x3, n_x), a, mod3, final_g.reshape(1, d), w_out)


def _both_dirs(a):
    _, g, r, p = a.shape
    return a.transpose(1, 2, 0, 3).reshape(g, r, 2 * p)


def kernel(x_prompt, x_sample, cache_l0_k, cache_l0_v, state_l1_s5, c, c_ctx, l0_norm_g, l0_w_ada, l0_b_ada, l0_w_in, l0_rpb, l0_w_out, l1_norm_g, l1_w_ada, l1_b_ada, l1_w_in, l1_a_re, l1_a_im, l1_log_dt, l1_b_re, l1_b_im, l1_c_re, l1_c_im, l1_d, l1_w_glu, l1_b_glu, l1_w_out, final_norm_g):
    bp, seq, d = x_prompt.shape
    bs, n_tok, _ = x_sample.shape
    lc = cache_l0_k.shape[1]
    g = d // S5_CH
    p = S5_P
    xp = x_prompt.reshape(bp * seq, d)
    xs = x_sample.reshape(bs * n_tok, d)

    ctx = bs
    cond8 = jnp.zeros((8, d), F32).at[0:bs].set(c).at[ctx].set(c_ctx)
    mod0 = _ada(cond8, l0_w_ada, l0_b_ada).reshape(8, 1, 3 * d)
    mod1 = _ada(cond8, l1_w_ada, l1_b_ada).reshape(8, 1, 3 * d)

    w_in0 = l0_w_in.astype(BF16)
    w_out0 = l0_w_out.astype(BF16)
    qp, kp, kpb, vp, vpb, zp = _inproj(xp, l0_norm_g, mod0, ctx, bp * seq, w_in0,
                                       ((0, BF16), (1, F32), (1, BF16), (2, F32), (2, BF16), (3, BF16)), 512, 512)
    qs, ks, vs, zs = _inproj(xs, l0_norm_g, mod0, 0, n_tok, w_in0,
                             ((0, BF16), (1, BF16), (2, BF16), (3, BF16)), 512, 512)
    op = _ctx_attn(qp, kpb, vpb, seq)
    rpb_pad = jnp.zeros((N_HEADS, 16, LANES), F32).at[:, :RPB_R, :RPB_C].set(l0_rpb)
    os_ = _latent_na(qs, ks, vs, cache_l0_k.reshape(bs, lc, d), cache_l0_v.reshape(bs, lc, d), rpb_pad, n_tok)
    x1p = _gated_out(op, zp, xp, mod0, ctx, bp * seq, w_out0, 512).reshape(-1, S5_CHUNK, d)
    x1s = _gated_out(os_, zs, xs, mod0, 0, n_tok, w_out0, 512).reshape(-1, S5_CHUNK, d)

    wt_u = l1_w_in[:, :d].T.astype(BF16)
    w_z = l1_w_in[:, d:].astype(BF16)
    utp, zp1 = _inproj_t(x1p, l1_norm_g, mod1, ctx, 1, wt_u, w_z)
    uts, zs1 = _inproj_t(x1s, l1_norm_g, mod1, 0, bs, wt_u, w_z)
    h0 = state_l1_s5.transpose(3, 0, 2, 1, 4).reshape(g, bs, 4 * p)
    log_dt = jnp.broadcast_to(l1_log_dt[:, :, None, None], (2, g, 1, p))
    ytp, yts, st = _s5(
        utp, uts, h0,
        _both_dirs(l1_a_re[:, :, None, :]), _both_dirs(l1_a_im[:, :, None, :]), _both_dirs(log_dt),
        _both_dirs(l1_b_re.transpose(0, 1, 3, 2)), _both_dirs(l1_b_im.transpose(0, 1, 3, 2)),
        _both_dirs(l1_c_re), _both_dirs(l1_c_im), l1_d, bp, bs)
    w_glu = l1_w_glu.astype(BF16)
    w_out1 = l1_w_out.astype(BF16)
    y_prompt = _out_norm(_glu(ytp, zp1, w_glu, l1_b_glu, 512), x1p, mod1, ctx, 1, final_norm_g, w_out1)
    y_sample = _out_norm(_glu(yts, zs1, w_glu, l1_b_glu, 512), x1s, mod1, 0, bs, final_norm_g, w_out1)
    new_state = st.reshape(g, bp, 2, 2, p).transpose(1, 3, 2, 0, 4)
    return (y_prompt.reshape(bp, seq, d), y_sample.reshape(bs, n_tok, d),
            kp.reshape(bp, seq, N_HEADS, HEAD_DIM), vp.reshape(bp, seq, N_HEADS, HEAD_DIM), new_state)
```

```python
import functools

import jax
import jax.numpy as jnp
from jax import lax
from jax.experimental import pallas as pl
from jax.experimental.pallas import tpu as pltpu

F32 = jnp.float32
BF16 = jnp.bfloat16

EPS = 1e-6
N_HEADS = 16
HEAD_DIM = 128
GRID_W = 64
NA_ROWS = 8
NA_COLS = 16
RPB_R = 2 * NA_ROWS - 1
RPB_C = 2 * NA_COLS - 1
S5_CH = 16
S5_P = 64
S5_CHUNK = 16
S5_SEG = 8
NEG = -1e30
LANES = 128
VMEM_LIMIT = 56 * 1024 * 1024


def _cparams(*sem):
    return pltpu.CompilerParams(dimension_semantics=sem, vmem_limit_bytes=VMEM_LIMIT)


def _dot(a, b):
    return jnp.dot(a, b, preferred_element_type=F32)


def _dot_nt(a, b, precision=None):
    return lax.dot_general(a, b, (((1,), (1,)), ((), ())), preferred_element_type=F32,
                           precision=precision)


def _silu(x):
    return x * jax.nn.sigmoid(x)


def _ada_kernel(cond_ref, w_ref, b_ref, o_ref):
    s = _silu(cond_ref[...])
    o_ref[...] = _dot(s.astype(BF16), w_ref[...].astype(BF16)) + b_ref[...]


def _ada(cond8, w_ada, b_ada):
    d, n = w_ada.shape
    tn = 1024
    return pl.pallas_call(
        _ada_kernel,
        grid=(n // tn,),
        in_specs=[pl.BlockSpec((8, d), lambda j: (0, 0)),
                  pl.BlockSpec((d, tn), lambda j: (0, j)),
                  pl.BlockSpec((1, tn), lambda j: (0, j))],
        out_specs=pl.BlockSpec((8, tn), lambda j: (0, j)),
        out_shape=jax.ShapeDtypeStruct((8, n), F32),
        compiler_params=_cparams("parallel"),
        name="ada",
    )(cond8, w_ada, b_ada.reshape(1, n))


def _inproj_kernel(x_ref, g_ref, shift_ref, scale_ref, *rest, out_split):
    n_split = max(out_split) + 1
    w_refs = rest[:n_split]
    o_refs = rest[n_split:-1]
    h_ref = rest[-1]

    @pl.when(pl.program_id(1) == 0)
    def _():
        x = x_ref[...]
        y = x * lax.rsqrt(jnp.mean(x * x, axis=-1, keepdims=True) + EPS) * g_ref[...]
        h_ref[...] = (y * (1.0 + scale_ref[...]) + shift_ref[...]).astype(BF16)

    h = h_ref[...]
    for s, w_ref in enumerate(w_refs):
        r = _dot(h, w_ref[...])
        for o_ref, o_s in zip(o_refs, out_split):
            if o_s == s:
                o_ref[...] = r.astype(o_ref.dtype)


def _inproj(x, g, mod3, cond0, rows_per_cond, w, outs, tm, tn):
    m, d = x.shape
    out_split = tuple(s for s, _ in outs)
    out_dtypes = [dt for _, dt in outs]
    n_split = max(out_split) + 1
    nj = d // tn

    def cond_of(i):
        return cond0 + (i * tm) // rows_per_cond

    in_specs = [pl.BlockSpec((tm, d), lambda i, j: (i, 0)),
                pl.BlockSpec((1, d), lambda i, j: (0, 0)),
                pl.BlockSpec((None, 1, d), lambda i, j: (cond_of(i), 0, 0)),
                pl.BlockSpec((None, 1, d), lambda i, j: (cond_of(i), 0, 1))]
    for s in range(n_split):
        in_specs.append(pl.BlockSpec((d, tn), lambda i, j, s=s: (0, s * nj + j)))
    return pl.pallas_call(
        functools.partial(_inproj_kernel, out_split=out_split),
        grid=(m // tm, nj),
        in_specs=in_specs,
        out_specs=[pl.BlockSpec((tm, tn), lambda i, j: (i, j)) for _ in outs],
        out_shape=[jax.ShapeDtypeStruct((m, d), dt) for dt in out_dtypes],
        scratch_shapes=[pltpu.VMEM((tm, d), BF16)],
        compiler_params=_cparams("parallel", "arbitrary"),
        name="inproj",
    )(x, g.reshape(1, d), mod3, mod3, *([w] * n_split))


def _softmax_pv(parts):
    m = parts[0][0].max(axis=-1, keepdims=True)
    for s, _ in parts[1:]:
        m = jnp.maximum(m, s.max(axis=-1, keepdims=True))
    l = None
    o = None
    for s, v in parts:
        e = jnp.exp(s - m)
        li = e.sum(axis=-1, keepdims=True)
        oi = _dot(e.astype(BF16), v)
        l = li if l is None else l + li
        o = oi if o is None else o + oi
    return o / l


def _ctx_attn_kernel(q_ref, k_ref, v_ref, o_ref):
    scale = HEAD_DIM ** -0.5
    for h in range(N_HEADS):
        sl = slice(h * HEAD_DIM, (h + 1) * HEAD_DIM)
        s = _dot_nt(q_ref[:, sl], k_ref[:, sl]) * scale
        o_ref[:, sl] = _softmax_pv([(s, v_ref[:, sl])]).astype(o_ref.dtype)


def _ctx_attn(q, k, v, seq):
    m, d = q.shape
    spec = pl.BlockSpec((seq, d), lambda b: (b, 0))
    return pl.pallas_call(
        _ctx_attn_kernel,
        grid=(m // seq,),
        in_specs=[spec, spec, spec],
        out_specs=spec,
        out_shape=jax.ShapeDtypeStruct((m, d), BF16),
        compiler_params=_cparams("parallel"),
        name="ctx_attn",
    )(q, k, v)


def _na_window_start(r, rows):
    return min(max(r - NA_ROWS // 2, 0), rows - NA_ROWS)


def _na_kernel(q_ref, k_ref, v_ref, kc_ref, vc_ref, rpb_ref, o_ref, *, rows, q_rows):
    scale = HEAD_DIM ** -0.5
    w = GRID_W
    qc = lax.broadcasted_iota(jnp.int32, (w, LANES), 0)
    lane = lax.broadcasted_iota(jnp.int32, (w, LANES), 1)
    c0 = jnp.clip(qc - NA_COLS // 2, 0, w - NA_COLS)
    ok_l = (lane >= c0) & (lane < c0 + NA_COLS)
    ok_r = (lane - w >= c0) & (lane - w < c0 + NA_COLS)
    neg_tile = jnp.full((w, LANES), NEG, F32)
    tile_l, tile_r = [], []
    for d in range(RPB_R):
        base = jnp.broadcast_to(rpb_ref[d:d + 1, :], (w, LANES))
        left = pltpu.roll(base, LANES - (NA_COLS - 1), 1, stride=1, stride_axis=0)
        right = pltpu.roll(base, w - (NA_COLS - 1), 1, stride=1, stride_axis=0)
        tile_l.append(jnp.where(ok_l, left, NEG))
        tile_r.append(jnp.where(ok_r, right, NEG))

    kc = kc_ref[...].astype(BF16)
    vc = vc_ref[...].astype(BF16)
    n_groups = rows // q_rows
    for gi in range(n_groups):
        rs = list(range(gi * q_rows, (gi + 1) * q_rows))
        klo = min(_na_window_start(r, rows) for r in rs) // 2 * 2
        khi = -(-(max(_na_window_start(r, rows) for r in rs) + NA_ROWS) // 2) * 2
        bias_rows = []
        for r in rs:
            r0 = _na_window_start(r, rows)
            tiles = []
            for kr in range(klo, khi, 2):
                ok0 = r0 <= kr < r0 + NA_ROWS
                ok1 = r0 <= kr + 1 < r0 + NA_ROWS
                t0 = tile_l[kr - r + NA_ROWS - 1] if ok0 else neg_tile
                t1 = tile_r[kr + 1 - r + NA_ROWS - 1] if ok1 else neg_tile
                tiles.append(jnp.maximum(t0, t1) if (ok0 or ok1) else neg_tile)
            bias_rows.append(jnp.concatenate(tiles, axis=1))
        bias = jnp.concatenate(bias_rows, axis=0)
        q = q_ref[gi * q_rows * w:(gi + 1) * q_rows * w, :]
        kl = k_ref[klo * w:khi * w, :]
        vl = v_ref[klo * w:khi * w, :]
        s_loc = _dot_nt(q, kl) * scale + bias
        s_ctx = _dot_nt(q, kc) * scale
        o = _softmax_pv([(s_loc, vl), (s_ctx, vc)])
        o_ref[gi * q_rows * w:(gi + 1) * q_rows * w, :] = o.astype(o_ref.dtype)


def _latent_na(q, k, v, k_ctx, v_ctx, rpb_pad, n_tok):
    m, d = q.shape
    nb = m // n_tok
    lc = k_ctx.shape[1]
    rows = n_tok // GRID_W
    spec = pl.BlockSpec((n_tok, HEAD_DIM), lambda b, h: (b, h))
    cspec = pl.BlockSpec((None, lc, HEAD_DIM), lambda b, h: (b, 0, h))
    return pl.pallas_call(
        functools.partial(_na_kernel, rows=rows, q_rows=4),
        grid=(nb, N_HEADS),
        in_specs=[spec, spec, spec, cspec, cspec,
                  pl.BlockSpec((None, 16, LANES), lambda b, h: (h, 0, 0))],
        out_specs=spec,
        out_shape=jax.ShapeDtypeStruct((m, d), BF16),
        compiler_params=_cparams("parallel", "parallel"),
        name="latent_na",
    )(q, k, v, k_ctx, v_ctx, rpb_pad)


def _gated_out_kernel(o_ref, z_ref, x_ref, gate_ref, w_ref, y_ref):
    a = o_ref[...].astype(F32) * _silu(z_ref[...].astype(F32))
    y_ref[...] = x_ref[...] + gate_ref[...] * _dot(a.astype(BF16), w_ref[...])


def _gated_out(o, z, x, mod3, cond0, rows_per_cond, w_out, tm):
    m, d = x.shape
    row = pl.BlockSpec((tm, d), lambda i: (i, 0))
    return pl.pallas_call(
        _gated_out_kernel,
        grid=(m // tm,),
        in_specs=[row, row, row,
                  pl.BlockSpec((None, 1, d), lambda i: (cond0 + (i * tm) // rows_per_cond, 0, 2)),
                  pl.BlockSpec((d, d), lambda i: (0, 0), pipeline_mode=pl.Buffered(1))],
        out_specs=row,
        out_shape=jax.ShapeDtypeStruct((m, d), F32),
        compiler_params=_cparams("parallel"),
        name="gated_out",
    )(o, z, x, mod3, w_out)


LANE_TILE = 512


def _slots_per_step(x4):
    nq, m, _, _ = x4.shape
    assert LANE_TILE % (nq * m) == 0 and S5_CHUNK % (LANE_TILE // (nq * m)) == 0
    return LANE_TILE // (nq * m)


def _slot_scratch(d):
    return [pltpu.VMEM((2, LANE_TILE, d), F32), pltpu.SemaphoreType.DMA((2,))]


def _slot_copies(x_hbm, buf, sem, l, to_hbm):
    nq, m, _, _ = x_hbm.shape
    n_x = buf.shape[1] // (nq * m)
    copies = []
    for xi in range(n_x):
        for k in range(m):
            hbm = x_hbm.at[:, k, l * n_x + xi, :]
            vmem = buf.at[l % 2, pl.ds((xi * m + k) * nq, nq), :]
            src, dst = (vmem, hbm) if to_hbm else (hbm, vmem)
            copies.append(pltpu.make_async_copy(src, dst, sem.at[l % 2]))
    return copies


def _fetch_slots(x_hbm, buf, sem):
    l = pl.program_id(0)

    @pl.when(l == 0)
    def _():
        for c in _slot_copies(x_hbm, buf, sem, l, False):
            c.start()

    @pl.when(l + 1 < pl.num_programs(0))
    def _():
        for c in _slot_copies(x_hbm, buf, sem, l + 1, False):
            c.start()

    for c in _slot_copies(x_hbm, buf, sem, l, False):
        c.wait()
    return buf[l % 2]


def _store_slots(res, o_hbm, buf, sem):
    l = pl.program_id(0)

    def wait(ll):
        for c in _slot_copies(o_hbm, buf, sem, ll, True):
            c.wait()

    @pl.when(l >= 2)
    def _():
        wait(l - 2)

    buf[l % 2] = res
    for c in _slot_copies(o_hbm, buf, sem, l, True):
        c.start()

    @pl.when(l == pl.num_programs(0) - 1)
    def _():
        @pl.when(l >= 1)
        def _():
            wait(l - 1)

        wait(l)


def _cond_rows(ref, n_rows, nq, n_cond):
    out = ref[n_cond - 1]
    if n_cond > 1:
        b = (lax.broadcasted_iota(jnp.int32, (n_rows, 1), 0) % nq) // (nq // n_cond)
        for i in range(n_cond - 2, -1, -1):
            out = jnp.where(b == i, ref[i], out)
    return out


def _inproj_t_kernel(x_hbm, g_ref, shift_ref, scale_ref, wtu_ref, wz_ref, u_ref, z_ref, buf, sem, *, n_cond):
    nq = x_hbm.shape[0]
    x = _fetch_slots(x_hbm, buf, sem)
    n_rows = x.shape[0]
    y = x * lax.rsqrt(jnp.mean(x * x, axis=-1, keepdims=True) + EPS) * g_ref[...]
    h = (y * (1.0 + _cond_rows(scale_ref, n_rows, nq, n_cond))
         + _cond_rows(shift_ref, n_rows, nq, n_cond)).astype(BF16)
    u_ref[...] = _dot_nt(wtu_ref[...], h).astype(u_ref.dtype)
    z_ref[...] = _dot(h, wz_ref[...]).astype(z_ref.dtype)


def _inproj_t(x4, g, mod3, cond0, n_cond, wt_u, w_z):
    nq, m, _, d = x4.shape
    n_tok = nq * m * S5_CHUNK
    _slots_per_step(x4)

    def mspec(col):
        return pl.BlockSpec((n_cond, 1, d), lambda l: (cond0 // n_cond, 0, col))

    wspec = pl.BlockSpec((d, d), lambda l: (0, 0), pipeline_mode=pl.Buffered(1))
    return pl.pallas_call(
        functools.partial(_inproj_t_kernel, n_cond=n_cond),
        grid=(n_tok // LANE_TILE,),
        in_specs=[pl.BlockSpec(memory_space=pl.ANY),
                  pl.BlockSpec((1, d), lambda l: (0, 0)), mspec(0), mspec(1), wspec, wspec],
        out_specs=[pl.BlockSpec((d, LANE_TILE), lambda l: (0, l)), pl.BlockSpec((LANE_TILE, d), lambda l: (l, 0))],
        out_shape=[jax.ShapeDtypeStruct((d, n_tok), BF16), jax.ShapeDtypeStruct((n_tok, d), BF16)],
        scratch_shapes=_slot_scratch(d),
        compiler_params=_cparams("arbitrary"),
        name="inproj_t",
    )(x4, g.reshape(1, d), mod3, mod3, wt_u, w_z)


def _cmul(ar, ai, br, bi):
    return ar * br - ai * bi, ar * bi + ai * br


def _s5_powers(ar, ai):
    width = ar.shape[-1]
    pw = [(jnp.ones_like(ar), jnp.zeros_like(ar))]
    for _ in range(S5_CHUNK):
        pw.append(_cmul(pw[-1][0], pw[-1][1], ar, ai))
    is_fwd = lax.broadcasted_iota(jnp.int32, (1, width), 1) < width // 2

    def pattern(fwd_ascending):
        out = []
        for part in range(2):
            blocks = []
            for j in range(S5_CHUNK):
                ef, eb = (j, S5_CHUNK - 1 - j) if fwd_ascending else (S5_CHUNK - 1 - j, j)
                blocks.append(jnp.broadcast_to(jnp.where(is_fwd, pw[ef][part], pw[eb][part]), (S5_CH, width)))
            out.append(jnp.concatenate(blocks, axis=0))
        return out

    return pattern(True), pattern(False), pw[S5_CHUNK]


def _s5_kernel(xp_ref, xs_ref, h0_ref, are_ref, aim_ref, ldt_ref, bre_ref, bim_ref, cre_ref, cim_ref, dsk_ref,
               yp_ref, ys_ref, st_ref, w_ref, ws_ref, wc_ref, s_ref, f_ref, *, nb_p, nb_s, n_seg_s):
    p = S5_P
    kc = S5_CHUNK * S5_CH
    hi = lax.Precision.HIGHEST
    a_re = are_ref[...]
    a_im = aim_ref[...]
    dt = jnp.exp(ldt_ref[...])
    mag = jnp.exp(a_re * dt)
    ab_re = mag * jnp.cos(a_im * dt)
    ab_im = mag * jnp.sin(a_im * dt)
    den = a_re * a_re + a_im * a_im
    nr = ab_re - 1.0
    f_re = (nr * a_re + ab_im * a_im) / den
    f_im = (ab_im * a_re - nr * a_im) / den
    bbt_re, bbt_im = _cmul(f_re, f_im, bre_ref[...], bim_ref[...])

    (pg_r, pg_i), (ps_r, ps_i), (ac_r, ac_i) = _s5_powers(ab_re, ab_im)
    c_re = jnp.concatenate([cre_ref[...]] * S5_CHUNK, axis=0)
    c_im = jnp.concatenate([cim_ref[...]] * S5_CHUNK, axis=0)
    bt_re = jnp.concatenate([bbt_re] * S5_CHUNK, axis=0)
    bt_im = jnp.concatenate([bbt_im] * S5_CHUNK, axis=0)

    g_re, g_im = _cmul(pg_r, pg_i, c_re, c_im)
    fwd16 = lax.broadcasted_iota(jnp.int32, (S5_CH, 2 * p), 1) < p
    r0 = []
    for d in range(2):
        msk = fwd16 if d == 0 else jnp.logical_not(fwd16)
        r0.append(_dot_nt(jnp.where(msk, bbt_re, 0.0), g_re, hi)
                  - _dot_nt(jnp.where(msk, bbt_im, 0.0), g_im, hi))
    lane = lax.broadcasted_iota(jnp.int32, (S5_CH, kc), 1)
    for s in range(S5_CHUNK):
        f = jnp.where(lane >= S5_CH * s, pltpu.roll(r0[0], S5_CH * s, 1), 0.0) if s else r0[0]
        b = jnp.where(lane < S5_CH * (s + 1), pltpu.roll(r0[1], (S5_CH * (s + 1)) % kc, 1), 0.0)
        w_ref[s * S5_CH:(s + 1) * S5_CH, :] = (f + b).astype(BF16)
    e_re, e_im = _cmul(ps_r, ps_i, bt_re, bt_im)
    ws_ref[:, 0:2 * p] = e_re.astype(BF16)
    ws_ref[:, 2 * p:4 * p] = e_im.astype(BF16)
    g1_re, g1_im = _cmul(g_re, g_im, ab_re, ab_im)
    wc_ref[:, 0:2 * p] = g1_re.astype(BF16)
    wc_ref[:, 2 * p:4 * p] = (-g1_im).astype(BF16)

    def run(xt_ref, yt_ref, nb, n_seg, hr, hm):
        rows = xt_ref.shape[1] // S5_CHUNK
        nbx = nb * n_seg
        m = rows // nbx
        xt = jnp.concatenate([xt_ref[:, s * rows:(s + 1) * rows] for s in range(S5_CHUNK)], axis=0)
        xf = xt.astype(F32).T
        x = xf.astype(BF16)
        y = _dot(x, w_ref[...]) + dsk_ref[...] * xf
        s_all = _dot(x, ws_ref[...])
        s_ref[0, 0:rows, :] = s_all[:, 0:2 * p]
        s_ref[1, 0:rows, :] = s_all[:, 2 * p:4 * p]

        def step(ar, ai, hr, hm, plane_r, plane_i, rf, rb):
            isf = lax.broadcasted_iota(jnp.int32, hr.shape, 1) < p
            sr = jnp.where(isf, plane_r[rf, :], plane_r[rb, :])
            sm = jnp.where(isf, plane_i[rf, :], plane_i[rb, :])
            return ar * hr - ai * hm + sr, ar * hm + ai * hr + sm

        def scan(hr, hm, store):
            arb = jnp.broadcast_to(ac_r, (nbx, 2 * p))
            aib = jnp.broadcast_to(ac_i, (nbx, 2 * p))
            for k in range(m):
                rf = pl.ds(k * nbx, nbx)
                rb = pl.ds((m - 1 - k) * nbx, nbx)
                if store:
                    s_ref[2, rf, :] = hr
                    s_ref[3, rf, :] = hm
                    s_ref[4, rb, :] = hr
                    s_ref[5, rb, :] = hm
                hr, hm = step(arb, aib, hr, hm, s_ref.at[0], s_ref.at[1], rf, rb)
            return hr, hm

        if n_seg > 1:
            zeros = jnp.zeros((nbx, 2 * p), F32)
            f_ref[0], f_ref[1] = scan(zeros, zeros, False)
            sr_, si_ = ac_r, ac_i
            for _ in range(m - 1):
                sr_, si_ = _cmul(sr_, si_, ac_r, ac_i)
            sr_ = jnp.broadcast_to(sr_, (nb, 2 * p))
            si_ = jnp.broadcast_to(si_, (nb, 2 * p))
            for j in range(n_seg):
                rf = pl.ds(j, nb, stride=n_seg)
                rb = pl.ds(n_seg - 1 - j, nb, stride=n_seg)
                f_ref[2, rf, :] = hr
                f_ref[3, rf, :] = hm
                f_ref[4, rb, :] = hr
                f_ref[5, rb, :] = hm
                hr, hm = step(sr_, si_, hr, hm, f_ref.at[0], f_ref.at[1], rf, rb)
            isf = lax.broadcasted_iota(jnp.int32, (nbx, 2 * p), 1) < p
            hr = jnp.where(isf, f_ref[2], f_ref[4])
            hm = jnp.where(isf, f_ref[3], f_ref[5])
        hr, hm = scan(hr, hm, True)
        isf_rows = lax.broadcasted_iota(jnp.int32, (rows, 2 * p), 1) < p
        h_prev = jnp.concatenate([jnp.where(isf_rows, s_ref[2, 0:rows, :], s_ref[4, 0:rows, :]),
                                  jnp.where(isf_rows, s_ref[3, 0:rows, :], s_ref[5, 0:rows, :])], axis=1)
        yt = (y + _dot_nt(h_prev.astype(BF16), wc_ref[...])).T
        for t in range(S5_CHUNK):
            yt_ref[:, t * rows:(t + 1) * rows] = yt[t * S5_CH:(t + 1) * S5_CH, :]
        return hr, hm

    zeros = jnp.zeros((nb_p, 2 * p), F32)
    hr, hm = run(xp_ref, yp_ref, nb_p, 1, zeros, zeros)
    st_ref[:, 0:2 * p] = hr
    st_ref[:, 2 * p:4 * p] = hm
    run(xs_ref, ys_ref, nb_s, n_seg_s, h0_ref[:, 0:2 * p], h0_ref[:, 2 * p:4 * p])


def _s5(utp, uts, h0, a_re, a_im, log_dt, bt_re, bt_im, c_re, c_im, d_skip, nb_p, nb_s, n_seg_s):
    d, lanes_p = utp.shape
    lanes_s = uts.shape[1]
    g = d // S5_CH
    rows_p = lanes_p // S5_CHUNK
    rows_s = lanes_s // S5_CHUNK
    kc = S5_CHUNK * S5_CH
    p = S5_P

    def gspec(shape):
        return pl.BlockSpec((None,) + shape, lambda i: (i, 0, 0))

    def tspec(lanes):
        return pl.BlockSpec((S5_CH, lanes), lambda i: (i, 0))

    return pl.pallas_call(
        functools.partial(_s5_kernel, nb_p=nb_p, nb_s=nb_s, n_seg_s=n_seg_s),
        grid=(g,),
        in_specs=[tspec(lanes_p), tspec(lanes_s), gspec((nb_s, 4 * p)),
                  gspec((1, 2 * p)), gspec((1, 2 * p)), gspec((1, 2 * p)),
                  gspec((S5_CH, 2 * p)), gspec((S5_CH, 2 * p)), gspec((S5_CH, 2 * p)), gspec((S5_CH, 2 * p)),
                  gspec((1, kc))],
        out_specs=[tspec(lanes_p), tspec(lanes_s), gspec((nb_p, 4 * p))],
        out_shape=[jax.ShapeDtypeStruct((d, lanes_p), F32),
                   jax.ShapeDtypeStruct((d, lanes_s), F32),
                   jax.ShapeDtypeStruct((g, nb_p, 4 * p), F32)],
        scratch_shapes=[pltpu.VMEM((kc, kc), BF16), pltpu.VMEM((kc, 4 * p), BF16), pltpu.VMEM((kc, 4 * p), BF16),
                        pltpu.VMEM((6, max(rows_p, rows_s), 2 * p), F32),
                        pltpu.VMEM((6, nb_s * n_seg_s, 2 * p), F32)],
        compiler_params=_cparams("parallel"),
        name="s5",
    )(utp, uts, h0, a_re, a_im, log_dt, bt_re, bt_im, c_re, c_im,
      jnp.tile(d_skip.reshape(g, 1, S5_CH), (1, 1, S5_CHUNK)))


def _glu_kernel(yt_ref, z_ref, w_ref, b_ref, a_ref, yf_ref, yb_ref):
    nj, _, tn = yf_ref.shape
    j = pl.program_id(1)

    @pl.when(j == 0)
    def _():
        y = jax.nn.gelu(yt_ref[...].T)
        yb_ref[...] = y.astype(BF16)
        for jj in range(nj):
            yf_ref[jj] = y[:, jj * tn:(jj + 1) * tn]

    gl = _dot(yb_ref[...], w_ref[...]) + b_ref[...]
    a_ref[...] = (yf_ref[j] * jax.nn.sigmoid(gl) * _silu(z_ref[...].astype(F32))).astype(a_ref.dtype)


def _glu(yt, z, w_glu, b_glu, tn):
    d, m = yt.shape
    tm = LANE_TILE
    nj = d // tn
    return pl.pallas_call(
        _glu_kernel,
        grid=(m // tm, nj),
        in_specs=[pl.BlockSpec((d, tm), lambda i, j: (0, i)),
                  pl.BlockSpec((tm, tn), lambda i, j: (i, j)),
                  pl.BlockSpec((d, tn), lambda i, j: (0, j)),
                  pl.BlockSpec((1, tn), lambda i, j: (0, j))],
        out_specs=pl.BlockSpec((tm, tn), lambda i, j: (i, j)),
        out_shape=jax.ShapeDtypeStruct((m, d), BF16),
        scratch_shapes=[pltpu.VMEM((nj, tm, tn), F32), pltpu.VMEM((tm, d), BF16)],
        compiler_params=_cparams("parallel", "arbitrary"),
        name="glu",
    )(yt, z, w_glu, b_glu.reshape(1, d))


def _out_norm_kernel(x_hbm, a_ref, gate_ref, fg_ref, w_ref, o_hbm, xbuf, xsem, obuf, osem, *, n_cond):
    nq = x_hbm.shape[0]
    x = _fetch_slots(x_hbm, xbuf, xsem)
    xn = x + _cond_rows(gate_ref, x.shape[0], nq, n_cond) * _dot(a_ref[...], w_ref[...])
    res = xn * lax.rsqrt(jnp.mean(xn * xn, axis=-1, keepdims=True) + EPS) * fg_ref[...]
    _store_slots(res, o_hbm, obuf, osem)


def _out_norm(a, x4, mod3, cond0, n_cond, final_g, w_out):
    nq, m, _, d = x4.shape
    _slots_per_step(x4)
    return pl.pallas_call(
        functools.partial(_out_norm_kernel, n_cond=n_cond),
        grid=(nq * m * S5_CHUNK // LANE_TILE,),
        in_specs=[pl.BlockSpec(memory_space=pl.ANY),
                  pl.BlockSpec((LANE_TILE, d), lambda l: (l, 0)),
                  pl.BlockSpec((n_cond, 1, d), lambda l: (cond0 // n_cond, 0, 2)),
                  pl.BlockSpec((1, d), lambda l: (0, 0)),
                  pl.BlockSpec((d, d), lambda l: (0, 0), pipeline_mode=pl.Buffered(1))],
        out_specs=pl.BlockSpec(memory_space=pl.ANY),
        out_shape=jax.ShapeDtypeStruct(x4.shape, F32),
        scratch_shapes=2 * _slot_scratch(d),
        compiler_params=_cparams("arbitrary"),
        name="out_norm",
    )(x4, a, mod3, final_g.reshape(1, d), w_out)


def _both_dirs(a):
    _, g, r, p = a.shape
    return a.transpose(1, 2, 0, 3).reshape(g, r, 2 * p)


def kernel(x_prompt, x_sample, cache_l0_k, cache_l0_v, state_l1_s5, c, c_ctx, l0_norm_g, l0_w_ada, l0_b_ada, l0_w_in, l0_rpb, l0_w_out, l1_norm_g, l1_w_ada, l1_b_ada, l1_w_in, l1_a_re, l1_a_im, l1_log_dt, l1_b_re, l1_b_im, l1_c_re, l1_c_im, l1_d, l1_w_glu, l1_b_glu, l1_w_out, final_norm_g):
    bp, seq, d = x_prompt.shape
    bs, n_tok, _ = x_sample.shape
    lc = cache_l0_k.shape[1]
    g = d // S5_CH
    p = S5_P
    xp = x_prompt.reshape(bp * seq, d)
    xs = x_sample.reshape(bs * n_tok, d)

    ctx = bs
    cond8 = jnp.zeros((8, d), F32).at[0:bs].set(c).at[ctx].set(c_ctx)
    mod0 = _ada(cond8, l0_w_ada, l0_b_ada).reshape(8, 1, 3 * d)
    mod1 = _ada(cond8, l1_w_ada, l1_b_ada).reshape(8, 1, 3 * d)

    w_in0 = l0_w_in.astype(BF16)
    w_out0 = l0_w_out.astype(BF16)
    qp, kp, kpb, vp, vpb, zp = _inproj(xp, l0_norm_g, mod0, ctx, bp * seq, w_in0,
                                       ((0, BF16), (1, F32), (1, BF16), (2, F32), (2, BF16), (3, BF16)), 1024, 256)
    qs, ks, vs, zs = _inproj(xs, l0_norm_g, mod0, 0, n_tok, w_in0,
                             ((0, BF16), (1, BF16), (2, BF16), (3, BF16)), 1024, 256)
    op = _ctx_attn(qp, kpb, vpb, seq)
    rpb_pad = jnp.zeros((N_HEADS, 16, LANES), F32).at[:, :RPB_R, :RPB_C].set(l0_rpb)
    os_ = _latent_na(qs, ks, vs, cache_l0_k.reshape(bs, lc, d), cache_l0_v.reshape(bs, lc, d), rpb_pad, n_tok)
    n_seg_s = max(1, n_tok // S5_CHUNK // S5_SEG)
    x1p = _gated_out(op, zp, xp, mod0, ctx, bp * seq, w_out0, 512).reshape(bp, seq // S5_CHUNK, S5_CHUNK, d)
    x1s = _gated_out(os_, zs, xs, mod0, 0, n_tok, w_out0, 512).reshape(bs * n_seg_s, -1, S5_CHUNK, d)

    wt_u = l1_w_in[:, :d].T.astype(BF16)
    w_z = l1_w_in[:, d:].astype(BF16)
    utp, zp1 = _inproj_t(x1p, l1_norm_g, mod1, ctx, 1, wt_u, w_z)
    uts, zs1 = _inproj_t(x1s, l1_norm_g, mod1, 0, bs, wt_u, w_z)
    h0 = state_l1_s5.transpose(3, 0, 2, 1, 4).reshape(g, bs, 4 * p)
    log_dt = jnp.broadcast_to(l1_log_dt[:, :, None, None], (2, g, 1, p))
    ytp, yts, st = _s5(
        utp, uts, h0,
        _both_dirs(l1_a_re[:, :, None, :]), _both_dirs(l1_a_im[:, :, None, :]), _both_dirs(log_dt),
        _both_dirs(l1_b_re.transpose(0, 1, 3, 2)), _both_dirs(l1_b_im.transpose(0, 1, 3, 2)),
        _both_dirs(l1_c_re), _both_dirs(l1_c_im), l1_d, bp, bs, n_seg_s)
    w_glu = l1_w_glu.astype(BF16)
    w_out1 = l1_w_out.astype(BF16)
    y_prompt = _out_norm(_glu(ytp, zp1, w_glu, l1_b_glu, 512), x1p, mod1, ctx, 1, final_norm_g, w_out1)
    y_sample = _out_norm(_glu(yts, zs1, w_glu, l1_b_glu, 512), x1s, mod1, 0, bs, final_norm_g, w_out1)
    new_state = st.reshape(g, bp, 2, 2, p).transpose(1, 3, 2, 0, 4)
    return (y_prompt.reshape(bp, seq, d), y_sample.reshape(bs, n_tok, d),
            kp.reshape(bp, seq, N_HEADS, HEAD_DIM), vp.reshape(bp, seq, N_HEADS, HEAD_DIM), new_state)
```

```python
import functools

import jax
import jax.numpy as jnp
from jax import lax
from jax.experimental import pallas as pl
from jax.experimental.pallas import tpu as pltpu

F32 = jnp.float32
BF16 = jnp.bfloat16

EPS = 1e-6
N_HEADS = 16
HEAD_DIM = 128
GRID_W = 64
NA_ROWS = 8
NA_COLS = 16
RPB_R = 2 * NA_ROWS - 1
RPB_C = 2 * NA_COLS - 1
S5_CH = 16
S5_P = 64
S5_CHUNK = 16
S5_SEG = 8
NEG = -1e30
LANES = 128
VMEM_LIMIT = 56 * 1024 * 1024


def _cparams(*sem):
    return pltpu.CompilerParams(dimension_semantics=sem, vmem_limit_bytes=VMEM_LIMIT)


def _dot(a, b):
    return jnp.dot(a, b, preferred_element_type=F32)


def _dot_nt(a, b, precision=None):
    return lax.dot_general(a, b, (((1,), (1,)), ((), ())), preferred_element_type=F32,
                           precision=precision)


def _silu(x):
    return x * jax.nn.sigmoid(x)


def _ada_kernel(cond_ref, w_ref, b_ref, o_ref):
    s = _silu(cond_ref[...])
    o_ref[...] = _dot(s.astype(BF16), w_ref[...].astype(BF16)) + b_ref[...]


def _ada(cond8, w_ada, b_ada):
    d, n = w_ada.shape
    tn = 1024
    return pl.pallas_call(
        _ada_kernel,
        grid=(n // tn,),
        in_specs=[pl.BlockSpec((8, d), lambda j: (0, 0)),
                  pl.BlockSpec((d, tn), lambda j: (0, j)),
                  pl.BlockSpec((1, tn), lambda j: (0, j))],
        out_specs=pl.BlockSpec((8, tn), lambda j: (0, j)),
        out_shape=jax.ShapeDtypeStruct((8, n), F32),
        compiler_params=_cparams("parallel"),
        name="ada",
    )(cond8, w_ada, b_ada.reshape(1, n))


def _inproj_kernel(x_ref, g_ref, shift_ref, scale_ref, *rest, out_split):
    n_split = max(out_split) + 1
    w_refs = rest[:n_split]
    o_refs = rest[n_split:-1]
    h_ref = rest[-1]

    @pl.when(pl.program_id(1) == 0)
    def _():
        x = x_ref[...]
        y = x * lax.rsqrt(jnp.mean(x * x, axis=-1, keepdims=True) + EPS) * g_ref[...]
        h_ref[...] = (y * (1.0 + scale_ref[...]) + shift_ref[...]).astype(BF16)

    h = h_ref[...]
    for s, w_ref in enumerate(w_refs):
        r = _dot(h, w_ref[...])
        for o_ref, o_s in zip(o_refs, out_split):
            if o_s == s:
                o_ref[...] = r.astype(o_ref.dtype)


def _inproj(x, g, mod3, cond0, rows_per_cond, w, outs, tm, tn):
    m, d = x.shape
    out_split = tuple(s for s, _ in outs)
    out_dtypes = [dt for _, dt in outs]
    n_split = max(out_split) + 1
    nj = d // tn

    def cond_of(i):
        return cond0 + (i * tm) // rows_per_cond

    in_specs = [pl.BlockSpec((tm, d), lambda i, j: (i, 0)),
                pl.BlockSpec((1, d), lambda i, j: (0, 0)),
                pl.BlockSpec((None, 1, d), lambda i, j: (cond_of(i), 0, 0)),
                pl.BlockSpec((None, 1, d), lambda i, j: (cond_of(i), 0, 1))]
    for s in range(n_split):
        in_specs.append(pl.BlockSpec((d, tn), lambda i, j, s=s: (0, s * nj + j)))
    return pl.pallas_call(
        functools.partial(_inproj_kernel, out_split=out_split),
        grid=(m // tm, nj),
        in_specs=in_specs,
        out_specs=[pl.BlockSpec((tm, tn), lambda i, j: (i, j)) for _ in outs],
        out_shape=[jax.ShapeDtypeStruct((m, d), dt) for dt in out_dtypes],
        scratch_shapes=[pltpu.VMEM((tm, d), BF16)],
        compiler_params=_cparams("parallel", "arbitrary"),
        name="inproj",
    )(x, g.reshape(1, d), mod3, mod3, *([w] * n_split))


def _softmax_pv(parts):
    m = parts[0][0].max(axis=-1, keepdims=True)
    for s, _ in parts[1:]:
        m = jnp.maximum(m, s.max(axis=-1, keepdims=True))
    l = None
    o = None
    for s, v in parts:
        e = jnp.exp(s - m)
        li = e.sum(axis=-1, keepdims=True)
        oi = _dot(e.astype(BF16), v)
        l = li if l is None else l + li
        o = oi if o is None else o + oi
    return o / l


def _ctx_attn_kernel(q_ref, k_ref, v_ref, o_ref):
    scale = HEAD_DIM ** -0.5
    for h in range(N_HEADS):
        sl = slice(h * HEAD_DIM, (h + 1) * HEAD_DIM)
        s = _dot_nt(q_ref[:, sl], k_ref[:, sl]) * scale
        o_ref[:, sl] = _softmax_pv([(s, v_ref[:, sl])]).astype(o_ref.dtype)


def _ctx_attn(q, k, v, seq):
    m, d = q.shape
    spec = pl.BlockSpec((seq, d), lambda b: (b, 0))
    return pl.pallas_call(
        _ctx_attn_kernel,
        grid=(m // seq,),
        in_specs=[spec, spec, spec],
        out_specs=spec,
        out_shape=jax.ShapeDtypeStruct((m, d), BF16),
        compiler_params=_cparams("parallel"),
        name="ctx_attn",
    )(q, k, v)


def _na_window_start(r, rows):
    return min(max(r - NA_ROWS // 2, 0), rows - NA_ROWS)


def _na_kernel(q_ref, k_ref, v_ref, kc_ref, vc_ref, rpb_ref, o_ref, *, rows, q_rows):
    scale = HEAD_DIM ** -0.5
    w = GRID_W
    qc = lax.broadcasted_iota(jnp.int32, (w, LANES), 0)
    lane = lax.broadcasted_iota(jnp.int32, (w, LANES), 1)
    c0 = jnp.clip(qc - NA_COLS // 2, 0, w - NA_COLS)
    ok_l = (lane >= c0) & (lane < c0 + NA_COLS)
    ok_r = (lane - w >= c0) & (lane - w < c0 + NA_COLS)
    neg_tile = jnp.full((w, LANES), NEG, F32)
    tile_l, tile_r = [], []
    for d in range(RPB_R):
        base = jnp.broadcast_to(rpb_ref[d:d + 1, :], (w, LANES))
        left = pltpu.roll(base, LANES - (NA_COLS - 1), 1, stride=1, stride_axis=0)
        right = pltpu.roll(base, w - (NA_COLS - 1), 1, stride=1, stride_axis=0)
        tile_l.append(jnp.where(ok_l, left, NEG))
        tile_r.append(jnp.where(ok_r, right, NEG))

    kc = kc_ref[...].astype(BF16)
    vc = vc_ref[...].astype(BF16)
    n_groups = rows // q_rows
    for gi in range(n_groups):
        rs = list(range(gi * q_rows, (gi + 1) * q_rows))
        klo = min(_na_window_start(r, rows) for r in rs) // 2 * 2
        khi = -(-(max(_na_window_start(r, rows) for r in rs) + NA_ROWS) // 2) * 2
        bias_rows = []
        for r in rs:
            r0 = _na_window_start(r, rows)
            tiles = []
            for kr in range(klo, khi, 2):
                ok0 = r0 <= kr < r0 + NA_ROWS
                ok1 = r0 <= kr + 1 < r0 + NA_ROWS
                t0 = tile_l[kr - r + NA_ROWS - 1] if ok0 else neg_tile
                t1 = tile_r[kr + 1 - r + NA_ROWS - 1] if ok1 else neg_tile
                tiles.append(jnp.maximum(t0, t1) if (ok0 or ok1) else neg_tile)
            bias_rows.append(jnp.concatenate(tiles, axis=1))
        bias = jnp.concatenate(bias_rows, axis=0)
        q = q_ref[gi * q_rows * w:(gi + 1) * q_rows * w, :]
        kl = k_ref[klo * w:khi * w, :]
        vl = v_ref[klo * w:khi * w, :]
        s_loc = _dot_nt(q, kl) * scale + bias
        s_ctx = _dot_nt(q, kc) * scale
        o = _softmax_pv([(s_loc, vl), (s_ctx, vc)])
        o_ref[gi * q_rows * w:(gi + 1) * q_rows * w, :] = o.astype(o_ref.dtype)


def _latent_na(q, k, v, k_ctx, v_ctx, rpb_pad, n_tok):
    m, d = q.shape
    nb = m // n_tok
    lc = k_ctx.shape[1]
    rows = n_tok // GRID_W
    spec = pl.BlockSpec((n_tok, HEAD_DIM), lambda b, h: (b, h))
    cspec = pl.BlockSpec((None, lc, HEAD_DIM), lambda b, h: (b, 0, h))
    return pl.pallas_call(
        functools.partial(_na_kernel, rows=rows, q_rows=4),
        grid=(nb, N_HEADS),
        in_specs=[spec, spec, spec, cspec, cspec,
                  pl.BlockSpec((None, 16, LANES), lambda b, h: (h, 0, 0))],
        out_specs=spec,
        out_shape=jax.ShapeDtypeStruct((m, d), BF16),
        compiler_params=_cparams("parallel", "parallel"),
        name="latent_na",
    )(q, k, v, k_ctx, v_ctx, rpb_pad)


def _gated_out_kernel(o_ref, z_ref, x_ref, gate_ref, w_ref, y_ref):
    a = o_ref[...].astype(F32) * _silu(z_ref[...].astype(F32))
    y_ref[...] = x_ref[...] + gate_ref[...] * _dot(a.astype(BF16), w_ref[...])


def _gated_out(o, z, x, mod3, cond0, rows_per_cond, w_out, tm):
    m, d = x.shape
    row = pl.BlockSpec((tm, d), lambda i: (i, 0))
    return pl.pallas_call(
        _gated_out_kernel,
        grid=(m // tm,),
        in_specs=[row, row, row,
                  pl.BlockSpec((None, 1, d), lambda i: (cond0 + (i * tm) // rows_per_cond, 0, 2)),
                  pl.BlockSpec((d, d), lambda i: (0, 0), pipeline_mode=pl.Buffered(1))],
        out_specs=row,
        out_shape=jax.ShapeDtypeStruct((m, d), F32),
        compiler_params=_cparams("parallel"),
        name="gated_out",
    )(o, z, x, mod3, w_out)


LANE_TILE = 512


def _slots_per_step(x4):
    nq, m, _, _ = x4.shape
    assert LANE_TILE % (nq * m) == 0 and S5_CHUNK % (LANE_TILE // (nq * m)) == 0
    return LANE_TILE // (nq * m)


def _slot_scratch(d):
    return [pltpu.VMEM((2, LANE_TILE, d), F32), pltpu.SemaphoreType.DMA((2,))]


def _slot_copies(x_hbm, buf, sem, l, to_hbm):
    nq, m, _, _ = x_hbm.shape
    n_x = buf.shape[1] // (nq * m)
    copies = []
    for xi in range(n_x):
        for k in range(m):
            hbm = x_hbm.at[:, k, l * n_x + xi, :]
            vmem = buf.at[l % 2, pl.ds((xi * m + k) * nq, nq), :]
            src, dst = (vmem, hbm) if to_hbm else (hbm, vmem)
            copies.append(pltpu.make_async_copy(src, dst, sem.at[l % 2]))
    return copies


def _fetch_slots(x_hbm, buf, sem):
    l = pl.program_id(0)

    @pl.when(l == 0)
    def _():
        for c in _slot_copies(x_hbm, buf, sem, l, False):
            c.start()

    @pl.when(l + 1 < pl.num_programs(0))
    def _():
        for c in _slot_copies(x_hbm, buf, sem, l + 1, False):
            c.start()

    for c in _slot_copies(x_hbm, buf, sem, l, False):
        c.wait()
    return buf[l % 2]


def _store_slots(res, o_hbm, buf, sem):
    l = pl.program_id(0)

    def wait(ll):
        for c in _slot_copies(o_hbm, buf, sem, ll, True):
            c.wait()

    @pl.when(l >= 2)
    def _():
        wait(l - 2)

    buf[l % 2] = res
    for c in _slot_copies(o_hbm, buf, sem, l, True):
        c.start()

    @pl.when(l == pl.num_programs(0) - 1)
    def _():
        @pl.when(l >= 1)
        def _():
            wait(l - 1)

        wait(l)


def _cond_rows(ref, n_rows, nq, n_cond):
    out = ref[n_cond - 1]
    if n_cond > 1:
        b = (lax.broadcasted_iota(jnp.int32, (n_rows, 1), 0) % nq) // (nq // n_cond)
        for i in range(n_cond - 2, -1, -1):
            out = jnp.where(b == i, ref[i], out)
    return out


def _inproj_t_kernel(x_hbm, g_ref, shift_ref, scale_ref, wu_ref, wz_ref, u_ref, z_ref, buf, sem, *, n_cond):
    nq = x_hbm.shape[0]
    x = _fetch_slots(x_hbm, buf, sem)
    n_rows = x.shape[0]
    y = x * lax.rsqrt(jnp.mean(x * x, axis=-1, keepdims=True) + EPS) * g_ref[...]
    h = (y * (1.0 + _cond_rows(scale_ref, n_rows, nq, n_cond))
         + _cond_rows(shift_ref, n_rows, nq, n_cond)).astype(BF16)
    u_ref[...] = _dot(h, wu_ref[...]).T.astype(u_ref.dtype)
    z_ref[...] = _dot(h, wz_ref[...]).astype(z_ref.dtype)


def _inproj_t(x4, g, mod3, cond0, n_cond, w_in):
    nq, m, _, d = x4.shape
    n_tok = nq * m * S5_CHUNK
    _slots_per_step(x4)

    def mspec(col):
        return pl.BlockSpec((n_cond, 1, d), lambda l: (cond0 // n_cond, 0, col))

    def wspec(col):
        return pl.BlockSpec((d, d), lambda l: (0, col), pipeline_mode=pl.Buffered(1))

    return pl.pallas_call(
        functools.partial(_inproj_t_kernel, n_cond=n_cond),
        grid=(n_tok // LANE_TILE,),
        in_specs=[pl.BlockSpec(memory_space=pl.ANY),
                  pl.BlockSpec((1, d), lambda l: (0, 0)), mspec(0), mspec(1), wspec(0), wspec(1)],
        out_specs=[pl.BlockSpec((d, LANE_TILE), lambda l: (0, l)), pl.BlockSpec((LANE_TILE, d), lambda l: (l, 0))],
        out_shape=[jax.ShapeDtypeStruct((d, n_tok), BF16), jax.ShapeDtypeStruct((n_tok, d), BF16)],
        scratch_shapes=_slot_scratch(d),
        compiler_params=_cparams("arbitrary"),
        name="inproj_t",
    )(x4, g.reshape(1, d), mod3, mod3, w_in, w_in)


def _cmul(ar, ai, br, bi):
    return ar * br - ai * bi, ar * bi + ai * br


def _s5_powers(ar, ai):
    width = ar.shape[-1]
    pw = [(jnp.ones_like(ar), jnp.zeros_like(ar))]
    for _ in range(S5_CHUNK):
        pw.append(_cmul(pw[-1][0], pw[-1][1], ar, ai))
    is_fwd = lax.broadcasted_iota(jnp.int32, (1, width), 1) < width // 2

    def pattern(fwd_ascending):
        out = []
        for part in range(2):
            blocks = []
            for j in range(S5_CHUNK):
                ef, eb = (j, S5_CHUNK - 1 - j) if fwd_ascending else (S5_CHUNK - 1 - j, j)
                blocks.append(jnp.broadcast_to(jnp.where(is_fwd, pw[ef][part], pw[eb][part]), (S5_CH, width)))
            out.append(jnp.concatenate(blocks, axis=0))
        return out

    return pattern(True), pattern(False), pw[S5_CHUNK]


S5_GROUPS_PER_STEP = 2


def _s5_kernel(xp_ref, xs_ref, *rest, nb_p, nb_s, n_seg_s):
    yp_ref, ys_ref = rest[9], rest[10]
    for gi in range(S5_GROUPS_PER_STEP):
        ch = pl.ds(gi * S5_CH, S5_CH)
        _s5_group(xp_ref.at[ch], xs_ref.at[ch], *[r.at[gi] for r in rest[:9]], yp_ref.at[ch], ys_ref.at[ch],
                  *[r.at[gi] for r in rest[11:]], nb_p=nb_p, nb_s=nb_s, n_seg_s=n_seg_s)


def _s5_group(xp_ref, xs_ref, h0_ref, are_ref, aim_ref, ldt_ref, bre_ref, bim_ref, cre_ref, cim_ref, dsk_ref,
              yp_ref, ys_ref, st_ref, w_ref, ws_ref, wc_ref, s_ref, f_ref, *, nb_p, nb_s, n_seg_s):
    p = S5_P
    kc = S5_CHUNK * S5_CH
    hi = lax.Precision.HIGHEST
    a_re = are_ref[...]
    a_im = aim_ref[...]
    dt = jnp.exp(ldt_ref[...])
    mag = jnp.exp(a_re * dt)
    ab_re = mag * jnp.cos(a_im * dt)
    ab_im = mag * jnp.sin(a_im * dt)
    den = a_re * a_re + a_im * a_im
    nr = ab_re - 1.0
    f_re = (nr * a_re + ab_im * a_im) / den
    f_im = (ab_im * a_re - nr * a_im) / den
    bbt_re, bbt_im = _cmul(f_re, f_im, bre_ref[...], bim_ref[...])

    (pg_r, pg_i), (ps_r, ps_i), (ac_r, ac_i) = _s5_powers(ab_re, ab_im)
    c_re = jnp.concatenate([cre_ref[...]] * S5_CHUNK, axis=0)
    c_im = jnp.concatenate([cim_ref[...]] * S5_CHUNK, axis=0)
    bt_re = jnp.concatenate([bbt_re] * S5_CHUNK, axis=0)
    bt_im = jnp.concatenate([bbt_im] * S5_CHUNK, axis=0)

    g_re, g_im = _cmul(pg_r, pg_i, c_re, c_im)
    fwd16 = lax.broadcasted_iota(jnp.int32, (S5_CH, 2 * p), 1) < p
    r0 = []
    for d in range(2):
        msk = fwd16 if d == 0 else jnp.logical_not(fwd16)
        r0.append(_dot_nt(jnp.where(msk, bbt_re, 0.0), g_re, hi)
                  - _dot_nt(jnp.where(msk, bbt_im, 0.0), g_im, hi))
    lane = lax.broadcasted_iota(jnp.int32, (S5_CH, kc), 1)
    for s in range(S5_CHUNK):
        f = jnp.where(lane >= S5_CH * s, pltpu.roll(r0[0], S5_CH * s, 1), 0.0) if s else r0[0]
        b = jnp.where(lane < S5_CH * (s + 1), pltpu.roll(r0[1], (S5_CH * (s + 1)) % kc, 1), 0.0)
        w_ref[s * S5_CH:(s + 1) * S5_CH, :] = (f + b).astype(BF16)
    e_re, e_im = _cmul(ps_r, ps_i, bt_re, bt_im)
    ws_ref[:, 0:2 * p] = e_re.astype(BF16)
    ws_ref[:, 2 * p:4 * p] = e_im.astype(BF16)
    g1_re, g1_im = _cmul(g_re, g_im, ab_re, ab_im)
    wc_ref[:, 0:2 * p] = g1_re.astype(BF16)
    wc_ref[:, 2 * p:4 * p] = (-g1_im).astype(BF16)

    def run(xt_ref, yt_ref, nb, n_seg, hr, hm):
        rows = xt_ref.shape[1] // S5_CHUNK
        nbx = nb * n_seg
        m = rows // nbx
        xt = jnp.concatenate([xt_ref[:, s * rows:(s + 1) * rows] for s in range(S5_CHUNK)], axis=0)
        xf = xt.astype(F32).T
        x = xf.astype(BF16)
        y = _dot(x, w_ref[...]) + dsk_ref[...] * xf
        s_all = _dot(x, ws_ref[...])
        s_ref[0, 0:rows, :] = s_all[:, 0:2 * p]
        s_ref[1, 0:rows, :] = s_all[:, 2 * p:4 * p]

        def step(ar, ai, hr, hm, plane_r, plane_i, rf, rb):
            isf = lax.broadcasted_iota(jnp.int32, hr.shape, 1) < p
            sr = jnp.where(isf, plane_r[rf, :], plane_r[rb, :])
            sm = jnp.where(isf, plane_i[rf, :], plane_i[rb, :])
            return ar * hr - ai * hm + sr, ar * hm + ai * hr + sm

        def scan(hr, hm, store):
            arb = jnp.broadcast_to(ac_r, (nbx, 2 * p))
            aib = jnp.broadcast_to(ac_i, (nbx, 2 * p))
            for k in range(m):
                rf = pl.ds(k * nbx, nbx)
                rb = pl.ds((m - 1 - k) * nbx, nbx)
                if store:
                    s_ref[2, rf, :] = hr
                    s_ref[3, rf, :] = hm
                    s_ref[4, rb, :] = hr
                    s_ref[5, rb, :] = hm
                hr, hm = step(arb, aib, hr, hm, s_ref.at[0], s_ref.at[1], rf, rb)
            return hr, hm

        if n_seg > 1:
            zeros = jnp.zeros((nbx, 2 * p), F32)
            f_ref[0], f_ref[1] = scan(zeros, zeros, False)
            sr_, si_ = ac_r, ac_i
            for _ in range(m - 1):
                sr_, si_ = _cmul(sr_, si_, ac_r, ac_i)
            sr_ = jnp.broadcast_to(sr_, (nb, 2 * p))
            si_ = jnp.broadcast_to(si_, (nb, 2 * p))
            for j in range(n_seg):
                rf = pl.ds(j, nb, stride=n_seg)
                rb = pl.ds(n_seg - 1 - j, nb, stride=n_seg)
                f_ref[2, rf, :] = hr
                f_ref[3, rf, :] = hm
                f_ref[4, rb, :] = hr
                f_ref[5, rb, :] = hm
                hr, hm = step(sr_, si_, hr, hm, f_ref.at[0], f_ref.at[1], rf, rb)
            isf = lax.broadcasted_iota(jnp.int32, (nbx, 2 * p), 1) < p
            hr = jnp.where(isf, f_ref[2], f_ref[4])
            hm = jnp.where(isf, f_ref[3], f_ref[5])
        hr, hm = scan(hr, hm, True)
        isf_rows = lax.broadcasted_iota(jnp.int32, (rows, 2 * p), 1) < p
        h_prev = jnp.concatenate([jnp.where(isf_rows, s_ref[2, 0:rows, :], s_ref[4, 0:rows, :]),
                                  jnp.where(isf_rows, s_ref[3, 0:rows, :], s_ref[5, 0:rows, :])], axis=1)
        yt = (y + _dot_nt(h_prev.astype(BF16), wc_ref[...])).T
        for t in range(S5_CHUNK):
            yt_ref[:, t * rows:(t + 1) * rows] = yt[t * S5_CH:(t + 1) * S5_CH, :]
        return hr, hm

    zeros = jnp.zeros((nb_p, 2 * p), F32)
    hr, hm = run(xp_ref, yp_ref, nb_p, 1, zeros, zeros)
    st_ref[:, 0:2 * p] = hr
    st_ref[:, 2 * p:4 * p] = hm
    run(xs_ref, ys_ref, nb_s, n_seg_s, h0_ref[:, 0:2 * p], h0_ref[:, 2 * p:4 * p])


def _s5(utp, uts, h0, a_re, a_im, log_dt, bt_re, bt_im, c_re, c_im, d_skip, nb_p, nb_s, n_seg_s):
    d, lanes_p = utp.shape
    lanes_s = uts.shape[1]
    g = d // S5_CH
    rows_p = lanes_p // S5_CHUNK
    rows_s = lanes_s // S5_CHUNK
    kc = S5_CHUNK * S5_CH
    p = S5_P

    gps = S5_GROUPS_PER_STEP

    def gspec(shape):
        return pl.BlockSpec((gps,) + shape, lambda i: (i, 0, 0))

    def tspec(lanes):
        return pl.BlockSpec((gps * S5_CH, lanes), lambda i: (i, 0))

    return pl.pallas_call(
        functools.partial(_s5_kernel, nb_p=nb_p, nb_s=nb_s, n_seg_s=n_seg_s),
        grid=(g // gps,),
        in_specs=[tspec(lanes_p), tspec(lanes_s), gspec((nb_s, 4 * p)),
                  gspec((1, 2 * p)), gspec((1, 2 * p)), gspec((1, 2 * p)),
                  gspec((S5_CH, 2 * p)), gspec((S5_CH, 2 * p)), gspec((S5_CH, 2 * p)), gspec((S5_CH, 2 * p)),
                  gspec((1, kc))],
        out_specs=[tspec(lanes_p), tspec(lanes_s), gspec((nb_p, 4 * p))],
        out_shape=[jax.ShapeDtypeStruct((d, lanes_p), F32),
                   jax.ShapeDtypeStruct((d, lanes_s), F32),
                   jax.ShapeDtypeStruct((g, nb_p, 4 * p), F32)],
        scratch_shapes=[pltpu.VMEM((gps, kc, kc), BF16), pltpu.VMEM((gps, kc, 4 * p), BF16),
                        pltpu.VMEM((gps, kc, 4 * p), BF16),
                        pltpu.VMEM((gps, 6, max(rows_p, rows_s), 2 * p), F32),
                        pltpu.VMEM((gps, 6, nb_s * n_seg_s, 2 * p), F32)],
        compiler_params=_cparams("parallel"),
        name="s5",
    )(utp, uts, h0, a_re, a_im, log_dt, bt_re, bt_im, c_re, c_im,
      jnp.tile(d_skip.reshape(g, 1, S5_CH), (1, 1, S5_CHUNK)))


def _glu_kernel(yt0_ref, ytn_ref, z_ref, w_ref, b_ref, a_ref, yf_ref, yb_ref):
    _, nj, _, tn = yf_ref.shape
    i = pl.program_id(0)
    j = pl.program_id(1)
    slot = i % 2

    def prepare(yt_block, dst_slot, jj):
        y = jax.nn.gelu(yt_block.T)
        yf_ref[dst_slot, jj] = y
        yb_ref[dst_slot, jj] = y.astype(BF16)

    @pl.when((i == 0) & (j == 0))
    def _():
        for jj in range(nj):
            prepare(yt0_ref[jj * tn:(jj + 1) * tn, :], 0, jj)

    @pl.when(i + 1 < pl.num_programs(0))
    def _():
        prepare(ytn_ref[...], 1 - slot, j)

    gl = b_ref[...]
    for jj in range(nj):
        gl = gl + _dot(yb_ref[slot, jj], w_ref[jj * tn:(jj + 1) * tn, :])
    a_ref[...] = (yf_ref[slot, j] * jax.nn.sigmoid(gl) * _silu(z_ref[...].astype(F32))).astype(a_ref.dtype)


def _glu(yt, z, w_glu, b_glu, tn):
    d, m = yt.shape
    tm = LANE_TILE
    ni = m // tm
    nj = d // tn
    return pl.pallas_call(
        _glu_kernel,
        grid=(ni, nj),
        in_specs=[pl.BlockSpec((d, tm), lambda i, j: (0, 0), pipeline_mode=pl.Buffered(1)),
                  pl.BlockSpec((tn, tm), lambda i, j: (j, jnp.minimum(i + 1, ni - 1))),
                  pl.BlockSpec((tm, tn), lambda i, j: (i, j)),
                  pl.BlockSpec((d, tn), lambda i, j: (0, j)),
                  pl.BlockSpec((1, tn), lambda i, j: (0, j))],
        out_specs=pl.BlockSpec((tm, tn), lambda i, j: (i, j)),
        out_shape=jax.ShapeDtypeStruct((m, d), BF16),
        scratch_shapes=[pltpu.VMEM((2, nj, tm, tn), F32), pltpu.VMEM((2, nj, tm, tn), BF16)],
        compiler_params=_cparams("arbitrary", "arbitrary"),
        name="glu",
    )(yt, yt, z, w_glu, b_glu.reshape(1, d))


def _out_norm_kernel(x_hbm, a_ref, gate_ref, fg_ref, w_ref, o_hbm, xbuf, xsem, obuf, osem, *, n_cond):
    nq = x_hbm.shape[0]
    x = _fetch_slots(x_hbm, xbuf, xsem)
    xn = x + _cond_rows(gate_ref, x.shape[0], nq, n_cond) * _dot(a_ref[...], w_ref[...])
    res = xn * lax.rsqrt(jnp.mean(xn * xn, axis=-1, keepdims=True) + EPS) * fg_ref[...]
    _store_slots(res, o_hbm, obuf, osem)


def _out_norm(a, x4, mod3, cond0, n_cond, final_g, w_out):
    nq, m, _, d = x4.shape
    _slots_per_step(x4)
    return pl.pallas_call(
        functools.partial(_out_norm_kernel, n_cond=n_cond),
        grid=(nq * m * S5_CHUNK // LANE_TILE,),
        in_specs=[pl.BlockSpec(memory_space=pl.ANY),
                  pl.BlockSpec((LANE_TILE, d), lambda l: (l, 0)),
                  pl.BlockSpec((n_cond, 1, d), lambda l: (cond0 // n_cond, 0, 2)),
                  pl.BlockSpec((1, d), lambda l: (0, 0)),
                  pl.BlockSpec((d, d), lambda l: (0, 0), pipeline_mode=pl.Buffered(1))],
        out_specs=pl.BlockSpec(memory_space=pl.ANY),
        out_shape=jax.ShapeDtypeStruct(x4.shape, F32),
        scratch_shapes=2 * _slot_scratch(d),
        compiler_params=_cparams("arbitrary"),
        name="out_norm",
    )(x4, a, mod3, final_g.reshape(1, d), w_out)


def _both_dirs(a):
    _, g, r, p = a.shape
    return a.transpose(1, 2, 0, 3).reshape(g, r, 2 * p)


def kernel(x_prompt, x_sample, cache_l0_k, cache_l0_v, state_l1_s5, c, c_ctx, l0_norm_g, l0_w_ada, l0_b_ada, l0_w_in, l0_rpb, l0_w_out, l1_norm_g, l1_w_ada, l1_b_ada, l1_w_in, l1_a_re, l1_a_im, l1_log_dt, l1_b_re, l1_b_im, l1_c_re, l1_c_im, l1_d, l1_w_glu, l1_b_glu, l1_w_out, final_norm_g):
    bp, seq, d = x_prompt.shape
    bs, n_tok, _ = x_sample.shape
    lc = cache_l0_k.shape[1]
    g = d // S5_CH
    p = S5_P
    xp = x_prompt.reshape(bp * seq, d)
    xs = x_sample.reshape(bs * n_tok, d)

    ctx = bs
    cond8 = jnp.zeros((8, d), F32).at[0:bs].set(c).at[ctx].set(c_ctx)
    mod0 = _ada(cond8, l0_w_ada, l0_b_ada).reshape(8, 1, 3 * d)
    mod1 = _ada(cond8, l1_w_ada, l1_b_ada).reshape(8, 1, 3 * d)

    w_in0 = l0_w_in.astype(BF16)
    w_out0 = l0_w_out.astype(BF16)
    qp, kp, kpb, vp, vpb, zp = _inproj(xp, l0_norm_g, mod0, ctx, bp * seq, w_in0,
                                       ((0, BF16), (1, F32), (1, BF16), (2, F32), (2, BF16), (3, BF16)), 1024, 256)
    qs, ks, vs, zs = _inproj(xs, l0_norm_g, mod0, 0, n_tok, w_in0,
                             ((0, BF16), (1, BF16), (2, BF16), (3, BF16)), 1024, 256)
    op = _ctx_attn(qp, kpb, vpb, seq)
    rpb_pad = jnp.zeros((N_HEADS, 16, LANES), F32).at[:, :RPB_R, :RPB_C].set(l0_rpb)
    os_ = _latent_na(qs, ks, vs, cache_l0_k.reshape(bs, lc, d), cache_l0_v.reshape(bs, lc, d), rpb_pad, n_tok)
    n_seg_s = max(1, n_tok // S5_CHUNK // S5_SEG)
    x1p = _gated_out(op, zp, xp, mod0, ctx, bp * seq, w_out0, 512).reshape(bp, seq // S5_CHUNK, S5_CHUNK, d)
    x1s = _gated_out(os_, zs, xs, mod0, 0, n_tok, w_out0, 512).reshape(bs * n_seg_s, -1, S5_CHUNK, d)

    w_in1 = l1_w_in.astype(BF16)
    utp, zp1 = _inproj_t(x1p, l1_norm_g, mod1, ctx, 1, w_in1)
    uts, zs1 = _inproj_t(x1s, l1_norm_g, mod1, 0, bs, w_in1)
    h0 = state_l1_s5.transpose(3, 0, 2, 1, 4).reshape(g, bs, 4 * p)
    log_dt = jnp.broadcast_to(l1_log_dt[:, :, None, None], (2, g, 1, p))
    ytp, yts, st = _s5(
        utp, uts, h0,
        _both_dirs(l1_a_re[:, :, None, :]), _both_dirs(l1_a_im[:, :, None, :]), _both_dirs(log_dt),
        _both_dirs(l1_b_re.transpose(0, 1, 3, 2)), _both_dirs(l1_b_im.transpose(0, 1, 3, 2)),
        _both_dirs(l1_c_re), _both_dirs(l1_c_im), l1_d, bp, bs, n_seg_s)
    w_glu = l1_w_glu.astype(BF16)
    w_out1 = l1_w_out.astype(BF16)
    y_prompt = _out_norm(_glu(ytp, zp1, w_glu, l1_b_glu, 512), x1p, mod1, ctx, 1, final_norm_g, w_out1)
    y_sample = _out_norm(_glu(yts, zs1, w_glu, l1_b_glu, 512), x1s, mod1, 0, bs, final_norm_g, w_out1)
    new_state = st.reshape(g, bp, 2, 2, p).transpose(1, 3, 2, 0, 4)
    return (y_prompt.reshape(bp, seq, d), y_sample.reshape(bs, n_tok, d),
            kp.reshape(bp, seq, N_HEADS, HEAD_DIM), vp.reshape(bp, seq, N_HEADS, HEAD_DIM), new_state)
```

```python
import functools

import jax
import jax.numpy as jnp
from jax import lax
from jax.experimental import pallas as pl
from jax.experimental.pallas import tpu as pltpu

F32 = jnp.float32
BF16 = jnp.bfloat16

EPS = 1e-6
N_HEADS = 16
HEAD_DIM = 128
GRID_W = 64
NA_ROWS = 8
NA_COLS = 16
RPB_R = 2 * NA_ROWS - 1
RPB_C = 2 * NA_COLS - 1
S5_CH = 16
S5_P = 64
S5_CHUNK = 16
S5_SEG = 8
NEG = -1e30
LANES = 128
VMEM_LIMIT = 56 * 1024 * 1024


def _cparams(*sem):
    return pltpu.CompilerParams(dimension_semantics=sem, vmem_limit_bytes=VMEM_LIMIT)


def _dot(a, b):
    return jnp.dot(a, b, preferred_element_type=F32)


def _dot_nt(a, b, precision=None):
    return lax.dot_general(a, b, (((1,), (1,)), ((), ())), preferred_element_type=F32,
                           precision=precision)


def _silu(x):
    return x * jax.nn.sigmoid(x)


def _ada_kernel(cond_ref, w_ref, b_ref, o_ref):
    s = _silu(cond_ref[...])
    o_ref[...] = _dot(s.astype(BF16), w_ref[...].astype(BF16)) + b_ref[...]


def _ada(cond8, w_ada, b_ada):
    d, n = w_ada.shape
    tn = 1024
    return pl.pallas_call(
        _ada_kernel,
        grid=(n // tn,),
        in_specs=[pl.BlockSpec((8, d), lambda j: (0, 0)),
                  pl.BlockSpec((d, tn), lambda j: (0, j)),
                  pl.BlockSpec((1, tn), lambda j: (0, j))],
        out_specs=pl.BlockSpec((8, tn), lambda j: (0, j)),
        out_shape=jax.ShapeDtypeStruct((8, n), F32),
        compiler_params=_cparams("parallel"),
        name="ada",
    )(cond8, w_ada, b_ada.reshape(1, n))


def _norm_mod(x, g, shift, scale):
    y = x * lax.rsqrt(jnp.mean(x * x, axis=-1, keepdims=True) + EPS) * g
    return (y * (1.0 + scale) + shift).astype(BF16)


def _inproj_kernel(x0_ref, xn_ref, g_ref, shift0_ref, scale0_ref, shiftn_ref, scalen_ref, *rest, out_split):
    n_split = max(out_split) + 1
    w_refs = rest[:n_split]
    o_refs = rest[n_split:-1]
    h_ref = rest[-1]
    i = pl.program_id(0)
    j = pl.program_id(1)
    slot = i % 2

    @pl.when((i == 0) & (j == 0))
    def _():
        h_ref[0] = _norm_mod(x0_ref[...], g_ref[...], shift0_ref[...], scale0_ref[...])

    h = h_ref[slot]
    for s, w_ref in enumerate(w_refs):
        r = _dot(h, w_ref[...])
        for o_ref, o_s in zip(o_refs, out_split):
            if o_s == s:
                o_ref[...] = r.astype(o_ref.dtype)
    rs = xn_ref.shape[0]
    h_ref[1 - slot, pl.ds(pl.multiple_of(j * rs, rs), rs), :] = _norm_mod(
        xn_ref[...], g_ref[...], shiftn_ref[...], scalen_ref[...])


def _inproj(x, g, mod3, cond0, rows_per_cond, w, outs, tm, tn):
    m, d = x.shape
    out_split = tuple(s for s, _ in outs)
    out_dtypes = [dt for _, dt in outs]
    n_split = max(out_split) + 1
    ni = m // tm
    nj = d // tn

    def nxt(i):
        return jnp.minimum(i + 1, ni - 1)

    def cond_of(i):
        return cond0 + (i * tm) // rows_per_cond

    in_specs = [pl.BlockSpec((tm, d), lambda i, j: (0, 0), pipeline_mode=pl.Buffered(1)),
                pl.BlockSpec((tm // nj, d), lambda i, j: (nxt(i) * nj + j, 0)),
                pl.BlockSpec((1, d), lambda i, j: (0, 0)),
                pl.BlockSpec((None, 1, d), lambda i, j: (cond_of(0), 0, 0)),
                pl.BlockSpec((None, 1, d), lambda i, j: (cond_of(0), 0, 1)),
                pl.BlockSpec((None, 1, d), lambda i, j: (cond_of(nxt(i)), 0, 0)),
                pl.BlockSpec((None, 1, d), lambda i, j: (cond_of(nxt(i)), 0, 1))]
    for s in range(n_split):
        in_specs.append(pl.BlockSpec((d, tn), lambda i, j, s=s: (0, s * nj + j)))
    return pl.pallas_call(
        functools.partial(_inproj_kernel, out_split=out_split),
        grid=(ni, nj),
        in_specs=in_specs,
        out_specs=[pl.BlockSpec((tm, tn), lambda i, j: (i, j)) for _ in outs],
        out_shape=[jax.ShapeDtypeStruct((m, d), dt) for dt in out_dtypes],
        scratch_shapes=[pltpu.VMEM((2, tm, d), BF16)],
        compiler_params=_cparams("arbitrary", "arbitrary"),
        name="inproj",
    )(x, x, g.reshape(1, d), mod3, mod3, mod3, mod3, *([w] * n_split))


def _softmax_pv(parts):
    m = parts[0][0].max(axis=-1, keepdims=True)
    for s, _ in parts[1:]:
        m = jnp.maximum(m, s.max(axis=-1, keepdims=True))
    l = None
    o = None
    for s, v in parts:
        e = jnp.exp(s - m)
        li = e.sum(axis=-1, keepdims=True)
        oi = _dot(e.astype(BF16), v)
        l = li if l is None else l + li
        o = oi if o is None else o + oi
    return o / l


def _ctx_attn_kernel(q_ref, k_ref, v_ref, o_ref):
    scale = HEAD_DIM ** -0.5
    for h in range(N_HEADS):
        sl = slice(h * HEAD_DIM, (h + 1) * HEAD_DIM)
        s = _dot_nt(q_ref[:, sl], k_ref[:, sl]) * scale
        o_ref[:, sl] = _softmax_pv([(s, v_ref[:, sl])]).astype(o_ref.dtype)


def _ctx_attn(q, k, v, seq):
    m, d = q.shape
    spec = pl.BlockSpec((seq, d), lambda b: (b, 0))
    return pl.pallas_call(
        _ctx_attn_kernel,
        grid=(m // seq,),
        in_specs=[spec, spec, spec],
        out_specs=spec,
        out_shape=jax.ShapeDtypeStruct((m, d), BF16),
        compiler_params=_cparams("parallel"),
        name="ctx_attn",
    )(q, k, v)


def _na_window_start(r, rows):
    return min(max(r - NA_ROWS // 2, 0), rows - NA_ROWS)


def _na_kernel(q_ref, k_ref, v_ref, kc_ref, vc_ref, rpb_ref, o_ref, *, rows, q_rows):
    scale = HEAD_DIM ** -0.5
    w = GRID_W
    qc = lax.broadcasted_iota(jnp.int32, (w, LANES), 0)
    lane = lax.broadcasted_iota(jnp.int32, (w, LANES), 1)
    c0 = jnp.clip(qc - NA_COLS // 2, 0, w - NA_COLS)
    ok_l = (lane >= c0) & (lane < c0 + NA_COLS)
    ok_r = (lane - w >= c0) & (lane - w < c0 + NA_COLS)
    neg_tile = jnp.full((w, LANES), NEG, F32)
    tile_l, tile_r = [], []
    for d in range(RPB_R):
        base = jnp.broadcast_to(rpb_ref[d:d + 1, :], (w, LANES))
        left = pltpu.roll(base, LANES - (NA_COLS - 1), 1, stride=1, stride_axis=0)
        right = pltpu.roll(base, w - (NA_COLS - 1), 1, stride=1, stride_axis=0)
        tile_l.append(jnp.where(ok_l, left, NEG))
        tile_r.append(jnp.where(ok_r, right, NEG))

    kc = kc_ref[...].astype(BF16)
    vc = vc_ref[...].astype(BF16)
    n_groups = rows // q_rows
    for gi in range(n_groups):
        rs = list(range(gi * q_rows, (gi + 1) * q_rows))
        klo = min(_na_window_start(r, rows) for r in rs) // 2 * 2
        khi = -(-(max(_na_window_start(r, rows) for r in rs) + NA_ROWS) // 2) * 2
        bias_rows = []
        for r in rs:
            r0 = _na_window_start(r, rows)
            tiles = []
            for kr in range(klo, khi, 2):
                ok0 = r0 <= kr < r0 + NA_ROWS
                ok1 = r0 <= kr + 1 < r0 + NA_ROWS
                t0 = tile_l[kr - r + NA_ROWS - 1] if ok0 else neg_tile
                t1 = tile_r[kr + 1 - r + NA_ROWS - 1] if ok1 else neg_tile
                tiles.append(jnp.maximum(t0, t1) if (ok0 or ok1) else neg_tile)
            bias_rows.append(jnp.concatenate(tiles, axis=1))
        bias = jnp.concatenate(bias_rows, axis=0)
        q = q_ref[gi * q_rows * w:(gi + 1) * q_rows * w, :]
        kl = k_ref[klo * w:khi * w, :]
        vl = v_ref[klo * w:khi * w, :]
        s_loc = _dot_nt(q, kl) * scale + bias
        s_ctx = _dot_nt(q, kc) * scale
        o = _softmax_pv([(s_loc, vl), (s_ctx, vc)])
        o_ref[gi * q_rows * w:(gi + 1) * q_rows * w, :] = o.astype(o_ref.dtype)


def _latent_na(q, k, v, k_ctx, v_ctx, rpb_pad, n_tok):
    m, d = q.shape
    nb = m // n_tok
    lc = k_ctx.shape[1]
    rows = n_tok // GRID_W
    spec = pl.BlockSpec((n_tok, HEAD_DIM), lambda b, h: (b, h))
    cspec = pl.BlockSpec((None, lc, HEAD_DIM), lambda b, h: (b, 0, h))
    return pl.pallas_call(
        functools.partial(_na_kernel, rows=rows, q_rows=4),
        grid=(nb, N_HEADS),
        in_specs=[spec, spec, spec, cspec, cspec,
                  pl.BlockSpec((None, 16, LANES), lambda b, h: (h, 0, 0))],
        out_specs=spec,
        out_shape=jax.ShapeDtypeStruct((m, d), BF16),
        compiler_params=_cparams("parallel", "parallel"),
        name="latent_na",
    )(q, k, v, k_ctx, v_ctx, rpb_pad)


def _gated_out_kernel(o_ref, z_ref, x_ref, gate_ref, w_ref, y_ref):
    a = o_ref[...].astype(F32) * _silu(z_ref[...].astype(F32))
    y_ref[...] = x_ref[...] + gate_ref[...] * _dot(a.astype(BF16), w_ref[...])


def _gated_out(o, z, x, mod3, cond0, rows_per_cond, w_out, tm):
    m, d = x.shape
    row = pl.BlockSpec((tm, d), lambda i: (i, 0))
    return pl.pallas_call(
        _gated_out_kernel,
        grid=(m // tm,),
        in_specs=[row, row, row,
                  pl.BlockSpec((None, 1, d), lambda i: (cond0 + (i * tm) // rows_per_cond, 0, 2)),
                  pl.BlockSpec((d, d), lambda i: (0, 0), pipeline_mode=pl.Buffered(1))],
        out_specs=row,
        out_shape=jax.ShapeDtypeStruct((m, d), F32),
        compiler_params=_cparams("parallel"),
        name="gated_out",
    )(o, z, x, mod3, w_out)


LANE_TILE = 512


def _slots_per_step(x4):
    nq, m, _, _ = x4.shape
    assert LANE_TILE % (nq * m) == 0 and S5_CHUNK % (LANE_TILE // (nq * m)) == 0
    return LANE_TILE // (nq * m)


def _slot_scratch(d):
    return [pltpu.VMEM((2, LANE_TILE, d), F32), pltpu.SemaphoreType.DMA((2,))]


def _slot_copies(x_hbm, buf, sem, l, to_hbm):
    nq, m, _, _ = x_hbm.shape
    n_x = buf.shape[1] // (nq * m)
    copies = []
    for xi in range(n_x):
        for k in range(m):
            hbm = x_hbm.at[:, k, l * n_x + xi, :]
            vmem = buf.at[l % 2, pl.ds((xi * m + k) * nq, nq), :]
            src, dst = (vmem, hbm) if to_hbm else (hbm, vmem)
            copies.append(pltpu.make_async_copy(src, dst, sem.at[l % 2]))
    return copies


def _fetch_slots(x_hbm, buf, sem):
    l = pl.program_id(0)

    @pl.when(l == 0)
    def _():
        for c in _slot_copies(x_hbm, buf, sem, l, False):
            c.start()

    @pl.when(l + 1 < pl.num_programs(0))
    def _():
        for c in _slot_copies(x_hbm, buf, sem, l + 1, False):
            c.start()

    for c in _slot_copies(x_hbm, buf, sem, l, False):
        c.wait()
    return buf[l % 2]


def _store_slots(res, o_hbm, buf, sem):
    l = pl.program_id(0)

    def wait(ll):
        for c in _slot_copies(o_hbm, buf, sem, ll, True):
            c.wait()

    @pl.when(l >= 2)
    def _():
        wait(l - 2)

    buf[l % 2] = res
    for c in _slot_copies(o_hbm, buf, sem, l, True):
        c.start()

    @pl.when(l == pl.num_programs(0) - 1)
    def _():
        @pl.when(l >= 1)
        def _():
            wait(l - 1)

        wait(l)


def _cond_rows(ref, n_rows, nq, n_cond):
    out = ref[n_cond - 1]
    if n_cond > 1:
        b = (lax.broadcasted_iota(jnp.int32, (n_rows, 1), 0) % nq) // (nq // n_cond)
        for i in range(n_cond - 2, -1, -1):
            out = jnp.where(b == i, ref[i], out)
    return out


def _inproj_t_kernel(x_hbm, g_ref, shift_ref, scale_ref, wu_ref, wz_ref, u_ref, z_ref, buf, sem, h_ref,
                     *, n_cond, n_steps):
    nq = x_hbm.shape[0]
    l = pl.program_id(0)

    def copies(ll):
        return _slot_copies(x_hbm, buf, sem, ll, False)

    def norm(x):
        n_rows = x.shape[0]
        return _norm_mod(x, g_ref[...], _cond_rows(shift_ref, n_rows, nq, n_cond),
                         _cond_rows(scale_ref, n_rows, nq, n_cond))

    @pl.when(l == 0)
    def _():
        for ll in range(min(2, n_steps)):
            for c in copies(ll):
                c.start()
        for c in copies(0):
            c.wait()
        h_ref[0] = norm(buf[0])

    @pl.when(l + 1 < n_steps)
    def _():
        for c in copies(l + 1):
            c.wait()

    @pl.when(l + 2 < n_steps)
    def _():
        for c in copies(l + 2):
            c.start()

    h = h_ref[l % 2]
    u_ref[...] = _dot(h, wu_ref[...]).T.astype(u_ref.dtype)
    z_ref[...] = _dot(h, wz_ref[...]).astype(z_ref.dtype)
    h_ref[(l + 1) % 2] = norm(buf[(l + 1) % 2])


def _inproj_t(x4, g, mod3, cond0, n_cond, w_in):
    nq, m, _, d = x4.shape
    n_tok = nq * m * S5_CHUNK
    _slots_per_step(x4)

    def mspec(col):
        return pl.BlockSpec((n_cond, 1, d), lambda l: (cond0 // n_cond, 0, col))

    def wspec(col):
        return pl.BlockSpec((d, d), lambda l: (0, col), pipeline_mode=pl.Buffered(1))

    return pl.pallas_call(
        functools.partial(_inproj_t_kernel, n_cond=n_cond, n_steps=n_tok // LANE_TILE),
        grid=(n_tok // LANE_TILE,),
        in_specs=[pl.BlockSpec(memory_space=pl.ANY),
                  pl.BlockSpec((1, d), lambda l: (0, 0)), mspec(0), mspec(1), wspec(0), wspec(1)],
        out_specs=[pl.BlockSpec((d, LANE_TILE), lambda l: (0, l)), pl.BlockSpec((LANE_TILE, d), lambda l: (l, 0))],
        out_shape=[jax.ShapeDtypeStruct((d, n_tok), BF16), jax.ShapeDtypeStruct((n_tok, d), BF16)],
        scratch_shapes=_slot_scratch(d) + [pltpu.VMEM((2, LANE_TILE, d), BF16)],
        compiler_params=_cparams("arbitrary"),
        name="inproj_t",
    )(x4, g.reshape(1, d), mod3, mod3, w_in, w_in)


def _cmul(ar, ai, br, bi):
    return ar * br - ai * bi, ar * bi + ai * br


def _split_bf16(a):
    hi = a.astype(BF16)
    return hi, (a - hi.astype(F32)).astype(BF16)


def _s5_powers(ar, ai):
    width = ar.shape[-1]
    pw = [(jnp.ones_like(ar), jnp.zeros_like(ar))]
    for _ in range(S5_CHUNK):
        pw.append(_cmul(pw[-1][0], pw[-1][1], ar, ai))
    is_fwd = lax.broadcasted_iota(jnp.int32, (1, width), 1) < width // 2

    def pattern(fwd_ascending):
        out = []
        for part in range(2):
            blocks = []
            for j in range(S5_CHUNK):
                ef, eb = (j, S5_CHUNK - 1 - j) if fwd_ascending else (S5_CHUNK - 1 - j, j)
                blocks.append(jnp.broadcast_to(jnp.where(is_fwd, pw[ef][part], pw[eb][part]), (S5_CH, width)))
            out.append(jnp.concatenate(blocks, axis=0))
        return out

    return pattern(True), pattern(False), pw[S5_CHUNK]


S5_GROUPS_PER_STEP = 4


def _s5_kernel(xp_ref, xs_ref, *rest, nb_p, nb_s, n_seg_s):
    yp_ref, ys_ref = rest[9], rest[10]
    for gi in range(S5_GROUPS_PER_STEP):
        ch = pl.ds(gi * S5_CH, S5_CH)
        _s5_group(xp_ref.at[ch], xs_ref.at[ch], *[r.at[gi] for r in rest[:9]], yp_ref.at[ch], ys_ref.at[ch],
                  *[r.at[gi] for r in rest[11:]], nb_p=nb_p, nb_s=nb_s, n_seg_s=n_seg_s)


def _s5_group(xp_ref, xs_ref, h0_ref, are_ref, aim_ref, ldt_ref, bre_ref, bim_ref, cre_ref, cim_ref, dsk_ref,
              yp_ref, ys_ref, st_ref, w_ref, ws_ref, wc_ref, s_ref, f_ref, *, nb_p, nb_s, n_seg_s):
    p = S5_P
    kc = S5_CHUNK * S5_CH
    a_re = are_ref[...]
    a_im = aim_ref[...]
    dt = jnp.exp(ldt_ref[...])
    mag = jnp.exp(a_re * dt)
    ab_re = mag * jnp.cos(a_im * dt)
    ab_im = mag * jnp.sin(a_im * dt)
    den = a_re * a_re + a_im * a_im
    nr = ab_re - 1.0
    f_re = (nr * a_re + ab_im * a_im) / den
    f_im = (ab_im * a_re - nr * a_im) / den
    bbt_re, bbt_im = _cmul(f_re, f_im, bre_ref[...], bim_ref[...])

    (pg_r, pg_i), (ps_r, ps_i), (ac_r, ac_i) = _s5_powers(ab_re, ab_im)
    c_re = jnp.concatenate([cre_ref[...]] * S5_CHUNK, axis=0)
    c_im = jnp.concatenate([cim_ref[...]] * S5_CHUNK, axis=0)
    bt_re = jnp.concatenate([bbt_re] * S5_CHUNK, axis=0)
    bt_im = jnp.concatenate([bbt_im] * S5_CHUNK, axis=0)

    g_re, g_im = _cmul(pg_r, pg_i, c_re, c_im)
    fwd16 = lax.broadcasted_iota(jnp.int32, (S5_CH, 2 * p), 1) < p
    g_hi, g_lo = _split_bf16(jnp.concatenate([g_re, g_im], axis=1))
    r0 = []
    for d in range(2):
        msk = fwd16 if d == 0 else jnp.logical_not(fwd16)
        b_hi, b_lo = _split_bf16(
            jnp.concatenate([jnp.where(msk, bbt_re, 0.0), jnp.where(msk, -bbt_im, 0.0)], axis=1))
        r0.append(_dot_nt(b_hi, g_hi) + (_dot_nt(b_hi, g_lo) + _dot_nt(b_lo, g_hi)))
    lane = lax.broadcasted_iota(jnp.int32, (S5_CH, kc), 1)
    for s in range(S5_CHUNK):
        f = jnp.where(lane >= S5_CH * s, pltpu.roll(r0[0], S5_CH * s, 1), 0.0) if s else r0[0]
        b = jnp.where(lane < S5_CH * (s + 1), pltpu.roll(r0[1], (S5_CH * (s + 1)) % kc, 1), 0.0)
        w_ref[s * S5_CH:(s + 1) * S5_CH, :] = (f + b).astype(BF16)
    e_re, e_im = _cmul(ps_r, ps_i, bt_re, bt_im)
    ws_ref[:, 0:2 * p] = e_re.astype(BF16)
    ws_ref[:, 2 * p:4 * p] = e_im.astype(BF16)
    g1_re, g1_im = _cmul(g_re, g_im, ab_re, ab_im)
    wc_ref[:, 0:2 * p] = g1_re.astype(BF16)
    wc_ref[:, 2 * p:4 * p] = (-g1_im).astype(BF16)

    def run(xt_ref, yt_ref, nb, n_seg, hr, hm):
        rows = xt_ref.shape[1] // S5_CHUNK
        nbx = nb * n_seg
        m = rows // nbx
        xt = jnp.concatenate([xt_ref[:, s * rows:(s + 1) * rows] for s in range(S5_CHUNK)], axis=0)
        xf = xt.astype(F32).T
        x = xf.astype(BF16)
        y = _dot(x, w_ref[...]) + dsk_ref[...] * xf
        s_all = _dot(x, ws_ref[...])
        s_ref[0, 0:rows, :] = s_all[:, 0:2 * p]
        s_ref[1, 0:rows, :] = s_all[:, 2 * p:4 * p]

        def step(ar, ai, hr, hm, plane_r, plane_i, rf, rb):
            isf = lax.broadcasted_iota(jnp.int32, hr.shape, 1) < p
            sr = jnp.where(isf, plane_r[rf, :], plane_r[rb, :])
            sm = jnp.where(isf, plane_i[rf, :], plane_i[rb, :])
            return ar * hr - ai * hm + sr, ar * hm + ai * hr + sm

        def scan(hr, hm, store):
            arb = jnp.broadcast_to(ac_r, (nbx, 2 * p))
            aib = jnp.broadcast_to(ac_i, (nbx, 2 * p))
            for k in range(m):
                rf = pl.ds(k * nbx, nbx)
                rb = pl.ds((m - 1 - k) * nbx, nbx)
                if store:
                    s_ref[2, rf, :] = hr
                    s_ref[3, rf, :] = hm
                    s_ref[4, rb, :] = hr
                    s_ref[5, rb, :] = hm
                hr, hm = step(arb, aib, hr, hm, s_ref.at[0], s_ref.at[1], rf, rb)
            return hr, hm

        if n_seg > 1:
            zeros = jnp.zeros((nbx, 2 * p), F32)
            f_ref[0], f_ref[1] = scan(zeros, zeros, False)
            sr_, si_ = ac_r, ac_i
            for _ in range(m - 1):
                sr_, si_ = _cmul(sr_, si_, ac_r, ac_i)
            sr_ = jnp.broadcast_to(sr_, (nb, 2 * p))
            si_ = jnp.broadcast_to(si_, (nb, 2 * p))
            for j in range(n_seg):
                rf = pl.ds(j, nb, stride=n_seg)
                rb = pl.ds(n_seg - 1 - j, nb, stride=n_seg)
                f_ref[2, rf, :] = hr
                f_ref[3, rf, :] = hm
                f_ref[4, rb, :] = hr
                f_ref[5, rb, :] = hm
                hr, hm = step(sr_, si_, hr, hm, f_ref.at[0], f_ref.at[1], rf, rb)
            isf = lax.broadcasted_iota(jnp.int32, (nbx, 2 * p), 1) < p
            hr = jnp.where(isf, f_ref[2], f_ref[4])
            hm = jnp.where(isf, f_ref[3], f_ref[5])
        hr, hm = scan(hr, hm, True)
        isf_rows = lax.broadcasted_iota(jnp.int32, (rows, 2 * p), 1) < p
        h_prev = jnp.concatenate([jnp.where(isf_rows, s_ref[2, 0:rows, :], s_ref[4, 0:rows, :]),
                                  jnp.where(isf_rows, s_ref[3, 0:rows, :], s_ref[5, 0:rows, :])], axis=1)
        yt = (y + _dot_nt(h_prev.astype(BF16), wc_ref[...])).T
        for t in range(S5_CHUNK):
            yt_ref[:, t * rows:(t + 1) * rows] = yt[t * S5_CH:(t + 1) * S5_CH, :]
        return hr, hm

    zeros = jnp.zeros((nb_p, 2 * p), F32)
    hr, hm = run(xp_ref, yp_ref, nb_p, 1, zeros, zeros)
    st_ref[:, 0:2 * p] = hr
    st_ref[:, 2 * p:4 * p] = hm
    run(xs_ref, ys_ref, nb_s, n_seg_s, h0_ref[:, 0:2 * p], h0_ref[:, 2 * p:4 * p])


def _s5(utp, uts, h0, a_re, a_im, log_dt, bt_re, bt_im, c_re, c_im, d_skip, nb_p, nb_s, n_seg_s):
    d, lanes_p = utp.shape
    lanes_s = uts.shape[1]
    g = d // S5_CH
    rows_p = lanes_p // S5_CHUNK
    rows_s = lanes_s // S5_CHUNK
    kc = S5_CHUNK * S5_CH
    p = S5_P

    gps = S5_GROUPS_PER_STEP

    def gspec(shape):
        return pl.BlockSpec((gps,) + shape, lambda i: (i, 0, 0))

    def tspec(lanes):
        return pl.BlockSpec((gps * S5_CH, lanes), lambda i: (i, 0))

    return pl.pallas_call(
        functools.partial(_s5_kernel, nb_p=nb_p, nb_s=nb_s, n_seg_s=n_seg_s),
        grid=(g // gps,),
        in_specs=[tspec(lanes_p), tspec(lanes_s), gspec((nb_s, 4 * p)),
                  gspec((1, 2 * p)), gspec((1, 2 * p)), gspec((1, 2 * p)),
                  gspec((S5_CH, 2 * p)), gspec((S5_CH, 2 * p)), gspec((S5_CH, 2 * p)), gspec((S5_CH, 2 * p)),
                  gspec((1, kc))],
        out_specs=[tspec(lanes_p), tspec(lanes_s), gspec((nb_p, 4 * p))],
        out_shape=[jax.ShapeDtypeStruct((d, lanes_p), F32),
                   jax.ShapeDtypeStruct((d, lanes_s), F32),
                   jax.ShapeDtypeStruct((g, nb_p, 4 * p), F32)],
        scratch_shapes=[pltpu.VMEM((gps, kc, kc), BF16), pltpu.VMEM((gps, kc, 4 * p), BF16),
                        pltpu.VMEM((gps, kc, 4 * p), BF16),
                        pltpu.VMEM((gps, 6, max(rows_p, rows_s), 2 * p), F32),
                        pltpu.VMEM((gps, 6, nb_s * n_seg_s, 2 * p), F32)],
        compiler_params=_cparams("parallel"),
        name="s5",
    )(utp, uts, h0, a_re, a_im, log_dt, bt_re, bt_im, c_re, c_im,
      jnp.tile(d_skip.reshape(g, 1, S5_CH), (1, 1, S5_CHUNK)))


def _glu_kernel(yt0_ref, ytn_ref, z_ref, w_ref, b_ref, a_ref, yb_ref):
    _, nj, _, tn = yb_ref.shape
    i = pl.program_id(0)
    j = pl.program_id(1)
    slot = i % 2

    def prepare(yt_block, dst_slot, jj):
        yb_ref[dst_slot, jj] = jax.nn.gelu(yt_block.T).astype(BF16)

    @pl.when((i == 0) & (j == 0))
    def _():
        for jj in range(nj):
            prepare(yt0_ref[jj * tn:(jj + 1) * tn, :], 0, jj)

    gl = b_ref[...]
    for jj in range(nj):
        gl = gl + _dot(yb_ref[slot, jj], w_ref[jj * tn:(jj + 1) * tn, :])
        q = slice(jj * (tn // nj), (jj + 1) * (tn // nj))
        yb_ref[1 - slot, j, :, q] = jax.nn.gelu(ytn_ref[q, :].T).astype(BF16)
    y = yb_ref[slot, j].astype(F32)
    a_ref[...] = (y * jax.nn.sigmoid(gl) * _silu(z_ref[...].astype(F32))).astype(a_ref.dtype)


def _glu(yt, z, w_glu, b_glu, tn):
    d, m = yt.shape
    tm = LANE_TILE
    ni = m // tm
    nj = d // tn
    return pl.pallas_call(
        _glu_kernel,
        grid=(ni, nj),
        in_specs=[pl.BlockSpec((d, tm), lambda i, j: (0, 0), pipeline_mode=pl.Buffered(1)),
                  pl.BlockSpec((tn, tm), lambda i, j: (j, jnp.minimum(i + 1, ni - 1))),
                  pl.BlockSpec((tm, tn), lambda i, j: (i, j)),
                  pl.BlockSpec((d, tn), lambda i, j: (0, j)),
                  pl.BlockSpec((1, tn), lambda i, j: (0, j))],
        out_specs=pl.BlockSpec((tm, tn), lambda i, j: (i, j)),
        out_shape=jax.ShapeDtypeStruct((m, d), BF16),
        scratch_shapes=[pltpu.VMEM((2, nj, tm, tn), BF16)],
        compiler_params=_cparams("arbitrary", "arbitrary"),
        name="glu",
    )(yt, yt, z, w_glu, b_glu.reshape(1, d))


def _out_norm_kernel(x_hbm, a_ref, gate_ref, fg_ref, w_ref, o_hbm, xbuf, xsem, obuf, osem, *, n_cond):
    nq = x_hbm.shape[0]
    x = _fetch_slots(x_hbm, xbuf, xsem)
    xn = x + _cond_rows(gate_ref, x.shape[0], nq, n_cond) * _dot(a_ref[...], w_ref[...])
    res = xn * lax.rsqrt(jnp.mean(xn * xn, axis=-1, keepdims=True) + EPS) * fg_ref[...]
    _store_slots(res, o_hbm, obuf, osem)


def _out_norm(a, x4, mod3, cond0, n_cond, final_g, w_out):
    nq, m, _, d = x4.shape
    _slots_per_step(x4)
    return pl.pallas_call(
        functools.partial(_out_norm_kernel, n_cond=n_cond),
        grid=(nq * m * S5_CHUNK // LANE_TILE,),
        in_specs=[pl.BlockSpec(memory_space=pl.ANY),
                  pl.BlockSpec((LANE_TILE, d), lambda l: (l, 0)),
                  pl.BlockSpec((n_cond, 1, d), lambda l: (cond0 // n_cond, 0, 2)),
                  pl.BlockSpec((1, d), lambda l: (0, 0)),
                  pl.BlockSpec((d, d), lambda l: (0, 0), pipeline_mode=pl.Buffered(1))],
        out_specs=pl.BlockSpec(memory_space=pl.ANY),
        out_shape=jax.ShapeDtypeStruct(x4.shape, F32),
        scratch_shapes=2 * _slot_scratch(d),
        compiler_params=_cparams("arbitrary"),
        name="out_norm",
    )(x4, a, mod3, final_g.reshape(1, d), w_out)


def _both_dirs(a):
    _, g, r, p = a.shape
    return a.transpose(1, 2, 0, 3).reshape(g, r, 2 * p)


def kernel(x_prompt, x_sample, cache_l0_k, cache_l0_v, state_l1_s5, c, c_ctx, l0_norm_g, l0_w_ada, l0_b_ada, l0_w_in, l0_rpb, l0_w_out, l1_norm_g, l1_w_ada, l1_b_ada, l1_w_in, l1_a_re, l1_a_im, l1_log_dt, l1_b_re, l1_b_im, l1_c_re, l1_c_im, l1_d, l1_w_glu, l1_b_glu, l1_w_out, final_norm_g):
    bp, seq, d = x_prompt.shape
    bs, n_tok, _ = x_sample.shape
    lc = cache_l0_k.shape[1]
    g = d // S5_CH
    p = S5_P
    xp = x_prompt.reshape(bp * seq, d)
    xs = x_sample.reshape(bs * n_tok, d)

    ctx = bs
    cond8 = jnp.zeros((8, d), F32).at[0:bs].set(c).at[ctx].set(c_ctx)
    mod0 = _ada(cond8, l0_w_ada, l0_b_ada).reshape(8, 1, 3 * d)
    mod1 = _ada(cond8, l1_w_ada, l1_b_ada).reshape(8, 1, 3 * d)

    w_in0 = l0_w_in.astype(BF16)
    w_out0 = l0_w_out.astype(BF16)
    qp, kp, kpb, vp, vpb, zp = _inproj(xp, l0_norm_g, mod0, ctx, bp * seq, w_in0,
                                       ((0, BF16), (1, F32), (1, BF16), (2, F32), (2, BF16), (3, BF16)), 1024, 256)
    qs, ks, vs, zs = _inproj(xs, l0_norm_g, mod0, 0, n_tok, w_in0,
                             ((0, BF16), (1, BF16), (2, BF16), (3, BF16)), 1024, 256)
    op = _ctx_attn(qp, kpb, vpb, seq)
    rpb_pad = jnp.zeros((N_HEADS, 16, LANES), F32).at[:, :RPB_R, :RPB_C].set(l0_rpb)
    os_ = _latent_na(qs, ks, vs, cache_l0_k.reshape(bs, lc, d), cache_l0_v.reshape(bs, lc, d), rpb_pad, n_tok)
    n_seg_s = max(1, n_tok // S5_CHUNK // S5_SEG)
    x1p = _gated_out(op, zp, xp, mod0, ctx, bp * seq, w_out0, 512).reshape(bp, seq // S5_CHUNK, S5_CHUNK, d)
    x1s = _gated_out(os_, zs, xs, mod0, 0, n_tok, w_out0, 512).reshape(bs * n_seg_s, -1, S5_CHUNK, d)

    w_in1 = l1_w_in.astype(BF16)
    utp, zp1 = _inproj_t(x1p, l1_norm_g, mod1, ctx, 1, w_in1)
    uts, zs1 = _inproj_t(x1s, l1_norm_g, mod1, 0, bs, w_in1)
    h0 = state_l1_s5.transpose(3, 0, 2, 1, 4).reshape(g, bs, 4 * p)
    log_dt = jnp.broadcast_to(l1_log_dt[:, :, None, None], (2, g, 1, p))
    ytp, yts, st = _s5(
        utp, uts, h0,
        _both_dirs(l1_a_re[:, :, None, :]), _both_dirs(l1_a_im[:, :, None, :]), _both_dirs(log_dt),
        _both_dirs(l1_b_re.transpose(0, 1, 3, 2)), _both_dirs(l1_b_im.transpose(0, 1, 3, 2)),
        _both_dirs(l1_c_re), _both_dirs(l1_c_im), l1_d, bp, bs, n_seg_s)
    w_glu = l1_w_glu.astype(BF16)
    w_out1 = l1_w_out.astype(BF16)
    y_prompt = _out_norm(_glu(ytp, zp1, w_glu, l1_b_glu, 512), x1p, mod1, ctx, 1, final_norm_g, w_out1)
    y_sample = _out_norm(_glu(yts, zs1, w_glu, l1_b_glu, 512), x1s, mod1, 0, bs, final_norm_g, w_out1)
    new_state = st.reshape(g, bp, 2, 2, p).transpose(1, 3, 2, 0, 4)
    return (y_prompt.reshape(bp, seq, d), y_sample.reshape(bs, n_tok, d),
            kp.reshape(bp, seq, N_HEADS, HEAD_DIM), vp.reshape(bp, seq, N_HEADS, HEAD_DIM), new_state)
```

```python
import functools

import jax
import jax.numpy as jnp
from jax import lax
from jax.experimental import pallas as pl
from jax.experimental.pallas import tpu as pltpu

F32 = jnp.float32
BF16 = jnp.bfloat16

EPS = 1e-6
N_HEADS = 16
HEAD_DIM = 128
GRID_W = 64
NA_ROWS = 8
NA_COLS = 16
RPB_R = 2 * NA_ROWS - 1
RPB_C = 2 * NA_COLS - 1
S5_CH = 16
S5_P = 64
S5_CHUNK = 16
S5_SEG = 8
NEG = -1e30
LANES = 128
VMEM_LIMIT = 56 * 1024 * 1024


def _cparams(*sem):
    return pltpu.CompilerParams(dimension_semantics=sem, vmem_limit_bytes=VMEM_LIMIT)


def _dot(a, b):
    return jnp.dot(a, b, preferred_element_type=F32)


def _dot_nt(a, b, precision=None):
    return lax.dot_general(a, b, (((1,), (1,)), ((), ())), preferred_element_type=F32,
                           precision=precision)


def _silu(x):
    return x * jax.nn.sigmoid(x)


def _ada_kernel(cond_ref, w_ref, b_ref, o_ref):
    s = _silu(cond_ref[...])
    o_ref[...] = _dot(s.astype(BF16), w_ref[...].astype(BF16)) + b_ref[...]


def _ada(cond8, w_ada, b_ada):
    d, n = w_ada.shape
    tn = 1024
    return pl.pallas_call(
        _ada_kernel,
        grid=(n // tn,),
        in_specs=[pl.BlockSpec((8, d), lambda j: (0, 0)),
                  pl.BlockSpec((d, tn), lambda j: (0, j)),
                  pl.BlockSpec((1, tn), lambda j: (0, j))],
        out_specs=pl.BlockSpec((8, tn), lambda j: (0, j)),
        out_shape=jax.ShapeDtypeStruct((8, n), F32),
        compiler_params=_cparams("parallel"),
        name="ada",
    )(cond8, w_ada, b_ada.reshape(1, n))


def _norm_mod(x, g, shift, scale):
    y = x * lax.rsqrt(jnp.mean(x * x, axis=-1, keepdims=True) + EPS) * g
    return (y * (1.0 + scale) + shift).astype(BF16)


def _inproj_kernel(x0_ref, xn_ref, g_ref, shift0_ref, scale0_ref, shiftn_ref, scalen_ref, *rest, out_split):
    n_split = max(out_split) + 1
    w_refs = rest[:n_split]
    o_refs = rest[n_split:-1]
    h_ref = rest[-1]
    i = pl.program_id(0)
    j = pl.program_id(1)
    slot = i % 2

    @pl.when((i == 0) & (j == 0))
    def _():
        h_ref[0] = _norm_mod(x0_ref[...], g_ref[...], shift0_ref[...], scale0_ref[...])

    h = h_ref[slot]
    for s, w_ref in enumerate(w_refs):
        r = _dot(h, w_ref[...])
        for o_ref, o_s in zip(o_refs, out_split):
            if o_s == s:
                o_ref[...] = r.astype(o_ref.dtype)
    rs = xn_ref.shape[0]
    h_ref[1 - slot, pl.ds(pl.multiple_of(j * rs, rs), rs), :] = _norm_mod(
        xn_ref[...], g_ref[...], shiftn_ref[...], scalen_ref[...])


def _inproj(x, g, mod3, cond0, rows_per_cond, w, outs, tm, tn):
    m, d = x.shape
    out_split = tuple(s for s, _ in outs)
    out_dtypes = [dt for _, dt in outs]
    n_split = max(out_split) + 1
    ni = m // tm
    nj = d // tn

    def nxt(i):
        return jnp.minimum(i + 1, ni - 1)

    def cond_of(i):
        return cond0 + (i * tm) // rows_per_cond

    in_specs = [pl.BlockSpec((tm, d), lambda i, j: (0, 0), pipeline_mode=pl.Buffered(1)),
                pl.BlockSpec((tm // nj, d), lambda i, j: (nxt(i) * nj + j, 0)),
                pl.BlockSpec((1, d), lambda i, j: (0, 0)),
                pl.BlockSpec((None, 1, d), lambda i, j: (cond_of(0), 0, 0)),
                pl.BlockSpec((None, 1, d), lambda i, j: (cond_of(0), 0, 1)),
                pl.BlockSpec((None, 1, d), lambda i, j: (cond_of(nxt(i)), 0, 0)),
                pl.BlockSpec((None, 1, d), lambda i, j: (cond_of(nxt(i)), 0, 1))]
    for s in range(n_split):
        in_specs.append(pl.BlockSpec((d, tn), lambda i, j, s=s: (0, s * nj + j)))
    return pl.pallas_call(
        functools.partial(_inproj_kernel, out_split=out_split),
        grid=(ni, nj),
        in_specs=in_specs,
        out_specs=[pl.BlockSpec((tm, tn), lambda i, j: (i, j)) for _ in outs],
        out_shape=[jax.ShapeDtypeStruct((m, d), dt) for dt in out_dtypes],
        scratch_shapes=[pltpu.VMEM((2, tm, d), BF16)],
        compiler_params=_cparams("arbitrary", "arbitrary"),
        name="inproj",
    )(x, x, g.reshape(1, d), mod3, mod3, mod3, mod3, *([w] * n_split))


def _softmax_pv(parts):
    m = parts[0][0].max(axis=-1, keepdims=True)
    for s, _ in parts[1:]:
        m = jnp.maximum(m, s.max(axis=-1, keepdims=True))
    l = None
    o = None
    for s, v in parts:
        e = jnp.exp(s - m)
        li = e.sum(axis=-1, keepdims=True)
        oi = _dot(e.astype(BF16), v)
        l = li if l is None else l + li
        o = oi if o is None else o + oi
    return o / l


def _ctx_attn_kernel(q_ref, k_ref, v_ref, o_ref):
    scale = HEAD_DIM ** -0.5
    for h in range(N_HEADS):
        sl = slice(h * HEAD_DIM, (h + 1) * HEAD_DIM)
        s = _dot_nt(q_ref[:, sl], k_ref[:, sl]) * scale
        o_ref[:, sl] = _softmax_pv([(s, v_ref[:, sl])]).astype(o_ref.dtype)


def _ctx_attn(q, k, v, seq):
    m, d = q.shape
    spec = pl.BlockSpec((seq, d), lambda b: (b, 0))
    return pl.pallas_call(
        _ctx_attn_kernel,
        grid=(m // seq,),
        in_specs=[spec, spec, spec],
        out_specs=spec,
        out_shape=jax.ShapeDtypeStruct((m, d), BF16),
        compiler_params=_cparams("parallel"),
        name="ctx_attn",
    )(q, k, v)


def _na_window_start(r, rows):
    return min(max(r - NA_ROWS // 2, 0), rows - NA_ROWS)


def _na_kernel(q_ref, k_ref, v_ref, kc_ref, vc_ref, rpb_ref, o_ref, *, rows, q_rows):
    scale = HEAD_DIM ** -0.5
    w = GRID_W
    qc = lax.broadcasted_iota(jnp.int32, (w, LANES), 0)
    lane = lax.broadcasted_iota(jnp.int32, (w, LANES), 1)
    c0 = jnp.clip(qc - NA_COLS // 2, 0, w - NA_COLS)
    ok_l = (lane >= c0) & (lane < c0 + NA_COLS)
    ok_r = (lane - w >= c0) & (lane - w < c0 + NA_COLS)
    neg_tile = jnp.full((w, LANES), NEG, F32)
    tile_l, tile_r = [], []
    for d in range(RPB_R):
        base = jnp.broadcast_to(rpb_ref[d:d + 1, :], (w, LANES))
        left = pltpu.roll(base, LANES - (NA_COLS - 1), 1, stride=1, stride_axis=0)
        right = pltpu.roll(base, w - (NA_COLS - 1), 1, stride=1, stride_axis=0)
        tile_l.append(jnp.where(ok_l, left, NEG))
        tile_r.append(jnp.where(ok_r, right, NEG))

    kc = kc_ref[...].astype(BF16)
    vc = vc_ref[...].astype(BF16)
    n_groups = rows // q_rows
    for gi in range(n_groups):
        rs = list(range(gi * q_rows, (gi + 1) * q_rows))
        klo = min(_na_window_start(r, rows) for r in rs) // 2 * 2
        khi = -(-(max(_na_window_start(r, rows) for r in rs) + NA_ROWS) // 2) * 2
        bias_rows = []
        for r in rs:
            r0 = _na_window_start(r, rows)
            tiles = []
            for kr in range(klo, khi, 2):
                ok0 = r0 <= kr < r0 + NA_ROWS
                ok1 = r0 <= kr + 1 < r0 + NA_ROWS
                t0 = tile_l[kr - r + NA_ROWS - 1] if ok0 else neg_tile
                t1 = tile_r[kr + 1 - r + NA_ROWS - 1] if ok1 else neg_tile
                tiles.append(jnp.maximum(t0, t1) if (ok0 or ok1) else neg_tile)
            bias_rows.append(jnp.concatenate(tiles, axis=1))
        bias = jnp.concatenate(bias_rows, axis=0)
        q = q_ref[gi * q_rows * w:(gi + 1) * q_rows * w, :]
        kl = k_ref[klo * w:khi * w, :]
        vl = v_ref[klo * w:khi * w, :]
        s_loc = _dot_nt(q, kl) * scale + bias
        s_ctx = _dot_nt(q, kc) * scale
        o = _softmax_pv([(s_loc, vl), (s_ctx, vc)])
        o_ref[gi * q_rows * w:(gi + 1) * q_rows * w, :] = o.astype(o_ref.dtype)


def _latent_na(q, k, v, k_ctx, v_ctx, rpb_pad, n_tok):
    m, d = q.shape
    nb = m // n_tok
    lc = k_ctx.shape[1]
    rows = n_tok // GRID_W
    spec = pl.BlockSpec((n_tok, HEAD_DIM), lambda b, h: (b, h))
    cspec = pl.BlockSpec((None, lc, HEAD_DIM), lambda b, h: (b, 0, h))
    return pl.pallas_call(
        functools.partial(_na_kernel, rows=rows, q_rows=4),
        grid=(nb, N_HEADS),
        in_specs=[spec, spec, spec, cspec, cspec,
                  pl.BlockSpec((None, 16, LANES), lambda b, h: (h, 0, 0))],
        out_specs=spec,
        out_shape=jax.ShapeDtypeStruct((m, d), BF16),
        compiler_params=_cparams("parallel", "parallel"),
        name="latent_na",
    )(q, k, v, k_ctx, v_ctx, rpb_pad)


def _gated_out_kernel(o_ref, z_ref, x_ref, gate_ref, w_ref, y_ref):
    a = o_ref[...].astype(F32) * _silu(z_ref[...].astype(F32))
    y_ref[...] = x_ref[...] + gate_ref[...] * _dot(a.astype(BF16), w_ref[...])


def _gated_out(o, z, x, mod3, cond0, rows_per_cond, w_out, tm):
    m, d = x.shape
    row = pl.BlockSpec((tm, d), lambda i: (i, 0))
    return pl.pallas_call(
        _gated_out_kernel,
        grid=(m // tm,),
        in_specs=[row, row, row,
                  pl.BlockSpec((None, 1, d), lambda i: (cond0 + (i * tm) // rows_per_cond, 0, 2)),
                  pl.BlockSpec((d, d), lambda i: (0, 0), pipeline_mode=pl.Buffered(1))],
        out_specs=row,
        out_shape=jax.ShapeDtypeStruct((m, d), F32),
        compiler_params=_cparams("parallel"),
        name="gated_out",
    )(o, z, x, mod3, w_out)


LANE_TILE = 512


def _slots_per_step(x4):
    nq, m, _, _ = x4.shape
    assert LANE_TILE % (nq * m) == 0 and S5_CHUNK % (LANE_TILE // (nq * m)) == 0
    return LANE_TILE // (nq * m)


def _slot_scratch(d):
    return [pltpu.VMEM((2, LANE_TILE, d), F32), pltpu.SemaphoreType.DMA((2,))]


def _slot_copies(x_hbm, buf, sem, l, to_hbm):
    nq, m, _, _ = x_hbm.shape
    n_x = buf.shape[1] // (nq * m)
    copies = []
    for xi in range(n_x):
        for k in range(m):
            hbm = x_hbm.at[:, k, l * n_x + xi, :]
            vmem = buf.at[l % 2, pl.ds((xi * m + k) * nq, nq), :]
            src, dst = (vmem, hbm) if to_hbm else (hbm, vmem)
            copies.append(pltpu.make_async_copy(src, dst, sem.at[l % 2]))
    return copies


def _fetch_slots(x_hbm, buf, sem):
    l = pl.program_id(0)

    @pl.when(l == 0)
    def _():
        for c in _slot_copies(x_hbm, buf, sem, l, False):
            c.start()

    @pl.when(l + 1 < pl.num_programs(0))
    def _():
        for c in _slot_copies(x_hbm, buf, sem, l + 1, False):
            c.start()

    for c in _slot_copies(x_hbm, buf, sem, l, False):
        c.wait()
    return buf[l % 2]


def _store_slots(res, o_hbm, buf, sem):
    l = pl.program_id(0)

    def wait(ll):
        for c in _slot_copies(o_hbm, buf, sem, ll, True):
            c.wait()

    @pl.when(l >= 2)
    def _():
        wait(l - 2)

    buf[l % 2] = res
    for c in _slot_copies(o_hbm, buf, sem, l, True):
        c.start()

    @pl.when(l == pl.num_programs(0) - 1)
    def _():
        @pl.when(l >= 1)
        def _():
            wait(l - 1)

        wait(l)


def _cond_rows(ref, n_rows, nq, n_cond):
    out = ref[n_cond - 1]
    if n_cond > 1:
        b = (lax.broadcasted_iota(jnp.int32, (n_rows, 1), 0) % nq) // (nq // n_cond)
        for i in range(n_cond - 2, -1, -1):
            out = jnp.where(b == i, ref[i], out)
    return out


def _inproj_t_kernel(x_hbm, g_ref, shift_ref, scale_ref, wu_ref, wz_ref, u_ref, z_ref, buf, sem, *, n_cond):
    nq = x_hbm.shape[0]
    x = _fetch_slots(x_hbm, buf, sem)
    n_rows = x.shape[0]
    h = _norm_mod(x, g_ref[...], _cond_rows(shift_ref, n_rows, nq, n_cond),
                  _cond_rows(scale_ref, n_rows, nq, n_cond))
    u_ref[...] = _dot(h, wu_ref[...]).T.astype(u_ref.dtype)
    z_ref[...] = _dot(h, wz_ref[...]).astype(z_ref.dtype)


def _inproj_t(x4, g, mod3, cond0, n_cond, w_in):
    nq, m, _, d = x4.shape
    n_tok = nq * m * S5_CHUNK
    _slots_per_step(x4)

    def mspec(col):
        return pl.BlockSpec((n_cond, 1, d), lambda l: (cond0 // n_cond, 0, col))

    def wspec(col):
        return pl.BlockSpec((d, d), lambda l: (0, col), pipeline_mode=pl.Buffered(1))

    return pl.pallas_call(
        functools.partial(_inproj_t_kernel, n_cond=n_cond),
        grid=(n_tok // LANE_TILE,),
        in_specs=[pl.BlockSpec(memory_space=pl.ANY),
                  pl.BlockSpec((1, d), lambda l: (0, 0)), mspec(0), mspec(1), wspec(0), wspec(1)],
        out_specs=[pl.BlockSpec((d, LANE_TILE), lambda l: (0, l)), pl.BlockSpec((LANE_TILE, d), lambda l: (l, 0))],
        out_shape=[jax.ShapeDtypeStruct((d, n_tok), BF16), jax.ShapeDtypeStruct((n_tok, d), BF16)],
        scratch_shapes=_slot_scratch(d),
        compiler_params=_cparams("arbitrary"),
        name="inproj_t",
    )(x4, g.reshape(1, d), mod3, mod3, w_in, w_in)


def _cmul(ar, ai, br, bi):
    return ar * br - ai * bi, ar * bi + ai * br


def _split_bf16(a):
    hi = a.astype(BF16)
    return hi, (a - hi.astype(F32)).astype(BF16)


def _s5_powers(ar, ai):
    width = ar.shape[-1]
    pw = [(jnp.ones_like(ar), jnp.zeros_like(ar))]
    for _ in range(S5_CHUNK):
        pw.append(_cmul(pw[-1][0], pw[-1][1], ar, ai))
    is_fwd = lax.broadcasted_iota(jnp.int32, (1, width), 1) < width // 2

    def pattern(fwd_ascending):
        out = []
        for part in range(2):
            blocks = []
            for j in range(S5_CHUNK):
                ef, eb = (j, S5_CHUNK - 1 - j) if fwd_ascending else (S5_CHUNK - 1 - j, j)
                blocks.append(jnp.broadcast_to(jnp.where(is_fwd, pw[ef][part], pw[eb][part]), (S5_CH, width)))
            out.append(jnp.concatenate(blocks, axis=0))
        return out

    return pattern(True), pattern(False), pw[S5_CHUNK]


S5_GROUPS_PER_STEP = 4


def _s5_kernel(xp_ref, xs_ref, *rest, nb_p, nb_s, n_seg_s):
    yp_ref, ys_ref = rest[9], rest[10]
    for gi in range(S5_GROUPS_PER_STEP):
        ch = pl.ds(gi * S5_CH, S5_CH)
        _s5_group(xp_ref.at[ch], xs_ref.at[ch], *[r.at[gi] for r in rest[:9]], yp_ref.at[ch], ys_ref.at[ch],
                  *[r.at[gi] for r in rest[11:]], nb_p=nb_p, nb_s=nb_s, n_seg_s=n_seg_s)


def _s5_group(xp_ref, xs_ref, h0_ref, are_ref, aim_ref, ldt_ref, bre_ref, bim_ref, cre_ref, cim_ref, dsk_ref,
              yp_ref, ys_ref, st_ref, w_ref, ws_ref, wc_ref, s_ref, f_ref, *, nb_p, nb_s, n_seg_s):
    p = S5_P
    kc = S5_CHUNK * S5_CH
    a_re = are_ref[...]
    a_im = aim_ref[...]
    dt = jnp.exp(ldt_ref[...])
    mag = jnp.exp(a_re * dt)
    ab_re = mag * jnp.cos(a_im * dt)
    ab_im = mag * jnp.sin(a_im * dt)
    den = a_re * a_re + a_im * a_im
    nr = ab_re - 1.0
    f_re = (nr * a_re + ab_im * a_im) / den
    f_im = (ab_im * a_re - nr * a_im) / den
    bbt_re, bbt_im = _cmul(f_re, f_im, bre_ref[...], bim_ref[...])

    (pg_r, pg_i), (ps_r, ps_i), (ac_r, ac_i) = _s5_powers(ab_re, ab_im)
    c_re = jnp.concatenate([cre_ref[...]] * S5_CHUNK, axis=0)
    c_im = jnp.concatenate([cim_ref[...]] * S5_CHUNK, axis=0)
    bt_re = jnp.concatenate([bbt_re] * S5_CHUNK, axis=0)
    bt_im = jnp.concatenate([bbt_im] * S5_CHUNK, axis=0)

    g_re, g_im = _cmul(pg_r, pg_i, c_re, c_im)
    fwd16 = lax.broadcasted_iota(jnp.int32, (S5_CH, 2 * p), 1) < p
    g_hi, g_lo = _split_bf16(jnp.concatenate([g_re, g_im], axis=1))
    r0 = []
    for d in range(2):
        msk = fwd16 if d == 0 else jnp.logical_not(fwd16)
        b_hi, b_lo = _split_bf16(
            jnp.concatenate([jnp.where(msk, bbt_re, 0.0), jnp.where(msk, -bbt_im, 0.0)], axis=1))
        r0.append(_dot_nt(b_hi, g_hi) + (_dot_nt(b_hi, g_lo) + _dot_nt(b_lo, g_hi)))
    lane = lax.broadcasted_iota(jnp.int32, (S5_CH, kc), 1)
    for s in range(S5_CHUNK):
        f = jnp.where(lane >= S5_CH * s, pltpu.roll(r0[0], S5_CH * s, 1), 0.0) if s else r0[0]
        b = jnp.where(lane < S5_CH * (s + 1), pltpu.roll(r0[1], (S5_CH * (s + 1)) % kc, 1), 0.0)
        w_ref[s * S5_CH:(s + 1) * S5_CH, :] = (f + b).astype(BF16)
    e_re, e_im = _cmul(ps_r, ps_i, bt_re, bt_im)
    ws_ref[:, 0:2 * p] = e_re.astype(BF16)
    ws_ref[:, 2 * p:4 * p] = e_im.astype(BF16)
    g1_re, g1_im = _cmul(g_re, g_im, ab_re, ab_im)
    wc_ref[:, 0:2 * p] = g1_re.astype(BF16)
    wc_ref[:, 2 * p:4 * p] = (-g1_im).astype(BF16)

    def run(xt_ref, yt_ref, nb, n_seg, hr, hm):
        rows = xt_ref.shape[1] // S5_CHUNK
        nbx = nb * n_seg
        m = rows // nbx
        xt = jnp.concatenate([xt_ref[:, s * rows:(s + 1) * rows] for s in range(S5_CHUNK)], axis=0)
        xf = xt.astype(F32).T
        x = xf.astype(BF16)
        y = _dot(x, w_ref[...]) + dsk_ref[...] * xf
        s_all = _dot(x, ws_ref[...])
        s_ref[0, 0:rows, :] = s_all[:, 0:2 * p]
        s_ref[1, 0:rows, :] = s_all[:, 2 * p:4 * p]

        def step(ar, ai, hr, hm, plane_r, plane_i, rf, rb):
            isf = lax.broadcasted_iota(jnp.int32, hr.shape, 1) < p
            sr = jnp.where(isf, plane_r[rf, :], plane_r[rb, :])
            sm = jnp.where(isf, plane_i[rf, :], plane_i[rb, :])
            return ar * hr - ai * hm + sr, ar * hm + ai * hr + sm

        def scan(hr, hm, store):
            arb = jnp.broadcast_to(ac_r, (nbx, 2 * p))
            aib = jnp.broadcast_to(ac_i, (nbx, 2 * p))
            for k in range(m):
                rf = pl.ds(k * nbx, nbx)
                rb = pl.ds((m - 1 - k) * nbx, nbx)
                if store:
                    s_ref[2, rf, :] = hr
                    s_ref[3, rf, :] = hm
                    s_ref[4, rb, :] = hr
                    s_ref[5, rb, :] = hm
                hr, hm = step(arb, aib, hr, hm, s_ref.at[0], s_ref.at[1], rf, rb)
            return hr, hm

        if n_seg > 1:
            zeros = jnp.zeros((nbx, 2 * p), F32)
            f_ref[0], f_ref[1] = scan(zeros, zeros, False)
            sr_, si_ = ac_r, ac_i
            for _ in range(m - 1):
                sr_, si_ = _cmul(sr_, si_, ac_r, ac_i)
            sr_ = jnp.broadcast_to(sr_, (nb, 2 * p))
            si_ = jnp.broadcast_to(si_, (nb, 2 * p))
            for j in range(n_seg):
                rf = pl.ds(j, nb, stride=n_seg)
                rb = pl.ds(n_seg - 1 - j, nb, stride=n_seg)
                f_ref[2, rf, :] = hr
                f_ref[3, rf, :] = hm
                f_ref[4, rb, :] = hr
                f_ref[5, rb, :] = hm
                hr, hm = step(sr_, si_, hr, hm, f_ref.at[0], f_ref.at[1], rf, rb)
            isf = lax.broadcasted_iota(jnp.int32, (nbx, 2 * p), 1) < p
            hr = jnp.where(isf, f_ref[2], f_ref[4])
            hm = jnp.where(isf, f_ref[3], f_ref[5])
        hr, hm = scan(hr, hm, True)
        isf_rows = lax.broadcasted_iota(jnp.int32, (rows, 2 * p), 1) < p
        h_prev = jnp.concatenate([jnp.where(isf_rows, s_ref[2, 0:rows, :], s_ref[4, 0:rows, :]),
                                  jnp.where(isf_rows, s_ref[3, 0:rows, :], s_ref[5, 0:rows, :])], axis=1)
        yt = (y + _dot_nt(h_prev.astype(BF16), wc_ref[...])).T
        for t in range(S5_CHUNK):
            yt_ref[:, t * rows:(t + 1) * rows] = yt[t * S5_CH:(t + 1) * S5_CH, :]
        return hr, hm

    zeros = jnp.zeros((nb_p, 2 * p), F32)
    hr, hm = run(xp_ref, yp_ref, nb_p, 1, zeros, zeros)
    st_ref[:, 0:2 * p] = hr
    st_ref[:, 2 * p:4 * p] = hm
    run(xs_ref, ys_ref, nb_s, n_seg_s, h0_ref[:, 0:2 * p], h0_ref[:, 2 * p:4 * p])


def _s5(utp, uts, h0, a_re, a_im, log_dt, bt_re, bt_im, c_re, c_im, d_skip, nb_p, nb_s, n_seg_s):
    d, lanes_p = utp.shape
    lanes_s = uts.shape[1]
    g = d // S5_CH
    rows_p = lanes_p // S5_CHUNK
    rows_s = lanes_s // S5_CHUNK
    kc = S5_CHUNK * S5_CH
    p = S5_P

    gps = S5_GROUPS_PER_STEP

    def gspec(shape):
        return pl.BlockSpec((gps,) + shape, lambda i: (i, 0, 0))

    def tspec(lanes):
        return pl.BlockSpec((gps * S5_CH, lanes), lambda i: (i, 0))

    return pl.pallas_call(
        functools.partial(_s5_kernel, nb_p=nb_p, nb_s=nb_s, n_seg_s=n_seg_s),
        grid=(g // gps,),
        in_specs=[tspec(lanes_p), tspec(lanes_s), gspec((nb_s, 4 * p)),
                  gspec((1, 2 * p)), gspec((1, 2 * p)), gspec((1, 2 * p)),
                  gspec((S5_CH, 2 * p)), gspec((S5_CH, 2 * p)), gspec((S5_CH, 2 * p)), gspec((S5_CH, 2 * p)),
                  gspec((1, kc))],
        out_specs=[tspec(lanes_p), tspec(lanes_s), gspec((nb_p, 4 * p))],
        out_shape=[jax.ShapeDtypeStruct((d, lanes_p), F32),
                   jax.ShapeDtypeStruct((d, lanes_s), F32),
                   jax.ShapeDtypeStruct((g, nb_p, 4 * p), F32)],
        scratch_shapes=[pltpu.VMEM((gps, kc, kc), BF16), pltpu.VMEM((gps, kc, 4 * p), BF16),
                        pltpu.VMEM((gps, kc, 4 * p), BF16),
                        pltpu.VMEM((gps, 6, max(rows_p, rows_s), 2 * p), F32),
                        pltpu.VMEM((gps, 6, nb_s * n_seg_s, 2 * p), F32)],
        compiler_params=_cparams("parallel"),
        name="s5",
    )(utp, uts, h0, a_re, a_im, log_dt, bt_re, bt_im, c_re, c_im,
      jnp.tile(d_skip.reshape(g, 1, S5_CH), (1, 1, S5_CHUNK)))


def _glu_kernel(yt0_ref, ytn_ref, z_ref, w_ref, b_ref, a_ref, yb_ref, *, tn):
    d = w_ref.shape[0]
    i = pl.program_id(0)
    slot = i % 2

    @pl.when(i == 0)
    def _():
        yb_ref[0] = jax.nn.gelu(yt0_ref[...].T).astype(BF16)

    yb = yb_ref[slot]
    for c in range(d // tn):
        cs = slice(c * tn, (c + 1) * tn)
        gl = _dot(yb, w_ref[:, cs]) + b_ref[:, cs]
        a = yb[:, cs].astype(F32) * jax.nn.sigmoid(gl) * _silu(z_ref[:, cs].astype(F32))
        a_ref[:, cs] = a.astype(a_ref.dtype)
        yb_ref[1 - slot, :, cs] = jax.nn.gelu(ytn_ref[cs, :].T).astype(BF16)


def _glu(yt, z, w_glu, b_glu, tn):
    d, m = yt.shape
    tm = LANE_TILE
    ni = m // tm
    return pl.pallas_call(
        functools.partial(_glu_kernel, tn=tn),
        grid=(ni,),
        in_specs=[pl.BlockSpec((d, tm), lambda i: (0, 0), pipeline_mode=pl.Buffered(1)),
                  pl.BlockSpec((d, tm), lambda i: (0, jnp.minimum(i + 1, ni - 1))),
                  pl.BlockSpec((tm, d), lambda i: (i, 0)),
                  pl.BlockSpec((d, d), lambda i: (0, 0), pipeline_mode=pl.Buffered(1)),
                  pl.BlockSpec((1, d), lambda i: (0, 0))],
        out_specs=pl.BlockSpec((tm, d), lambda i: (i, 0)),
        out_shape=jax.ShapeDtypeStruct((m, d), BF16),
        scratch_shapes=[pltpu.VMEM((2, tm, d), BF16)],
        compiler_params=_cparams("arbitrary"),
        name="glu",
    )(yt, yt, z, w_glu, b_glu.reshape(1, d))


def _out_norm_kernel(x_hbm, a_ref, gate_ref, fg_ref, w_ref, o_hbm, xbuf, xsem, obuf, osem, *, n_cond):
    nq = x_hbm.shape[0]
    x = _fetch_slots(x_hbm, xbuf, xsem)
    xn = x + _cond_rows(gate_ref, x.shape[0], nq, n_cond) * _dot(a_ref[...], w_ref[...])
    res = xn * lax.rsqrt(jnp.mean(xn * xn, axis=-1, keepdims=True) + EPS) * fg_ref[...]
    _store_slots(res, o_hbm, obuf, osem)


def _out_norm(a, x4, mod3, cond0, n_cond, final_g, w_out):
    nq, m, _, d = x4.shape
    _slots_per_step(x4)
    return pl.pallas_call(
        functools.partial(_out_norm_kernel, n_cond=n_cond),
        grid=(nq * m * S5_CHUNK // LANE_TILE,),
        in_specs=[pl.BlockSpec(memory_space=pl.ANY),
                  pl.BlockSpec((LANE_TILE, d), lambda l: (l, 0)),
                  pl.BlockSpec((n_cond, 1, d), lambda l: (cond0 // n_cond, 0, 2)),
                  pl.BlockSpec((1, d), lambda l: (0, 0)),
                  pl.BlockSpec((d, d), lambda l: (0, 0), pipeline_mode=pl.Buffered(1))],
        out_specs=pl.BlockSpec(memory_space=pl.ANY),
        out_shape=jax.ShapeDtypeStruct(x4.shape, F32),
        scratch_shapes=2 * _slot_scratch(d),
        compiler_params=_cparams("arbitrary"),
        name="out_norm",
    )(x4, a, mod3, final_g.reshape(1, d), w_out)


def _both_dirs(a):
    _, g, r, p = a.shape
    return a.transpose(1, 2, 0, 3).reshape(g, r, 2 * p)


def kernel(x_prompt, x_sample, cache_l0_k, cache_l0_v, state_l1_s5, c, c_ctx, l0_norm_g, l0_w_ada, l0_b_ada, l0_w_in, l0_rpb, l0_w_out, l1_norm_g, l1_w_ada, l1_b_ada, l1_w_in, l1_a_re, l1_a_im, l1_log_dt, l1_b_re, l1_b_im, l1_c_re, l1_c_im, l1_d, l1_w_glu, l1_b_glu, l1_w_out, final_norm_g):
    bp, seq, d = x_prompt.shape
    bs, n_tok, _ = x_sample.shape
    lc = cache_l0_k.shape[1]
    g = d // S5_CH
    p = S5_P
    xp = x_prompt.reshape(bp * seq, d)
    xs = x_sample.reshape(bs * n_tok, d)

    ctx = bs
    cond8 = jnp.zeros((8, d), F32).at[0:bs].set(c).at[ctx].set(c_ctx)
    mod0 = _ada(cond8, l0_w_ada, l0_b_ada).reshape(8, 1, 3 * d)
    mod1 = _ada(cond8, l1_w_ada, l1_b_ada).reshape(8, 1, 3 * d)

    w_in0 = l0_w_in.astype(BF16)
    w_out0 = l0_w_out.astype(BF16)
    qp, kp, kpb, vp, vpb, zp = _inproj(xp, l0_norm_g, mod0, ctx, bp * seq, w_in0,
                                       ((0, BF16), (1, F32), (1, BF16), (2, F32), (2, BF16), (3, BF16)), 1024, 256)
    qs, ks, vs, zs = _inproj(xs, l0_norm_g, mod0, 0, n_tok, w_in0,
                             ((0, BF16), (1, BF16), (2, BF16), (3, BF16)), 1024, 256)
    op = _ctx_attn(qp, kpb, vpb, seq)
    rpb_pad = jnp.zeros((N_HEADS, 16, LANES), F32).at[:, :RPB_R, :RPB_C].set(l0_rpb)
    os_ = _latent_na(qs, ks, vs, cache_l0_k.reshape(bs, lc, d), cache_l0_v.reshape(bs, lc, d), rpb_pad, n_tok)
    n_seg_s = max(1, n_tok // S5_CHUNK // S5_SEG)
    x1p = _gated_out(op, zp, xp, mod0, ctx, bp * seq, w_out0, 512).reshape(bp, seq // S5_CHUNK, S5_CHUNK, d)
    x1s = _gated_out(os_, zs, xs, mod0, 0, n_tok, w_out0, 512).reshape(bs * n_seg_s, -1, S5_CHUNK, d)

    w_in1 = l1_w_in.astype(BF16)
    utp, zp1 = _inproj_t(x1p, l1_norm_g, mod1, ctx, 1, w_in1)
    uts, zs1 = _inproj_t(x1s, l1_norm_g, mod1, 0, bs, w_in1)
    h0 = state_l1_s5.transpose(3, 0, 2, 1, 4).reshape(g, bs, 4 * p)
    log_dt = jnp.broadcast_to(l1_log_dt[:, :, None, None], (2, g, 1, p))
    ytp, yts, st = _s5(
        utp, uts, h0,
        _both_dirs(l1_a_re[:, :, None, :]), _both_dirs(l1_a_im[:, :, None, :]), _both_dirs(log_dt),
        _both_dirs(l1_b_re.transpose(0, 1, 3, 2)), _both_dirs(l1_b_im.transpose(0, 1, 3, 2)),
        _both_dirs(l1_c_re), _both_dirs(l1_c_im), l1_d, bp, bs, n_seg_s)
    w_glu = l1_w_glu.astype(BF16)
    w_out1 = l1_w_out.astype(BF16)
    y_prompt = _out_norm(_glu(ytp, zp1, w_glu, l1_b_glu, 512), x1p, mod1, ctx, 1, final_norm_g, w_out1)
    y_sample = _out_norm(_glu(yts, zs1, w_glu, l1_b_glu, 512), x1s, mod1, 0, bs, final_norm_g, w_out1)
    new_state = st.reshape(g, bp, 2, 2, p).transpose(1, 3, 2, 0, 4)
    return (y_prompt.reshape(bp, seq, d), y_sample.reshape(bs, n_tok, d),
            kp.reshape(bp, seq, N_HEADS, HEAD_DIM), vp.reshape(bp, seq, N_HEADS, HEAD_DIM), new_state)
```

```python
import functools

import jax
import jax.numpy as jnp
from jax import lax
from jax.experimental import pallas as pl
from jax.experimental.pallas import tpu as pltpu

F32 = jnp.float32
BF16 = jnp.bfloat16

EPS = 1e-6
N_HEADS = 16
HEAD_DIM = 128
GRID_W = 64
NA_ROWS = 8
NA_COLS = 16
RPB_R = 2 * NA_ROWS - 1
RPB_C = 2 * NA_COLS - 1
S5_CH = 16
S5_P = 64
S5_CHUNK = 16
S5_SEG = 8
NEG = -1e30
LANES = 128
VMEM_LIMIT = 56 * 1024 * 1024


def _cparams(*sem):
    return pltpu.CompilerParams(dimension_semantics=sem, vmem_limit_bytes=VMEM_LIMIT)


def _dot(a, b):
    return jnp.dot(a, b, preferred_element_type=F32)


def _dot_nt(a, b, precision=None):
    return lax.dot_general(a, b, (((1,), (1,)), ((), ())), preferred_element_type=F32,
                           precision=precision)


def _silu(x):
    return x * jax.nn.sigmoid(x)


def _ada_kernel(cond_ref, *refs, n_layers):
    w_refs, b_refs, o_ref = refs[:n_layers], refs[n_layers:2 * n_layers], refs[2 * n_layers]
    s = _silu(cond_ref[...]).astype(BF16)
    layer = pl.program_id(0)
    for i in range(n_layers):
        @pl.when(layer == i)
        def _(i=i):
            o_ref[...] = _dot(s, w_refs[i][...].astype(BF16)) + b_refs[i][...]


def _ada(cond8, w_adas, b_adas):
    n_layers = len(w_adas)
    d, n = w_adas[0].shape
    tn = 1024
    nj = n // tn

    def own(i):
        return lambda l, j: (0, jnp.where(l == i, j, jnp.where(l < i, 0, nj - 1)))

    return pl.pallas_call(
        functools.partial(_ada_kernel, n_layers=n_layers),
        grid=(n_layers, nj),
        in_specs=[pl.BlockSpec((8, d), lambda l, j: (0, 0))]
        + [pl.BlockSpec((d, tn), own(i)) for i in range(n_layers)]
        + [pl.BlockSpec((1, tn), own(i)) for i in range(n_layers)],
        out_specs=pl.BlockSpec((None, 8, tn), lambda l, j: (l, 0, j)),
        out_shape=jax.ShapeDtypeStruct((n_layers, 8, n), F32),
        compiler_params=_cparams("arbitrary", "arbitrary"),
        name="ada",
    )(cond8, *w_adas, *[b.reshape(1, n) for b in b_adas])


def _norm_mod(x, g, shift, scale):
    y = x * lax.rsqrt(jnp.mean(x * x, axis=-1, keepdims=True) + EPS) * g
    return (y * (1.0 + scale) + shift).astype(BF16)


def _inproj_kernel(x0_ref, xn_ref, g_ref, shift0_ref, scale0_ref, shiftn_ref, scalen_ref, *rest, out_split):
    n_split = max(out_split) + 1
    w_refs = rest[:n_split]
    o_refs = rest[n_split:-1]
    h_ref = rest[-1]
    i = pl.program_id(0)
    j = pl.program_id(1)
    slot = i % 2

    @pl.when((i == 0) & (j == 0))
    def _():
        h_ref[0] = _norm_mod(x0_ref[...], g_ref[...], shift0_ref[...], scale0_ref[...])

    h = h_ref[slot]
    for s, w_ref in enumerate(w_refs):
        r = _dot(h, w_ref[...])
        for o_ref, o_s in zip(o_refs, out_split):
            if o_s == s:
                o_ref[...] = r.astype(o_ref.dtype)
    rs = xn_ref.shape[0]
    h_ref[1 - slot, pl.ds(pl.multiple_of(j * rs, rs), rs), :] = _norm_mod(
        xn_ref[...], g_ref[...], shiftn_ref[...], scalen_ref[...])


def _inproj(x, g, mod3, cond0, rows_per_cond, w, outs, tm, tn):
    m, d = x.shape
    out_split = tuple(s for s, _ in outs)
    out_dtypes = [dt for _, dt in outs]
    n_split = max(out_split) + 1
    ni = m // tm
    nj = d // tn

    def nxt(i):
        return jnp.minimum(i + 1, ni - 1)

    def cond_of(i):
        return cond0 + (i * tm) // rows_per_cond

    in_specs = [pl.BlockSpec((tm, d), lambda i, j: (0, 0), pipeline_mode=pl.Buffered(1)),
                pl.BlockSpec((tm // nj, d), lambda i, j: (nxt(i) * nj + j, 0)),
                pl.BlockSpec((1, d), lambda i, j: (0, 0)),
                pl.BlockSpec((None, 1, d), lambda i, j: (cond_of(0), 0, 0)),
                pl.BlockSpec((None, 1, d), lambda i, j: (cond_of(0), 0, 1)),
                pl.BlockSpec((None, 1, d), lambda i, j: (cond_of(nxt(i)), 0, 0)),
                pl.BlockSpec((None, 1, d), lambda i, j: (cond_of(nxt(i)), 0, 1))]
    for s in range(n_split):
        in_specs.append(pl.BlockSpec((d, tn), lambda i, j, s=s: (0, s * nj + j)))
    return pl.pallas_call(
        functools.partial(_inproj_kernel, out_split=out_split),
        grid=(ni, nj),
        in_specs=in_specs,
        out_specs=[pl.BlockSpec((tm, tn), lambda i, j: (i, j)) for _ in outs],
        out_shape=[jax.ShapeDtypeStruct((m, d), dt) for dt in out_dtypes],
        scratch_shapes=[pltpu.VMEM((2, tm, d), BF16)],
        compiler_params=_cparams("arbitrary", "arbitrary"),
        name="inproj",
    )(x, x, g.reshape(1, d), mod3, mod3, mod3, mod3, *([w] * n_split))


def _softmax_pv(parts):
    m = parts[0][0].max(axis=-1, keepdims=True)
    for s, _ in parts[1:]:
        m = jnp.maximum(m, s.max(axis=-1, keepdims=True))
    l = None
    o = None
    for s, v in parts:
        e = jnp.exp(s - m)
        li = e.sum(axis=-1, keepdims=True)
        oi = _dot(e.astype(BF16), v)
        l = li if l is None else l + li
        o = oi if o is None else o + oi
    return o / l


def _ctx_attn_kernel(q_ref, k_ref, v_ref, o_ref):
    scale = HEAD_DIM ** -0.5
    for h in range(N_HEADS):
        sl = slice(h * HEAD_DIM, (h + 1) * HEAD_DIM)
        s = _dot_nt(q_ref[:, sl], k_ref[:, sl]) * scale
        o_ref[:, sl] = _softmax_pv([(s, v_ref[:, sl])]).astype(o_ref.dtype)


def _ctx_attn(q, k, v, seq):
    m, d = q.shape
    spec = pl.BlockSpec((seq, d), lambda b: (b, 0))
    return pl.pallas_call(
        _ctx_attn_kernel,
        grid=(m // seq,),
        in_specs=[spec, spec, spec],
        out_specs=spec,
        out_shape=jax.ShapeDtypeStruct((m, d), BF16),
        compiler_params=_cparams("parallel"),
        name="ctx_attn",
    )(q, k, v)


def _na_window_start(r, rows):
    return min(max(r - NA_ROWS // 2, 0), rows - NA_ROWS)


def _na_kernel(q_ref, k_ref, v_ref, kc_ref, vc_ref, rpb_ref, o_ref, *, rows, q_rows):
    scale = HEAD_DIM ** -0.5
    w = GRID_W
    qc = lax.broadcasted_iota(jnp.int32, (w, LANES), 0)
    lane = lax.broadcasted_iota(jnp.int32, (w, LANES), 1)
    c0 = jnp.clip(qc - NA_COLS // 2, 0, w - NA_COLS)
    ok_l = (lane >= c0) & (lane < c0 + NA_COLS)
    ok_r = (lane - w >= c0) & (lane - w < c0 + NA_COLS)
    neg_tile = jnp.full((w, LANES), NEG, F32)
    tile_l, tile_r = [], []
    for d in range(RPB_R):
        base = jnp.broadcast_to(rpb_ref[d:d + 1, :], (w, LANES))
        left = pltpu.roll(base, LANES - (NA_COLS - 1), 1, stride=1, stride_axis=0)
        right = pltpu.roll(base, w - (NA_COLS - 1), 1, stride=1, stride_axis=0)
        tile_l.append(jnp.where(ok_l, left, NEG))
        tile_r.append(jnp.where(ok_r, right, NEG))

    kc = kc_ref[...].astype(BF16)
    vc = vc_ref[...].astype(BF16)
    n_groups = rows // q_rows
    for gi in range(n_groups):
        rs = list(range(gi * q_rows, (gi + 1) * q_rows))
        klo = min(_na_window_start(r, rows) for r in rs) // 2 * 2
        khi = -(-(max(_na_window_start(r, rows) for r in rs) + NA_ROWS) // 2) * 2
        bias_rows = []
        for r in rs:
            r0 = _na_window_start(r, rows)
            tiles = []
            for kr in range(klo, khi, 2):
                ok0 = r0 <= kr < r0 + NA_ROWS
                ok1 = r0 <= kr + 1 < r0 + NA_ROWS
                t0 = tile_l[kr - r + NA_ROWS - 1] if ok0 else neg_tile
                t1 = tile_r[kr + 1 - r + NA_ROWS - 1] if ok1 else neg_tile
                tiles.append(jnp.maximum(t0, t1) if (ok0 or ok1) else neg_tile)
            bias_rows.append(jnp.concatenate(tiles, axis=1))
        bias = jnp.concatenate(bias_rows, axis=0)
        q = q_ref[gi * q_rows * w:(gi + 1) * q_rows * w, :]
        kl = k_ref[klo * w:khi * w, :]
        vl = v_ref[klo * w:khi * w, :]
        s_loc = _dot_nt(q, kl) * scale + bias
        s_ctx = _dot_nt(q, kc) * scale
        o = _softmax_pv([(s_loc, vl), (s_ctx, vc)])
        o_ref[gi * q_rows * w:(gi + 1) * q_rows * w, :] = o.astype(o_ref.dtype)


def _latent_na(q, k, v, k_ctx, v_ctx, rpb_pad, n_tok):
    m, d = q.shape
    nb = m // n_tok
    lc = k_ctx.shape[1]
    rows = n_tok // GRID_W
    spec = pl.BlockSpec((n_tok, HEAD_DIM), lambda b, h: (b, h))
    cspec = pl.BlockSpec((None, lc, HEAD_DIM), lambda b, h: (b, 0, h))
    return pl.pallas_call(
        functools.partial(_na_kernel, rows=rows, q_rows=4),
        grid=(nb, N_HEADS),
        in_specs=[spec, spec, spec, cspec, cspec,
                  pl.BlockSpec((None, 16, LANES), lambda b, h: (h, 0, 0))],
        out_specs=spec,
        out_shape=jax.ShapeDtypeStruct((m, d), BF16),
        compiler_params=_cparams("parallel", "parallel"),
        name="latent_na",
    )(q, k, v, k_ctx, v_ctx, rpb_pad)


def _gated_out_kernel(o_ref, z_ref, x_ref, gate_ref, w_ref, y_ref):
    a = o_ref[...].astype(F32) * _silu(z_ref[...].astype(F32))
    y_ref[...] = x_ref[...] + gate_ref[...] * _dot(a.astype(BF16), w_ref[...])


def _gated_out(o, z, x, mod3, cond0, rows_per_cond, w_out, tm):
    m, d = x.shape
    row = pl.BlockSpec((tm, d), lambda i: (i, 0))
    return pl.pallas_call(
        _gated_out_kernel,
        grid=(m // tm,),
        in_specs=[row, row, row,
                  pl.BlockSpec((None, 1, d), lambda i: (cond0 + (i * tm) // rows_per_cond, 0, 2)),
                  pl.BlockSpec((d, d), lambda i: (0, 0), pipeline_mode=pl.Buffered(1))],
        out_specs=row,
        out_shape=jax.ShapeDtypeStruct((m, d), F32),
        compiler_params=_cparams("parallel"),
        name="gated_out",
    )(o, z, x, mod3, w_out)


LANE_TILE = 512


def _slots_per_step(x4):
    nq, m, _, _ = x4.shape
    assert LANE_TILE % (nq * m) == 0 and S5_CHUNK % (LANE_TILE // (nq * m)) == 0
    return LANE_TILE // (nq * m)


def _slot_scratch(d):
    return [pltpu.VMEM((2, LANE_TILE, d), F32), pltpu.SemaphoreType.DMA((2,))]


def _slot_copies(x_hbm, buf, sem, l, to_hbm):
    nq, m, _, _ = x_hbm.shape
    n_x = buf.shape[1] // (nq * m)
    copies = []
    for xi in range(n_x):
        for k in range(m):
            hbm = x_hbm.at[:, k, l * n_x + xi, :]
            vmem = buf.at[l % 2, pl.ds((xi * m + k) * nq, nq), :]
            src, dst = (vmem, hbm) if to_hbm else (hbm, vmem)
            copies.append(pltpu.make_async_copy(src, dst, sem.at[l % 2]))
    return copies


def _fetch_slots(x_hbm, buf, sem):
    l = pl.program_id(0)

    @pl.when(l == 0)
    def _():
        for c in _slot_copies(x_hbm, buf, sem, l, False):
            c.start()

    @pl.when(l + 1 < pl.num_programs(0))
    def _():
        for c in _slot_copies(x_hbm, buf, sem, l + 1, False):
            c.start()

    for c in _slot_copies(x_hbm, buf, sem, l, False):
        c.wait()
    return buf[l % 2]


def _store_slots(res, o_hbm, buf, sem):
    l = pl.program_id(0)

    def wait(ll):
        for c in _slot_copies(o_hbm, buf, sem, ll, True):
            c.wait()

    @pl.when(l >= 2)
    def _():
        wait(l - 2)

    buf[l % 2] = res
    for c in _slot_copies(o_hbm, buf, sem, l, True):
        c.start()

    @pl.when(l == pl.num_programs(0) - 1)
    def _():
        @pl.when(l >= 1)
        def _():
            wait(l - 1)

        wait(l)


def _cond_rows(ref, n_rows, nq, n_cond):
    out = ref[n_cond - 1]
    if n_cond > 1:
        b = (lax.broadcasted_iota(jnp.int32, (n_rows, 1), 0) % nq) // (nq // n_cond)
        for i in range(n_cond - 2, -1, -1):
            out = jnp.where(b == i, ref[i], out)
    return out


def _inproj_t_kernel(x_hbm, g_ref, shift_ref, scale_ref, wu_ref, wz_ref, u_ref, z_ref, buf, sem, *, n_cond):
    nq = x_hbm.shape[0]
    x = _fetch_slots(x_hbm, buf, sem)
    n_rows = x.shape[0]
    h = _norm_mod(x, g_ref[...], _cond_rows(shift_ref, n_rows, nq, n_cond),
                  _cond_rows(scale_ref, n_rows, nq, n_cond))
    u_ref[...] = _dot(h, wu_ref[...]).T.astype(u_ref.dtype)
    z_ref[...] = _dot(h, wz_ref[...]).astype(z_ref.dtype)


def _inproj_t(x4, g, mod3, cond0, n_cond, w_in):
    nq, m, _, d = x4.shape
    n_tok = nq * m * S5_CHUNK
    _slots_per_step(x4)

    def mspec(col):
        return pl.BlockSpec((n_cond, 1, d), lambda l: (cond0 // n_cond, 0, col))

    def wspec(col):
        return pl.BlockSpec((d, d), lambda l: (0, col), pipeline_mode=pl.Buffered(1))

    return pl.pallas_call(
        functools.partial(_inproj_t_kernel, n_cond=n_cond),
        grid=(n_tok // LANE_TILE,),
        in_specs=[pl.BlockSpec(memory_space=pl.ANY),
                  pl.BlockSpec((1, d), lambda l: (0, 0)), mspec(0), mspec(1), wspec(0), wspec(1)],
        out_specs=[pl.BlockSpec((d, LANE_TILE), lambda l: (0, l)), pl.BlockSpec((LANE_TILE, d), lambda l: (l, 0))],
        out_shape=[jax.ShapeDtypeStruct((d, n_tok), BF16), jax.ShapeDtypeStruct((n_tok, d), BF16)],
        scratch_shapes=_slot_scratch(d),
        compiler_params=_cparams("arbitrary"),
        name="inproj_t",
    )(x4, g.reshape(1, d), mod3, mod3, w_in, w_in)


def _cmul(ar, ai, br, bi):
    return ar * br - ai * bi, ar * bi + ai * br


def _split_bf16(a):
    hi = a.astype(BF16)
    return hi, (a - hi.astype(F32)).astype(BF16)


def _s5_powers(ar, ai):
    width = ar.shape[-1]
    pw = [(jnp.ones_like(ar), jnp.zeros_like(ar))]
    for _ in range(S5_CHUNK):
        pw.append(_cmul(pw[-1][0], pw[-1][1], ar, ai))
    is_fwd = lax.broadcasted_iota(jnp.int32, (1, width), 1) < width // 2

    def pattern(fwd_ascending):
        out = []
        for part in range(2):
            blocks = []
            for j in range(S5_CHUNK):
                ef, eb = (j, S5_CHUNK - 1 - j) if fwd_ascending else (S5_CHUNK - 1 - j, j)
                blocks.append(jnp.broadcast_to(jnp.where(is_fwd, pw[ef][part], pw[eb][part]), (S5_CH, width)))
            out.append(jnp.concatenate(blocks, axis=0))
        return out

    return pattern(True), pattern(False), pw[S5_CHUNK]


S5_GROUPS_PER_STEP = 4


def _s5_kernel(xp_ref, xs_ref, *rest, nb_p, nb_s, n_seg_s):
    yp_ref, ys_ref = rest[9], rest[10]
    for gi in range(S5_GROUPS_PER_STEP):
        ch = pl.ds(gi * S5_CH, S5_CH)
        _s5_group(xp_ref.at[ch], xs_ref.at[ch], *[r.at[gi] for r in rest[:9]], yp_ref.at[ch], ys_ref.at[ch],
                  *[r.at[gi] for r in rest[11:]], nb_p=nb_p, nb_s=nb_s, n_seg_s=n_seg_s)


def _s5_group(xp_ref, xs_ref, h0_ref, are_ref, aim_ref, ldt_ref, bre_ref, bim_ref, cre_ref, cim_ref, dsk_ref,
              yp_ref, ys_ref, st_ref, w_ref, ws_ref, wc_ref, s_ref, f_ref, *, nb_p, nb_s, n_seg_s):
    p = S5_P
    kc = S5_CHUNK * S5_CH
    a_re = are_ref[...]
    a_im = aim_ref[...]
    dt = jnp.exp(ldt_ref[...])
    mag = jnp.exp(a_re * dt)
    ab_re = mag * jnp.cos(a_im * dt)
    ab_im = mag * jnp.sin(a_im * dt)
    den = a_re * a_re + a_im * a_im
    nr = ab_re - 1.0
    f_re = (nr * a_re + ab_im * a_im) / den
    f_im = (ab_im * a_re - nr * a_im) / den
    bbt_re, bbt_im = _cmul(f_re, f_im, bre_ref[...], bim_ref[...])

    (pg_r, pg_i), (ps_r, ps_i), (ac_r, ac_i) = _s5_powers(ab_re, ab_im)
    c_re = jnp.concatenate([cre_ref[...]] * S5_CHUNK, axis=0)
    c_im = jnp.concatenate([cim_ref[...]] * S5_CHUNK, axis=0)
    bt_re = jnp.concatenate([bbt_re] * S5_CHUNK, axis=0)
    bt_im = jnp.concatenate([bbt_im] * S5_CHUNK, axis=0)

    g_re, g_im = _cmul(pg_r, pg_i, c_re, c_im)
    fwd16 = lax.broadcasted_iota(jnp.int32, (S5_CH, 2 * p), 1) < p
    g_hi, g_lo = _split_bf16(jnp.concatenate([g_re, g_im], axis=1))
    r0 = []
    for d in range(2):
        msk = fwd16 if d == 0 else jnp.logical_not(fwd16)
        b_hi, b_lo = _split_bf16(
            jnp.concatenate([jnp.where(msk, bbt_re, 0.0), jnp.where(msk, -bbt_im, 0.0)], axis=1))
        r0.append(_dot_nt(b_hi, g_hi) + (_dot_nt(b_hi, g_lo) + _dot_nt(b_lo, g_hi)))
    lane = lax.broadcasted_iota(jnp.int32, (S5_CH, kc), 1)
    for s in range(S5_CHUNK):
        f = jnp.where(lane >= S5_CH * s, pltpu.roll(r0[0], S5_CH * s, 1), 0.0) if s else r0[0]
        b = jnp.where(lane < S5_CH * (s + 1), pltpu.roll(r0[1], (S5_CH * (s + 1)) % kc, 1), 0.0)
        w_ref[s * S5_CH:(s + 1) * S5_CH, :] = (f + b).astype(BF16)
    e_re, e_im = _cmul(ps_r, ps_i, bt_re, bt_im)
    ws_ref[:, 0:2 * p] = e_re.astype(BF16)
    ws_ref[:, 2 * p:4 * p] = e_im.astype(BF16)
    g1_re, g1_im = _cmul(g_re, g_im, ab_re, ab_im)
    wc_ref[:, 0:2 * p] = g1_re.astype(BF16)
    wc_ref[:, 2 * p:4 * p] = (-g1_im).astype(BF16)

    def run(xt_ref, yt_ref, nb, n_seg, hr, hm):
        rows = xt_ref.shape[1] // S5_CHUNK
        nbx = nb * n_seg
        m = rows // nbx
        xt = jnp.concatenate([xt_ref[:, s * rows:(s + 1) * rows] for s in range(S5_CHUNK)], axis=0)
        xf = xt.astype(F32).T
        x = xf.astype(BF16)
        y = _dot(x, w_ref[...]) + dsk_ref[...] * xf
        s_all = _dot(x, ws_ref[...])
        s_ref[0, 0:rows, :] = s_all[:, 0:2 * p]
        s_ref[1, 0:rows, :] = s_all[:, 2 * p:4 * p]

        def step(ar, ai, hr, hm, plane_r, plane_i, rf, rb):
            isf = lax.broadcasted_iota(jnp.int32, hr.shape, 1) < p
            sr = jnp.where(isf, plane_r[rf, :], plane_r[rb, :])
            sm = jnp.where(isf, plane_i[rf, :], plane_i[rb, :])
            return ar * hr - ai * hm + sr, ar * hm + ai * hr + sm

        def scan(hr, hm, store):
            arb = jnp.broadcast_to(ac_r, (nbx, 2 * p))
            aib = jnp.broadcast_to(ac_i, (nbx, 2 * p))
            for k in range(m):
                rf = pl.ds(k * nbx, nbx)
                rb = pl.ds((m - 1 - k) * nbx, nbx)
                if store:
                    s_ref[2, rf, :] = hr
                    s_ref[3, rf, :] = hm
                    s_ref[4, rb, :] = hr
                    s_ref[5, rb, :] = hm
                hr, hm = step(arb, aib, hr, hm, s_ref.at[0], s_ref.at[1], rf, rb)
            return hr, hm

        if n_seg > 1:
            zeros = jnp.zeros((nbx, 2 * p), F32)
            f_ref[0], f_ref[1] = scan(zeros, zeros, False)
            sr_, si_ = ac_r, ac_i
            for _ in range(m - 1):
                sr_, si_ = _cmul(sr_, si_, ac_r, ac_i)
            sr_ = jnp.broadcast_to(sr_, (nb, 2 * p))
            si_ = jnp.broadcast_to(si_, (nb, 2 * p))
            for j in range(n_seg):
                rf = pl.ds(j, nb, stride=n_seg)
                rb = pl.ds(n_seg - 1 - j, nb, stride=n_seg)
                f_ref[2, rf, :] = hr
                f_ref[3, rf, :] = hm
                f_ref[4, rb, :] = hr
                f_ref[5, rb, :] = hm
                hr, hm = step(sr_, si_, hr, hm, f_ref.at[0], f_ref.at[1], rf, rb)
            isf = lax.broadcasted_iota(jnp.int32, (nbx, 2 * p), 1) < p
            hr = jnp.where(isf, f_ref[2], f_ref[4])
            hm = jnp.where(isf, f_ref[3], f_ref[5])
        hr, hm = scan(hr, hm, True)
        isf_rows = lax.broadcasted_iota(jnp.int32, (rows, 2 * p), 1) < p
        h_prev = jnp.concatenate([jnp.where(isf_rows, s_ref[2, 0:rows, :], s_ref[4, 0:rows, :]),
                                  jnp.where(isf_rows, s_ref[3, 0:rows, :], s_ref[5, 0:rows, :])], axis=1)
        yt = (y + _dot_nt(h_prev.astype(BF16), wc_ref[...])).T
        for t in range(S5_CHUNK):
            yt_ref[:, t * rows:(t + 1) * rows] = yt[t * S5_CH:(t + 1) * S5_CH, :].astype(yt_ref.dtype)
        return hr, hm

    zeros = jnp.zeros((nb_p, 2 * p), F32)
    hr, hm = run(xp_ref, yp_ref, nb_p, 1, zeros, zeros)
    st_ref[:, 0:2 * p] = hr
    st_ref[:, 2 * p:4 * p] = hm
    run(xs_ref, ys_ref, nb_s, n_seg_s, h0_ref[:, 0:2 * p], h0_ref[:, 2 * p:4 * p])


def _s5(utp, uts, h0, a_re, a_im, log_dt, bt_re, bt_im, c_re, c_im, d_skip, nb_p, nb_s, n_seg_s):
    d, lanes_p = utp.shape
    lanes_s = uts.shape[1]
    g = d // S5_CH
    rows_p = lanes_p // S5_CHUNK
    rows_s = lanes_s // S5_CHUNK
    kc = S5_CHUNK * S5_CH
    p = S5_P

    gps = S5_GROUPS_PER_STEP

    def gspec(shape):
        return pl.BlockSpec((gps,) + shape, lambda i: (i, 0, 0))

    def tspec(lanes):
        return pl.BlockSpec((gps * S5_CH, lanes), lambda i: (i, 0))

    return pl.pallas_call(
        functools.partial(_s5_kernel, nb_p=nb_p, nb_s=nb_s, n_seg_s=n_seg_s),
        grid=(g // gps,),
        in_specs=[tspec(lanes_p), tspec(lanes_s), gspec((nb_s, 4 * p)),
                  gspec((1, 2 * p)), gspec((1, 2 * p)), gspec((1, 2 * p)),
                  gspec((S5_CH, 2 * p)), gspec((S5_CH, 2 * p)), gspec((S5_CH, 2 * p)), gspec((S5_CH, 2 * p)),
                  gspec((1, kc))],
        out_specs=[tspec(lanes_p), tspec(lanes_s), gspec((nb_p, 4 * p))],
        out_shape=[jax.ShapeDtypeStruct((d, lanes_p), BF16),
                   jax.ShapeDtypeStruct((d, lanes_s), BF16),
                   jax.ShapeDtypeStruct((g, nb_p, 4 * p), F32)],
        scratch_shapes=[pltpu.VMEM((gps, kc, kc), BF16), pltpu.VMEM((gps, kc, 4 * p), BF16),
                        pltpu.VMEM((gps, kc, 4 * p), BF16),
                        pltpu.VMEM((gps, 6, max(rows_p, rows_s), 2 * p), F32),
                        pltpu.VMEM((gps, 6, nb_s * n_seg_s, 2 * p), F32)],
        compiler_params=_cparams("parallel"),
        name="s5",
    )(utp, uts, h0, a_re, a_im, log_dt, bt_re, bt_im, c_re, c_im,
      jnp.tile(d_skip.reshape(g, 1, S5_CH), (1, 1, S5_CHUNK)))


def _glu_kernel(yt0_ref, ytn_ref, z_ref, w_ref, b_ref, a_ref, yb_ref, *, tn):
    d = w_ref.shape[0]
    i = pl.program_id(0)
    slot = i % 2

    @pl.when(i == 0)
    def _():
        yb_ref[0] = jax.nn.gelu(yt0_ref[...].astype(F32)).astype(BF16).T

    for c in range(d // tn):
        cs = slice(c * tn, (c + 1) * tn)
        yb = yb_ref[slot]
        gl = _dot(yb, w_ref[:, cs]) + b_ref[:, cs]
        a = yb[:, cs].astype(F32) * jax.nn.sigmoid(gl) * _silu(z_ref[:, cs].astype(F32))
        a_ref[:, cs] = a.astype(a_ref.dtype)
        yb_ref[1 - slot, :, cs] = jax.nn.gelu(ytn_ref[cs, :].astype(F32)).astype(BF16).T


def _glu(yt, z, w_glu, b_glu, tn):
    d, m = yt.shape
    tm = LANE_TILE
    ni = m // tm
    return pl.pallas_call(
        functools.partial(_glu_kernel, tn=tn),
        grid=(ni,),
        in_specs=[pl.BlockSpec((d, tm), lambda i: (0, 0), pipeline_mode=pl.Buffered(1)),
                  pl.BlockSpec((d, tm), lambda i: (0, jnp.minimum(i + 1, ni - 1))),
                  pl.BlockSpec((tm, d), lambda i: (i, 0)),
                  pl.BlockSpec((d, d), lambda i: (0, 0), pipeline_mode=pl.Buffered(1)),
                  pl.BlockSpec((1, d), lambda i: (0, 0))],
        out_specs=pl.BlockSpec((tm, d), lambda i: (i, 0)),
        out_shape=jax.ShapeDtypeStruct((m, d), BF16),
        scratch_shapes=[pltpu.VMEM((2, tm, d), BF16)],
        compiler_params=_cparams("arbitrary"),
        name="glu",
    )(yt, yt, z, w_glu, b_glu.reshape(1, d))


def _out_norm_kernel(x_hbm, a_ref, gate_ref, fg_ref, w_ref, o_hbm, xbuf, xsem, obuf, osem, *, n_cond):
    nq = x_hbm.shape[0]
    x = _fetch_slots(x_hbm, xbuf, xsem)
    xn = x + _cond_rows(gate_ref, x.shape[0], nq, n_cond) * _dot(a_ref[...], w_ref[...])
    res = xn * lax.rsqrt(jnp.mean(xn * xn, axis=-1, keepdims=True) + EPS) * fg_ref[...]
    _store_slots(res, o_hbm, obuf, osem)


def _out_norm(a, x4, mod3, cond0, n_cond, final_g, w_out):
    nq, m, _, d = x4.shape
    _slots_per_step(x4)
    return pl.pallas_call(
        functools.partial(_out_norm_kernel, n_cond=n_cond),
        grid=(nq * m * S5_CHUNK // LANE_TILE,),
        in_specs=[pl.BlockSpec(memory_space=pl.ANY),
                  pl.BlockSpec((LANE_TILE, d), lambda l: (l, 0)),
                  pl.BlockSpec((n_cond, 1, d), lambda l: (cond0 // n_cond, 0, 2)),
                  pl.BlockSpec((1, d), lambda l: (0, 0)),
                  pl.BlockSpec((d, d), lambda l: (0, 0), pipeline_mode=pl.Buffered(1))],
        out_specs=pl.BlockSpec(memory_space=pl.ANY),
        out_shape=jax.ShapeDtypeStruct(x4.shape, F32),
        scratch_shapes=2 * _slot_scratch(d),
        compiler_params=_cparams("arbitrary"),
        name="out_norm",
    )(x4, a, mod3, final_g.reshape(1, d), w_out)


def _both_dirs(a):
    _, g, r, p = a.shape
    return a.transpose(1, 2, 0, 3).reshape(g, r, 2 * p)


def kernel(x_prompt, x_sample, cache_l0_k, cache_l0_v, state_l1_s5, c, c_ctx, l0_norm_g, l0_w_ada, l0_b_ada, l0_w_in, l0_rpb, l0_w_out, l1_norm_g, l1_w_ada, l1_b_ada, l1_w_in, l1_a_re, l1_a_im, l1_log_dt, l1_b_re, l1_b_im, l1_c_re, l1_c_im, l1_d, l1_w_glu, l1_b_glu, l1_w_out, final_norm_g):
    bp, seq, d = x_prompt.shape
    bs, n_tok, _ = x_sample.shape
    lc = cache_l0_k.shape[1]
    g = d // S5_CH
    p = S5_P
    xp = x_prompt.reshape(bp * seq, d)
    xs = x_sample.reshape(bs * n_tok, d)

    ctx = bs
    cond8 = jnp.zeros((8, d), F32).at[0:bs].set(c).at[ctx].set(c_ctx)
    mods = _ada(cond8, (l0_w_ada, l1_w_ada), (l0_b_ada, l1_b_ada))
    mod0 = mods[0].reshape(8, 1, 3 * d)
    mod1 = mods[1].reshape(8, 1, 3 * d)

    w_in0 = l0_w_in.astype(BF16)
    w_out0 = l0_w_out.astype(BF16)
    qp, kp, kpb, vp, vpb, zp = _inproj(xp, l0_norm_g, mod0, ctx, bp * seq, w_in0,
                                       ((0, BF16), (1, F32), (1, BF16), (2, F32), (2, BF16), (3, BF16)), 1024, 256)
    qs, ks, vs, zs = _inproj(xs, l0_norm_g, mod0, 0, n_tok, w_in0,
                             ((0, BF16), (1, BF16), (2, BF16), (3, BF16)), 1024, 256)
    op = _ctx_attn(qp, kpb, vpb, seq)
    rpb_pad = jnp.zeros((N_HEADS, 16, LANES), F32).at[:, :RPB_R, :RPB_C].set(l0_rpb)
    os_ = _latent_na(qs, ks, vs, cache_l0_k.reshape(bs, lc, d), cache_l0_v.reshape(bs, lc, d), rpb_pad, n_tok)
    n_seg_s = max(1, n_tok // S5_CHUNK // S5_SEG)
    x1p = _gated_out(op, zp, xp, mod0, ctx, bp * seq, w_out0, 512).reshape(bp, seq // S5_CHUNK, S5_CHUNK, d)
    x1s = _gated_out(os_, zs, xs, mod0, 0, n_tok, w_out0, 512).reshape(bs * n_seg_s, -1, S5_CHUNK, d)

    w_in1 = l1_w_in.astype(BF16)
    utp, zp1 = _inproj_t(x1p, l1_norm_g, mod1, ctx, 1, w_in1)
    uts, zs1 = _inproj_t(x1s, l1_norm_g, mod1, 0, bs, w_in1)
    h0 = state_l1_s5.transpose(3, 0, 2, 1, 4).reshape(g, bs, 4 * p)
    log_dt = jnp.broadcast_to(l1_log_dt[:, :, None, None], (2, g, 1, p))
    ytp, yts, st = _s5(
        utp, uts, h0,
        _both_dirs(l1_a_re[:, :, None, :]), _both_dirs(l1_a_im[:, :, None, :]), _both_dirs(log_dt),
        _both_dirs(l1_b_re.transpose(0, 1, 3, 2)), _both_dirs(l1_b_im.transpose(0, 1, 3, 2)),
        _both_dirs(l1_c_re), _both_dirs(l1_c_im), l1_d, bp, bs, n_seg_s)
    w_glu = l1_w_glu.astype(BF16)
    w_out1 = l1_w_out.astype(BF16)
    y_prompt = _out_norm(_glu(ytp, zp1, w_glu, l1_b_glu, 512), x1p, mod1, ctx, 1, final_norm_g, w_out1)
    y_sample = _out_norm(_glu(yts, zs1, w_glu, l1_b_glu, 512), x1s, mod1, 0, bs, final_norm_g, w_out1)
    new_state = st.reshape(g, bp, 2, 2, p).transpose(1, 3, 2, 0, 4)
    return (y_prompt.reshape(bp, seq, d), y_sample.reshape(bs, n_tok, d),
            kp.reshape(bp, seq, N_HEADS, HEAD_DIM), vp.reshape(bp, seq, N_HEADS, HEAD_DIM), new_state)
```

```python
import functools

import jax
import jax.numpy as jnp
from jax import lax
from jax.experimental import pallas as pl
from jax.experimental.pallas import tpu as pltpu

F32 = jnp.float32
BF16 = jnp.bfloat16

EPS = 1e-6
N_HEADS = 16
HEAD_DIM = 128
GRID_W = 64
NA_ROWS = 8
NA_COLS = 16
RPB_R = 2 * NA_ROWS - 1
RPB_C = 2 * NA_COLS - 1
S5_CH = 16
S5_P = 64
S5_CHUNK = 16
S5_SEG = 8
NEG = -1e30
LANES = 128
VMEM_LIMIT = 56 * 1024 * 1024


def _cparams(*sem):
    return pltpu.CompilerParams(dimension_semantics=sem, vmem_limit_bytes=VMEM_LIMIT)


def _dot(a, b):
    return jnp.dot(a, b, preferred_element_type=F32)


def _dot_nt(a, b, precision=None):
    return lax.dot_general(a, b, (((1,), (1,)), ((), ())), preferred_element_type=F32,
                           precision=precision)


def _silu(x):
    return x * jax.nn.sigmoid(x)


def _ada_kernel(cond_ref, *refs, n_layers):
    w_refs, b_refs, o_ref = refs[:n_layers], refs[n_layers:2 * n_layers], refs[2 * n_layers]
    s = _silu(cond_ref[...]).astype(BF16)
    layer = pl.program_id(0)
    for i in range(n_layers):
        @pl.when(layer == i)
        def _(i=i):
            o_ref[...] = _dot(s, w_refs[i][...].astype(BF16)) + b_refs[i][...]


def _ada(cond8, w_adas, b_adas):
    n_layers = len(w_adas)
    d, n = w_adas[0].shape
    tn = 1024
    nj = n // tn

    def own(i):
        return lambda l, j: (0, jnp.where(l == i, j, jnp.where(l < i, 0, nj - 1)))

    return pl.pallas_call(
        functools.partial(_ada_kernel, n_layers=n_layers),
        grid=(n_layers, nj),
        in_specs=[pl.BlockSpec((8, d), lambda l, j: (0, 0))]
        + [pl.BlockSpec((d, tn), own(i)) for i in range(n_layers)]
        + [pl.BlockSpec((1, tn), own(i)) for i in range(n_layers)],
        out_specs=pl.BlockSpec((None, 8, tn), lambda l, j: (l, 0, j)),
        out_shape=jax.ShapeDtypeStruct((n_layers, 8, n), F32),
        compiler_params=_cparams("arbitrary", "arbitrary"),
        name="ada",
    )(cond8, *w_adas, *[b.reshape(1, n) for b in b_adas])


def _norm_mod(x, g, shift, scale):
    y = x * lax.rsqrt(jnp.mean(x * x, axis=-1, keepdims=True) + EPS) * g
    return (y * (1.0 + scale) + shift).astype(BF16)


def _inproj_kernel(x0_ref, xn_ref, g_ref, shift0_ref, scale0_ref, shiftn_ref, scalen_ref, *rest, out_split,
                   n_after):
    n_split = max(out_split) + 1
    w_refs = rest[:n_split]
    o_refs = rest[n_split + n_after:-1]
    h_ref = rest[-1]
    i = pl.program_id(0)
    j = pl.program_id(1)
    slot = i % 2

    @pl.when((i == 0) & (j == 0))
    def _():
        h_ref[0] = _norm_mod(x0_ref[...], g_ref[...], shift0_ref[...], scale0_ref[...])

    h = h_ref[slot]
    for s, w_ref in enumerate(w_refs):
        r = _dot(h, w_ref[...])
        for o_ref, o_s in zip(o_refs, out_split):
            if o_s == s:
                o_ref[...] = r.astype(o_ref.dtype)
    rs = xn_ref.shape[0]
    h_ref[1 - slot, pl.ds(pl.multiple_of(j * rs, rs), rs), :] = _norm_mod(
        xn_ref[...], g_ref[...], shiftn_ref[...], scalen_ref[...])


def _inproj(x, g, mod3, cond0, rows_per_cond, w, outs, tm, tn, after=()):
    m, d = x.shape
    out_split = tuple(s for s, _ in outs)
    out_dtypes = [dt for _, dt in outs]
    n_split = max(out_split) + 1
    ni = m // tm
    nj = d // tn

    def nxt(i):
        return jnp.minimum(i + 1, ni - 1)

    def cond_of(i):
        return cond0 + (i * tm) // rows_per_cond

    in_specs = [pl.BlockSpec((tm, d), lambda i, j: (0, 0), pipeline_mode=pl.Buffered(1)),
                pl.BlockSpec((tm // nj, d), lambda i, j: (nxt(i) * nj + j, 0)),
                pl.BlockSpec((1, d), lambda i, j: (0, 0)),
                pl.BlockSpec((None, 1, d), lambda i, j: (cond_of(0), 0, 0)),
                pl.BlockSpec((None, 1, d), lambda i, j: (cond_of(0), 0, 1)),
                pl.BlockSpec((None, 1, d), lambda i, j: (cond_of(nxt(i)), 0, 0)),
                pl.BlockSpec((None, 1, d), lambda i, j: (cond_of(nxt(i)), 0, 1))]
    for s in range(n_split):
        in_specs.append(pl.BlockSpec((d, tn), lambda i, j, s=s: (0, s * nj + j)))
    in_specs += [pl.BlockSpec(memory_space=pl.ANY)] * len(after)
    return pl.pallas_call(
        functools.partial(_inproj_kernel, out_split=out_split, n_after=len(after)),
        grid=(ni, nj),
        in_specs=in_specs,
        out_specs=[pl.BlockSpec((tm, tn), lambda i, j: (i, j)) for _ in outs],
        out_shape=[jax.ShapeDtypeStruct((m, d), dt) for dt in out_dtypes],
        scratch_shapes=[pltpu.VMEM((2, tm, d), BF16)],
        compiler_params=_cparams("arbitrary", "arbitrary"),
        name="inproj",
    )(x, x, g.reshape(1, d), mod3, mod3, mod3, mod3, *([w] * n_split), *after)


def _softmax_pv(parts):
    m = parts[0][0].max(axis=-1, keepdims=True)
    for s, _ in parts[1:]:
        m = jnp.maximum(m, s.max(axis=-1, keepdims=True))
    l = None
    o = None
    for s, v in parts:
        e = jnp.exp(s - m)
        li = e.sum(axis=-1, keepdims=True)
        oi = _dot(e.astype(BF16), v)
        l = li if l is None else l + li
        o = oi if o is None else o + oi
    return o / l


def _ctx_attn_kernel(q_ref, k_ref, v_ref, o_ref):
    scale = HEAD_DIM ** -0.5
    for h in range(N_HEADS):
        sl = slice(h * HEAD_DIM, (h + 1) * HEAD_DIM)
        s = _dot_nt(q_ref[:, sl], k_ref[:, sl]) * scale
        o_ref[:, sl] = _softmax_pv([(s, v_ref[:, sl])]).astype(o_ref.dtype)


def _ctx_attn(q, k, v, seq):
    m, d = q.shape
    spec = pl.BlockSpec((seq, d), lambda b: (b, 0))
    return pl.pallas_call(
        _ctx_attn_kernel,
        grid=(m // seq,),
        in_specs=[spec, spec, spec],
        out_specs=spec,
        out_shape=jax.ShapeDtypeStruct((m, d), BF16),
        compiler_params=_cparams("parallel"),
        name="ctx_attn",
    )(q, k, v)


def _na_window_start(r, rows):
    return min(max(r - NA_ROWS // 2, 0), rows - NA_ROWS)


def _na_kernel(q_ref, k_ref, v_ref, kc_ref, vc_ref, rpb_ref, o_ref, *, rows, q_rows):
    scale = HEAD_DIM ** -0.5
    w = GRID_W
    qc = lax.broadcasted_iota(jnp.int32, (w, LANES), 0)
    lane = lax.broadcasted_iota(jnp.int32, (w, LANES), 1)
    c0 = jnp.clip(qc - NA_COLS // 2, 0, w - NA_COLS)
    ok_l = (lane >= c0) & (lane < c0 + NA_COLS)
    ok_r = (lane - w >= c0) & (lane - w < c0 + NA_COLS)
    neg_tile = jnp.full((w, LANES), NEG, F32)
    tile_l, tile_r = [], []
    for d in range(RPB_R):
        base = jnp.broadcast_to(rpb_ref[d:d + 1, :], (w, LANES))
        left = pltpu.roll(base, LANES - (NA_COLS - 1), 1, stride=1, stride_axis=0)
        right = pltpu.roll(base, w - (NA_COLS - 1), 1, stride=1, stride_axis=0)
        tile_l.append(jnp.where(ok_l, left, NEG))
        tile_r.append(jnp.where(ok_r, right, NEG))

    kc = kc_ref[...].astype(BF16)
    vc = vc_ref[...].astype(BF16)
    n_groups = rows // q_rows
    for gi in range(n_groups):
        rs = list(range(gi * q_rows, (gi + 1) * q_rows))
        klo = min(_na_window_start(r, rows) for r in rs) // 2 * 2
        khi = -(-(max(_na_window_start(r, rows) for r in rs) + NA_ROWS) // 2) * 2
        bias_rows = []
        for r in rs:
            r0 = _na_window_start(r, rows)
            tiles = []
            for kr in range(klo, khi, 2):
                ok0 = r0 <= kr < r0 + NA_ROWS
                ok1 = r0 <= kr + 1 < r0 + NA_ROWS
                t0 = tile_l[kr - r + NA_ROWS - 1] if ok0 else neg_tile
                t1 = tile_r[kr + 1 - r + NA_ROWS - 1] if ok1 else neg_tile
                tiles.append(jnp.maximum(t0, t1) if (ok0 or ok1) else neg_tile)
            bias_rows.append(jnp.concatenate(tiles, axis=1))
        bias = jnp.concatenate(bias_rows, axis=0)
        q = q_ref[gi * q_rows * w:(gi + 1) * q_rows * w, :]
        kl = k_ref[klo * w:khi * w, :]
        vl = v_ref[klo * w:khi * w, :]
        s_loc = _dot_nt(q, kl) * scale + bias
        s_ctx = _dot_nt(q, kc) * scale
        o = _softmax_pv([(s_loc, vl), (s_ctx, vc)])
        o_ref[gi * q_rows * w:(gi + 1) * q_rows * w, :] = o.astype(o_ref.dtype)


def _latent_na(q, k, v, k_ctx, v_ctx, rpb_pad, n_tok):
    m, d = q.shape
    nb = m // n_tok
    lc = k_ctx.shape[1]
    rows = n_tok // GRID_W
    spec = pl.BlockSpec((n_tok, HEAD_DIM), lambda b, h: (b, h))
    cspec = pl.BlockSpec((None, lc, HEAD_DIM), lambda b, h: (b, 0, h))
    return pl.pallas_call(
        functools.partial(_na_kernel, rows=rows, q_rows=4),
        grid=(nb, N_HEADS),
        in_specs=[spec, spec, spec, cspec, cspec,
                  pl.BlockSpec((None, 16, LANES), lambda b, h: (h, 0, 0))],
        out_specs=spec,
        out_shape=jax.ShapeDtypeStruct((m, d), BF16),
        compiler_params=_cparams("parallel", "parallel"),
        name="latent_na",
    )(q, k, v, k_ctx, v_ctx, rpb_pad)


def _gated_out_kernel(o_ref, z_ref, x_ref, gate_ref, w_ref, y_ref):
    a = o_ref[...].astype(F32) * _silu(z_ref[...].astype(F32))
    y_ref[...] = x_ref[...] + gate_ref[...] * _dot(a.astype(BF16), w_ref[...])


def _gated_out(o, z, x, mod3, cond0, rows_per_cond, w_out, tm):
    m, d = x.shape
    row = pl.BlockSpec((tm, d), lambda i: (i, 0))
    return pl.pallas_call(
        _gated_out_kernel,
        grid=(m // tm,),
        in_specs=[row, row, row,
                  pl.BlockSpec((None, 1, d), lambda i: (cond0 + (i * tm) // rows_per_cond, 0, 2)),
                  pl.BlockSpec((d, d), lambda i: (0, 0), pipeline_mode=pl.Buffered(1))],
        out_specs=row,
        out_shape=jax.ShapeDtypeStruct((m, d), F32),
        compiler_params=_cparams("parallel"),
        name="gated_out",
    )(o, z, x, mod3, w_out)


LANE_TILE = 512


def _slots_per_step(x4):
    nq, m, _, _ = x4.shape
    assert LANE_TILE % (nq * m) == 0 and S5_CHUNK % (LANE_TILE // (nq * m)) == 0
    return LANE_TILE // (nq * m)


def _slot_scratch(d):
    return [pltpu.VMEM((2, LANE_TILE, d), F32), pltpu.SemaphoreType.DMA((2,))]


def _slot_copies(x_hbm, buf, sem, l, to_hbm):
    nq, m, _, _ = x_hbm.shape
    n_x = buf.shape[1] // (nq * m)
    copies = []
    for xi in range(n_x):
        for k in range(m):
            hbm = x_hbm.at[:, k, l * n_x + xi, :]
            vmem = buf.at[l % 2, pl.ds((xi * m + k) * nq, nq), :]
            src, dst = (vmem, hbm) if to_hbm else (hbm, vmem)
            copies.append(pltpu.make_async_copy(src, dst, sem.at[l % 2]))
    return copies


def _fetch_slots(x_hbm, buf, sem):
    l = pl.program_id(0)

    @pl.when(l == 0)
    def _():
        for c in _slot_copies(x_hbm, buf, sem, l, False):
            c.start()

    @pl.when(l + 1 < pl.num_programs(0))
    def _():
        for c in _slot_copies(x_hbm, buf, sem, l + 1, False):
            c.start()

    for c in _slot_copies(x_hbm, buf, sem, l, False):
        c.wait()
    return buf[l % 2]


def _store_slots(res, o_hbm, buf, sem):
    l = pl.program_id(0)

    def wait(ll):
        for c in _slot_copies(o_hbm, buf, sem, ll, True):
            c.wait()

    @pl.when(l >= 2)
    def _():
        wait(l - 2)

    buf[l % 2] = res
    for c in _slot_copies(o_hbm, buf, sem, l, True):
        c.start()

    @pl.when(l == pl.num_programs(0) - 1)
    def _():
        @pl.when(l >= 1)
        def _():
            wait(l - 1)

        wait(l)


def _cond_rows(ref, n_rows, nq, n_cond):
    out = ref[n_cond - 1]
    if n_cond > 1:
        b = (lax.broadcasted_iota(jnp.int32, (n_rows, 1), 0) % nq) // (nq // n_cond)
        for i in range(n_cond - 2, -1, -1):
            out = jnp.where(b == i, ref[i], out)
    return out


def _inproj_t_kernel(x_hbm, g_ref, shift_ref, scale_ref, wu_ref, wz_ref, u_ref, z_ref, buf, sem, *, n_cond):
    nq = x_hbm.shape[0]
    x = _fetch_slots(x_hbm, buf, sem)
    n_rows = x.shape[0]
    h = _norm_mod(x, g_ref[...], _cond_rows(shift_ref, n_rows, nq, n_cond),
                  _cond_rows(scale_ref, n_rows, nq, n_cond))
    u_ref[...] = _dot(h, wu_ref[...]).T.astype(u_ref.dtype)
    z_ref[...] = _dot(h, wz_ref[...]).astype(z_ref.dtype)


def _inproj_t(x4, g, mod3, cond0, n_cond, w_in):
    nq, m, _, d = x4.shape
    n_tok = nq * m * S5_CHUNK
    _slots_per_step(x4)

    def mspec(col):
        return pl.BlockSpec((n_cond, 1, d), lambda l: (cond0 // n_cond, 0, col))

    def wspec(col):
        return pl.BlockSpec((d, d), lambda l: (0, col), pipeline_mode=pl.Buffered(1))

    return pl.pallas_call(
        functools.partial(_inproj_t_kernel, n_cond=n_cond),
        grid=(n_tok // LANE_TILE,),
        in_specs=[pl.BlockSpec(memory_space=pl.ANY),
                  pl.BlockSpec((1, d), lambda l: (0, 0)), mspec(0), mspec(1), wspec(0), wspec(1)],
        out_specs=[pl.BlockSpec((d, LANE_TILE), lambda l: (0, l)), pl.BlockSpec((LANE_TILE, d), lambda l: (l, 0))],
        out_shape=[jax.ShapeDtypeStruct((d, n_tok), BF16), jax.ShapeDtypeStruct((n_tok, d), BF16)],
        scratch_shapes=_slot_scratch(d),
        compiler_params=_cparams("arbitrary"),
        name="inproj_t",
    )(x4, g.reshape(1, d), mod3, mod3, w_in, w_in)


def _cmul(ar, ai, br, bi):
    return ar * br - ai * bi, ar * bi + ai * br


def _split_bf16(a):
    hi = a.astype(BF16)
    return hi, (a - hi.astype(F32)).astype(BF16)


def _s5_powers(ar, ai):
    width = ar.shape[-1]
    pw = [(jnp.ones_like(ar), jnp.zeros_like(ar))]
    for _ in range(S5_CHUNK):
        pw.append(_cmul(pw[-1][0], pw[-1][1], ar, ai))
    is_fwd = lax.broadcasted_iota(jnp.int32, (1, width), 1) < width // 2

    def pattern(fwd_ascending):
        out = []
        for part in range(2):
            blocks = []
            for j in range(S5_CHUNK):
                ef, eb = (j, S5_CHUNK - 1 - j) if fwd_ascending else (S5_CHUNK - 1 - j, j)
                blocks.append(jnp.broadcast_to(jnp.where(is_fwd, pw[ef][part], pw[eb][part]), (S5_CH, width)))
            out.append(jnp.concatenate(blocks, axis=0))
        return out

    return pattern(True), pattern(False), pw[S5_CHUNK]


S5_GROUPS_PER_STEP = 8


def _s5_kernel(xp_ref, xs_ref, *rest, nb_p, nb_s, n_seg_s):
    yp_ref, ys_ref = rest[9], rest[10]
    for gi in range(S5_GROUPS_PER_STEP):
        ch = pl.ds(gi * S5_CH, S5_CH)
        _s5_group(xp_ref.at[ch], xs_ref.at[ch], *[r.at[gi] for r in rest[:9]], yp_ref.at[ch], ys_ref.at[ch],
                  *[r.at[gi] for r in rest[11:]], nb_p=nb_p, nb_s=nb_s, n_seg_s=n_seg_s)


def _s5_group(xp_ref, xs_ref, h0_ref, are_ref, aim_ref, ldt_ref, bre_ref, bim_ref, cre_ref, cim_ref, dsk_ref,
              yp_ref, ys_ref, st_ref, w_ref, ws_ref, wc_ref, s_ref, f_ref, *, nb_p, nb_s, n_seg_s):
    p = S5_P
    kc = S5_CHUNK * S5_CH
    a_re = are_ref[...]
    a_im = aim_ref[...]
    dt = jnp.exp(ldt_ref[...])
    mag = jnp.exp(a_re * dt)
    ab_re = mag * jnp.cos(a_im * dt)
    ab_im = mag * jnp.sin(a_im * dt)
    den = a_re * a_re + a_im * a_im
    nr = ab_re - 1.0
    f_re = (nr * a_re + ab_im * a_im) / den
    f_im = (ab_im * a_re - nr * a_im) / den
    bbt_re, bbt_im = _cmul(f_re, f_im, bre_ref[...], bim_ref[...])

    (pg_r, pg_i), (ps_r, ps_i), (ac_r, ac_i) = _s5_powers(ab_re, ab_im)
    c_re = jnp.concatenate([cre_ref[...]] * S5_CHUNK, axis=0)
    c_im = jnp.concatenate([cim_ref[...]] * S5_CHUNK, axis=0)
    bt_re = jnp.concatenate([bbt_re] * S5_CHUNK, axis=0)
    bt_im = jnp.concatenate([bbt_im] * S5_CHUNK, axis=0)

    g_re, g_im = _cmul(pg_r, pg_i, c_re, c_im)
    fwd16 = lax.broadcasted_iota(jnp.int32, (S5_CH, 2 * p), 1) < p
    g_hi, g_lo = _split_bf16(jnp.concatenate([g_re, g_im], axis=1))
    r0 = []
    for d in range(2):
        msk = fwd16 if d == 0 else jnp.logical_not(fwd16)
        b_hi, b_lo = _split_bf16(
            jnp.concatenate([jnp.where(msk, bbt_re, 0.0), jnp.where(msk, -bbt_im, 0.0)], axis=1))
        r0.append(_dot_nt(b_hi, g_hi) + (_dot_nt(b_hi, g_lo) + _dot_nt(b_lo, g_hi)))
    lane = lax.broadcasted_iota(jnp.int32, (S5_CH, kc), 1)
    for s in range(S5_CHUNK):
        f = jnp.where(lane >= S5_CH * s, pltpu.roll(r0[0], S5_CH * s, 1), 0.0) if s else r0[0]
        b = jnp.where(lane < S5_CH * (s + 1), pltpu.roll(r0[1], (S5_CH * (s + 1)) % kc, 1), 0.0)
        w_ref[s * S5_CH:(s + 1) * S5_CH, :] = (f + b).astype(BF16)
    e_re, e_im = _cmul(ps_r, ps_i, bt_re, bt_im)
    ws_ref[:, 0:2 * p] = e_re.astype(BF16)
    ws_ref[:, 2 * p:4 * p] = e_im.astype(BF16)
    g1_re, g1_im = _cmul(g_re, g_im, ab_re, ab_im)
    wc_ref[:, 0:2 * p] = g1_re.astype(BF16)
    wc_ref[:, 2 * p:4 * p] = (-g1_im).astype(BF16)

    def run(xt_ref, yt_ref, nb, n_seg, hr, hm):
        rows = xt_ref.shape[1] // S5_CHUNK
        nbx = nb * n_seg
        m = rows // nbx
        xt = jnp.concatenate([xt_ref[:, s * rows:(s + 1) * rows] for s in range(S5_CHUNK)], axis=0)
        xf = xt.astype(F32).T
        x = xf.astype(BF16)
        y = _dot(x, w_ref[...]) + dsk_ref[...] * xf
        s_all = _dot(x, ws_ref[...])
        s_ref[0, 0:rows, :] = s_all[:, 0:2 * p]
        s_ref[1, 0:rows, :] = s_all[:, 2 * p:4 * p]

        def step(ar, ai, hr, hm, plane_r, plane_i, rf, rb):
            isf = lax.broadcasted_iota(jnp.int32, hr.shape, 1) < p
            sr = jnp.where(isf, plane_r[rf, :], plane_r[rb, :])
            sm = jnp.where(isf, plane_i[rf, :], plane_i[rb, :])
            return ar * hr - ai * hm + sr, ar * hm + ai * hr + sm

        def scan(hr, hm, store):
            arb = jnp.broadcast_to(ac_r, (nbx, 2 * p))
            aib = jnp.broadcast_to(ac_i, (nbx, 2 * p))
            for k in range(m):
                rf = pl.ds(k * nbx, nbx)
                rb = pl.ds((m - 1 - k) * nbx, nbx)
                if store:
                    s_ref[2, rf, :] = hr
                    s_ref[3, rf, :] = hm
                    s_ref[4, rb, :] = hr
                    s_ref[5, rb, :] = hm
                hr, hm = step(arb, aib, hr, hm, s_ref.at[0], s_ref.at[1], rf, rb)
            return hr, hm

        if n_seg > 1:
            zeros = jnp.zeros((nbx, 2 * p), F32)
            f_ref[0], f_ref[1] = scan(zeros, zeros, False)
            sr_, si_ = ac_r, ac_i
            for _ in range(m - 1):
                sr_, si_ = _cmul(sr_, si_, ac_r, ac_i)
            sr_ = jnp.broadcast_to(sr_, (nb, 2 * p))
            si_ = jnp.broadcast_to(si_, (nb, 2 * p))
            for j in range(n_seg):
                rf = pl.ds(j, nb, stride=n_seg)
                rb = pl.ds(n_seg - 1 - j, nb, stride=n_seg)
                f_ref[2, rf, :] = hr
                f_ref[3, rf, :] = hm
                f_ref[4, rb, :] = hr
                f_ref[5, rb, :] = hm
                hr, hm = step(sr_, si_, hr, hm, f_ref.at[0], f_ref.at[1], rf, rb)
            isf = lax.broadcasted_iota(jnp.int32, (nbx, 2 * p), 1) < p
            hr = jnp.where(isf, f_ref[2], f_ref[4])
            hm = jnp.where(isf, f_ref[3], f_ref[5])
        hr, hm = scan(hr, hm, True)
        isf_rows = lax.broadcasted_iota(jnp.int32, (rows, 2 * p), 1) < p
        h_prev = jnp.concatenate([jnp.where(isf_rows, s_ref[2, 0:rows, :], s_ref[4, 0:rows, :]),
                                  jnp.where(isf_rows, s_ref[3, 0:rows, :], s_ref[5, 0:rows, :])], axis=1)
        yt = (y + _dot_nt(h_prev.astype(BF16), wc_ref[...])).T
        for t in range(S5_CHUNK):
            yt_ref[:, t * rows:(t + 1) * rows] = yt[t * S5_CH:(t + 1) * S5_CH, :].astype(yt_ref.dtype)
        return hr, hm

    zeros = jnp.zeros((nb_p, 2 * p), F32)
    hr, hm = run(xp_ref, yp_ref, nb_p, 1, zeros, zeros)
    st_ref[:, 0:2 * p] = hr
    st_ref[:, 2 * p:4 * p] = hm
    run(xs_ref, ys_ref, nb_s, n_seg_s, h0_ref[:, 0:2 * p], h0_ref[:, 2 * p:4 * p])


def _s5(utp, uts, h0, a_re, a_im, log_dt, bt_re, bt_im, c_re, c_im, d_skip, nb_p, nb_s, n_seg_s):
    d, lanes_p = utp.shape
    lanes_s = uts.shape[1]
    g = d // S5_CH
    rows_p = lanes_p // S5_CHUNK
    rows_s = lanes_s // S5_CHUNK
    kc = S5_CHUNK * S5_CH
    p = S5_P

    gps = S5_GROUPS_PER_STEP

    def gspec(shape):
        return pl.BlockSpec((gps,) + shape, lambda i: (i, 0, 0))

    def tspec(lanes):
        return pl.BlockSpec((gps * S5_CH, lanes), lambda i: (i, 0))

    return pl.pallas_call(
        functools.partial(_s5_kernel, nb_p=nb_p, nb_s=nb_s, n_seg_s=n_seg_s),
        grid=(g // gps,),
        in_specs=[tspec(lanes_p), tspec(lanes_s), gspec((nb_s, 4 * p)),
                  gspec((1, 2 * p)), gspec((1, 2 * p)), gspec((1, 2 * p)),
                  gspec((S5_CH, 2 * p)), gspec((S5_CH, 2 * p)), gspec((S5_CH, 2 * p)), gspec((S5_CH, 2 * p)),
                  gspec((1, kc))],
        out_specs=[tspec(lanes_p), tspec(lanes_s), gspec((nb_p, 4 * p))],
        out_shape=[jax.ShapeDtypeStruct((d, lanes_p), BF16),
                   jax.ShapeDtypeStruct((d, lanes_s), BF16),
                   jax.ShapeDtypeStruct((g, nb_p, 4 * p), F32)],
        scratch_shapes=[pltpu.VMEM((gps, kc, kc), BF16), pltpu.VMEM((gps, kc, 4 * p), BF16),
                        pltpu.VMEM((gps, kc, 4 * p), BF16),
                        pltpu.VMEM((gps, 6, max(rows_p, rows_s), 2 * p), F32),
                        pltpu.VMEM((gps, 6, nb_s * n_seg_s, 2 * p), F32)],
        compiler_params=_cparams("parallel"),
        name="s5",
    )(utp, uts, h0, a_re, a_im, log_dt, bt_re, bt_im, c_re, c_im,
      jnp.tile(d_skip.reshape(g, 1, S5_CH), (1, 1, S5_CHUNK)))


def _glu_kernel(yt0_ref, ytn_ref, z_ref, w_ref, b_ref, a_ref, yb_ref, *, tn):
    d = w_ref.shape[0]
    i = pl.program_id(0)
    slot = i % 2

    @pl.when(i == 0)
    def _():
        yb_ref[0] = jax.nn.gelu(yt0_ref[...].astype(F32)).astype(BF16).T

    for c in range(d // tn):
        cs = slice(c * tn, (c + 1) * tn)
        yb = yb_ref[slot]
        gl = _dot(yb, w_ref[:, cs]) + b_ref[:, cs]
        a = yb[:, cs].astype(F32) * jax.nn.sigmoid(gl) * _silu(z_ref[:, cs].astype(F32))
        a_ref[:, cs] = a.astype(a_ref.dtype)
        yb_ref[1 - slot, :, cs] = jax.nn.gelu(ytn_ref[cs, :].astype(F32)).astype(BF16).T


def _glu(yt, z, w_glu, b_glu, tn):
    d, m = yt.shape
    tm = LANE_TILE
    ni = m // tm
    return pl.pallas_call(
        functools.partial(_glu_kernel, tn=tn),
        grid=(ni,),
        in_specs=[pl.BlockSpec((d, tm), lambda i: (0, 0), pipeline_mode=pl.Buffered(1)),
                  pl.BlockSpec((d, tm), lambda i: (0, jnp.minimum(i + 1, ni - 1))),
                  pl.BlockSpec((tm, d), lambda i: (i, 0)),
                  pl.BlockSpec((d, d), lambda i: (0, 0), pipeline_mode=pl.Buffered(1)),
                  pl.BlockSpec((1, d), lambda i: (0, 0))],
        out_specs=pl.BlockSpec((tm, d), lambda i: (i, 0)),
        out_shape=jax.ShapeDtypeStruct((m, d), BF16),
        scratch_shapes=[pltpu.VMEM((2, tm, d), BF16)],
        compiler_params=_cparams("arbitrary"),
        name="glu",
    )(yt, yt, z, w_glu, b_glu.reshape(1, d))


def _out_norm_kernel(x_hbm, a_ref, gate_ref, fg_ref, w_ref, o_hbm, xbuf, xsem, obuf, osem, *, n_cond):
    nq = x_hbm.shape[0]
    x = _fetch_slots(x_hbm, xbuf, xsem)
    xn = x + _cond_rows(gate_ref, x.shape[0], nq, n_cond) * _dot(a_ref[...], w_ref[...])
    res = xn * lax.rsqrt(jnp.mean(xn * xn, axis=-1, keepdims=True) + EPS) * fg_ref[...]
    _store_slots(res, o_hbm, obuf, osem)


def _out_norm(a, x4, mod3, cond0, n_cond, final_g, w_out):
    nq, m, _, d = x4.shape
    _slots_per_step(x4)
    return pl.pallas_call(
        functools.partial(_out_norm_kernel, n_cond=n_cond),
        grid=(nq * m * S5_CHUNK // LANE_TILE,),
        in_specs=[pl.BlockSpec(memory_space=pl.ANY),
                  pl.BlockSpec((LANE_TILE, d), lambda l: (l, 0)),
                  pl.BlockSpec((n_cond, 1, d), lambda l: (cond0 // n_cond, 0, 2)),
                  pl.BlockSpec((1, d), lambda l: (0, 0)),
                  pl.BlockSpec((d, d), lambda l: (0, 0), pipeline_mode=pl.Buffered(1))],
        out_specs=pl.BlockSpec(memory_space=pl.ANY),
        out_shape=jax.ShapeDtypeStruct(x4.shape, F32),
        scratch_shapes=2 * _slot_scratch(d),
        compiler_params=_cparams("arbitrary"),
        name="out_norm",
    )(x4, a, mod3, final_g.reshape(1, d), w_out)


def _both_dirs(a):
    _, g, r, p = a.shape
    return a.transpose(1, 2, 0, 3).reshape(g, r, 2 * p)


def kernel(x_prompt, x_sample, cache_l0_k, cache_l0_v, state_l1_s5, c, c_ctx, l0_norm_g, l0_w_ada, l0_b_ada, l0_w_in, l0_rpb, l0_w_out, l1_norm_g, l1_w_ada, l1_b_ada, l1_w_in, l1_a_re, l1_a_im, l1_log_dt, l1_b_re, l1_b_im, l1_c_re, l1_c_im, l1_d, l1_w_glu, l1_b_glu, l1_w_out, final_norm_g):
    bp, seq, d = x_prompt.shape
    bs, n_tok, _ = x_sample.shape
    lc = cache_l0_k.shape[1]
    g = d // S5_CH
    p = S5_P
    xp = x_prompt.reshape(bp * seq, d)
    xs = x_sample.reshape(bs * n_tok, d)

    ctx = bs
    cond8 = jnp.zeros((8, d), F32).at[0:bs].set(c).at[ctx].set(c_ctx)
    mods = _ada(cond8, (l0_w_ada, l1_w_ada), (l0_b_ada, l1_b_ada))
    mod0 = mods[0].reshape(8, 1, 3 * d)
    mod1 = mods[1].reshape(8, 1, 3 * d)

    w_in0 = l0_w_in.astype(BF16)
    w_out0 = l0_w_out.astype(BF16)
    w_in1 = l1_w_in.astype(BF16)
    w_glu = l1_w_glu.astype(BF16)
    w_out1 = l1_w_out.astype(BF16)
    k_ctx = cache_l0_k.reshape(bs, lc, d)
    v_ctx = cache_l0_v.reshape(bs, lc, d)

    qs, ks, vs, zs = _inproj(xs, l0_norm_g, mod0, 0, n_tok, w_in0,
                             ((0, BF16), (1, BF16), (2, BF16), (3, BF16)), 1024, 256)
    qp, kp, kpb, vp, vpb, zp = _inproj(xp, l0_norm_g, mod0, ctx, bp * seq, w_in0,
                                       ((0, BF16), (1, F32), (1, BF16), (2, F32), (2, BF16), (3, BF16)), 1024, 256,
                                       after=(qs, w_out0, w_in1, w_glu, w_out1, mod1, k_ctx, v_ctx))
    op = _ctx_attn(qp, kpb, vpb, seq)
    rpb_pad = jnp.zeros((N_HEADS, 16, LANES), F32).at[:, :RPB_R, :RPB_C].set(l0_rpb)
    os_ = _latent_na(qs, ks, vs, k_ctx, v_ctx, rpb_pad, n_tok)
    n_seg_s = max(1, n_tok // S5_CHUNK // S5_SEG)
    x1p = _gated_out(op, zp, xp, mod0, ctx, bp * seq, w_out0, 512).reshape(bp, seq // S5_CHUNK, S5_CHUNK, d)
    x1s = _gated_out(os_, zs, xs, mod0, 0, n_tok, w_out0, 512).reshape(bs * n_seg_s, -1, S5_CHUNK, d)

    utp, zp1 = _inproj_t(x1p, l1_norm_g, mod1, ctx, 1, w_in1)
    uts, zs1 = _inproj_t(x1s, l1_norm_g, mod1, 0, bs, w_in1)
    h0 = state_l1_s5.transpose(3, 0, 2, 1, 4).reshape(g, bs, 4 * p)
    log_dt = jnp.broadcast_to(l1_log_dt[:, :, None, None], (2, g, 1, p))
    ytp, yts, st = _s5(
        utp, uts, h0,
        _both_dirs(l1_a_re[:, :, None, :]), _both_dirs(l1_a_im[:, :, None, :]), _both_dirs(log_dt),
        _both_dirs(l1_b_re.transpose(0, 1, 3, 2)), _both_dirs(l1_b_im.transpose(0, 1, 3, 2)),
        _both_dirs(l1_c_re), _both_dirs(l1_c_im), l1_d, bp, bs, n_seg_s)
    y_prompt = _out_norm(_glu(ytp, zp1, w_glu, l1_b_glu, 512), x1p, mod1, ctx, 1, final_norm_g, w_out1)
    y_sample = _out_norm(_glu(yts, zs1, w_glu, l1_b_glu, 512), x1s, mod1, 0, bs, final_norm_g, w_out1)
    new_state = st.reshape(g, bp, 2, 2, p).transpose(1, 3, 2, 0, 4)
    return (y_prompt.reshape(bp, seq, d), y_sample.reshape(bs, n_tok, d),
            kp.reshape(bp, seq, N_HEADS, HEAD_DIM), vp.reshape(bp, seq, N_HEADS, HEAD_DIM), new_state)
```

```python
import functools

import jax
import jax.numpy as jnp
from jax import lax
from jax.experimental import pallas as pl
from jax.experimental.pallas import tpu as pltpu

F32 = jnp.float32
BF16 = jnp.bfloat16

EPS = 1e-6
N_HEADS = 16
HEAD_DIM = 128
GRID_W = 64
NA_ROWS = 8
NA_COLS = 16
RPB_R = 2 * NA_ROWS - 1
RPB_C = 2 * NA_COLS - 1
S5_CH = 16
S5_P = 64
S5_CHUNK = 16
S5_SEG = 8
NEG = -1e30
LANES = 128
VMEM_LIMIT = 56 * 1024 * 1024


def _cparams(*sem):
    return pltpu.CompilerParams(dimension_semantics=sem, vmem_limit_bytes=VMEM_LIMIT)


def _dot(a, b):
    return jnp.dot(a, b, preferred_element_type=F32)


def _dot_nt(a, b, precision=None):
    return lax.dot_general(a, b, (((1,), (1,)), ((), ())), preferred_element_type=F32,
                           precision=precision)


def _silu(x):
    return x * jax.nn.sigmoid(x)


def _ada_kernel(cond_ref, *refs, n_layers):
    w_refs, b_refs, o_ref = refs[:n_layers], refs[n_layers:2 * n_layers], refs[2 * n_layers]
    s = _silu(cond_ref[...]).astype(BF16)
    layer = pl.program_id(0)
    for i in range(n_layers):
        @pl.when(layer == i)
        def _(i=i):
            o_ref[...] = _dot(s, w_refs[i][...].astype(BF16)) + b_refs[i][...]


def _ada(cond8, w_adas, b_adas):
    n_layers = len(w_adas)
    d, n = w_adas[0].shape
    tn = 1024
    nj = n // tn

    def own(i):
        return lambda l, j: (0, jnp.where(l == i, j, jnp.where(l < i, 0, nj - 1)))

    return pl.pallas_call(
        functools.partial(_ada_kernel, n_layers=n_layers),
        grid=(n_layers, nj),
        in_specs=[pl.BlockSpec((8, d), lambda l, j: (0, 0))]
        + [pl.BlockSpec((d, tn), own(i)) for i in range(n_layers)]
        + [pl.BlockSpec((1, tn), own(i)) for i in range(n_layers)],
        out_specs=pl.BlockSpec((None, 8, tn), lambda l, j: (l, 0, j)),
        out_shape=jax.ShapeDtypeStruct((n_layers, 8, n), F32),
        compiler_params=_cparams("arbitrary", "arbitrary"),
        name="ada",
    )(cond8, *w_adas, *[b.reshape(1, n) for b in b_adas])


def _norm_mod(x, g, shift, scale):
    y = x * lax.rsqrt(jnp.mean(x * x, axis=-1, keepdims=True) + EPS) * g
    return (y * (1.0 + scale) + shift).astype(BF16)


def _inproj_kernel(x0_ref, xn_ref, g_ref, shift0_ref, scale0_ref, shiftn_ref, scalen_ref, *rest, out_split,
                   head_split, n_after):
    n_split = max(out_split + head_split) + 1
    w_refs = rest[:n_split]
    rest = rest[n_split + n_after:]
    o_refs = rest[:len(out_split)]
    hd_refs = rest[len(out_split):len(out_split) + len(head_split)]
    h_ref = rest[len(out_split) + len(head_split)]
    i = pl.program_id(0)
    j = pl.program_id(1)
    nj = pl.num_programs(1)
    slot = i % 2

    @pl.when((i == 0) & (j == 0))
    def _():
        h_ref[0] = _norm_mod(x0_ref[...], g_ref[...], shift0_ref[...], scale0_ref[...])

    if head_split:
        stage_ref, sem = rest[len(out_split) + len(head_split) + 1:]
        tm, tn = stage_ref.shape[2:]
        t = i * nj + j

        def head_copies(tt):
            ii, jj = tt // nj, tt % nj
            return [pltpu.make_async_copy(stage_ref.at[tt % 2, n, :, pl.ds(hh * HEAD_DIM, HEAD_DIM)],
                                          hd.at[pl.ds(ii * tm, tm), jj * (tn // HEAD_DIM) + hh, :], sem.at[tt % 2])
                    for n, hd in enumerate(hd_refs) for hh in range(tn // HEAD_DIM)]

        @pl.when(t >= 2)
        def _():
            for c in head_copies(t - 2):
                c.wait()

    h = h_ref[slot]
    for s, w_ref in enumerate(w_refs):
        r = _dot(h, w_ref[...])
        for o_ref, o_s in zip(o_refs, out_split):
            if o_s == s:
                o_ref[...] = r.astype(o_ref.dtype)
        if s in head_split:
            stage_ref[t % 2, head_split.index(s)] = r
    rs = xn_ref.shape[0]
    h_ref[1 - slot, pl.ds(pl.multiple_of(j * rs, rs), rs), :] = _norm_mod(
        xn_ref[...], g_ref[...], shiftn_ref[...], scalen_ref[...])

    if head_split:
        for c in head_copies(t):
            c.start()

        @pl.when(t == pl.num_programs(0) * nj - 1)
        def _():
            @pl.when(t >= 1)
            def _():
                for c in head_copies(t - 1):
                    c.wait()

            for c in head_copies(t):
                c.wait()


def _inproj(x, g, mod3, cond0, rows_per_cond, w, outs, tm, tn, head_split=(), after=()):
    m, d = x.shape
    out_split = tuple(s for s, _ in outs)
    out_dtypes = [dt for _, dt in outs]
    head_split = tuple(head_split)
    n_split = max(out_split + head_split) + 1
    ni = m // tm
    nj = d // tn

    def nxt(i):
        return jnp.minimum(i + 1, ni - 1)

    def cond_of(i):
        return cond0 + (i * tm) // rows_per_cond

    in_specs = [pl.BlockSpec((tm, d), lambda i, j: (0, 0), pipeline_mode=pl.Buffered(1)),
                pl.BlockSpec((tm // nj, d), lambda i, j: (nxt(i) * nj + j, 0)),
                pl.BlockSpec((1, d), lambda i, j: (0, 0)),
                pl.BlockSpec((None, 1, d), lambda i, j: (cond_of(0), 0, 0)),
                pl.BlockSpec((None, 1, d), lambda i, j: (cond_of(0), 0, 1)),
                pl.BlockSpec((None, 1, d), lambda i, j: (cond_of(nxt(i)), 0, 0)),
                pl.BlockSpec((None, 1, d), lambda i, j: (cond_of(nxt(i)), 0, 1))]
    for s in range(n_split):
        in_specs.append(pl.BlockSpec((d, tn), lambda i, j, s=s: (0, s * nj + j)))
    in_specs += [pl.BlockSpec(memory_space=pl.ANY)] * len(after)
    scratch = [pltpu.VMEM((2, tm, d), BF16)]
    if head_split:
        scratch += [pltpu.VMEM((2, len(head_split), tm, tn), F32), pltpu.SemaphoreType.DMA((2,))]
    return pl.pallas_call(
        functools.partial(_inproj_kernel, out_split=out_split, head_split=head_split, n_after=len(after)),
        grid=(ni, nj),
        in_specs=in_specs,
        out_specs=[pl.BlockSpec((tm, tn), lambda i, j: (i, j)) for _ in outs]
        + [pl.BlockSpec(memory_space=pl.ANY)] * len(head_split),
        out_shape=[jax.ShapeDtypeStruct((m, d), dt) for dt in out_dtypes]
        + [jax.ShapeDtypeStruct((m, d // HEAD_DIM, HEAD_DIM), F32)] * len(head_split),
        scratch_shapes=scratch,
        compiler_params=_cparams("arbitrary", "arbitrary"),
        name="inproj",
    )(x, x, g.reshape(1, d), mod3, mod3, mod3, mod3, *([w] * n_split), *after)


def _softmax_pv(parts):
    m = parts[0][0].max(axis=-1, keepdims=True)
    for s, _ in parts[1:]:
        m = jnp.maximum(m, s.max(axis=-1, keepdims=True))
    l = None
    o = None
    for s, v in parts:
        e = jnp.exp(s - m)
        li = e.sum(axis=-1, keepdims=True)
        oi = _dot(e.astype(BF16), v)
        l = li if l is None else l + li
        o = oi if o is None else o + oi
    return o / l


def _ctx_attn_kernel(q_ref, k_ref, v_ref, o_ref):
    scale = HEAD_DIM ** -0.5
    for h in range(N_HEADS):
        sl = slice(h * HEAD_DIM, (h + 1) * HEAD_DIM)
        s = _dot_nt(q_ref[:, sl], k_ref[:, sl]) * scale
        o_ref[:, sl] = _softmax_pv([(s, v_ref[:, sl])]).astype(o_ref.dtype)


def _ctx_attn(q, k, v, seq):
    m, d = q.shape
    spec = pl.BlockSpec((seq, d), lambda b: (b, 0))
    return pl.pallas_call(
        _ctx_attn_kernel,
        grid=(m // seq,),
        in_specs=[spec, spec, spec],
        out_specs=spec,
        out_shape=jax.ShapeDtypeStruct((m, d), BF16),
        compiler_params=_cparams("parallel"),
        name="ctx_attn",
    )(q, k, v)


def _na_window_start(r, rows):
    return min(max(r - NA_ROWS // 2, 0), rows - NA_ROWS)


def _na_kernel(q_ref, k_ref, v_ref, kc_ref, vc_ref, rpb_ref, o_ref, *, rows, q_rows):
    scale = HEAD_DIM ** -0.5
    w = GRID_W
    qc = lax.broadcasted_iota(jnp.int32, (w, LANES), 0)
    lane = lax.broadcasted_iota(jnp.int32, (w, LANES), 1)
    c0 = jnp.clip(qc - NA_COLS // 2, 0, w - NA_COLS)
    ok_l = (lane >= c0) & (lane < c0 + NA_COLS)
    ok_r = (lane - w >= c0) & (lane - w < c0 + NA_COLS)
    neg_tile = jnp.full((w, LANES), NEG, F32)
    tile_l, tile_r = [], []
    for d in range(RPB_R):
        base = jnp.broadcast_to(rpb_ref[d:d + 1, :], (w, LANES))
        left = pltpu.roll(base, LANES - (NA_COLS - 1), 1, stride=1, stride_axis=0)
        right = pltpu.roll(base, w - (NA_COLS - 1), 1, stride=1, stride_axis=0)
        tile_l.append(jnp.where(ok_l, left, NEG))
        tile_r.append(jnp.where(ok_r, right, NEG))

    kc = kc_ref[...].astype(BF16)
    vc = vc_ref[...].astype(BF16)
    n_groups = rows // q_rows
    for gi in range(n_groups):
        rs = list(range(gi * q_rows, (gi + 1) * q_rows))
        klo = min(_na_window_start(r, rows) for r in rs) // 2 * 2
        khi = -(-(max(_na_window_start(r, rows) for r in rs) + NA_ROWS) // 2) * 2
        bias_rows = []
        for r in rs:
            r0 = _na_window_start(r, rows)
            tiles = []
            for kr in range(klo, khi, 2):
                ok0 = r0 <= kr < r0 + NA_ROWS
                ok1 = r0 <= kr + 1 < r0 + NA_ROWS
                t0 = tile_l[kr - r + NA_ROWS - 1] if ok0 else neg_tile
                t1 = tile_r[kr + 1 - r + NA_ROWS - 1] if ok1 else neg_tile
                tiles.append(jnp.maximum(t0, t1) if (ok0 or ok1) else neg_tile)
            bias_rows.append(jnp.concatenate(tiles, axis=1))
        bias = jnp.concatenate(bias_rows, axis=0)
        q = q_ref[gi * q_rows * w:(gi + 1) * q_rows * w, :]
        kl = k_ref[klo * w:khi * w, :]
        vl = v_ref[klo * w:khi * w, :]
        s_loc = _dot_nt(q, kl) * scale + bias
        s_ctx = _dot_nt(q, kc) * scale
        o = _softmax_pv([(s_loc, vl), (s_ctx, vc)])
        o_ref[gi * q_rows * w:(gi + 1) * q_rows * w, :] = o.astype(o_ref.dtype)


def _latent_na(q, k, v, k_ctx, v_ctx, rpb_pad, n_tok):
    m, d = q.shape
    nb = m // n_tok
    lc = k_ctx.shape[1]
    rows = n_tok // GRID_W
    spec = pl.BlockSpec((n_tok, HEAD_DIM), lambda b, h: (b, h))
    cspec = pl.BlockSpec((None, lc, HEAD_DIM), lambda b, h: (b, 0, h))
    return pl.pallas_call(
        functools.partial(_na_kernel, rows=rows, q_rows=4),
        grid=(nb, N_HEADS),
        in_specs=[spec, spec, spec, cspec, cspec,
                  pl.BlockSpec((None, 16, LANES), lambda b, h: (h, 0, 0))],
        out_specs=spec,
        out_shape=jax.ShapeDtypeStruct((m, d), BF16),
        compiler_params=_cparams("parallel", "parallel"),
        name="latent_na",
    )(q, k, v, k_ctx, v_ctx, rpb_pad)


def _gated_out_kernel(o_ref, z_ref, x_ref, gate_ref, w_ref, y_ref):
    a = o_ref[...].astype(F32) * _silu(z_ref[...].astype(F32))
    y_ref[...] = x_ref[...] + gate_ref[...] * _dot(a.astype(BF16), w_ref[...])


def _gated_out(o, z, x, mod3, cond0, rows_per_cond, w_out, tm):
    m, d = x.shape
    row = pl.BlockSpec((tm, d), lambda i: (i, 0))
    return pl.pallas_call(
        _gated_out_kernel,
        grid=(m // tm,),
        in_specs=[row, row, row,
                  pl.BlockSpec((None, 1, d), lambda i: (cond0 + (i * tm) // rows_per_cond, 0, 2)),
                  pl.BlockSpec((d, d), lambda i: (0, 0), pipeline_mode=pl.Buffered(1))],
        out_specs=row,
        out_shape=jax.ShapeDtypeStruct((m, d), F32),
        compiler_params=_cparams("parallel"),
        name="gated_out",
    )(o, z, x, mod3, w_out)


LANE_TILE = 512


def _slots_per_step(x4):
    nq, m, _, _ = x4.shape
    assert LANE_TILE % (nq * m) == 0 and S5_CHUNK % (LANE_TILE // (nq * m)) == 0
    return LANE_TILE // (nq * m)


def _slot_scratch(d):
    return [pltpu.VMEM((2, LANE_TILE, d), F32), pltpu.SemaphoreType.DMA((2,))]


def _slot_copies(x_hbm, buf, sem, l, to_hbm):
    nq, m, _, _ = x_hbm.shape
    n_x = buf.shape[1] // (nq * m)
    copies = []
    for xi in range(n_x):
        for k in range(m):
            hbm = x_hbm.at[:, k, l * n_x + xi, :]
            vmem = buf.at[l % 2, pl.ds((xi * m + k) * nq, nq), :]
            src, dst = (vmem, hbm) if to_hbm else (hbm, vmem)
            copies.append(pltpu.make_async_copy(src, dst, sem.at[l % 2]))
    return copies


def _fetch_slots(x_hbm, buf, sem):
    l = pl.program_id(0)

    @pl.when(l == 0)
    def _():
        for c in _slot_copies(x_hbm, buf, sem, l, False):
            c.start()

    @pl.when(l + 1 < pl.num_programs(0))
    def _():
        for c in _slot_copies(x_hbm, buf, sem, l + 1, False):
            c.start()

    for c in _slot_copies(x_hbm, buf, sem, l, False):
        c.wait()
    return buf[l % 2]


def _store_slots(res, o_hbm, buf, sem):
    l = pl.program_id(0)

    def wait(ll):
        for c in _slot_copies(o_hbm, buf, sem, ll, True):
            c.wait()

    @pl.when(l >= 2)
    def _():
        wait(l - 2)

    buf[l % 2] = res
    for c in _slot_copies(o_hbm, buf, sem, l, True):
        c.start()

    @pl.when(l == pl.num_programs(0) - 1)
    def _():
        @pl.when(l >= 1)
        def _():
            wait(l - 1)

        wait(l)


def _cond_rows(ref, n_rows, nq, n_cond):
    out = ref[n_cond - 1]
    if n_cond > 1:
        b = (lax.broadcasted_iota(jnp.int32, (n_rows, 1), 0) % nq) // (nq // n_cond)
        for i in range(n_cond - 2, -1, -1):
            out = jnp.where(b == i, ref[i], out)
    return out


def _inproj_t_kernel(x_hbm, g_ref, shift_ref, scale_ref, wu_ref, wz_ref, u_ref, z_ref, buf, sem, *, n_cond):
    nq = x_hbm.shape[0]
    x = _fetch_slots(x_hbm, buf, sem)
    n_rows = x.shape[0]
    h = _norm_mod(x, g_ref[...], _cond_rows(shift_ref, n_rows, nq, n_cond),
                  _cond_rows(scale_ref, n_rows, nq, n_cond))
    u_ref[...] = _dot(h, wu_ref[...]).T.astype(u_ref.dtype)
    z_ref[...] = _dot(h, wz_ref[...]).astype(z_ref.dtype)


def _inproj_t(x4, g, mod3, cond0, n_cond, w_in):
    nq, m, _, d = x4.shape
    n_tok = nq * m * S5_CHUNK
    _slots_per_step(x4)

    def mspec(col):
        return pl.BlockSpec((n_cond, 1, d), lambda l: (cond0 // n_cond, 0, col))

    def wspec(col):
        return pl.BlockSpec((d, d), lambda l: (0, col), pipeline_mode=pl.Buffered(1))

    return pl.pallas_call(
        functools.partial(_inproj_t_kernel, n_cond=n_cond),
        grid=(n_tok // LANE_TILE,),
        in_specs=[pl.BlockSpec(memory_space=pl.ANY),
                  pl.BlockSpec((1, d), lambda l: (0, 0)), mspec(0), mspec(1), wspec(0), wspec(1)],
        out_specs=[pl.BlockSpec((d, LANE_TILE), lambda l: (0, l)), pl.BlockSpec((LANE_TILE, d), lambda l: (l, 0))],
        out_shape=[jax.ShapeDtypeStruct((d, n_tok), BF16), jax.ShapeDtypeStruct((n_tok, d), BF16)],
        scratch_shapes=_slot_scratch(d),
        compiler_params=_cparams("arbitrary"),
        name="inproj_t",
    )(x4, g.reshape(1, d), mod3, mod3, w_in, w_in)


def _cmul(ar, ai, br, bi):
    return ar * br - ai * bi, ar * bi + ai * br


def _split_bf16(a):
    hi = a.astype(BF16)
    return hi, (a - hi.astype(F32)).astype(BF16)


def _s5_powers(ar, ai):
    width = ar.shape[-1]
    pw = [(jnp.ones_like(ar), jnp.zeros_like(ar))]
    for _ in range(S5_CHUNK):
        pw.append(_cmul(pw[-1][0], pw[-1][1], ar, ai))
    is_fwd = lax.broadcasted_iota(jnp.int32, (1, width), 1) < width // 2

    def pattern(fwd_ascending):
        out = []
        for part in range(2):
            blocks = []
            for j in range(S5_CHUNK):
                ef, eb = (j, S5_CHUNK - 1 - j) if fwd_ascending else (S5_CHUNK - 1 - j, j)
                blocks.append(jnp.broadcast_to(jnp.where(is_fwd, pw[ef][part], pw[eb][part]), (S5_CH, width)))
            out.append(jnp.concatenate(blocks, axis=0))
        return out

    return pattern(True), pattern(False), pw[S5_CHUNK]


S5_GROUPS_PER_STEP = 8


def _s5_kernel(xp_ref, xs_ref, *rest, nb_p, nb_s, n_seg_s):
    yp_ref, ys_ref = rest[9], rest[10]
    for gi in range(S5_GROUPS_PER_STEP):
        ch = pl.ds(gi * S5_CH, S5_CH)
        _s5_group(xp_ref.at[ch], xs_ref.at[ch], *[r.at[gi] for r in rest[:9]], yp_ref.at[ch], ys_ref.at[ch],
                  *[r.at[gi] for r in rest[11:]], nb_p=nb_p, nb_s=nb_s, n_seg_s=n_seg_s)


def _s5_group(xp_ref, xs_ref, h0_ref, are_ref, aim_ref, ldt_ref, bre_ref, bim_ref, cre_ref, cim_ref, dsk_ref,
              yp_ref, ys_ref, st_ref, w_ref, ws_ref, wc_ref, s_ref, f_ref, *, nb_p, nb_s, n_seg_s):
    p = S5_P
    kc = S5_CHUNK * S5_CH
    a_re = are_ref[...]
    a_im = aim_ref[...]
    dt = jnp.exp(ldt_ref[...])
    mag = jnp.exp(a_re * dt)
    ab_re = mag * jnp.cos(a_im * dt)
    ab_im = mag * jnp.sin(a_im * dt)
    den = a_re * a_re + a_im * a_im
    nr = ab_re - 1.0
    f_re = (nr * a_re + ab_im * a_im) / den
    f_im = (ab_im * a_re - nr * a_im) / den
    bbt_re, bbt_im = _cmul(f_re, f_im, bre_ref[...], bim_ref[...])

    (pg_r, pg_i), (ps_r, ps_i), (ac_r, ac_i) = _s5_powers(ab_re, ab_im)
    c_re = jnp.concatenate([cre_ref[...]] * S5_CHUNK, axis=0)
    c_im = jnp.concatenate([cim_ref[...]] * S5_CHUNK, axis=0)
    bt_re = jnp.concatenate([bbt_re] * S5_CHUNK, axis=0)
    bt_im = jnp.concatenate([bbt_im] * S5_CHUNK, axis=0)

    g_re, g_im = _cmul(pg_r, pg_i, c_re, c_im)
    fwd16 = lax.broadcasted_iota(jnp.int32, (S5_CH, 2 * p), 1) < p
    g_hi, g_lo = _split_bf16(jnp.concatenate([g_re, g_im], axis=1))
    r0 = []
    for d in range(2):
        msk = fwd16 if d == 0 else jnp.logical_not(fwd16)
        b_hi, b_lo = _split_bf16(
            jnp.concatenate([jnp.where(msk, bbt_re, 0.0), jnp.where(msk, -bbt_im, 0.0)], axis=1))
        r0.append(_dot_nt(b_hi, g_hi) + (_dot_nt(b_hi, g_lo) + _dot_nt(b_lo, g_hi)))
    lane = lax.broadcasted_iota(jnp.int32, (S5_CH, kc), 1)
    for s in range(S5_CHUNK):
        f = jnp.where(lane >= S5_CH * s, pltpu.roll(r0[0], S5_CH * s, 1), 0.0) if s else r0[0]
        b = jnp.where(lane < S5_CH * (s + 1), pltpu.roll(r0[1], (S5_CH * (s + 1)) % kc, 1), 0.0)
        w_ref[s * S5_CH:(s + 1) * S5_CH, :] = (f + b).astype(BF16)
    e_re, e_im = _cmul(ps_r, ps_i, bt_re, bt_im)
    ws_ref[:, 0:2 * p] = e_re.astype(BF16)
    ws_ref[:, 2 * p:4 * p] = e_im.astype(BF16)
    g1_re, g1_im = _cmul(g_re, g_im, ab_re, ab_im)
    wc_ref[:, 0:2 * p] = g1_re.astype(BF16)
    wc_ref[:, 2 * p:4 * p] = (-g1_im).astype(BF16)

    def run(xt_ref, yt_ref, nb, n_seg, hr, hm):
        rows = xt_ref.shape[1] // S5_CHUNK
        nbx = nb * n_seg
        m = rows // nbx
        xt = jnp.concatenate([xt_ref[:, s * rows:(s + 1) * rows] for s in range(S5_CHUNK)], axis=0)
        xf = xt.astype(F32).T
        x = xf.astype(BF16)
        y = _dot(x, w_ref[...]) + dsk_ref[...] * xf
        s_all = _dot(x, ws_ref[...])
        s_ref[0, 0:rows, :] = s_all[:, 0:2 * p]
        s_ref[1, 0:rows, :] = s_all[:, 2 * p:4 * p]

        def step(ar, ai, hr, hm, plane_r, plane_i, rf, rb):
            isf = lax.broadcasted_iota(jnp.int32, hr.shape, 1) < p
            sr = jnp.where(isf, plane_r[rf, :], plane_r[rb, :])
            sm = jnp.where(isf, plane_i[rf, :], plane_i[rb, :])
            return ar * hr - ai * hm + sr, ar * hm + ai * hr + sm

        def scan(hr, hm, store):
            arb = jnp.broadcast_to(ac_r, (nbx, 2 * p))
            aib = jnp.broadcast_to(ac_i, (nbx, 2 * p))
            for k in range(m):
                rf = pl.ds(k * nbx, nbx)
                rb = pl.ds((m - 1 - k) * nbx, nbx)
                if store:
                    s_ref[2, rf, :] = hr
                    s_ref[3, rf, :] = hm
                    s_ref[4, rb, :] = hr
                    s_ref[5, rb, :] = hm
                hr, hm = step(arb, aib, hr, hm, s_ref.at[0], s_ref.at[1], rf, rb)
            return hr, hm

        if n_seg > 1:
            zeros = jnp.zeros((nbx, 2 * p), F32)
            f_ref[0], f_ref[1] = scan(zeros, zeros, False)
            sr_, si_ = ac_r, ac_i
            for _ in range(m - 1):
                sr_, si_ = _cmul(sr_, si_, ac_r, ac_i)
            sr_ = jnp.broadcast_to(sr_, (nb, 2 * p))
            si_ = jnp.broadcast_to(si_, (nb, 2 * p))
            for j in range(n_seg):
                rf = pl.ds(j, nb, stride=n_seg)
                rb = pl.ds(n_seg - 1 - j, nb, stride=n_seg)
                f_ref[2, rf, :] = hr
                f_ref[3, rf, :] = hm
                f_ref[4, rb, :] = hr
                f_ref[5, rb, :] = hm
                hr, hm = step(sr_, si_, hr, hm, f_ref.at[0], f_ref.at[1], rf, rb)
            isf = lax.broadcasted_iota(jnp.int32, (nbx, 2 * p), 1) < p
            hr = jnp.where(isf, f_ref[2], f_ref[4])
            hm = jnp.where(isf, f_ref[3], f_ref[5])
        hr, hm = scan(hr, hm, True)
        isf_rows = lax.broadcasted_iota(jnp.int32, (rows, 2 * p), 1) < p
        h_prev = jnp.concatenate([jnp.where(isf_rows, s_ref[2, 0:rows, :], s_ref[4, 0:rows, :]),
                                  jnp.where(isf_rows, s_ref[3, 0:rows, :], s_ref[5, 0:rows, :])], axis=1)
        yt = (y + _dot_nt(h_prev.astype(BF16), wc_ref[...])).T
        for t in range(S5_CHUNK):
            yt_ref[:, t * rows:(t + 1) * rows] = yt[t * S5_CH:(t + 1) * S5_CH, :].astype(yt_ref.dtype)
        return hr, hm

    zeros = jnp.zeros((nb_p, 2 * p), F32)
    hr, hm = run(xp_ref, yp_ref, nb_p, 1, zeros, zeros)
    st_ref[:, 0:2 * p] = hr
    st_ref[:, 2 * p:4 * p] = hm
    run(xs_ref, ys_ref, nb_s, n_seg_s, h0_ref[:, 0:2 * p], h0_ref[:, 2 * p:4 * p])


def _s5(utp, uts, h0, a_re, a_im, log_dt, bt_re, bt_im, c_re, c_im, d_skip, nb_p, nb_s, n_seg_s):
    d, lanes_p = utp.shape
    lanes_s = uts.shape[1]
    g = d // S5_CH
    rows_p = lanes_p // S5_CHUNK
    rows_s = lanes_s // S5_CHUNK
    kc = S5_CHUNK * S5_CH
    p = S5_P

    gps = S5_GROUPS_PER_STEP

    def gspec(shape):
        return pl.BlockSpec((gps,) + shape, lambda i: (i, 0, 0))

    def tspec(lanes):
        return pl.BlockSpec((gps * S5_CH, lanes), lambda i: (i, 0))

    return pl.pallas_call(
        functools.partial(_s5_kernel, nb_p=nb_p, nb_s=nb_s, n_seg_s=n_seg_s),
        grid=(g // gps,),
        in_specs=[tspec(lanes_p), tspec(lanes_s), gspec((nb_s, 4 * p)),
                  gspec((1, 2 * p)), gspec((1, 2 * p)), gspec((1, 2 * p)),
                  gspec((S5_CH, 2 * p)), gspec((S5_CH, 2 * p)), gspec((S5_CH, 2 * p)), gspec((S5_CH, 2 * p)),
                  gspec((1, kc))],
        out_specs=[tspec(lanes_p), tspec(lanes_s), gspec((nb_p, 4 * p))],
        out_shape=[jax.ShapeDtypeStruct((d, lanes_p), BF16),
                   jax.ShapeDtypeStruct((d, lanes_s), BF16),
                   jax.ShapeDtypeStruct((g, nb_p, 4 * p), F32)],
        scratch_shapes=[pltpu.VMEM((gps, kc, kc), BF16), pltpu.VMEM((gps, kc, 4 * p), BF16),
                        pltpu.VMEM((gps, kc, 4 * p), BF16),
                        pltpu.VMEM((gps, 6, max(rows_p, rows_s), 2 * p), F32),
                        pltpu.VMEM((gps, 6, nb_s * n_seg_s, 2 * p), F32)],
        compiler_params=_cparams("parallel"),
        name="s5",
    )(utp, uts, h0, a_re, a_im, log_dt, bt_re, bt_im, c_re, c_im,
      jnp.tile(d_skip.reshape(g, 1, S5_CH), (1, 1, S5_CHUNK)))


def _glu_kernel(yt0_ref, ytn_ref, z_ref, w_ref, b_ref, a_ref, yb_ref, *, tn):
    d = w_ref.shape[0]
    i = pl.program_id(0)
    slot = i % 2

    @pl.when(i == 0)
    def _():
        yb_ref[0] = jax.nn.gelu(yt0_ref[...].astype(F32)).astype(BF16).T

    for c in range(d // tn):
        cs = slice(c * tn, (c + 1) * tn)
        yb = yb_ref[slot]
        gl = _dot(yb, w_ref[:, cs]) + b_ref[:, cs]
        a = yb[:, cs].astype(F32) * jax.nn.sigmoid(gl) * _silu(z_ref[:, cs].astype(F32))
        a_ref[:, cs] = a.astype(a_ref.dtype)
        yb_ref[1 - slot, :, cs] = jax.nn.gelu(ytn_ref[cs, :].astype(F32)).astype(BF16).T


def _glu(yt, z, w_glu, b_glu, tn):
    d, m = yt.shape
    tm = LANE_TILE
    ni = m // tm
    return pl.pallas_call(
        functools.partial(_glu_kernel, tn=tn),
        grid=(ni,),
        in_specs=[pl.BlockSpec((d, tm), lambda i: (0, 0), pipeline_mode=pl.Buffered(1)),
                  pl.BlockSpec((d, tm), lambda i: (0, jnp.minimum(i + 1, ni - 1))),
                  pl.BlockSpec((tm, d), lambda i: (i, 0)),
                  pl.BlockSpec((d, d), lambda i: (0, 0), pipeline_mode=pl.Buffered(1)),
                  pl.BlockSpec((1, d), lambda i: (0, 0))],
        out_specs=pl.BlockSpec((tm, d), lambda i: (i, 0)),
        out_shape=jax.ShapeDtypeStruct((m, d), BF16),
        scratch_shapes=[pltpu.VMEM((2, tm, d), BF16)],
        compiler_params=_cparams("arbitrary"),
        name="glu",
    )(yt, yt, z, w_glu, b_glu.reshape(1, d))


def _out_norm_kernel(x_hbm, a_ref, gate_ref, fg_ref, w_ref, o_hbm, xbuf, xsem, obuf, osem, *, n_cond):
    nq = x_hbm.shape[0]
    x = _fetch_slots(x_hbm, xbuf, xsem)
    xn = x + _cond_rows(gate_ref, x.shape[0], nq, n_cond) * _dot(a_ref[...], w_ref[...])
    res = xn * lax.rsqrt(jnp.mean(xn * xn, axis=-1, keepdims=True) + EPS) * fg_ref[...]
    _store_slots(res, o_hbm, obuf, osem)


def _out_norm(a, x4, mod3, cond0, n_cond, final_g, w_out):
    nq, m, _, d = x4.shape
    _slots_per_step(x4)
    return pl.pallas_call(
        functools.partial(_out_norm_kernel, n_cond=n_cond),
        grid=(nq * m * S5_CHUNK // LANE_TILE,),
        in_specs=[pl.BlockSpec(memory_space=pl.ANY),
                  pl.BlockSpec((LANE_TILE, d), lambda l: (l, 0)),
                  pl.BlockSpec((n_cond, 1, d), lambda l: (cond0 // n_cond, 0, 2)),
                  pl.BlockSpec((1, d), lambda l: (0, 0)),
                  pl.BlockSpec((d, d), lambda l: (0, 0), pipeline_mode=pl.Buffered(1))],
        out_specs=pl.BlockSpec(memory_space=pl.ANY),
        out_shape=jax.ShapeDtypeStruct(x4.shape, F32),
        scratch_shapes=2 * _slot_scratch(d),
        compiler_params=_cparams("arbitrary"),
        name="out_norm",
    )(x4, a, mod3, final_g.reshape(1, d), w_out)


def _both_dirs(a):
    _, g, r, p = a.shape
    return a.transpose(1, 2, 0, 3).reshape(g, r, 2 * p)


def kernel(x_prompt, x_sample, cache_l0_k, cache_l0_v, state_l1_s5, c, c_ctx, l0_norm_g, l0_w_ada, l0_b_ada, l0_w_in, l0_rpb, l0_w_out, l1_norm_g, l1_w_ada, l1_b_ada, l1_w_in, l1_a_re, l1_a_im, l1_log_dt, l1_b_re, l1_b_im, l1_c_re, l1_c_im, l1_d, l1_w_glu, l1_b_glu, l1_w_out, final_norm_g):
    bp, seq, d = x_prompt.shape
    bs, n_tok, _ = x_sample.shape
    lc = cache_l0_k.shape[1]
    g = d // S5_CH
    p = S5_P
    xp = x_prompt.reshape(bp * seq, d)
    xs = x_sample.reshape(bs * n_tok, d)

    ctx = bs
    cond8 = jnp.zeros((8, d), F32).at[0:bs].set(c).at[ctx].set(c_ctx)
    mods = _ada(cond8, (l0_w_ada, l1_w_ada), (l0_b_ada, l1_b_ada))
    mod0 = mods[0].reshape(8, 1, 3 * d)
    mod1 = mods[1].reshape(8, 1, 3 * d)

    w_in0 = l0_w_in.astype(BF16)
    w_out0 = l0_w_out.astype(BF16)
    w_in1 = l1_w_in.astype(BF16)
    w_glu = l1_w_glu.astype(BF16)
    w_out1 = l1_w_out.astype(BF16)
    k_ctx = cache_l0_k.reshape(bs, lc, d)
    v_ctx = cache_l0_v.reshape(bs, lc, d)

    qs, ks, vs, zs = _inproj(xs, l0_norm_g, mod0, 0, n_tok, w_in0,
                             ((0, BF16), (1, BF16), (2, BF16), (3, BF16)), 1024, 256)
    qp, kpb, vpb, zp, kp, vp = _inproj(xp, l0_norm_g, mod0, ctx, bp * seq, w_in0,
                                       ((0, BF16), (1, BF16), (2, BF16), (3, BF16)), 1024, 256, head_split=(1, 2),
                                       after=(qs, w_out0, w_in1, w_glu, w_out1, mod1, k_ctx, v_ctx))
    op = _ctx_attn(qp, kpb, vpb, seq)
    rpb_pad = jnp.zeros((N_HEADS, 16, LANES), F32).at[:, :RPB_R, :RPB_C].set(l0_rpb)
    os_ = _latent_na(qs, ks, vs, k_ctx, v_ctx, rpb_pad, n_tok)
    n_seg_s = max(1, n_tok // S5_CHUNK // S5_SEG)
    x1p = _gated_out(op, zp, xp, mod0, ctx, bp * seq, w_out0, 512).reshape(bp, seq // S5_CHUNK, S5_CHUNK, d)
    x1s = _gated_out(os_, zs, xs, mod0, 0, n_tok, w_out0, 512).reshape(bs * n_seg_s, -1, S5_CHUNK, d)

    utp, zp1 = _inproj_t(x1p, l1_norm_g, mod1, ctx, 1, w_in1)
    uts, zs1 = _inproj_t(x1s, l1_norm_g, mod1, 0, bs, w_in1)
    h0 = state_l1_s5.transpose(3, 0, 2, 1, 4).reshape(g, bs, 4 * p)
    log_dt = jnp.broadcast_to(l1_log_dt[:, :, None, None], (2, g, 1, p))
    ytp, yts, st = _s5(
        utp, uts, h0,
        _both_dirs(l1_a_re[:, :, None, :]), _both_dirs(l1_a_im[:, :, None, :]), _both_dirs(log_dt),
        _both_dirs(l1_b_re.transpose(0, 1, 3, 2)), _both_dirs(l1_b_im.transpose(0, 1, 3, 2)),
        _both_dirs(l1_c_re), _both_dirs(l1_c_im), l1_d, bp, bs, n_seg_s)
    y_prompt = _out_norm(_glu(ytp, zp1, w_glu, l1_b_glu, 512), x1p, mod1, ctx, 1, final_norm_g, w_out1)
    y_sample = _out_norm(_glu(yts, zs1, w_glu, l1_b_glu, 512), x1s, mod1, 0, bs, final_norm_g, w_out1)
    new_state = st.reshape(g, bp, 2, 2, p).transpose(1, 3, 2, 0, 4)
    return (y_prompt.reshape(bp, seq, d), y_sample.reshape(bs, n_tok, d),
            kp.reshape(bp, seq, N_HEADS, HEAD_DIM), vp.reshape(bp, seq, N_HEADS, HEAD_DIM), new_state)
```

```python
import functools

import jax
import jax.numpy as jnp
from jax import lax
from jax.experimental import pallas as pl
from jax.experimental.pallas import tpu as pltpu

F32 = jnp.float32
BF16 = jnp.bfloat16

EPS = 1e-6
N_HEADS = 16
HEAD_DIM = 128
GRID_W = 64
NA_ROWS = 8
NA_COLS = 16
RPB_R = 2 * NA_ROWS - 1
RPB_C = 2 * NA_COLS - 1
S5_CH = 16
S5_P = 64
S5_CHUNK = 16
S5_SEG = 8
NEG = -1e30
LANES = 128
VMEM_LIMIT = 56 * 1024 * 1024


def _cparams(*sem):
    return pltpu.CompilerParams(dimension_semantics=sem, vmem_limit_bytes=VMEM_LIMIT)


def _dot(a, b):
    return jnp.dot(a, b, preferred_element_type=F32)


def _dot_nt(a, b, precision=None):
    return lax.dot_general(a, b, (((1,), (1,)), ((), ())), preferred_element_type=F32,
                           precision=precision)


def _silu(x):
    return x * jax.nn.sigmoid(x)


def _ada_kernel(cond_ref, *refs, n_layers):
    w_refs, b_refs, o_ref = refs[:n_layers], refs[n_layers:2 * n_layers], refs[2 * n_layers]
    s = _silu(cond_ref[...]).astype(BF16)
    layer = pl.program_id(0)
    for i in range(n_layers):
        @pl.when(layer == i)
        def _(i=i):
            o_ref[...] = _dot(s, w_refs[i][...].astype(BF16)) + b_refs[i][...]


def _ada(cond8, w_adas, b_adas):
    n_layers = len(w_adas)
    d, n = w_adas[0].shape
    tn = 1024
    nj = n // tn

    def own(i):
        return lambda l, j: (0, jnp.where(l == i, j, jnp.where(l < i, 0, nj - 1)))

    return pl.pallas_call(
        functools.partial(_ada_kernel, n_layers=n_layers),
        grid=(n_layers, nj),
        in_specs=[pl.BlockSpec((8, d), lambda l, j: (0, 0))]
        + [pl.BlockSpec((d, tn), own(i)) for i in range(n_layers)]
        + [pl.BlockSpec((1, tn), own(i)) for i in range(n_layers)],
        out_specs=pl.BlockSpec((None, 8, tn), lambda l, j: (l, 0, j)),
        out_shape=jax.ShapeDtypeStruct((n_layers, 8, n), F32),
        compiler_params=_cparams("arbitrary", "arbitrary"),
        name="ada",
    )(cond8, *w_adas, *[b.reshape(1, n) for b in b_adas])


def _norm_mod(x, g, shift, scale):
    y = x * lax.rsqrt(jnp.mean(x * x, axis=-1, keepdims=True) + EPS) * g
    return (y * (1.0 + scale) + shift).astype(BF16)


def _inproj_kernel(x0_ref, xn_ref, g_ref, shift0_ref, scale0_ref, shiftn_ref, scalen_ref, *rest, out_split,
                   head_split):
    n_split = max(out_split + head_split) + 1
    w_refs = rest[:n_split]
    rest = rest[n_split:]
    o_refs = rest[:len(out_split)]
    hd_refs = rest[len(out_split):len(out_split) + len(head_split)]
    h_ref = rest[len(out_split) + len(head_split)]
    i = pl.program_id(0)
    j = pl.program_id(1)
    nj = pl.num_programs(1)
    slot = i % 2

    @pl.when((i == 0) & (j == 0))
    def _():
        h_ref[0] = _norm_mod(x0_ref[...], g_ref[...], shift0_ref[...], scale0_ref[...])

    if head_split:
        stage_ref, sem = rest[len(out_split) + len(head_split) + 1:]
        tm, tn = stage_ref.shape[2:]
        t = i * nj + j

        def head_copies(tt):
            ii, jj = tt // nj, tt % nj
            return [pltpu.make_async_copy(stage_ref.at[tt % 2, n, :, pl.ds(hh * HEAD_DIM, HEAD_DIM)],
                                          hd.at[pl.ds(ii * tm, tm), jj * (tn // HEAD_DIM) + hh, :], sem.at[tt % 2])
                    for n, hd in enumerate(hd_refs) for hh in range(tn // HEAD_DIM)]

        @pl.when(t >= 2)
        def _():
            for c in head_copies(t - 2):
                c.wait()

    h = h_ref[slot]
    for s, w_ref in enumerate(w_refs):
        r = _dot(h, w_ref[...])
        for o_ref, o_s in zip(o_refs, out_split):
            if o_s == s:
                o_ref[...] = r.astype(o_ref.dtype)
        if s in head_split:
            stage_ref[t % 2, head_split.index(s)] = r
    rs = xn_ref.shape[0]
    h_ref[1 - slot, pl.ds(pl.multiple_of(j * rs, rs), rs), :] = _norm_mod(
        xn_ref[...], g_ref[...], shiftn_ref[...], scalen_ref[...])

    if head_split:
        for c in head_copies(t):
            c.start()

        @pl.when(t == pl.num_programs(0) * nj - 1)
        def _():
            @pl.when(t >= 1)
            def _():
                for c in head_copies(t - 1):
                    c.wait()

            for c in head_copies(t):
                c.wait()


def _inproj(x, g, mod3, cond0, rows_per_cond, w, outs, tm, tn, head_split=()):
    m, d = x.shape
    out_split = tuple(s for s, _ in outs)
    out_dtypes = [dt for _, dt in outs]
    head_split = tuple(head_split)
    n_split = max(out_split + head_split) + 1
    ni = m // tm
    nj = d // tn

    def nxt(i):
        return jnp.minimum(i + 1, ni - 1)

    def cond_of(i):
        return cond0 + (i * tm) // rows_per_cond

    in_specs = [pl.BlockSpec((tm, d), lambda i, j: (0, 0), pipeline_mode=pl.Buffered(1)),
                pl.BlockSpec((tm // nj, d), lambda i, j: (nxt(i) * nj + j, 0)),
                pl.BlockSpec((1, d), lambda i, j: (0, 0)),
                pl.BlockSpec((None, 1, d), lambda i, j: (cond_of(0), 0, 0)),
                pl.BlockSpec((None, 1, d), lambda i, j: (cond_of(0), 0, 1)),
                pl.BlockSpec((None, 1, d), lambda i, j: (cond_of(nxt(i)), 0, 0)),
                pl.BlockSpec((None, 1, d), lambda i, j: (cond_of(nxt(i)), 0, 1))]
    for s in range(n_split):
        in_specs.append(pl.BlockSpec((d, tn), lambda i, j, s=s: (0, s * nj + j)))
    scratch = [pltpu.VMEM((2, tm, d), BF16)]
    if head_split:
        scratch += [pltpu.VMEM((2, len(head_split), tm, tn), F32), pltpu.SemaphoreType.DMA((2,))]
    return pl.pallas_call(
        functools.partial(_inproj_kernel, out_split=out_split, head_split=head_split),
        grid=(ni, nj),
        in_specs=in_specs,
        out_specs=[pl.BlockSpec((tm, tn), lambda i, j: (i, j)) for _ in outs]
        + [pl.BlockSpec(memory_space=pl.ANY)] * len(head_split),
        out_shape=[jax.ShapeDtypeStruct((m, d), dt) for dt in out_dtypes]
        + [jax.ShapeDtypeStruct((m, d // HEAD_DIM, HEAD_DIM), F32)] * len(head_split),
        scratch_shapes=scratch,
        compiler_params=_cparams("arbitrary", "arbitrary"),
        name="inproj",
    )(x, x, g.reshape(1, d), mod3, mod3, mod3, mod3, *([w] * n_split))


def _softmax_pv(parts):
    m = parts[0][0].max(axis=-1, keepdims=True)
    for s, _ in parts[1:]:
        m = jnp.maximum(m, s.max(axis=-1, keepdims=True))
    l = None
    o = None
    for s, v in parts:
        e = jnp.exp(s - m)
        li = e.sum(axis=-1, keepdims=True)
        oi = _dot(e.astype(BF16), v)
        l = li if l is None else l + li
        o = oi if o is None else o + oi
    return o / l


def _ctx_attn_kernel(q_ref, k_ref, v_ref, o_ref):
    scale = HEAD_DIM ** -0.5
    for h in range(N_HEADS):
        sl = slice(h * HEAD_DIM, (h + 1) * HEAD_DIM)
        s = _dot_nt(q_ref[:, sl], k_ref[:, sl]) * scale
        o_ref[:, sl] = _softmax_pv([(s, v_ref[:, sl])]).astype(o_ref.dtype)


def _ctx_attn(q, k, v, seq):
    m, d = q.shape
    spec = pl.BlockSpec((seq, d), lambda b: (b, 0))
    return pl.pallas_call(
        _ctx_attn_kernel,
        grid=(m // seq,),
        in_specs=[spec, spec, spec],
        out_specs=spec,
        out_shape=jax.ShapeDtypeStruct((m, d), BF16),
        compiler_params=_cparams("parallel"),
        name="ctx_attn",
    )(q, k, v)


def _na_window_start(r, rows):
    return min(max(r - NA_ROWS // 2, 0), rows - NA_ROWS)


def _na_kernel(q_ref, k_ref, v_ref, kc_hbm, vc_hbm, rpb_ref, o_ref, cbuf, csem, *, rows, q_rows):
    scale = HEAD_DIM ** -0.5
    w = GRID_W
    n_heads = pl.num_programs(1)
    t = pl.program_id(0) * n_heads + pl.program_id(1)

    def ctx_copies(tt):
        return [pltpu.make_async_copy(c.at[tt // n_heads, :, tt % n_heads, :], cbuf.at[tt % 2, n], csem.at[tt % 2])
                for n, c in enumerate((kc_hbm, vc_hbm))]

    @pl.when(t == 0)
    def _():
        for c in ctx_copies(t):
            c.start()

    @pl.when(t + 1 < pl.num_programs(0) * n_heads)
    def _():
        for c in ctx_copies(t + 1):
            c.start()
    qc = lax.broadcasted_iota(jnp.int32, (w, LANES), 0)
    lane = lax.broadcasted_iota(jnp.int32, (w, LANES), 1)
    c0 = jnp.clip(qc - NA_COLS // 2, 0, w - NA_COLS)
    ok_l = (lane >= c0) & (lane < c0 + NA_COLS)
    ok_r = (lane - w >= c0) & (lane - w < c0 + NA_COLS)
    neg_tile = jnp.full((w, LANES), NEG, F32)
    tile_l, tile_r = [], []
    for d in range(RPB_R):
        base = jnp.broadcast_to(rpb_ref[d:d + 1, :], (w, LANES))
        left = pltpu.roll(base, LANES - (NA_COLS - 1), 1, stride=1, stride_axis=0)
        right = pltpu.roll(base, w - (NA_COLS - 1), 1, stride=1, stride_axis=0)
        tile_l.append(jnp.where(ok_l, left, NEG))
        tile_r.append(jnp.where(ok_r, right, NEG))

    for c in ctx_copies(t):
        c.wait()
    kc = cbuf[t % 2, 0].astype(BF16)
    vc = cbuf[t % 2, 1].astype(BF16)
    n_groups = rows // q_rows
    for gi in range(n_groups):
        rs = list(range(gi * q_rows, (gi + 1) * q_rows))
        klo = min(_na_window_start(r, rows) for r in rs) // 2 * 2
        khi = -(-(max(_na_window_start(r, rows) for r in rs) + NA_ROWS) // 2) * 2
        bias_rows = []
        for r in rs:
            r0 = _na_window_start(r, rows)
            tiles = []
            for kr in range(klo, khi, 2):
                ok0 = r0 <= kr < r0 + NA_ROWS
                ok1 = r0 <= kr + 1 < r0 + NA_ROWS
                t0 = tile_l[kr - r + NA_ROWS - 1] if ok0 else neg_tile
                t1 = tile_r[kr + 1 - r + NA_ROWS - 1] if ok1 else neg_tile
                tiles.append(jnp.maximum(t0, t1) if (ok0 or ok1) else neg_tile)
            bias_rows.append(jnp.concatenate(tiles, axis=1))
        bias = jnp.concatenate(bias_rows, axis=0)
        q = q_ref[gi * q_rows * w:(gi + 1) * q_rows * w, :]
        kl = k_ref[klo * w:khi * w, :]
        vl = v_ref[klo * w:khi * w, :]
        s_loc = _dot_nt(q, kl) * scale + bias
        s_ctx = _dot_nt(q, kc) * scale
        o = _softmax_pv([(s_loc, vl), (s_ctx, vc)])
        o_ref[gi * q_rows * w:(gi + 1) * q_rows * w, :] = o.astype(o_ref.dtype)


def _latent_na(q, k, v, k_ctx, v_ctx, rpb_pad, n_tok):
    m, d = q.shape
    nb = m // n_tok
    lc = k_ctx.shape[1]
    rows = n_tok // GRID_W
    spec = pl.BlockSpec((n_tok, HEAD_DIM), lambda b, h: (b, h))
    cspec = pl.BlockSpec(memory_space=pl.ANY)
    return pl.pallas_call(
        functools.partial(_na_kernel, rows=rows, q_rows=4),
        grid=(nb, N_HEADS),
        in_specs=[spec, spec, spec, cspec, cspec,
                  pl.BlockSpec((None, 16, LANES), lambda b, h: (h, 0, 0))],
        out_specs=spec,
        out_shape=jax.ShapeDtypeStruct((m, d), BF16),
        scratch_shapes=[pltpu.VMEM((2, 2, lc, HEAD_DIM), F32), pltpu.SemaphoreType.DMA((2,))],
        compiler_params=_cparams("arbitrary", "arbitrary"),
        name="latent_na",
    )(q, k, v, k_ctx, v_ctx, rpb_pad)


def _gated_out_kernel(o_ref, z_ref, x_ref, gate_ref, w_ref, y_ref):
    a = o_ref[...].astype(F32) * _silu(z_ref[...].astype(F32))
    y_ref[...] = x_ref[...] + gate_ref[...] * _dot(a.astype(BF16), w_ref[...])


def _gated_out(o, z, x, mod3, cond0, rows_per_cond, w_out, tm):
    m, d = x.shape
    row = pl.BlockSpec((tm, d), lambda i: (i, 0))
    return pl.pallas_call(
        _gated_out_kernel,
        grid=(m // tm,),
        in_specs=[row, row, row,
                  pl.BlockSpec((None, 1, d), lambda i: (cond0 + (i * tm) // rows_per_cond, 0, 2)),
                  pl.BlockSpec((d, d), lambda i: (0, 0), pipeline_mode=pl.Buffered(1))],
        out_specs=row,
        out_shape=jax.ShapeDtypeStruct((m, d), F32),
        compiler_params=_cparams("parallel"),
        name="gated_out",
    )(o, z, x, mod3, w_out)


LANE_TILE = 512


def _slots_per_step(x4):
    nq, m, _, _ = x4.shape
    assert LANE_TILE % (nq * m) == 0 and S5_CHUNK % (LANE_TILE // (nq * m)) == 0
    return LANE_TILE // (nq * m)


def _slot_scratch(d):
    return [pltpu.VMEM((2, LANE_TILE, d), F32), pltpu.SemaphoreType.DMA((2,))]


def _slot_copies(x_hbm, buf, sem, l, to_hbm):
    nq, m, _, _ = x_hbm.shape
    n_x = buf.shape[1] // (nq * m)
    copies = []
    for xi in range(n_x):
        for k in range(m):
            hbm = x_hbm.at[:, k, l * n_x + xi, :]
            vmem = buf.at[l % 2, pl.ds((xi * m + k) * nq, nq), :]
            src, dst = (vmem, hbm) if to_hbm else (hbm, vmem)
            copies.append(pltpu.make_async_copy(src, dst, sem.at[l % 2]))
    return copies


def _fetch_slots(x_hbm, buf, sem):
    l = pl.program_id(0)

    @pl.when(l == 0)
    def _():
        for c in _slot_copies(x_hbm, buf, sem, l, False):
            c.start()

    @pl.when(l + 1 < pl.num_programs(0))
    def _():
        for c in _slot_copies(x_hbm, buf, sem, l + 1, False):
            c.start()

    for c in _slot_copies(x_hbm, buf, sem, l, False):
        c.wait()
    return buf[l % 2]


def _store_slots(res, o_hbm, buf, sem):
    l = pl.program_id(0)

    def wait(ll):
        for c in _slot_copies(o_hbm, buf, sem, ll, True):
            c.wait()

    @pl.when(l >= 2)
    def _():
        wait(l - 2)

    buf[l % 2] = res
    for c in _slot_copies(o_hbm, buf, sem, l, True):
        c.start()

    @pl.when(l == pl.num_programs(0) - 1)
    def _():
        @pl.when(l >= 1)
        def _():
            wait(l - 1)

        wait(l)


def _cond_rows(ref, n_rows, nq, n_cond):
    out = ref[n_cond - 1]
    if n_cond > 1:
        b = (lax.broadcasted_iota(jnp.int32, (n_rows, 1), 0) % nq) // (nq // n_cond)
        for i in range(n_cond - 2, -1, -1):
            out = jnp.where(b == i, ref[i], out)
    return out


def _inproj_t_kernel(x_hbm, g_ref, shift_ref, scale_ref, wu_ref, wz_ref, u_ref, z_ref, buf, sem, *, n_cond):
    nq = x_hbm.shape[0]
    x = _fetch_slots(x_hbm, buf, sem)
    n_rows = x.shape[0]
    h = _norm_mod(x, g_ref[...], _cond_rows(shift_ref, n_rows, nq, n_cond),
                  _cond_rows(scale_ref, n_rows, nq, n_cond))
    u_ref[...] = _dot(h, wu_ref[...]).T.astype(u_ref.dtype)
    z_ref[...] = _dot(h, wz_ref[...]).astype(z_ref.dtype)


def _inproj_t(x4, g, mod3, cond0, n_cond, w_in):
    nq, m, _, d = x4.shape
    n_tok = nq * m * S5_CHUNK
    _slots_per_step(x4)

    def mspec(col):
        return pl.BlockSpec((n_cond, 1, d), lambda l: (cond0 // n_cond, 0, col))

    def wspec(col):
        return pl.BlockSpec((d, d), lambda l: (0, col), pipeline_mode=pl.Buffered(1))

    return pl.pallas_call(
        functools.partial(_inproj_t_kernel, n_cond=n_cond),
        grid=(n_tok // LANE_TILE,),
        in_specs=[pl.BlockSpec(memory_space=pl.ANY),
                  pl.BlockSpec((1, d), lambda l: (0, 0)), mspec(0), mspec(1), wspec(0), wspec(1)],
        out_specs=[pl.BlockSpec((d, LANE_TILE), lambda l: (0, l)), pl.BlockSpec((LANE_TILE, d), lambda l: (l, 0))],
        out_shape=[jax.ShapeDtypeStruct((d, n_tok), BF16), jax.ShapeDtypeStruct((n_tok, d), BF16)],
        scratch_shapes=_slot_scratch(d),
        compiler_params=_cparams("arbitrary"),
        name="inproj_t",
    )(x4, g.reshape(1, d), mod3, mod3, w_in, w_in)


def _cmul(ar, ai, br, bi):
    return ar * br - ai * bi, ar * bi + ai * br


def _split_bf16(a):
    hi = a.astype(BF16)
    return hi, (a - hi.astype(F32)).astype(BF16)


def _s5_powers(ar, ai):
    width = ar.shape[-1]
    pw = [(jnp.ones_like(ar), jnp.zeros_like(ar))]
    for _ in range(S5_CHUNK):
        pw.append(_cmul(pw[-1][0], pw[-1][1], ar, ai))
    is_fwd = lax.broadcasted_iota(jnp.int32, (1, width), 1) < width // 2

    def pattern(fwd_ascending):
        out = []
        for part in range(2):
            blocks = []
            for j in range(S5_CHUNK):
                ef, eb = (j, S5_CHUNK - 1 - j) if fwd_ascending else (S5_CHUNK - 1 - j, j)
                blocks.append(jnp.broadcast_to(jnp.where(is_fwd, pw[ef][part], pw[eb][part]), (S5_CH, width)))
            out.append(jnp.concatenate(blocks, axis=0))
        return out

    return pattern(True), pattern(False), pw[S5_CHUNK]


S5_GROUPS_PER_STEP = 8


def _s5_kernel(xp_ref, xs_ref, *rest, nb_p, nb_s, n_seg_s):
    yp_ref, ys_ref = rest[9], rest[10]
    for gi in range(S5_GROUPS_PER_STEP):
        ch = pl.ds(gi * S5_CH, S5_CH)
        _s5_group(xp_ref.at[ch], xs_ref.at[ch], *[r.at[gi] for r in rest[:9]], yp_ref.at[ch], ys_ref.at[ch],
                  *[r.at[gi] for r in rest[11:]], nb_p=nb_p, nb_s=nb_s, n_seg_s=n_seg_s)


def _s5_group(xp_ref, xs_ref, h0_ref, are_ref, aim_ref, ldt_ref, bre_ref, bim_ref, cre_ref, cim_ref, dsk_ref,
              yp_ref, ys_ref, st_ref, w_ref, ws_ref, wc_ref, s_ref, f_ref, *, nb_p, nb_s, n_seg_s):
    p = S5_P
    kc = S5_CHUNK * S5_CH
    a_re = are_ref[...]
    a_im = aim_ref[...]
    dt = jnp.exp(ldt_ref[...])
    mag = jnp.exp(a_re * dt)
    ab_re = mag * jnp.cos(a_im * dt)
    ab_im = mag * jnp.sin(a_im * dt)
    den = a_re * a_re + a_im * a_im
    nr = ab_re - 1.0
    f_re = (nr * a_re + ab_im * a_im) / den
    f_im = (ab_im * a_re - nr * a_im) / den
    bbt_re, bbt_im = _cmul(f_re, f_im, bre_ref[...], bim_ref[...])

    (pg_r, pg_i), (ps_r, ps_i), (ac_r, ac_i) = _s5_powers(ab_re, ab_im)
    c_re = jnp.concatenate([cre_ref[...]] * S5_CHUNK, axis=0)
    c_im = jnp.concatenate([cim_ref[...]] * S5_CHUNK, axis=0)
    bt_re = jnp.concatenate([bbt_re] * S5_CHUNK, axis=0)
    bt_im = jnp.concatenate([bbt_im] * S5_CHUNK, axis=0)

    g_re, g_im = _cmul(pg_r, pg_i, c_re, c_im)
    fwd16 = lax.broadcasted_iota(jnp.int32, (S5_CH, 2 * p), 1) < p
    g_hi, g_lo = _split_bf16(jnp.concatenate([g_re, g_im], axis=1))
    r0 = []
    for d in range(2):
        msk = fwd16 if d == 0 else jnp.logical_not(fwd16)
        b_hi, b_lo = _split_bf16(
            jnp.concatenate([jnp.where(msk, bbt_re, 0.0), jnp.where(msk, -bbt_im, 0.0)], axis=1))
        r0.append(_dot_nt(b_hi, g_hi) + (_dot_nt(b_hi, g_lo) + _dot_nt(b_lo, g_hi)))
    lane = lax.broadcasted_iota(jnp.int32, (S5_CH, kc), 1)
    for s in range(S5_CHUNK):
        f = jnp.where(lane >= S5_CH * s, pltpu.roll(r0[0], S5_CH * s, 1), 0.0) if s else r0[0]
        b = jnp.where(lane < S5_CH * (s + 1), pltpu.roll(r0[1], (S5_CH * (s + 1)) % kc, 1), 0.0)
        w_ref[s * S5_CH:(s + 1) * S5_CH, :] = (f + b).astype(BF16)
    e_re, e_im = _cmul(ps_r, ps_i, bt_re, bt_im)
    ws_ref[:, 0:2 * p] = e_re.astype(BF16)
    ws_ref[:, 2 * p:4 * p] = e_im.astype(BF16)
    g1_re, g1_im = _cmul(g_re, g_im, ab_re, ab_im)
    wc_ref[:, 0:2 * p] = g1_re.astype(BF16)
    wc_ref[:, 2 * p:4 * p] = (-g1_im).astype(BF16)

    def run(xt_ref, yt_ref, nb, n_seg, hr, hm):
        rows = xt_ref.shape[1] // S5_CHUNK
        nbx = nb * n_seg
        m = rows // nbx
        xt = jnp.concatenate([xt_ref[:, s * rows:(s + 1) * rows] for s in range(S5_CHUNK)], axis=0)
        xf = xt.astype(F32).T
        x = xf.astype(BF16)
        y = _dot(x, w_ref[...]) + dsk_ref[...] * xf
        s_all = _dot(x, ws_ref[...])
        s_ref[0, 0:rows, :] = s_all[:, 0:2 * p]
        s_ref[1, 0:rows, :] = s_all[:, 2 * p:4 * p]

        def step(ar, ai, hr, hm, plane_r, plane_i, rf, rb):
            isf = lax.broadcasted_iota(jnp.int32, hr.shape, 1) < p
            sr = jnp.where(isf, plane_r[rf, :], plane_r[rb, :])
            sm = jnp.where(isf, plane_i[rf, :], plane_i[rb, :])
            return ar * hr - ai * hm + sr, ar * hm + ai * hr + sm

        def scan(hr, hm, store):
            arb = jnp.broadcast_to(ac_r, (nbx, 2 * p))
            aib = jnp.broadcast_to(ac_i, (nbx, 2 * p))
            for k in range(m):
                rf = pl.ds(k * nbx, nbx)
                rb = pl.ds((m - 1 - k) * nbx, nbx)
                if store:
                    s_ref[2, rf, :] = hr
                    s_ref[3, rf, :] = hm
                    s_ref[4, rb, :] = hr
                    s_ref[5, rb, :] = hm
                hr, hm = step(arb, aib, hr, hm, s_ref.at[0], s_ref.at[1], rf, rb)
            return hr, hm

        if n_seg > 1:
            zeros = jnp.zeros((nbx, 2 * p), F32)
            f_ref[0], f_ref[1] = scan(zeros, zeros, False)
            sr_, si_ = ac_r, ac_i
            for _ in range(m - 1):
                sr_, si_ = _cmul(sr_, si_, ac_r, ac_i)
            sr_ = jnp.broadcast_to(sr_, (nb, 2 * p))
            si_ = jnp.broadcast_to(si_, (nb, 2 * p))
            for j in range(n_seg):
                rf = pl.ds(j, nb, stride=n_seg)
                rb = pl.ds(n_seg - 1 - j, nb, stride=n_seg)
                f_ref[2, rf, :] = hr
                f_ref[3, rf, :] = hm
                f_ref[4, rb, :] = hr
                f_ref[5, rb, :] = hm
                hr, hm = step(sr_, si_, hr, hm, f_ref.at[0], f_ref.at[1], rf, rb)
            isf = lax.broadcasted_iota(jnp.int32, (nbx, 2 * p), 1) < p
            hr = jnp.where(isf, f_ref[2], f_ref[4])
            hm = jnp.where(isf, f_ref[3], f_ref[5])
        hr, hm = scan(hr, hm, True)
        isf_rows = lax.broadcasted_iota(jnp.int32, (rows, 2 * p), 1) < p
        h_prev = jnp.concatenate([jnp.where(isf_rows, s_ref[2, 0:rows, :], s_ref[4, 0:rows, :]),
                                  jnp.where(isf_rows, s_ref[3, 0:rows, :], s_ref[5, 0:rows, :])], axis=1)
        yt = jax.nn.gelu(y + _dot_nt(h_prev.astype(BF16), wc_ref[...])).T
        for t in range(S5_CHUNK):
            yt_ref[:, t * rows:(t + 1) * rows] = yt[t * S5_CH:(t + 1) * S5_CH, :].astype(yt_ref.dtype)
        return hr, hm

    zeros = jnp.zeros((nb_p, 2 * p), F32)
    hr, hm = run(xp_ref, yp_ref, nb_p, 1, zeros, zeros)
    st_ref[:, 0:2 * p] = hr
    st_ref[:, 2 * p:4 * p] = hm
    run(xs_ref, ys_ref, nb_s, n_seg_s, h0_ref[:, 0:2 * p], h0_ref[:, 2 * p:4 * p])


def _s5(utp, uts, h0, a_re, a_im, log_dt, bt_re, bt_im, c_re, c_im, d_skip, nb_p, nb_s, n_seg_s):
    d, lanes_p = utp.shape
    lanes_s = uts.shape[1]
    g = d // S5_CH
    rows_p = lanes_p // S5_CHUNK
    rows_s = lanes_s // S5_CHUNK
    kc = S5_CHUNK * S5_CH
    p = S5_P

    gps = S5_GROUPS_PER_STEP

    def gspec(shape):
        return pl.BlockSpec((gps,) + shape, lambda i: (i, 0, 0))

    def tspec(lanes):
        return pl.BlockSpec((gps * S5_CH, lanes), lambda i: (i, 0))

    return pl.pallas_call(
        functools.partial(_s5_kernel, nb_p=nb_p, nb_s=nb_s, n_seg_s=n_seg_s),
        grid=(g // gps,),
        in_specs=[tspec(lanes_p), tspec(lanes_s), gspec((nb_s, 4 * p)),
                  gspec((1, 2 * p)), gspec((1, 2 * p)), gspec((1, 2 * p)),
                  gspec((S5_CH, 2 * p)), gspec((S5_CH, 2 * p)), gspec((S5_CH, 2 * p)), gspec((S5_CH, 2 * p)),
                  gspec((1, kc))],
        out_specs=[tspec(lanes_p), tspec(lanes_s), gspec((nb_p, 4 * p))],
        out_shape=[jax.ShapeDtypeStruct((d, lanes_p), BF16),
                   jax.ShapeDtypeStruct((d, lanes_s), BF16),
                   jax.ShapeDtypeStruct((g, nb_p, 4 * p), F32)],
        scratch_shapes=[pltpu.VMEM((gps, kc, kc), BF16), pltpu.VMEM((gps, kc, 4 * p), BF16),
                        pltpu.VMEM((gps, kc, 4 * p), BF16),
                        pltpu.VMEM((gps, 6, max(rows_p, rows_s), 2 * p), F32),
                        pltpu.VMEM((gps, 6, nb_s * n_seg_s, 2 * p), F32)],
        compiler_params=_cparams("parallel"),
        name="s5",
    )(utp, uts, h0, a_re, a_im, log_dt, bt_re, bt_im, c_re, c_im,
      jnp.tile(d_skip.reshape(g, 1, S5_CH), (1, 1, S5_CHUNK)))


def _glu_kernel(yt0_ref, ytn_ref, z_ref, w_ref, b_ref, a_ref, yb_ref, *, tn):
    d = w_ref.shape[0]
    i = pl.program_id(0)
    slot = i % 2

    @pl.when(i == 0)
    def _():
        yb_ref[0] = yt0_ref[...].T

    for c in range(d // tn):
        cs = slice(c * tn, (c + 1) * tn)
        yb = yb_ref[slot]
        gl = _dot(yb, w_ref[:, cs]) + b_ref[:, cs]
        a = yb[:, cs].astype(F32) * jax.nn.sigmoid(gl) * _silu(z_ref[:, cs].astype(F32))
        a_ref[:, cs] = a.astype(a_ref.dtype)
        yb_ref[1 - slot, :, cs] = ytn_ref[cs, :].T


def _glu(yt, z, w_glu, b_glu, tn):
    d, m = yt.shape
    tm = LANE_TILE
    ni = m // tm
    return pl.pallas_call(
        functools.partial(_glu_kernel, tn=tn),
        grid=(ni,),
        in_specs=[pl.BlockSpec((d, tm), lambda i: (0, 0), pipeline_mode=pl.Buffered(1)),
                  pl.BlockSpec((d, tm), lambda i: (0, jnp.minimum(i + 1, ni - 1))),
                  pl.BlockSpec((tm, d), lambda i: (i, 0)),
                  pl.BlockSpec((d, d), lambda i: (0, 0), pipeline_mode=pl.Buffered(1)),
                  pl.BlockSpec((1, d), lambda i: (0, 0))],
        out_specs=pl.BlockSpec((tm, d), lambda i: (i, 0)),
        out_shape=jax.ShapeDtypeStruct((m, d), BF16),
        scratch_shapes=[pltpu.VMEM((2, tm, d), BF16)],
        compiler_params=_cparams("arbitrary"),
        name="glu",
    )(yt, yt, z, w_glu, b_glu.reshape(1, d))


def _out_norm_kernel(x_hbm, a_ref, gate_ref, fg_ref, w_ref, o_hbm, xbuf, xsem, obuf, osem, *, n_cond):
    nq = x_hbm.shape[0]
    x = _fetch_slots(x_hbm, xbuf, xsem)
    xn = x + _cond_rows(gate_ref, x.shape[0], nq, n_cond) * _dot(a_ref[...], w_ref[...])
    res = xn * lax.rsqrt(jnp.mean(xn * xn, axis=-1, keepdims=True) + EPS) * fg_ref[...]
    _store_slots(res, o_hbm, obuf, osem)


def _out_norm(a, x4, mod3, cond0, n_cond, final_g, w_out):
    nq, m, _, d = x4.shape
    _slots_per_step(x4)
    return pl.pallas_call(
        functools.partial(_out_norm_kernel, n_cond=n_cond),
        grid=(nq * m * S5_CHUNK // LANE_TILE,),
        in_specs=[pl.BlockSpec(memory_space=pl.ANY),
                  pl.BlockSpec((LANE_TILE, d), lambda l: (l, 0)),
                  pl.BlockSpec((n_cond, 1, d), lambda l: (cond0 // n_cond, 0, 2)),
                  pl.BlockSpec((1, d), lambda l: (0, 0)),
                  pl.BlockSpec((d, d), lambda l: (0, 0), pipeline_mode=pl.Buffered(1))],
        out_specs=pl.BlockSpec(memory_space=pl.ANY),
        out_shape=jax.ShapeDtypeStruct(x4.shape, F32),
        scratch_shapes=2 * _slot_scratch(d),
        compiler_params=_cparams("arbitrary"),
        name="out_norm",
    )(x4, a, mod3, final_g.reshape(1, d), w_out)


def _both_dirs(a):
    _, g, r, p = a.shape
    return a.transpose(1, 2, 0, 3).reshape(g, r, 2 * p)


def kernel(x_prompt, x_sample, cache_l0_k, cache_l0_v, state_l1_s5, c, c_ctx, l0_norm_g, l0_w_ada, l0_b_ada, l0_w_in, l0_rpb, l0_w_out, l1_norm_g, l1_w_ada, l1_b_ada, l1_w_in, l1_a_re, l1_a_im, l1_log_dt, l1_b_re, l1_b_im, l1_c_re, l1_c_im, l1_d, l1_w_glu, l1_b_glu, l1_w_out, final_norm_g):
    bp, seq, d = x_prompt.shape
    bs, n_tok, _ = x_sample.shape
    g = d // S5_CH
    p = S5_P
    xp = x_prompt.reshape(bp * seq, d)
    xs = x_sample.reshape(bs * n_tok, d)

    ctx = bs
    cond8 = jnp.zeros((8, d), F32).at[0:bs].set(c).at[ctx].set(c_ctx)
    mods = _ada(cond8, (l0_w_ada, l1_w_ada), (l0_b_ada, l1_b_ada))
    mod0 = mods[0].reshape(8, 1, 3 * d)
    mod1 = mods[1].reshape(8, 1, 3 * d)

    w_in0 = l0_w_in.astype(BF16)
    w_out0 = l0_w_out.astype(BF16)
    w_in1 = l1_w_in.astype(BF16)
    w_glu = l1_w_glu.astype(BF16)
    w_out1 = l1_w_out.astype(BF16)

    qs, ks, vs, zs = _inproj(xs, l0_norm_g, mod0, 0, n_tok, w_in0,
                             ((0, BF16), (1, BF16), (2, BF16), (3, BF16)), 1024, 256)
    qp, kpb, vpb, zp, kp, vp = _inproj(xp, l0_norm_g, mod0, ctx, bp * seq, w_in0,
                                       ((0, BF16), (1, BF16), (2, BF16), (3, BF16)), 1024, 256, head_split=(1, 2))
    op = _ctx_attn(qp, kpb, vpb, seq)
    rpb_pad = jnp.zeros((N_HEADS, 16, LANES), F32).at[:, :RPB_R, :RPB_C].set(l0_rpb)
    os_ = _latent_na(qs, ks, vs, cache_l0_k, cache_l0_v, rpb_pad, n_tok)
    n_seg_s = max(1, n_tok // S5_CHUNK // S5_SEG)
    x1p = _gated_out(op, zp, xp, mod0, ctx, bp * seq, w_out0, 512).reshape(bp, seq // S5_CHUNK, S5_CHUNK, d)
    x1s = _gated_out(os_, zs, xs, mod0, 0, n_tok, w_out0, 512).reshape(bs * n_seg_s, -1, S5_CHUNK, d)

    utp, zp1 = _inproj_t(x1p, l1_norm_g, mod1, ctx, 1, w_in1)
    uts, zs1 = _inproj_t(x1s, l1_norm_g, mod1, 0, bs, w_in1)
    h0 = state_l1_s5.transpose(3, 0, 2, 1, 4).reshape(g, bs, 4 * p)
    log_dt = jnp.broadcast_to(l1_log_dt[:, :, None, None], (2, g, 1, p))
    ytp, yts, st = _s5(
        utp, uts, h0,
        _both_dirs(l1_a_re[:, :, None, :]), _both_dirs(l1_a_im[:, :, None, :]), _both_dirs(log_dt),
        _both_dirs(l1_b_re.transpose(0, 1, 3, 2)), _both_dirs(l1_b_im.transpose(0, 1, 3, 2)),
        _both_dirs(l1_c_re), _both_dirs(l1_c_im), l1_d, bp, bs, n_seg_s)
    y_prompt = _out_norm(_glu(ytp, zp1, w_glu, l1_b_glu, 512), x1p, mod1, ctx, 1, final_norm_g, w_out1)
    y_sample = _out_norm(_glu(yts, zs1, w_glu, l1_b_glu, 512), x1s, mod1, 0, bs, final_norm_g, w_out1)
    new_state = st.reshape(g, bp, 2, 2, p).transpose(1, 3, 2, 0, 4)
    return (y_prompt.reshape(bp, seq, d), y_sample.reshape(bs, n_tok, d),
            kp.reshape(bp, seq, N_HEADS, HEAD_DIM), vp.reshape(bp, seq, N_HEADS, HEAD_DIM), new_state)
```

```python
import functools

import jax
import jax.numpy as jnp
from jax import lax
from jax.experimental import pallas as pl
from jax.experimental.pallas import tpu as pltpu

F32 = jnp.float32
BF16 = jnp.bfloat16

EPS = 1e-6
N_HEADS = 16
HEAD_DIM = 128
GRID_W = 64
NA_ROWS = 8
NA_COLS = 16
RPB_R = 2 * NA_ROWS - 1
RPB_C = 2 * NA_COLS - 1
S5_CH = 16
S5_P = 64
S5_CHUNK = 16
S5_SEG = 8
NEG = -1e30
LANES = 128
VMEM_LIMIT = 56 * 1024 * 1024


def _cparams(*sem):
    return pltpu.CompilerParams(dimension_semantics=sem, vmem_limit_bytes=VMEM_LIMIT)


def _dot(a, b):
    return jnp.dot(a, b, preferred_element_type=F32)


def _dot_nt(a, b, precision=None):
    return lax.dot_general(a, b, (((1,), (1,)), ((), ())), preferred_element_type=F32,
                           precision=precision)


def _silu(x):
    return x * jax.nn.sigmoid(x)


def _ada_kernel(cond_ref, *refs, n_layers):
    w_refs, b_refs, o_ref = refs[:n_layers], refs[n_layers:2 * n_layers], refs[2 * n_layers]
    s = _silu(cond_ref[...]).astype(BF16)
    layer = pl.program_id(0)
    for i in range(n_layers):
        @pl.when(layer == i)
        def _(i=i):
            o_ref[...] = _dot(s, w_refs[i][...].astype(BF16)) + b_refs[i][...]


def _ada(cond8, w_adas, b_adas):
    n_layers = len(w_adas)
    d, n = w_adas[0].shape
    tn = 1024
    nj = n // tn

    def own(i):
        return lambda l, j: (0, jnp.where(l == i, j, jnp.where(l < i, 0, nj - 1)))

    return pl.pallas_call(
        functools.partial(_ada_kernel, n_layers=n_layers),
        grid=(n_layers, nj),
        in_specs=[pl.BlockSpec((8, d), lambda l, j: (0, 0))]
        + [pl.BlockSpec((d, tn), own(i)) for i in range(n_layers)]
        + [pl.BlockSpec((1, tn), own(i)) for i in range(n_layers)],
        out_specs=pl.BlockSpec((None, 8, tn), lambda l, j: (l, 0, j)),
        out_shape=jax.ShapeDtypeStruct((n_layers, 8, n), F32),
        compiler_params=_cparams("arbitrary", "arbitrary"),
        name="ada",
    )(cond8, *w_adas, *[b.reshape(1, n) for b in b_adas])


def _norm_mod(x, g, shift, scale):
    y = x * lax.rsqrt(jnp.mean(x * x, axis=-1, keepdims=True) + EPS) * g
    return (y * (1.0 + scale) + shift).astype(BF16)


def _inproj_kernel(x0_ref, xn_ref, g_ref, shift0_ref, scale0_ref, shiftn_ref, scalen_ref, *rest, out_split,
                   head_split):
    n_split = max(out_split + head_split) + 1
    w_refs = rest[:n_split]
    rest = rest[n_split:]
    o_refs = rest[:len(out_split)]
    hd_refs = rest[len(out_split):len(out_split) + len(head_split)]
    h_ref = rest[len(out_split) + len(head_split)]
    i = pl.program_id(0)
    j = pl.program_id(1)
    nj = pl.num_programs(1)
    slot = i % 2

    @pl.when((i == 0) & (j == 0))
    def _():
        h_ref[0] = _norm_mod(x0_ref[...], g_ref[...], shift0_ref[...], scale0_ref[...])

    if head_split:
        stage_ref, sem = rest[len(out_split) + len(head_split) + 1:]
        tm, tn = stage_ref.shape[2:]
        t = i * nj + j

        def head_copies(tt):
            ii, jj = tt // nj, tt % nj
            return [pltpu.make_async_copy(stage_ref.at[tt % 2, n, :, pl.ds(hh * HEAD_DIM, HEAD_DIM)],
                                          hd.at[pl.ds(ii * tm, tm), jj * (tn // HEAD_DIM) + hh, :], sem.at[tt % 2])
                    for n, hd in enumerate(hd_refs) for hh in range(tn // HEAD_DIM)]

        @pl.when(t >= 2)
        def _():
            for c in head_copies(t - 2):
                c.wait()

    h = h_ref[slot]
    for s, w_ref in enumerate(w_refs):
        r = _dot(h, w_ref[...])
        for o_ref, o_s in zip(o_refs, out_split):
            if o_s == s:
                o_ref[...] = r.astype(o_ref.dtype)
        if s in head_split:
            stage_ref[t % 2, head_split.index(s)] = r
    rs = xn_ref.shape[0]
    h_ref[1 - slot, pl.ds(pl.multiple_of(j * rs, rs), rs), :] = _norm_mod(
        xn_ref[...], g_ref[...], shiftn_ref[...], scalen_ref[...])

    if head_split:
        for c in head_copies(t):
            c.start()

        @pl.when(t == pl.num_programs(0) * nj - 1)
        def _():
            @pl.when(t >= 1)
            def _():
                for c in head_copies(t - 1):
                    c.wait()

            for c in head_copies(t):
                c.wait()


def _inproj(x, g, mod3, cond0, rows_per_cond, w, outs, tm, tn, head_split=()):
    m, d = x.shape
    out_split = tuple(s for s, _ in outs)
    out_dtypes = [dt for _, dt in outs]
    head_split = tuple(head_split)
    n_split = max(out_split + head_split) + 1
    ni = m // tm
    nj = d // tn

    def nxt(i):
        return jnp.minimum(i + 1, ni - 1)

    def cond_of(i):
        return cond0 + (i * tm) // rows_per_cond

    in_specs = [pl.BlockSpec((tm, d), lambda i, j: (0, 0), pipeline_mode=pl.Buffered(1)),
                pl.BlockSpec((tm // nj, d), lambda i, j: (nxt(i) * nj + j, 0)),
                pl.BlockSpec((1, d), lambda i, j: (0, 0)),
                pl.BlockSpec((None, 1, d), lambda i, j: (cond_of(0), 0, 0)),
                pl.BlockSpec((None, 1, d), lambda i, j: (cond_of(0), 0, 1)),
                pl.BlockSpec((None, 1, d), lambda i, j: (cond_of(nxt(i)), 0, 0)),
                pl.BlockSpec((None, 1, d), lambda i, j: (cond_of(nxt(i)), 0, 1))]
    for s in range(n_split):
        in_specs.append(pl.BlockSpec((d, tn), lambda i, j, s=s: (0, s * nj + j)))
    scratch = [pltpu.VMEM((2, tm, d), BF16)]
    if head_split:
        scratch += [pltpu.VMEM((2, len(head_split), tm, tn), F32), pltpu.SemaphoreType.DMA((2,))]
    return pl.pallas_call(
        functools.partial(_inproj_kernel, out_split=out_split, head_split=head_split),
        grid=(ni, nj),
        in_specs=in_specs,
        out_specs=[pl.BlockSpec((tm, tn), lambda i, j: (i, j)) for _ in outs]
        + [pl.BlockSpec(memory_space=pl.ANY)] * len(head_split),
        out_shape=[jax.ShapeDtypeStruct((m, d), dt) for dt in out_dtypes]
        + [jax.ShapeDtypeStruct((m, d // HEAD_DIM, HEAD_DIM), F32)] * len(head_split),
        scratch_shapes=scratch,
        compiler_params=_cparams("arbitrary", "arbitrary"),
        name="inproj",
    )(x, x, g.reshape(1, d), mod3, mod3, mod3, mod3, *([w] * n_split))


def _softmax_pv(parts):
    m = parts[0][0].max(axis=-1, keepdims=True)
    for s, _ in parts[1:]:
        m = jnp.maximum(m, s.max(axis=-1, keepdims=True))
    l = None
    o = None
    for s, v in parts:
        e = jnp.exp(s - m)
        li = e.sum(axis=-1, keepdims=True)
        oi = _dot(e.astype(BF16), v)
        l = li if l is None else l + li
        o = oi if o is None else o + oi
    return o / l


def _ctx_layer_kernel(q_ref, k_ref, v_ref, z_ref, x_ref, gate_ref, w_ref, y_ref, a_ref, *, seq):
    scale = HEAD_DIM ** -0.5
    for b in range(q_ref.shape[0] // seq):
        rows = slice(b * seq, (b + 1) * seq)
        for h in range(N_HEADS):
            sl = slice(h * HEAD_DIM, (h + 1) * HEAD_DIM)
            s = _dot_nt(q_ref[rows, sl], k_ref[rows, sl]) * scale
            o = _softmax_pv([(s, v_ref[rows, sl])])
            a_ref[rows, sl] = (o * _silu(z_ref[rows, sl].astype(F32))).astype(BF16)
    y_ref[...] = x_ref[...] + gate_ref[...] * _dot(a_ref[...], w_ref[...])


def _ctx_layer(q, k, v, z, x, mod3, cond, w_out, seq, tm):
    m, d = x.shape
    row = pl.BlockSpec((tm, d), lambda i: (i, 0))
    return pl.pallas_call(
        functools.partial(_ctx_layer_kernel, seq=seq),
        grid=(m // tm,),
        in_specs=[row, row, row, row, row,
                  pl.BlockSpec((None, 1, d), lambda i: (cond, 0, 2)),
                  pl.BlockSpec((d, d), lambda i: (0, 0), pipeline_mode=pl.Buffered(1))],
        out_specs=row,
        out_shape=jax.ShapeDtypeStruct((m, d), F32),
        scratch_shapes=[pltpu.VMEM((tm, d), BF16)],
        compiler_params=_cparams("parallel"),
        name="ctx_layer",
    )(q, k, v, z, x, mod3, w_out)


def _na_window_start(r, rows):
    return min(max(r - NA_ROWS // 2, 0), rows - NA_ROWS)


def _na_kernel(q_ref, k_ref, v_ref, kc_hbm, vc_hbm, rpb_ref, o_ref, cbuf, csem, *, rows, q_rows):
    scale = HEAD_DIM ** -0.5
    w = GRID_W
    n_heads = pl.num_programs(1)
    t = pl.program_id(0) * n_heads + pl.program_id(1)

    def ctx_copies(tt):
        return [pltpu.make_async_copy(c.at[tt // n_heads, :, tt % n_heads, :], cbuf.at[tt % 2, n], csem.at[tt % 2])
                for n, c in enumerate((kc_hbm, vc_hbm))]

    @pl.when(t == 0)
    def _():
        for c in ctx_copies(t):
            c.start()

    @pl.when(t + 1 < pl.num_programs(0) * n_heads)
    def _():
        for c in ctx_copies(t + 1):
            c.start()
    qc = lax.broadcasted_iota(jnp.int32, (w, LANES), 0)
    lane = lax.broadcasted_iota(jnp.int32, (w, LANES), 1)
    c0 = jnp.clip(qc - NA_COLS // 2, 0, w - NA_COLS)
    ok_l = (lane >= c0) & (lane < c0 + NA_COLS)
    ok_r = (lane - w >= c0) & (lane - w < c0 + NA_COLS)
    neg_tile = jnp.full((w, LANES), NEG, F32)
    tile_l, tile_r = [], []
    for d in range(RPB_R):
        base = jnp.broadcast_to(rpb_ref[d:d + 1, :], (w, LANES))
        left = pltpu.roll(base, LANES - (NA_COLS - 1), 1, stride=1, stride_axis=0)
        right = pltpu.roll(base, w - (NA_COLS - 1), 1, stride=1, stride_axis=0)
        tile_l.append(jnp.where(ok_l, left, NEG))
        tile_r.append(jnp.where(ok_r, right, NEG))

    for c in ctx_copies(t):
        c.wait()
    kc = cbuf[t % 2, 0].astype(BF16)
    vc = cbuf[t % 2, 1].astype(BF16)
    n_groups = rows // q_rows
    for gi in range(n_groups):
        rs = list(range(gi * q_rows, (gi + 1) * q_rows))
        klo = min(_na_window_start(r, rows) for r in rs) // 2 * 2
        khi = -(-(max(_na_window_start(r, rows) for r in rs) + NA_ROWS) // 2) * 2
        bias_rows = []
        for r in rs:
            r0 = _na_window_start(r, rows)
            tiles = []
            for kr in range(klo, khi, 2):
                ok0 = r0 <= kr < r0 + NA_ROWS
                ok1 = r0 <= kr + 1 < r0 + NA_ROWS
                t0 = tile_l[kr - r + NA_ROWS - 1] if ok0 else neg_tile
                t1 = tile_r[kr + 1 - r + NA_ROWS - 1] if ok1 else neg_tile
                tiles.append(jnp.maximum(t0, t1) if (ok0 or ok1) else neg_tile)
            bias_rows.append(jnp.concatenate(tiles, axis=1))
        bias = jnp.concatenate(bias_rows, axis=0)
        q = q_ref[gi * q_rows * w:(gi + 1) * q_rows * w, :]
        kl = k_ref[klo * w:khi * w, :]
        vl = v_ref[klo * w:khi * w, :]
        s_loc = _dot_nt(q, kl) * scale + bias
        s_ctx = _dot_nt(q, kc) * scale
        o = _softmax_pv([(s_loc, vl), (s_ctx, vc)])
        o_ref[gi * q_rows * w:(gi + 1) * q_rows * w, :] = o.astype(o_ref.dtype)


def _latent_na(q, k, v, k_ctx, v_ctx, rpb_pad, n_tok):
    m, d = q.shape
    nb = m // n_tok
    lc = k_ctx.shape[1]
    rows = n_tok // GRID_W
    spec = pl.BlockSpec((n_tok, HEAD_DIM), lambda b, h: (b, h))
    cspec = pl.BlockSpec(memory_space=pl.ANY)
    return pl.pallas_call(
        functools.partial(_na_kernel, rows=rows, q_rows=4),
        grid=(nb, N_HEADS),
        in_specs=[spec, spec, spec, cspec, cspec,
                  pl.BlockSpec((None, 16, LANES), lambda b, h: (h, 0, 0))],
        out_specs=spec,
        out_shape=jax.ShapeDtypeStruct((m, d), BF16),
        scratch_shapes=[pltpu.VMEM((2, 2, lc, HEAD_DIM), F32), pltpu.SemaphoreType.DMA((2,))],
        compiler_params=_cparams("arbitrary", "arbitrary"),
        name="latent_na",
    )(q, k, v, k_ctx, v_ctx, rpb_pad)


def _gated_out_kernel(o_ref, z_ref, x_ref, gate_ref, w_ref, y_ref):
    a = o_ref[...].astype(F32) * _silu(z_ref[...].astype(F32))
    y_ref[...] = x_ref[...] + gate_ref[...] * _dot(a.astype(BF16), w_ref[...])


def _gated_out(o, z, x, mod3, cond0, rows_per_cond, w_out, tm):
    m, d = x.shape
    row = pl.BlockSpec((tm, d), lambda i: (i, 0))
    return pl.pallas_call(
        _gated_out_kernel,
        grid=(m // tm,),
        in_specs=[row, row, row,
                  pl.BlockSpec((None, 1, d), lambda i: (cond0 + (i * tm) // rows_per_cond, 0, 2)),
                  pl.BlockSpec((d, d), lambda i: (0, 0), pipeline_mode=pl.Buffered(1))],
        out_specs=row,
        out_shape=jax.ShapeDtypeStruct((m, d), F32),
        compiler_params=_cparams("parallel"),
        name="gated_out",
    )(o, z, x, mod3, w_out)


LANE_TILE = 512


def _slots_per_step(x4):
    nq, m, _, _ = x4.shape
    assert LANE_TILE % (nq * m) == 0 and S5_CHUNK % (LANE_TILE // (nq * m)) == 0
    return LANE_TILE // (nq * m)


def _slot_scratch(d):
    return [pltpu.VMEM((2, LANE_TILE, d), F32), pltpu.SemaphoreType.DMA((2,))]


def _slot_copies(x_hbm, buf, sem, l, to_hbm):
    nq, m, _, _ = x_hbm.shape
    n_x = buf.shape[1] // (nq * m)
    copies = []
    for xi in range(n_x):
        for k in range(m):
            hbm = x_hbm.at[:, k, l * n_x + xi, :]
            vmem = buf.at[l % 2, pl.ds((xi * m + k) * nq, nq), :]
            src, dst = (vmem, hbm) if to_hbm else (hbm, vmem)
            copies.append(pltpu.make_async_copy(src, dst, sem.at[l % 2]))
    return copies


def _fetch_slots(x_hbm, buf, sem):
    l = pl.program_id(0)

    @pl.when(l == 0)
    def _():
        for c in _slot_copies(x_hbm, buf, sem, l, False):
            c.start()

    @pl.when(l + 1 < pl.num_programs(0))
    def _():
        for c in _slot_copies(x_hbm, buf, sem, l + 1, False):
            c.start()

    for c in _slot_copies(x_hbm, buf, sem, l, False):
        c.wait()
    return buf[l % 2]


def _store_slots(res, o_hbm, buf, sem):
    l = pl.program_id(0)

    def wait(ll):
        for c in _slot_copies(o_hbm, buf, sem, ll, True):
            c.wait()

    @pl.when(l >= 2)
    def _():
        wait(l - 2)

    buf[l % 2] = res
    for c in _slot_copies(o_hbm, buf, sem, l, True):
        c.start()

    @pl.when(l == pl.num_programs(0) - 1)
    def _():
        @pl.when(l >= 1)
        def _():
            wait(l - 1)

        wait(l)


def _cond_rows(ref, n_rows, nq, n_cond):
    out = ref[n_cond - 1]
    if n_cond > 1:
        b = (lax.broadcasted_iota(jnp.int32, (n_rows, 1), 0) % nq) // (nq // n_cond)
        for i in range(n_cond - 2, -1, -1):
            out = jnp.where(b == i, ref[i], out)
    return out


def _inproj_t_kernel(x_hbm, g_ref, shift_ref, scale_ref, wu_ref, wz_ref, u_ref, z_ref, buf, sem, *, n_cond):
    nq = x_hbm.shape[0]
    x = _fetch_slots(x_hbm, buf, sem)
    n_rows = x.shape[0]
    h = _norm_mod(x, g_ref[...], _cond_rows(shift_ref, n_rows, nq, n_cond),
                  _cond_rows(scale_ref, n_rows, nq, n_cond))
    u_ref[...] = _dot(h, wu_ref[...]).T.astype(u_ref.dtype)
    z_ref[...] = _dot(h, wz_ref[...]).astype(z_ref.dtype)


def _inproj_t(x4, g, mod3, cond0, n_cond, w_in):
    nq, m, _, d = x4.shape
    n_tok = nq * m * S5_CHUNK
    _slots_per_step(x4)

    def mspec(col):
        return pl.BlockSpec((n_cond, 1, d), lambda l: (cond0 // n_cond, 0, col))

    def wspec(col):
        return pl.BlockSpec((d, d), lambda l: (0, col), pipeline_mode=pl.Buffered(1))

    return pl.pallas_call(
        functools.partial(_inproj_t_kernel, n_cond=n_cond),
        grid=(n_tok // LANE_TILE,),
        in_specs=[pl.BlockSpec(memory_space=pl.ANY),
                  pl.BlockSpec((1, d), lambda l: (0, 0)), mspec(0), mspec(1), wspec(0), wspec(1)],
        out_specs=[pl.BlockSpec((d, LANE_TILE), lambda l: (0, l)), pl.BlockSpec((LANE_TILE, d), lambda l: (l, 0))],
        out_shape=[jax.ShapeDtypeStruct((d, n_tok), BF16), jax.ShapeDtypeStruct((n_tok, d), BF16)],
        scratch_shapes=_slot_scratch(d),
        compiler_params=_cparams("arbitrary"),
        name="inproj_t",
    )(x4, g.reshape(1, d), mod3, mod3, w_in, w_in)


def _cmul(ar, ai, br, bi):
    return ar * br - ai * bi, ar * bi + ai * br


def _split_bf16(a):
    hi = a.astype(BF16)
    return hi, (a - hi.astype(F32)).astype(BF16)


def _s5_powers(ar, ai):
    width = ar.shape[-1]
    pw = [(jnp.ones_like(ar), jnp.zeros_like(ar))]
    for _ in range(S5_CHUNK):
        pw.append(_cmul(pw[-1][0], pw[-1][1], ar, ai))
    is_fwd = lax.broadcasted_iota(jnp.int32, (1, width), 1) < width // 2

    def pattern(fwd_ascending):
        out = []
        for part in range(2):
            blocks = []
            for j in range(S5_CHUNK):
                ef, eb = (j, S5_CHUNK - 1 - j) if fwd_ascending else (S5_CHUNK - 1 - j, j)
                blocks.append(jnp.broadcast_to(jnp.where(is_fwd, pw[ef][part], pw[eb][part]), (S5_CH, width)))
            out.append(jnp.concatenate(blocks, axis=0))
        return out

    return pattern(True), pattern(False), pw[S5_CHUNK]


S5_GROUPS_PER_STEP = 8


def _s5_kernel(xp_ref, xs_ref, *rest, nb_p, nb_s, n_seg_s):
    yp_ref, ys_ref = rest[9], rest[10]
    for gi in range(S5_GROUPS_PER_STEP):
        ch = pl.ds(gi * S5_CH, S5_CH)
        _s5_group(xp_ref.at[ch], xs_ref.at[ch], *[r.at[gi] for r in rest[:9]], yp_ref.at[ch], ys_ref.at[ch],
                  *[r.at[gi] for r in rest[11:]], nb_p=nb_p, nb_s=nb_s, n_seg_s=n_seg_s)


def _s5_group(xp_ref, xs_ref, h0_ref, are_ref, aim_ref, ldt_ref, bre_ref, bim_ref, cre_ref, cim_ref, dsk_ref,
              yp_ref, ys_ref, st_ref, w_ref, ws_ref, wc_ref, s_ref, f_ref, *, nb_p, nb_s, n_seg_s):
    p = S5_P
    kc = S5_CHUNK * S5_CH
    a_re = are_ref[...]
    a_im = aim_ref[...]
    dt = jnp.exp(ldt_ref[...])
    mag = jnp.exp(a_re * dt)
    ab_re = mag * jnp.cos(a_im * dt)
    ab_im = mag * jnp.sin(a_im * dt)
    den = a_re * a_re + a_im * a_im
    nr = ab_re - 1.0
    f_re = (nr * a_re + ab_im * a_im) / den
    f_im = (ab_im * a_re - nr * a_im) / den
    bbt_re, bbt_im = _cmul(f_re, f_im, bre_ref[...], bim_ref[...])

    (pg_r, pg_i), (ps_r, ps_i), (ac_r, ac_i) = _s5_powers(ab_re, ab_im)
    c_re = jnp.concatenate([cre_ref[...]] * S5_CHUNK, axis=0)
    c_im = jnp.concatenate([cim_ref[...]] * S5_CHUNK, axis=0)
    bt_re = jnp.concatenate([bbt_re] * S5_CHUNK, axis=0)
    bt_im = jnp.concatenate([bbt_im] * S5_CHUNK, axis=0)

    g_re, g_im = _cmul(pg_r, pg_i, c_re, c_im)
    fwd16 = lax.broadcasted_iota(jnp.int32, (S5_CH, 2 * p), 1) < p
    g_hi, g_lo = _split_bf16(jnp.concatenate([g_re, g_im], axis=1))
    r0 = []
    for d in range(2):
        msk = fwd16 if d == 0 else jnp.logical_not(fwd16)
        b_hi, b_lo = _split_bf16(
            jnp.concatenate([jnp.where(msk, bbt_re, 0.0), jnp.where(msk, -bbt_im, 0.0)], axis=1))
        r0.append(_dot_nt(b_hi, g_hi) + (_dot_nt(b_hi, g_lo) + _dot_nt(b_lo, g_hi)))
    lane = lax.broadcasted_iota(jnp.int32, (S5_CH, kc), 1)
    for s in range(S5_CHUNK):
        f = jnp.where(lane >= S5_CH * s, pltpu.roll(r0[0], S5_CH * s, 1), 0.0) if s else r0[0]
        b = jnp.where(lane < S5_CH * (s + 1), pltpu.roll(r0[1], (S5_CH * (s + 1)) % kc, 1), 0.0)
        w_ref[s * S5_CH:(s + 1) * S5_CH, :] = (f + b).astype(BF16)
    e_re, e_im = _cmul(ps_r, ps_i, bt_re, bt_im)
    ws_ref[:, 0:2 * p] = e_re.astype(BF16)
    ws_ref[:, 2 * p:4 * p] = e_im.astype(BF16)
    g1_re, g1_im = _cmul(g_re, g_im, ab_re, ab_im)
    wc_ref[:, 0:2 * p] = g1_re.astype(BF16)
    wc_ref[:, 2 * p:4 * p] = (-g1_im).astype(BF16)

    def run(xt_ref, yt_ref, nb, n_seg, hr, hm):
        rows = xt_ref.shape[1] // S5_CHUNK
        nbx = nb * n_seg
        m = rows // nbx
        xt = jnp.concatenate([xt_ref[:, s * rows:(s + 1) * rows] for s in range(S5_CHUNK)], axis=0)
        xf = xt.astype(F32).T
        x = xf.astype(BF16)
        y = _dot(x, w_ref[...]) + dsk_ref[...] * xf
        s_all = _dot(x, ws_ref[...])
        s_ref[0, 0:rows, :] = s_all[:, 0:2 * p]
        s_ref[1, 0:rows, :] = s_all[:, 2 * p:4 * p]

        def step(ar, ai, hr, hm, plane_r, plane_i, rf, rb):
            isf = lax.broadcasted_iota(jnp.int32, hr.shape, 1) < p
            sr = jnp.where(isf, plane_r[rf, :], plane_r[rb, :])
            sm = jnp.where(isf, plane_i[rf, :], plane_i[rb, :])
            return ar * hr - ai * hm + sr, ar * hm + ai * hr + sm

        def scan(hr, hm, store):
            arb = jnp.broadcast_to(ac_r, (nbx, 2 * p))
            aib = jnp.broadcast_to(ac_i, (nbx, 2 * p))
            for k in range(m):
                rf = pl.ds(k * nbx, nbx)
                rb = pl.ds((m - 1 - k) * nbx, nbx)
                if store:
                    s_ref[2, rf, :] = hr
                    s_ref[3, rf, :] = hm
                    s_ref[4, rb, :] = hr
                    s_ref[5, rb, :] = hm
                hr, hm = step(arb, aib, hr, hm, s_ref.at[0], s_ref.at[1], rf, rb)
            return hr, hm

        if n_seg > 1:
            zeros = jnp.zeros((nbx, 2 * p), F32)
            f_ref[0], f_ref[1] = scan(zeros, zeros, False)
            sr_, si_ = ac_r, ac_i
            for _ in range(m - 1):
                sr_, si_ = _cmul(sr_, si_, ac_r, ac_i)
            sr_ = jnp.broadcast_to(sr_, (nb, 2 * p))
            si_ = jnp.broadcast_to(si_, (nb, 2 * p))
            for j in range(n_seg):
                rf = pl.ds(j, nb, stride=n_seg)
                rb = pl.ds(n_seg - 1 - j, nb, stride=n_seg)
                f_ref[2, rf, :] = hr
                f_ref[3, rf, :] = hm
                f_ref[4, rb, :] = hr
                f_ref[5, rb, :] = hm
                hr, hm = step(sr_, si_, hr, hm, f_ref.at[0], f_ref.at[1], rf, rb)
            isf = lax.broadcasted_iota(jnp.int32, (nbx, 2 * p), 1) < p
            hr = jnp.where(isf, f_ref[2], f_ref[4])
            hm = jnp.where(isf, f_ref[3], f_ref[5])
        hr, hm = scan(hr, hm, True)
        isf_rows = lax.broadcasted_iota(jnp.int32, (rows, 2 * p), 1) < p
        h_prev = jnp.concatenate([jnp.where(isf_rows, s_ref[2, 0:rows, :], s_ref[4, 0:rows, :]),
                                  jnp.where(isf_rows, s_ref[3, 0:rows, :], s_ref[5, 0:rows, :])], axis=1)
        yt = jax.nn.gelu(y + _dot_nt(h_prev.astype(BF16), wc_ref[...])).T
        for t in range(S5_CHUNK):
            yt_ref[:, t * rows:(t + 1) * rows] = yt[t * S5_CH:(t + 1) * S5_CH, :].astype(yt_ref.dtype)
        return hr, hm

    zeros = jnp.zeros((nb_p, 2 * p), F32)
    hr, hm = run(xp_ref, yp_ref, nb_p, 1, zeros, zeros)
    st_ref[:, 0:2 * p] = hr
    st_ref[:, 2 * p:4 * p] = hm
    run(xs_ref, ys_ref, nb_s, n_seg_s, h0_ref[:, 0:2 * p], h0_ref[:, 2 * p:4 * p])


def _s5(utp, uts, h0, a_re, a_im, log_dt, bt_re, bt_im, c_re, c_im, d_skip, nb_p, nb_s, n_seg_s):
    d, lanes_p = utp.shape
    lanes_s = uts.shape[1]
    g = d // S5_CH
    rows_p = lanes_p // S5_CHUNK
    rows_s = lanes_s // S5_CHUNK
    kc = S5_CHUNK * S5_CH
    p = S5_P

    gps = S5_GROUPS_PER_STEP

    def gspec(shape):
        return pl.BlockSpec((gps,) + shape, lambda i: (i, 0, 0))

    def tspec(lanes):
        return pl.BlockSpec((gps * S5_CH, lanes), lambda i: (i, 0))

    return pl.pallas_call(
        functools.partial(_s5_kernel, nb_p=nb_p, nb_s=nb_s, n_seg_s=n_seg_s),
        grid=(g // gps,),
        in_specs=[tspec(lanes_p), tspec(lanes_s), gspec((nb_s, 4 * p)),
                  gspec((1, 2 * p)), gspec((1, 2 * p)), gspec((1, 2 * p)),
                  gspec((S5_CH, 2 * p)), gspec((S5_CH, 2 * p)), gspec((S5_CH, 2 * p)), gspec((S5_CH, 2 * p)),
                  gspec((1, kc))],
        out_specs=[tspec(lanes_p), tspec(lanes_s), gspec((nb_p, 4 * p))],
        out_shape=[jax.ShapeDtypeStruct((d, lanes_p), BF16),
                   jax.ShapeDtypeStruct((d, lanes_s), BF16),
                   jax.ShapeDtypeStruct((g, nb_p, 4 * p), F32)],
        scratch_shapes=[pltpu.VMEM((gps, kc, kc), BF16), pltpu.VMEM((gps, kc, 4 * p), BF16),
                        pltpu.VMEM((gps, kc, 4 * p), BF16),
                        pltpu.VMEM((gps, 6, max(rows_p, rows_s), 2 * p), F32),
                        pltpu.VMEM((gps, 6, nb_s * n_seg_s, 2 * p), F32)],
        compiler_params=_cparams("parallel"),
        name="s5",
    )(utp, uts, h0, a_re, a_im, log_dt, bt_re, bt_im, c_re, c_im,
      jnp.tile(d_skip.reshape(g, 1, S5_CH), (1, 1, S5_CHUNK)))


def _glu_kernel(yt0_ref, ytn_ref, z_ref, w_ref, b_ref, a_ref, yb_ref, *, tn):
    d = w_ref.shape[0]
    i = pl.program_id(0)
    slot = i % 2

    @pl.when(i == 0)
    def _():
        yb_ref[0] = yt0_ref[...].T

    for c in range(d // tn):
        cs = slice(c * tn, (c + 1) * tn)
        yb = yb_ref[slot]
        gl = _dot(yb, w_ref[:, cs]) + b_ref[:, cs]
        a = yb[:, cs].astype(F32) * jax.nn.sigmoid(gl) * _silu(z_ref[:, cs].astype(F32))
        a_ref[:, cs] = a.astype(a_ref.dtype)
        yb_ref[1 - slot, :, cs] = ytn_ref[cs, :].T


def _glu(yt, z, w_glu, b_glu, tn):
    d, m = yt.shape
    tm = LANE_TILE
    ni = m // tm
    return pl.pallas_call(
        functools.partial(_glu_kernel, tn=tn),
        grid=(ni,),
        in_specs=[pl.BlockSpec((d, tm), lambda i: (0, 0), pipeline_mode=pl.Buffered(1)),
                  pl.BlockSpec((d, tm), lambda i: (0, jnp.minimum(i + 1, ni - 1))),
                  pl.BlockSpec((tm, d), lambda i: (i, 0)),
                  pl.BlockSpec((d, d), lambda i: (0, 0), pipeline_mode=pl.Buffered(1)),
                  pl.BlockSpec((1, d), lambda i: (0, 0))],
        out_specs=pl.BlockSpec((tm, d), lambda i: (i, 0)),
        out_shape=jax.ShapeDtypeStruct((m, d), BF16),
        scratch_shapes=[pltpu.VMEM((2, tm, d), BF16)],
        compiler_params=_cparams("arbitrary"),
        name="glu",
    )(yt, yt, z, w_glu, b_glu.reshape(1, d))


def _out_norm_kernel(x_hbm, a_ref, gate_ref, fg_ref, w_ref, o_hbm, xbuf, xsem, obuf, osem, *, n_cond):
    nq = x_hbm.shape[0]
    x = _fetch_slots(x_hbm, xbuf, xsem)
    xn = x + _cond_rows(gate_ref, x.shape[0], nq, n_cond) * _dot(a_ref[...], w_ref[...])
    res = xn * lax.rsqrt(jnp.mean(xn * xn, axis=-1, keepdims=True) + EPS) * fg_ref[...]
    _store_slots(res, o_hbm, obuf, osem)


def _out_norm(a, x4, mod3, cond0, n_cond, final_g, w_out):
    nq, m, _, d = x4.shape
    _slots_per_step(x4)
    return pl.pallas_call(
        functools.partial(_out_norm_kernel, n_cond=n_cond),
        grid=(nq * m * S5_CHUNK // LANE_TILE,),
        in_specs=[pl.BlockSpec(memory_space=pl.ANY),
                  pl.BlockSpec((LANE_TILE, d), lambda l: (l, 0)),
                  pl.BlockSpec((n_cond, 1, d), lambda l: (cond0 // n_cond, 0, 2)),
                  pl.BlockSpec((1, d), lambda l: (0, 0)),
                  pl.BlockSpec((d, d), lambda l: (0, 0), pipeline_mode=pl.Buffered(1))],
        out_specs=pl.BlockSpec(memory_space=pl.ANY),
        out_shape=jax.ShapeDtypeStruct(x4.shape, F32),
        scratch_shapes=2 * _slot_scratch(d),
        compiler_params=_cparams("arbitrary"),
        name="out_norm",
    )(x4, a, mod3, final_g.reshape(1, d), w_out)


def _both_dirs(a):
    _, g, r, p = a.shape
    return a.transpose(1, 2, 0, 3).reshape(g, r, 2 * p)


def kernel(x_prompt, x_sample, cache_l0_k, cache_l0_v, state_l1_s5, c, c_ctx, l0_norm_g, l0_w_ada, l0_b_ada, l0_w_in, l0_rpb, l0_w_out, l1_norm_g, l1_w_ada, l1_b_ada, l1_w_in, l1_a_re, l1_a_im, l1_log_dt, l1_b_re, l1_b_im, l1_c_re, l1_c_im, l1_d, l1_w_glu, l1_b_glu, l1_w_out, final_norm_g):
    bp, seq, d = x_prompt.shape
    bs, n_tok, _ = x_sample.shape
    g = d // S5_CH
    p = S5_P
    xp = x_prompt.reshape(bp * seq, d)
    xs = x_sample.reshape(bs * n_tok, d)

    ctx = bs
    cond8 = jnp.zeros((8, d), F32).at[0:bs].set(c).at[ctx].set(c_ctx)
    mods = _ada(cond8, (l0_w_ada, l1_w_ada), (l0_b_ada, l1_b_ada))
    mod0 = mods[0].reshape(8, 1, 3 * d)
    mod1 = mods[1].reshape(8, 1, 3 * d)

    w_in0 = l0_w_in.astype(BF16)
    w_out0 = l0_w_out.astype(BF16)
    w_in1 = l1_w_in.astype(BF16)
    w_glu = l1_w_glu.astype(BF16)
    w_out1 = l1_w_out.astype(BF16)

    qs, ks, vs, zs = _inproj(xs, l0_norm_g, mod0, 0, n_tok, w_in0,
                             ((0, BF16), (1, BF16), (2, BF16), (3, BF16)), 1024, 256)
    qp, kpb, vpb, zp, kp, vp = _inproj(xp, l0_norm_g, mod0, ctx, bp * seq, w_in0,
                                       ((0, BF16), (1, BF16), (2, BF16), (3, BF16)), 1024, 256, head_split=(1, 2))
    rpb_pad = jnp.zeros((N_HEADS, 16, LANES), F32).at[:, :RPB_R, :RPB_C].set(l0_rpb)
    os_ = _latent_na(qs, ks, vs, cache_l0_k, cache_l0_v, rpb_pad, n_tok)
    n_seg_s = max(1, n_tok // S5_CHUNK // S5_SEG)
    x1p = _ctx_layer(qp, kpb, vpb, zp, xp, mod0, ctx, w_out0, seq, 512).reshape(bp, seq // S5_CHUNK, S5_CHUNK, d)
    x1s = _gated_out(os_, zs, xs, mod0, 0, n_tok, w_out0, 512).reshape(bs * n_seg_s, -1, S5_CHUNK, d)

    utp, zp1 = _inproj_t(x1p, l1_norm_g, mod1, ctx, 1, w_in1)
    uts, zs1 = _inproj_t(x1s, l1_norm_g, mod1, 0, bs, w_in1)
    h0 = state_l1_s5.transpose(3, 0, 2, 1, 4).reshape(g, bs, 4 * p)
    log_dt = jnp.broadcast_to(l1_log_dt[:, :, None, None], (2, g, 1, p))
    ytp, yts, st = _s5(
        utp, uts, h0,
        _both_dirs(l1_a_re[:, :, None, :]), _both_dirs(l1_a_im[:, :, None, :]), _both_dirs(log_dt),
        _both_dirs(l1_b_re.transpose(0, 1, 3, 2)), _both_dirs(l1_b_im.transpose(0, 1, 3, 2)),
        _both_dirs(l1_c_re), _both_dirs(l1_c_im), l1_d, bp, bs, n_seg_s)
    y_prompt = _out_norm(_glu(ytp, zp1, w_glu, l1_b_glu, 512), x1p, mod1, ctx, 1, final_norm_g, w_out1)
    y_sample = _out_norm(_glu(yts, zs1, w_glu, l1_b_glu, 512), x1s, mod1, 0, bs, final_norm_g, w_out1)
    new_state = st.reshape(g, bp, 2, 2, p).transpose(1, 3, 2, 0, 4)
    return (y_prompt.reshape(bp, seq, d), y_sample.reshape(bs, n_tok, d),
            kp.reshape(bp, seq, N_HEADS, HEAD_DIM), vp.reshape(bp, seq, N_HEADS, HEAD_DIM), new_state)
```

```python
import functools

import jax
import jax.numpy as jnp
from jax import lax
from jax.experimental import pallas as pl
from jax.experimental.pallas import tpu as pltpu

F32 = jnp.float32
BF16 = jnp.bfloat16

EPS = 1e-6
N_HEADS = 16
HEAD_DIM = 128
GRID_W = 64
NA_ROWS = 8
NA_COLS = 16
RPB_R = 2 * NA_ROWS - 1
RPB_C = 2 * NA_COLS - 1
S5_CH = 16
S5_P = 64
S5_CHUNK = 16
S5_SEG = 8
NEG = -1e30
LANES = 128
VMEM_LIMIT = 56 * 1024 * 1024


def _cparams(*sem):
    return pltpu.CompilerParams(dimension_semantics=sem, vmem_limit_bytes=VMEM_LIMIT)


def _dot(a, b):
    return jnp.dot(a, b, preferred_element_type=F32)


def _dot_nt(a, b, precision=None):
    return lax.dot_general(a, b, (((1,), (1,)), ((), ())), preferred_element_type=F32,
                           precision=precision)


def _silu(x):
    return x * jax.nn.sigmoid(x)


def _ada_kernel(cond_ref, *refs, n_layers):
    w_refs, b_refs, o_ref = refs[:n_layers], refs[n_layers:2 * n_layers], refs[2 * n_layers]
    s = _silu(cond_ref[...]).astype(BF16)
    layer = pl.program_id(0)
    for i in range(n_layers):
        @pl.when(layer == i)
        def _(i=i):
            o_ref[...] = _dot(s, w_refs[i][...].astype(BF16)) + b_refs[i][...]


def _ada(cond8, w_adas, b_adas):
    n_layers = len(w_adas)
    d, n = w_adas[0].shape
    tn = 1024
    nj = n // tn

    def own(i):
        return lambda l, j: (0, jnp.where(l == i, j, jnp.where(l < i, 0, nj - 1)))

    return pl.pallas_call(
        functools.partial(_ada_kernel, n_layers=n_layers),
        grid=(n_layers, nj),
        in_specs=[pl.BlockSpec((8, d), lambda l, j: (0, 0))]
        + [pl.BlockSpec((d, tn), own(i)) for i in range(n_layers)]
        + [pl.BlockSpec((1, tn), own(i)) for i in range(n_layers)],
        out_specs=pl.BlockSpec((None, 8, tn), lambda l, j: (l, 0, j)),
        out_shape=jax.ShapeDtypeStruct((n_layers, 8, n), F32),
        compiler_params=_cparams("arbitrary", "arbitrary"),
        name="ada",
    )(cond8, *w_adas, *[b.reshape(1, n) for b in b_adas])


def _norm_mod(x, g, shift, scale):
    y = x * lax.rsqrt(jnp.mean(x * x, axis=-1, keepdims=True) + EPS) * g
    return (y * (1.0 + scale) + shift).astype(BF16)


def _inproj_kernel(x0_ref, xn_ref, g_ref, shift0_ref, scale0_ref, shiftn_ref, scalen_ref, *rest, out_split,
                   head_split):
    n_split = max(out_split + head_split) + 1
    w_refs = rest[:n_split]
    rest = rest[n_split:]
    o_refs = rest[:len(out_split)]
    hd_refs = rest[len(out_split):len(out_split) + len(head_split)]
    h_ref = rest[len(out_split) + len(head_split)]
    i = pl.program_id(0)
    j = pl.program_id(1)
    nj = pl.num_programs(1)
    slot = i % 2

    @pl.when((i == 0) & (j == 0))
    def _():
        h_ref[0] = _norm_mod(x0_ref[...], g_ref[...], shift0_ref[...], scale0_ref[...])

    if head_split:
        stage_ref, sem = rest[len(out_split) + len(head_split) + 1:]
        tm, tn = stage_ref.shape[2:]
        t = i * nj + j

        def head_copies(tt):
            ii, jj = tt // nj, tt % nj
            return [pltpu.make_async_copy(stage_ref.at[tt % 2, n, :, pl.ds(hh * HEAD_DIM, HEAD_DIM)],
                                          hd.at[pl.ds(ii * tm, tm), jj * (tn // HEAD_DIM) + hh, :], sem.at[tt % 2])
                    for n, hd in enumerate(hd_refs) for hh in range(tn // HEAD_DIM)]

        @pl.when(t >= 2)
        def _():
            for c in head_copies(t - 2):
                c.wait()

    h = h_ref[slot]
    for s, w_ref in enumerate(w_refs):
        r = _dot(h, w_ref[...])
        for o_ref, o_s in zip(o_refs, out_split):
            if o_s == s:
                o_ref[...] = r.astype(o_ref.dtype)
        if s in head_split:
            stage_ref[t % 2, head_split.index(s)] = r
    rs = xn_ref.shape[0]
    h_ref[1 - slot, pl.ds(pl.multiple_of(j * rs, rs), rs), :] = _norm_mod(
        xn_ref[...], g_ref[...], shiftn_ref[...], scalen_ref[...])

    if head_split:
        for c in head_copies(t):
            c.start()

        @pl.when(t == pl.num_programs(0) * nj - 1)
        def _():
            @pl.when(t >= 1)
            def _():
                for c in head_copies(t - 1):
                    c.wait()

            for c in head_copies(t):
                c.wait()


def _inproj(x, g, mod3, cond0, rows_per_cond, w, outs, tm, tn, head_split=()):
    m, d = x.shape
    out_split = tuple(s for s, _ in outs)
    out_dtypes = [dt for _, dt in outs]
    head_split = tuple(head_split)
    n_split = max(out_split + head_split) + 1
    ni = m // tm
    nj = d // tn

    def nxt(i):
        return jnp.minimum(i + 1, ni - 1)

    def cond_of(i):
        return cond0 + (i * tm) // rows_per_cond

    in_specs = [pl.BlockSpec((tm, d), lambda i, j: (0, 0), pipeline_mode=pl.Buffered(1)),
                pl.BlockSpec((tm // nj, d), lambda i, j: (nxt(i) * nj + j, 0)),
                pl.BlockSpec((1, d), lambda i, j: (0, 0)),
                pl.BlockSpec((None, 1, d), lambda i, j: (cond_of(0), 0, 0)),
                pl.BlockSpec((None, 1, d), lambda i, j: (cond_of(0), 0, 1)),
                pl.BlockSpec((None, 1, d), lambda i, j: (cond_of(nxt(i)), 0, 0)),
                pl.BlockSpec((None, 1, d), lambda i, j: (cond_of(nxt(i)), 0, 1))]
    for s in range(n_split):
        in_specs.append(pl.BlockSpec((d, tn), lambda i, j, s=s: (0, s * nj + j)))
    scratch = [pltpu.VMEM((2, tm, d), BF16)]
    if head_split:
        scratch += [pltpu.VMEM((2, len(head_split), tm, tn), F32), pltpu.SemaphoreType.DMA((2,))]
    return pl.pallas_call(
        functools.partial(_inproj_kernel, out_split=out_split, head_split=head_split),
        grid=(ni, nj),
        in_specs=in_specs,
        out_specs=[pl.BlockSpec((tm, tn), lambda i, j: (i, j)) for _ in outs]
        + [pl.BlockSpec(memory_space=pl.ANY)] * len(head_split),
        out_shape=[jax.ShapeDtypeStruct((m, d), dt) for dt in out_dtypes]
        + [jax.ShapeDtypeStruct((m, d // HEAD_DIM, HEAD_DIM), F32)] * len(head_split),
        scratch_shapes=scratch,
        compiler_params=_cparams("arbitrary", "arbitrary"),
        name="inproj",
    )(x, x, g.reshape(1, d), mod3, mod3, mod3, mod3, *([w] * n_split))


def _softmax_pv(parts):
    m = parts[0][0].max(axis=-1, keepdims=True)
    for s, _ in parts[1:]:
        m = jnp.maximum(m, s.max(axis=-1, keepdims=True))
    l = None
    o = None
    for s, v in parts:
        e = jnp.exp(s - m)
        li = e.sum(axis=-1, keepdims=True)
        oi = _dot(e.astype(BF16), v)
        l = li if l is None else l + li
        o = oi if o is None else o + oi
    return o / l


def _ctx_layer_kernel(q_ref, k_ref, v_ref, z_ref, x_ref, gate_ref, w_ref, y_ref, a_ref, *, seq):
    scale = HEAD_DIM ** -0.5
    for b in range(q_ref.shape[0] // seq):
        rows = slice(b * seq, (b + 1) * seq)
        for h in range(N_HEADS):
            sl = slice(h * HEAD_DIM, (h + 1) * HEAD_DIM)
            s = _dot_nt(q_ref[rows, sl], k_ref[rows, sl]) * scale
            o = _softmax_pv([(s, v_ref[rows, sl])])
            a_ref[rows, sl] = (o * _silu(z_ref[rows, sl].astype(F32))).astype(BF16)
    y_ref[...] = x_ref[...] + gate_ref[...] * _dot(a_ref[...], w_ref[...])


def _ctx_layer(q, k, v, z, x, mod3, cond, w_out, seq, tm):
    m, d = x.shape
    row = pl.BlockSpec((tm, d), lambda i: (i, 0))
    return pl.pallas_call(
        functools.partial(_ctx_layer_kernel, seq=seq),
        grid=(m // tm,),
        in_specs=[row, row, row, row, row,
                  pl.BlockSpec((None, 1, d), lambda i: (cond, 0, 2)),
                  pl.BlockSpec((d, d), lambda i: (0, 0), pipeline_mode=pl.Buffered(1))],
        out_specs=row,
        out_shape=jax.ShapeDtypeStruct((m, d), F32),
        scratch_shapes=[pltpu.VMEM((tm, d), BF16)],
        compiler_params=_cparams("parallel"),
        name="ctx_layer",
    )(q, k, v, z, x, mod3, w_out)


def _na_window_start(r, rows):
    return min(max(r - NA_ROWS // 2, 0), rows - NA_ROWS)


def _na_kernel(q_ref, k_ref, v_ref, kc_hbm, vc_hbm, rpb_ref, o_ref, cbuf, csem, *, rows, q_rows):
    scale = HEAD_DIM ** -0.5
    w = GRID_W
    n_heads = pl.num_programs(1)
    t = pl.program_id(0) * n_heads + pl.program_id(1)

    def ctx_copies(tt):
        return [pltpu.make_async_copy(c.at[tt // n_heads, :, tt % n_heads, :], cbuf.at[tt % 2, n], csem.at[tt % 2])
                for n, c in enumerate((kc_hbm, vc_hbm))]

    @pl.when(t == 0)
    def _():
        for c in ctx_copies(t):
            c.start()

    @pl.when(t + 1 < pl.num_programs(0) * n_heads)
    def _():
        for c in ctx_copies(t + 1):
            c.start()
    qc = lax.broadcasted_iota(jnp.int32, (w, LANES), 0)
    lane = lax.broadcasted_iota(jnp.int32, (w, LANES), 1)
    c0 = jnp.clip(qc - NA_COLS // 2, 0, w - NA_COLS)
    ok_l = (lane >= c0) & (lane < c0 + NA_COLS)
    ok_r = (lane - w >= c0) & (lane - w < c0 + NA_COLS)
    neg_tile = jnp.full((w, LANES), NEG, F32)
    tile_l, tile_r = [], []
    for d in range(RPB_R):
        base = jnp.broadcast_to(rpb_ref[d:d + 1, :], (w, LANES))
        left = pltpu.roll(base, LANES - (NA_COLS - 1), 1, stride=1, stride_axis=0)
        right = pltpu.roll(base, w - (NA_COLS - 1), 1, stride=1, stride_axis=0)
        tile_l.append(jnp.where(ok_l, left, NEG))
        tile_r.append(jnp.where(ok_r, right, NEG))

    for c in ctx_copies(t):
        c.wait()
    kc = cbuf[t % 2, 0].astype(BF16)
    vc = cbuf[t % 2, 1].astype(BF16)
    n_groups = rows // q_rows
    for gi in range(n_groups):
        rs = list(range(gi * q_rows, (gi + 1) * q_rows))
        klo = min(_na_window_start(r, rows) for r in rs) // 2 * 2
        khi = -(-(max(_na_window_start(r, rows) for r in rs) + NA_ROWS) // 2) * 2
        bias_rows = []
        for r in rs:
            r0 = _na_window_start(r, rows)
            tiles = []
            for kr in range(klo, khi, 2):
                ok0 = r0 <= kr < r0 + NA_ROWS
                ok1 = r0 <= kr + 1 < r0 + NA_ROWS
                t0 = tile_l[kr - r + NA_ROWS - 1] if ok0 else neg_tile
                t1 = tile_r[kr + 1 - r + NA_ROWS - 1] if ok1 else neg_tile
                tiles.append(jnp.maximum(t0, t1) if (ok0 or ok1) else neg_tile)
            bias_rows.append(jnp.concatenate(tiles, axis=1))
        bias = jnp.concatenate(bias_rows, axis=0)
        q = q_ref[gi * q_rows * w:(gi + 1) * q_rows * w, :]
        kl = k_ref[klo * w:khi * w, :]
        vl = v_ref[klo * w:khi * w, :]
        s_loc = _dot_nt(q, kl) * scale + bias
        s_ctx = _dot_nt(q, kc) * scale
        o = _softmax_pv([(s_loc, vl), (s_ctx, vc)])
        o_ref[gi * q_rows * w:(gi + 1) * q_rows * w, :] = o.astype(o_ref.dtype)


def _latent_na(q, k, v, k_ctx, v_ctx, rpb_pad, n_tok):
    m, d = q.shape
    nb = m // n_tok
    lc = k_ctx.shape[1]
    rows = n_tok // GRID_W
    spec = pl.BlockSpec((n_tok, HEAD_DIM), lambda b, h: (b, h))
    cspec = pl.BlockSpec(memory_space=pl.ANY)
    return pl.pallas_call(
        functools.partial(_na_kernel, rows=rows, q_rows=4),
        grid=(nb, N_HEADS),
        in_specs=[spec, spec, spec, cspec, cspec,
                  pl.BlockSpec((None, 16, LANES), lambda b, h: (h, 0, 0))],
        out_specs=spec,
        out_shape=jax.ShapeDtypeStruct((m, d), BF16),
        scratch_shapes=[pltpu.VMEM((2, 2, lc, HEAD_DIM), F32), pltpu.SemaphoreType.DMA((2,))],
        compiler_params=_cparams("arbitrary", "arbitrary"),
        name="latent_na",
    )(q, k, v, k_ctx, v_ctx, rpb_pad)


def _gated_out_kernel(o_ref, z_ref, x_ref, gate_ref, w_ref, y_ref):
    a = o_ref[...].astype(F32) * _silu(z_ref[...].astype(F32))
    y_ref[...] = x_ref[...] + gate_ref[...] * _dot(a.astype(BF16), w_ref[...])


def _gated_out(o, z, x, mod3, cond0, rows_per_cond, w_out, tm):
    m, d = x.shape
    row = pl.BlockSpec((tm, d), lambda i: (i, 0))
    return pl.pallas_call(
        _gated_out_kernel,
        grid=(m // tm,),
        in_specs=[row, row, row,
                  pl.BlockSpec((None, 1, d), lambda i: (cond0 + (i * tm) // rows_per_cond, 0, 2)),
                  pl.BlockSpec((d, d), lambda i: (0, 0), pipeline_mode=pl.Buffered(1))],
        out_specs=row,
        out_shape=jax.ShapeDtypeStruct((m, d), F32),
        compiler_params=_cparams("parallel"),
        name="gated_out",
    )(o, z, x, mod3, w_out)


LANE_TILE = 512


def _slots_per_step(x4):
    nq, m, _, _ = x4.shape
    assert LANE_TILE % (nq * m) == 0 and S5_CHUNK % (LANE_TILE // (nq * m)) == 0
    return LANE_TILE // (nq * m)


def _slot_scratch(d):
    return [pltpu.VMEM((2, LANE_TILE, d), F32), pltpu.SemaphoreType.DMA((2,))]


def _slot_copies(x_hbm, buf, sem, l, to_hbm):
    nq, m, _, _ = x_hbm.shape
    n_x = buf.shape[1] // (nq * m)
    copies = []
    for xi in range(n_x):
        for k in range(m):
            hbm = x_hbm.at[:, k, l * n_x + xi, :]
            vmem = buf.at[l % 2, pl.ds((xi * m + k) * nq, nq), :]
            src, dst = (vmem, hbm) if to_hbm else (hbm, vmem)
            copies.append(pltpu.make_async_copy(src, dst, sem.at[l % 2]))
    return copies


def _fetch_slots(x_hbm, buf, sem):
    l = pl.program_id(0)

    @pl.when(l == 0)
    def _():
        for c in _slot_copies(x_hbm, buf, sem, l, False):
            c.start()

    @pl.when(l + 1 < pl.num_programs(0))
    def _():
        for c in _slot_copies(x_hbm, buf, sem, l + 1, False):
            c.start()

    for c in _slot_copies(x_hbm, buf, sem, l, False):
        c.wait()
    return buf[l % 2]


def _store_slots(res, o_hbm, buf, sem):
    l = pl.program_id(0)

    def wait(ll):
        for c in _slot_copies(o_hbm, buf, sem, ll, True):
            c.wait()

    @pl.when(l >= 2)
    def _():
        wait(l - 2)

    buf[l % 2] = res
    for c in _slot_copies(o_hbm, buf, sem, l, True):
        c.start()

    @pl.when(l == pl.num_programs(0) - 1)
    def _():
        @pl.when(l >= 1)
        def _():
            wait(l - 1)

        wait(l)


def _cond_rows(ref, n_rows, nq, n_cond):
    out = ref[n_cond - 1]
    if n_cond > 1:
        b = (lax.broadcasted_iota(jnp.int32, (n_rows, 1), 0) % nq) // (nq // n_cond)
        for i in range(n_cond - 2, -1, -1):
            out = jnp.where(b == i, ref[i], out)
    return out


def _inproj_t_kernel(x_hbm, g_ref, shift_ref, scale_ref, wu_ref, wz_ref, u_ref, z_ref, buf, sem, *, n_cond):
    nq = x_hbm.shape[0]
    x = _fetch_slots(x_hbm, buf, sem)
    n_rows = x.shape[0]
    h = _norm_mod(x, g_ref[...], _cond_rows(shift_ref, n_rows, nq, n_cond),
                  _cond_rows(scale_ref, n_rows, nq, n_cond))
    u_ref[...] = _dot(h, wu_ref[...]).T.astype(u_ref.dtype)
    z_ref[...] = _dot(h, wz_ref[...]).astype(z_ref.dtype)


def _inproj_t(x4, g, mod3, cond0, n_cond, w_in):
    nq, m, _, d = x4.shape
    n_tok = nq * m * S5_CHUNK
    _slots_per_step(x4)

    def mspec(col):
        return pl.BlockSpec((n_cond, 1, d), lambda l: (cond0 // n_cond, 0, col))

    def wspec(col):
        return pl.BlockSpec((d, d), lambda l: (0, col), pipeline_mode=pl.Buffered(1))

    return pl.pallas_call(
        functools.partial(_inproj_t_kernel, n_cond=n_cond),
        grid=(n_tok // LANE_TILE,),
        in_specs=[pl.BlockSpec(memory_space=pl.ANY),
                  pl.BlockSpec((1, d), lambda l: (0, 0)), mspec(0), mspec(1), wspec(0), wspec(1)],
        out_specs=[pl.BlockSpec((d, LANE_TILE), lambda l: (0, l)), pl.BlockSpec((LANE_TILE, d), lambda l: (l, 0))],
        out_shape=[jax.ShapeDtypeStruct((d, n_tok), BF16), jax.ShapeDtypeStruct((n_tok, d), BF16)],
        scratch_shapes=_slot_scratch(d),
        compiler_params=_cparams("arbitrary"),
        name="inproj_t",
    )(x4, g.reshape(1, d), mod3, mod3, w_in, w_in)


def _cmul(ar, ai, br, bi):
    return ar * br - ai * bi, ar * bi + ai * br


def _split_bf16(a):
    hi = a.astype(BF16)
    return hi, (a - hi.astype(F32)).astype(BF16)


def _s5_powers(ar, ai):
    width = ar.shape[-1]
    pw = [(jnp.ones_like(ar), jnp.zeros_like(ar))]
    for _ in range(S5_CHUNK):
        pw.append(_cmul(pw[-1][0], pw[-1][1], ar, ai))
    is_fwd = lax.broadcasted_iota(jnp.int32, (1, width), 1) < width // 2

    def pattern(fwd_ascending):
        out = []
        for part in range(2):
            blocks = []
            for j in range(S5_CHUNK):
                ef, eb = (j, S5_CHUNK - 1 - j) if fwd_ascending else (S5_CHUNK - 1 - j, j)
                blocks.append(jnp.broadcast_to(jnp.where(is_fwd, pw[ef][part], pw[eb][part]), (S5_CH, width)))
            out.append(jnp.concatenate(blocks, axis=0))
        return out

    return pattern(True), pattern(False), pw[S5_CHUNK]


S5_GROUPS_PER_STEP = 8


def _s5_kernel(xp_ref, xs_ref, *rest, nb_p, nb_s, n_seg_s):
    yp_ref, ys_ref = rest[9], rest[10]
    for gi in range(S5_GROUPS_PER_STEP):
        ch = pl.ds(gi * S5_CH, S5_CH)
        _s5_group(xp_ref.at[ch], xs_ref.at[ch], *[r.at[gi] for r in rest[:9]], yp_ref.at[ch], ys_ref.at[ch],
                  *[r.at[gi] for r in rest[11:]], nb_p=nb_p, nb_s=nb_s, n_seg_s=n_seg_s)


def _s5_group(xp_ref, xs_ref, h0_ref, are_ref, aim_ref, ldt_ref, bre_ref, bim_ref, cre_ref, cim_ref, dsk_ref,
              yp_ref, ys_ref, st_ref, w_ref, ws_ref, wc_ref, s_ref, f_ref, *, nb_p, nb_s, n_seg_s):
    p = S5_P
    kc = S5_CHUNK * S5_CH
    a_re = are_ref[...]
    a_im = aim_ref[...]
    dt = jnp.exp(ldt_ref[...])
    mag = jnp.exp(a_re * dt)
    ab_re = mag * jnp.cos(a_im * dt)
    ab_im = mag * jnp.sin(a_im * dt)
    den = a_re * a_re + a_im * a_im
    nr = ab_re - 1.0
    f_re = (nr * a_re + ab_im * a_im) / den
    f_im = (ab_im * a_re - nr * a_im) / den
    bbt_re, bbt_im = _cmul(f_re, f_im, bre_ref[...], bim_ref[...])

    (pg_r, pg_i), (ps_r, ps_i), (ac_r, ac_i) = _s5_powers(ab_re, ab_im)
    c_re = jnp.concatenate([cre_ref[...]] * S5_CHUNK, axis=0)
    c_im = jnp.concatenate([cim_ref[...]] * S5_CHUNK, axis=0)
    bt_re = jnp.concatenate([bbt_re] * S5_CHUNK, axis=0)
    bt_im = jnp.concatenate([bbt_im] * S5_CHUNK, axis=0)

    g_re, g_im = _cmul(pg_r, pg_i, c_re, c_im)
    fwd16 = lax.broadcasted_iota(jnp.int32, (S5_CH, 2 * p), 1) < p
    g_hi, g_lo = _split_bf16(jnp.concatenate([g_re, g_im], axis=1))
    r0 = []
    for d in range(2):
        msk = fwd16 if d == 0 else jnp.logical_not(fwd16)
        b_hi, b_lo = _split_bf16(
            jnp.concatenate([jnp.where(msk, bbt_re, 0.0), jnp.where(msk, -bbt_im, 0.0)], axis=1))
        r0.append(_dot_nt(b_hi, g_hi) + (_dot_nt(b_hi, g_lo) + _dot_nt(b_lo, g_hi)))
    lane = lax.broadcasted_iota(jnp.int32, (S5_CH, kc), 1)
    for s in range(S5_CHUNK):
        f = jnp.where(lane >= S5_CH * s, pltpu.roll(r0[0], S5_CH * s, 1), 0.0) if s else r0[0]
        b = jnp.where(lane < S5_CH * (s + 1), pltpu.roll(r0[1], (S5_CH * (s + 1)) % kc, 1), 0.0)
        w_ref[s * S5_CH:(s + 1) * S5_CH, :] = (f + b).astype(BF16)
    e_re, e_im = _cmul(ps_r, ps_i, bt_re, bt_im)
    ws_ref[:, 0:2 * p] = e_re.astype(BF16)
    ws_ref[:, 2 * p:4 * p] = e_im.astype(BF16)
    g1_re, g1_im = _cmul(g_re, g_im, ab_re, ab_im)
    wc_ref[:, 0:2 * p] = g1_re.astype(BF16)
    wc_ref[:, 2 * p:4 * p] = (-g1_im).astype(BF16)

    def run(xt_ref, yt_ref, nb, n_seg, hr, hm):
        rows = xt_ref.shape[1] // S5_CHUNK
        nbx = nb * n_seg
        m = rows // nbx
        xt = jnp.concatenate([xt_ref[:, s * rows:(s + 1) * rows] for s in range(S5_CHUNK)], axis=0)
        xf = xt.astype(F32).T
        x = xf.astype(BF16)
        y = _dot(x, w_ref[...]) + dsk_ref[...] * xf
        s_all = _dot(x, ws_ref[...])
        s_ref[0, 0:rows, :] = s_all[:, 0:2 * p]
        s_ref[1, 0:rows, :] = s_all[:, 2 * p:4 * p]

        def step(ar, ai, hr, hm, plane_r, plane_i, rf, rb):
            isf = lax.broadcasted_iota(jnp.int32, hr.shape, 1) < p
            sr = jnp.where(isf, plane_r[rf, :], plane_r[rb, :])
            sm = jnp.where(isf, plane_i[rf, :], plane_i[rb, :])
            return ar * hr - ai * hm + sr, ar * hm + ai * hr + sm

        def scan(hr, hm, store):
            arb = jnp.broadcast_to(ac_r, (nbx, 2 * p))
            aib = jnp.broadcast_to(ac_i, (nbx, 2 * p))
            for k in range(m):
                rf = pl.ds(k * nbx, nbx)
                rb = pl.ds((m - 1 - k) * nbx, nbx)
                if store:
                    s_ref[2, rf, :] = hr
                    s_ref[3, rf, :] = hm
                    s_ref[4, rb, :] = hr
                    s_ref[5, rb, :] = hm
                hr, hm = step(arb, aib, hr, hm, s_ref.at[0], s_ref.at[1], rf, rb)
            return hr, hm

        if n_seg > 1:
            zeros = jnp.zeros((nbx, 2 * p), F32)
            f_ref[0], f_ref[1] = scan(zeros, zeros, False)
            sr_, si_ = ac_r, ac_i
            for _ in range(m - 1):
                sr_, si_ = _cmul(sr_, si_, ac_r, ac_i)
            sr_ = jnp.broadcast_to(sr_, (nb, 2 * p))
            si_ = jnp.broadcast_to(si_, (nb, 2 * p))
            for j in range(n_seg):
                rf = pl.ds(j, nb, stride=n_seg)
                rb = pl.ds(n_seg - 1 - j, nb, stride=n_seg)
                f_ref[2, rf, :] = hr
                f_ref[3, rf, :] = hm
                f_ref[4, rb, :] = hr
                f_ref[5, rb, :] = hm
                hr, hm = step(sr_, si_, hr, hm, f_ref.at[0], f_ref.at[1], rf, rb)
            isf = lax.broadcasted_iota(jnp.int32, (nbx, 2 * p), 1) < p
            hr = jnp.where(isf, f_ref[2], f_ref[4])
            hm = jnp.where(isf, f_ref[3], f_ref[5])
        hr, hm = scan(hr, hm, True)
        isf_rows = lax.broadcasted_iota(jnp.int32, (rows, 2 * p), 1) < p
        h_prev = jnp.concatenate([jnp.where(isf_rows, s_ref[2, 0:rows, :], s_ref[4, 0:rows, :]),
                                  jnp.where(isf_rows, s_ref[3, 0:rows, :], s_ref[5, 0:rows, :])], axis=1)
        yt = jax.nn.gelu(y + _dot_nt(h_prev.astype(BF16), wc_ref[...])).T
        for t in range(S5_CHUNK):
            yt_ref[:, t * rows:(t + 1) * rows] = yt[t * S5_CH:(t + 1) * S5_CH, :].astype(yt_ref.dtype)
        return hr, hm

    zeros = jnp.zeros((nb_p, 2 * p), F32)
    hr, hm = run(xp_ref, yp_ref, nb_p, 1, zeros, zeros)
    st_ref[:, 0:2 * p] = hr
    st_ref[:, 2 * p:4 * p] = hm
    run(xs_ref, ys_ref, nb_s, n_seg_s, h0_ref[:, 0:2 * p], h0_ref[:, 2 * p:4 * p])


def _s5(utp, uts, h0, a_re, a_im, log_dt, bt_re, bt_im, c_re, c_im, d_skip, nb_p, nb_s, n_seg_s):
    d, lanes_p = utp.shape
    lanes_s = uts.shape[1]
    g = d // S5_CH
    rows_p = lanes_p // S5_CHUNK
    rows_s = lanes_s // S5_CHUNK
    kc = S5_CHUNK * S5_CH
    p = S5_P

    gps = S5_GROUPS_PER_STEP

    def gspec(shape):
        return pl.BlockSpec((gps,) + shape, lambda i: (i, 0, 0))

    def tspec(lanes):
        return pl.BlockSpec((gps * S5_CH, lanes), lambda i: (i, 0))

    return pl.pallas_call(
        functools.partial(_s5_kernel, nb_p=nb_p, nb_s=nb_s, n_seg_s=n_seg_s),
        grid=(g // gps,),
        in_specs=[tspec(lanes_p), tspec(lanes_s), gspec((nb_s, 4 * p)),
                  gspec((1, 2 * p)), gspec((1, 2 * p)), gspec((1, 2 * p)),
                  gspec((S5_CH, 2 * p)), gspec((S5_CH, 2 * p)), gspec((S5_CH, 2 * p)), gspec((S5_CH, 2 * p)),
                  gspec((1, kc))],
        out_specs=[tspec(lanes_p), tspec(lanes_s), gspec((nb_p, 4 * p))],
        out_shape=[jax.ShapeDtypeStruct((d, lanes_p), BF16),
                   jax.ShapeDtypeStruct((d, lanes_s), BF16),
                   jax.ShapeDtypeStruct((g, nb_p, 4 * p), F32)],
        scratch_shapes=[pltpu.VMEM((gps, kc, kc), BF16), pltpu.VMEM((gps, kc, 4 * p), BF16),
                        pltpu.VMEM((gps, kc, 4 * p), BF16),
                        pltpu.VMEM((gps, 6, max(rows_p, rows_s), 2 * p), F32),
                        pltpu.VMEM((gps, 6, nb_s * n_seg_s, 2 * p), F32)],
        compiler_params=_cparams("parallel"),
        name="s5",
    )(utp, uts, h0, a_re, a_im, log_dt, bt_re, bt_im, c_re, c_im,
      jnp.tile(d_skip.reshape(g, 1, S5_CH), (1, 1, S5_CHUNK)))


def _l1_tail_kernel(yt0_ref, ytn_ref, z_ref, wglu_ref, b_ref, x_hbm, gate_ref, fg_ref, wout_ref, o_hbm,
                    yb_ref, a_ref, xbuf, xsem, obuf, osem, *, tn, n_cond):
    d = wglu_ref.shape[0]
    nq = x_hbm.shape[0]
    l = pl.program_id(0)
    slot = l % 2
    _fetch_slots(x_hbm, xbuf, xsem)

    @pl.when(l == 0)
    def _():
        yb_ref[0] = yt0_ref[...].T

    for c in range(d // tn):
        cs = slice(c * tn, (c + 1) * tn)
        yb = yb_ref[slot]
        gl = _dot(yb, wglu_ref[:, cs]) + b_ref[:, cs]
        a = yb[:, cs].astype(F32) * jax.nn.sigmoid(gl) * _silu(z_ref[:, cs].astype(F32))
        a_ref[:, cs] = a.astype(BF16)
        yb_ref[1 - slot, :, cs] = ytn_ref[cs, :].T

    x = xbuf[slot]
    xn = x + _cond_rows(gate_ref, x.shape[0], nq, n_cond) * _dot(a_ref[...], wout_ref[...])
    res = xn * lax.rsqrt(jnp.mean(xn * xn, axis=-1, keepdims=True) + EPS) * fg_ref[...]
    _store_slots(res, o_hbm, obuf, osem)


def _l1_tail(yt, z, x4, w_glu, b_glu, mod3, cond0, n_cond, final_g, w_out, tn):
    nq, m, _, d = x4.shape
    _slots_per_step(x4)
    tm = LANE_TILE
    ni = nq * m * S5_CHUNK // tm
    wspec = pl.BlockSpec((d, d), lambda i: (0, 0), pipeline_mode=pl.Buffered(1))
    vec = pl.BlockSpec((1, d), lambda i: (0, 0))
    return pl.pallas_call(
        functools.partial(_l1_tail_kernel, tn=tn, n_cond=n_cond),
        grid=(ni,),
        in_specs=[pl.BlockSpec((d, tm), lambda i: (0, 0), pipeline_mode=pl.Buffered(1)),
                  pl.BlockSpec((d, tm), lambda i: (0, jnp.minimum(i + 1, ni - 1))),
                  pl.BlockSpec((tm, d), lambda i: (i, 0)),
                  wspec, vec,
                  pl.BlockSpec(memory_space=pl.ANY),
                  pl.BlockSpec((n_cond, 1, d), lambda i: (cond0 // n_cond, 0, 2)),
                  vec, wspec],
        out_specs=pl.BlockSpec(memory_space=pl.ANY),
        out_shape=jax.ShapeDtypeStruct(x4.shape, F32),
        scratch_shapes=[pltpu.VMEM((2, tm, d), BF16), pltpu.VMEM((tm, d), BF16)] + 2 * _slot_scratch(d),
        compiler_params=_cparams("arbitrary"),
        name="l1_tail",
    )(yt, yt, z, w_glu, b_glu.reshape(1, d), x4, mod3, final_g.reshape(1, d), w_out)


def _both_dirs(a):
    _, g, r, p = a.shape
    return a.transpose(1, 2, 0, 3).reshape(g, r, 2 * p)


def kernel(x_prompt, x_sample, cache_l0_k, cache_l0_v, state_l1_s5, c, c_ctx, l0_norm_g, l0_w_ada, l0_b_ada, l0_w_in, l0_rpb, l0_w_out, l1_norm_g, l1_w_ada, l1_b_ada, l1_w_in, l1_a_re, l1_a_im, l1_log_dt, l1_b_re, l1_b_im, l1_c_re, l1_c_im, l1_d, l1_w_glu, l1_b_glu, l1_w_out, final_norm_g):
    bp, seq, d = x_prompt.shape
    bs, n_tok, _ = x_sample.shape
    g = d // S5_CH
    p = S5_P
    xp = x_prompt.reshape(bp * seq, d)
    xs = x_sample.reshape(bs * n_tok, d)

    ctx = bs
    cond8 = jnp.zeros((8, d), F32).at[0:bs].set(c).at[ctx].set(c_ctx)
    mods = _ada(cond8, (l0_w_ada, l1_w_ada), (l0_b_ada, l1_b_ada))
    mod0 = mods[0].reshape(8, 1, 3 * d)
    mod1 = mods[1].reshape(8, 1, 3 * d)

    w_in0 = l0_w_in.astype(BF16)
    w_out0 = l0_w_out.astype(BF16)
    w_in1 = l1_w_in.astype(BF16)
    w_glu = l1_w_glu.astype(BF16)
    w_out1 = l1_w_out.astype(BF16)

    qs, ks, vs, zs = _inproj(xs, l0_norm_g, mod0, 0, n_tok, w_in0,
                             ((0, BF16), (1, BF16), (2, BF16), (3, BF16)), 1024, 256)
    qp, kpb, vpb, zp, kp, vp = _inproj(xp, l0_norm_g, mod0, ctx, bp * seq, w_in0,
                                       ((0, BF16), (1, BF16), (2, BF16), (3, BF16)), 1024, 256, head_split=(1, 2))
    rpb_pad = jnp.zeros((N_HEADS, 16, LANES), F32).at[:, :RPB_R, :RPB_C].set(l0_rpb)
    os_ = _latent_na(qs, ks, vs, cache_l0_k, cache_l0_v, rpb_pad, n_tok)
    n_seg_s = max(1, n_tok // S5_CHUNK // S5_SEG)
    x1p = _ctx_layer(qp, kpb, vpb, zp, xp, mod0, ctx, w_out0, seq, 512).reshape(bp, seq // S5_CHUNK, S5_CHUNK, d)
    x1s = _gated_out(os_, zs, xs, mod0, 0, n_tok, w_out0, 512).reshape(bs * n_seg_s, -1, S5_CHUNK, d)

    utp, zp1 = _inproj_t(x1p, l1_norm_g, mod1, ctx, 1, w_in1)
    uts, zs1 = _inproj_t(x1s, l1_norm_g, mod1, 0, bs, w_in1)
    h0 = state_l1_s5.transpose(3, 0, 2, 1, 4).reshape(g, bs, 4 * p)
    log_dt = jnp.broadcast_to(l1_log_dt[:, :, None, None], (2, g, 1, p))
    ytp, yts, st = _s5(
        utp, uts, h0,
        _both_dirs(l1_a_re[:, :, None, :]), _both_dirs(l1_a_im[:, :, None, :]), _both_dirs(log_dt),
        _both_dirs(l1_b_re.transpose(0, 1, 3, 2)), _both_dirs(l1_b_im.transpose(0, 1, 3, 2)),
        _both_dirs(l1_c_re), _both_dirs(l1_c_im), l1_d, bp, bs, n_seg_s)
    y_prompt = _l1_tail(ytp, zp1, x1p, w_glu, l1_b_glu, mod1, ctx, 1, final_norm_g, w_out1, 512)
    y_sample = _l1_tail(yts, zs1, x1s, w_glu, l1_b_glu, mod1, 0, bs, final_norm_g, w_out1, 512)
    new_state = st.reshape(g, bp, 2, 2, p).transpose(1, 3, 2, 0, 4)
    return (y_prompt.reshape(bp, seq, d), y_sample.reshape(bs, n_tok, d),
            kp.reshape(bp, seq, N_HEADS, HEAD_DIM), vp.reshape(bp, seq, N_HEADS, HEAD_DIM), new_state)
```

```python
import functools

import jax
import jax.numpy as jnp
from jax import lax
from jax.experimental import pallas as pl
from jax.experimental.pallas import tpu as pltpu

F32 = jnp.float32
BF16 = jnp.bfloat16

EPS = 1e-6
N_HEADS = 16
HEAD_DIM = 128
GRID_W = 64
NA_ROWS = 8
NA_COLS = 16
RPB_R = 2 * NA_ROWS - 1
RPB_C = 2 * NA_COLS - 1
S5_CH = 16
S5_P = 64
S5_CHUNK = 16
S5_SEG = 8
NEG = -1e30
LANES = 128
VMEM_LIMIT = 56 * 1024 * 1024


def _cparams(*sem):
    return pltpu.CompilerParams(dimension_semantics=sem, vmem_limit_bytes=VMEM_LIMIT)


def _dot(a, b):
    return jnp.dot(a, b, preferred_element_type=F32)


def _dot_nt(a, b, precision=None):
    return lax.dot_general(a, b, (((1,), (1,)), ((), ())), preferred_element_type=F32,
                           precision=precision)


def _sigmoid(x):
    return 0.5 * jnp.tanh(0.5 * x) + 0.5


def _silu(x):
    return x * _sigmoid(x)


def _gelu_tanh(x):
    c = 0.7978845608028654
    return (0.5 * x) * (1.0 + jnp.tanh(x * (c + (c * 0.044715) * (x * x))))


def _ada_kernel(cond_ref, *refs, n_layers):
    w_refs, b_refs, o_ref = refs[:n_layers], refs[n_layers:2 * n_layers], refs[2 * n_layers]
    s = _silu(cond_ref[...]).astype(BF16)
    layer = pl.program_id(0)
    for i in range(n_layers):
        @pl.when(layer == i)
        def _(i=i):
            o_ref[...] = _dot(s, w_refs[i][...].astype(BF16)) + b_refs[i][...]


def _ada(cond8, w_adas, b_adas):
    n_layers = len(w_adas)
    d, n = w_adas[0].shape
    tn = 1024
    nj = n // tn

    def own(i):
        return lambda l, j: (0, jnp.where(l == i, j, jnp.where(l < i, 0, nj - 1)))

    return pl.pallas_call(
        functools.partial(_ada_kernel, n_layers=n_layers),
        grid=(n_layers, nj),
        in_specs=[pl.BlockSpec((8, d), lambda l, j: (0, 0))]
        + [pl.BlockSpec((d, tn), own(i)) for i in range(n_layers)]
        + [pl.BlockSpec((1, tn), own(i)) for i in range(n_layers)],
        out_specs=pl.BlockSpec((None, 8, tn), lambda l, j: (l, 0, j)),
        out_shape=jax.ShapeDtypeStruct((n_layers, 8, n), F32),
        compiler_params=_cparams("arbitrary", "arbitrary"),
        name="ada",
    )(cond8, *w_adas, *[b.reshape(1, n) for b in b_adas])


def _norm_mod(x, g, shift, scale):
    y = x * lax.rsqrt(jnp.mean(x * x, axis=-1, keepdims=True) + EPS) * g
    return (y * (1.0 + scale) + shift).astype(BF16)


def _inproj_kernel(x0_ref, xn_ref, g_ref, shift0_ref, scale0_ref, shiftn_ref, scalen_ref, *rest, out_split,
                   head_split):
    n_split = max(out_split + head_split) + 1
    w_refs = rest[:n_split]
    rest = rest[n_split:]
    o_refs = rest[:len(out_split)]
    hd_refs = rest[len(out_split):len(out_split) + len(head_split)]
    h_ref = rest[len(out_split) + len(head_split)]
    i = pl.program_id(0)
    j = pl.program_id(1)
    nj = pl.num_programs(1)
    slot = i % 2

    @pl.when((i == 0) & (j == 0))
    def _():
        h_ref[0] = _norm_mod(x0_ref[...], g_ref[...], shift0_ref[...], scale0_ref[...])

    if head_split:
        stage_ref, sem = rest[len(out_split) + len(head_split) + 1:]
        tm, tn = stage_ref.shape[2:]
        t = i * nj + j

        def head_copies(tt):
            ii, jj = tt // nj, tt % nj
            return [pltpu.make_async_copy(stage_ref.at[tt % 2, n, :, pl.ds(hh * HEAD_DIM, HEAD_DIM)],
                                          hd.at[pl.ds(ii * tm, tm), jj * (tn // HEAD_DIM) + hh, :], sem.at[tt % 2])
                    for n, hd in enumerate(hd_refs) for hh in range(tn // HEAD_DIM)]

        @pl.when(t >= 2)
        def _():
            for c in head_copies(t - 2):
                c.wait()

    h = h_ref[slot]
    for s, w_ref in enumerate(w_refs):
        r = _dot(h, w_ref[...])
        for o_ref, o_s in zip(o_refs, out_split):
            if o_s == s:
                o_ref[...] = r.astype(o_ref.dtype)
        if s in head_split:
            stage_ref[t % 2, head_split.index(s)] = r
    rs = xn_ref.shape[0]
    h_ref[1 - slot, pl.ds(pl.multiple_of(j * rs, rs), rs), :] = _norm_mod(
        xn_ref[...], g_ref[...], shiftn_ref[...], scalen_ref[...])

    if head_split:
        for c in head_copies(t):
            c.start()

        @pl.when(t == pl.num_programs(0) * nj - 1)
        def _():
            @pl.when(t >= 1)
            def _():
                for c in head_copies(t - 1):
                    c.wait()

            for c in head_copies(t):
                c.wait()


def _inproj(x, g, mod3, cond0, rows_per_cond, w, outs, tm, tn, head_split=()):
    m, d = x.shape
    out_split = tuple(s for s, _ in outs)
    out_dtypes = [dt for _, dt in outs]
    head_split = tuple(head_split)
    n_split = max(out_split + head_split) + 1
    ni = m // tm
    nj = d // tn

    def nxt(i):
        return jnp.minimum(i + 1, ni - 1)

    def cond_of(i):
        return cond0 + (i * tm) // rows_per_cond

    in_specs = [pl.BlockSpec((tm, d), lambda i, j: (0, 0), pipeline_mode=pl.Buffered(1)),
                pl.BlockSpec((tm // nj, d), lambda i, j: (nxt(i) * nj + j, 0)),
                pl.BlockSpec((1, d), lambda i, j: (0, 0)),
                pl.BlockSpec((None, 1, d), lambda i, j: (cond_of(0), 0, 0)),
                pl.BlockSpec((None, 1, d), lambda i, j: (cond_of(0), 0, 1)),
                pl.BlockSpec((None, 1, d), lambda i, j: (cond_of(nxt(i)), 0, 0)),
                pl.BlockSpec((None, 1, d), lambda i, j: (cond_of(nxt(i)), 0, 1))]
    for s in range(n_split):
        in_specs.append(pl.BlockSpec((d, tn), lambda i, j, s=s: (0, s * nj + j)))
    scratch = [pltpu.VMEM((2, tm, d), BF16)]
    if head_split:
        scratch += [pltpu.VMEM((2, len(head_split), tm, tn), F32), pltpu.SemaphoreType.DMA((2,))]
    return pl.pallas_call(
        functools.partial(_inproj_kernel, out_split=out_split, head_split=head_split),
        grid=(ni, nj),
        in_specs=in_specs,
        out_specs=[pl.BlockSpec((tm, tn), lambda i, j: (i, j)) for _ in outs]
        + [pl.BlockSpec(memory_space=pl.ANY)] * len(head_split),
        out_shape=[jax.ShapeDtypeStruct((m, d), dt) for dt in out_dtypes]
        + [jax.ShapeDtypeStruct((m, d // HEAD_DIM, HEAD_DIM), F32)] * len(head_split),
        scratch_shapes=scratch,
        compiler_params=_cparams("arbitrary", "arbitrary"),
        name="inproj",
    )(x, x, g.reshape(1, d), mod3, mod3, mod3, mod3, *([w] * n_split))


def _softmax_pv(parts):
    m = parts[0][0].max(axis=-1, keepdims=True)
    for s, _ in parts[1:]:
        m = jnp.maximum(m, s.max(axis=-1, keepdims=True))
    l = None
    o = None
    for s, v in parts:
        e = jnp.exp(s - m)
        li = e.sum(axis=-1, keepdims=True)
        oi = _dot(e.astype(BF16), v)
        l = li if l is None else l + li
        o = oi if o is None else o + oi
    return o / l


def _ctx_layer_kernel(q_ref, k_ref, v_ref, z_ref, x_ref, gate_ref, w_ref, y_ref, a_ref, *, seq):
    scale = HEAD_DIM ** -0.5
    for b in range(q_ref.shape[0] // seq):
        rows = slice(b * seq, (b + 1) * seq)
        for h in range(N_HEADS):
            sl = slice(h * HEAD_DIM, (h + 1) * HEAD_DIM)
            s = _dot_nt(q_ref[rows, sl], k_ref[rows, sl]) * scale
            o = _softmax_pv([(s, v_ref[rows, sl])])
            a_ref[rows, sl] = (o * _silu(z_ref[rows, sl].astype(F32))).astype(BF16)
    y_ref[...] = x_ref[...] + gate_ref[...] * _dot(a_ref[...], w_ref[...])


def _ctx_layer(q, k, v, z, x, mod3, cond, w_out, seq, tm):
    m, d = x.shape
    row = pl.BlockSpec((tm, d), lambda i: (i, 0))
    return pl.pallas_call(
        functools.partial(_ctx_layer_kernel, seq=seq),
        grid=(m // tm,),
        in_specs=[row, row, row, row, row,
                  pl.BlockSpec((None, 1, d), lambda i: (cond, 0, 2)),
                  pl.BlockSpec((d, d), lambda i: (0, 0), pipeline_mode=pl.Buffered(1))],
        out_specs=row,
        out_shape=jax.ShapeDtypeStruct((m, d), F32),
        scratch_shapes=[pltpu.VMEM((tm, d), BF16)],
        compiler_params=_cparams("parallel"),
        name="ctx_layer",
    )(q, k, v, z, x, mod3, w_out)


def _na_window_start(r, rows):
    return min(max(r - NA_ROWS // 2, 0), rows - NA_ROWS)


def _na_kernel(q_ref, k_ref, v_ref, kc_hbm, vc_hbm, rpb_ref, o_ref, cbuf, csem, *, rows, q_rows):
    scale = HEAD_DIM ** -0.5
    w = GRID_W
    n_heads = pl.num_programs(1)
    t = pl.program_id(0) * n_heads + pl.program_id(1)

    def ctx_copies(tt):
        return [pltpu.make_async_copy(c.at[tt // n_heads, :, tt % n_heads, :], cbuf.at[tt % 2, n], csem.at[tt % 2])
                for n, c in enumerate((kc_hbm, vc_hbm))]

    @pl.when(t == 0)
    def _():
        for c in ctx_copies(t):
            c.start()

    @pl.when(t + 1 < pl.num_programs(0) * n_heads)
    def _():
        for c in ctx_copies(t + 1):
            c.start()
    qc = lax.broadcasted_iota(jnp.int32, (w, LANES), 0)
    lane = lax.broadcasted_iota(jnp.int32, (w, LANES), 1)
    c0 = jnp.clip(qc - NA_COLS // 2, 0, w - NA_COLS)
    ok_l = (lane >= c0) & (lane < c0 + NA_COLS)
    ok_r = (lane - w >= c0) & (lane - w < c0 + NA_COLS)
    neg_tile = jnp.full((w, LANES), NEG, F32)
    tile_l, tile_r = [], []
    for d in range(RPB_R):
        base = jnp.broadcast_to(rpb_ref[d:d + 1, :], (w, LANES))
        left = pltpu.roll(base, LANES - (NA_COLS - 1), 1, stride=1, stride_axis=0)
        right = pltpu.roll(base, w - (NA_COLS - 1), 1, stride=1, stride_axis=0)
        tile_l.append(jnp.where(ok_l, left, NEG))
        tile_r.append(jnp.where(ok_r, right, NEG))

    for c in ctx_copies(t):
        c.wait()
    kc = cbuf[t % 2, 0].astype(BF16)
    vc = cbuf[t % 2, 1].astype(BF16)
    n_groups = rows // q_rows
    for gi in range(n_groups):
        rs = list(range(gi * q_rows, (gi + 1) * q_rows))
        klo = min(_na_window_start(r, rows) for r in rs) // 2 * 2
        khi = -(-(max(_na_window_start(r, rows) for r in rs) + NA_ROWS) // 2) * 2
        bias_rows = []
        for r in rs:
            r0 = _na_window_start(r, rows)
            tiles = []
            for kr in range(klo, khi, 2):
                ok0 = r0 <= kr < r0 + NA_ROWS
                ok1 = r0 <= kr + 1 < r0 + NA_ROWS
                t0 = tile_l[kr - r + NA_ROWS - 1] if ok0 else neg_tile
                t1 = tile_r[kr + 1 - r + NA_ROWS - 1] if ok1 else neg_tile
                tiles.append(jnp.maximum(t0, t1) if (ok0 or ok1) else neg_tile)
            bias_rows.append(jnp.concatenate(tiles, axis=1))
        bias = jnp.concatenate(bias_rows, axis=0)
        q = q_ref[gi * q_rows * w:(gi + 1) * q_rows * w, :]
        kl = k_ref[klo * w:khi * w, :]
        vl = v_ref[klo * w:khi * w, :]
        s_loc = _dot_nt(q, kl) * scale + bias
        s_ctx = _dot_nt(q, kc) * scale
        o = _softmax_pv([(s_loc, vl), (s_ctx, vc)])
        o_ref[gi * q_rows * w:(gi + 1) * q_rows * w, :] = o.astype(o_ref.dtype)


def _latent_na(q, k, v, k_ctx, v_ctx, rpb_pad, n_tok):
    m, d = q.shape
    nb = m // n_tok
    lc = k_ctx.shape[1]
    rows = n_tok // GRID_W
    spec = pl.BlockSpec((n_tok, HEAD_DIM), lambda b, h: (b, h))
    cspec = pl.BlockSpec(memory_space=pl.ANY)
    return pl.pallas_call(
        functools.partial(_na_kernel, rows=rows, q_rows=4),
        grid=(nb, N_HEADS),
        in_specs=[spec, spec, spec, cspec, cspec,
                  pl.BlockSpec((None, 16, LANES), lambda b, h: (h, 0, 0))],
        out_specs=spec,
        out_shape=jax.ShapeDtypeStruct((m, d), BF16),
        scratch_shapes=[pltpu.VMEM((2, 2, lc, HEAD_DIM), F32), pltpu.SemaphoreType.DMA((2,))],
        compiler_params=_cparams("arbitrary", "arbitrary"),
        name="latent_na",
    )(q, k, v, k_ctx, v_ctx, rpb_pad)


def _gated_out_kernel(o_ref, z_ref, x_ref, gate_ref, w_ref, y_ref):
    a = o_ref[...].astype(F32) * _silu(z_ref[...].astype(F32))
    y_ref[...] = x_ref[...] + gate_ref[...] * _dot(a.astype(BF16), w_ref[...])


def _gated_out(o, z, x, mod3, cond0, rows_per_cond, w_out, tm):
    m, d = x.shape
    row = pl.BlockSpec((tm, d), lambda i: (i, 0))
    return pl.pallas_call(
        _gated_out_kernel,
        grid=(m // tm,),
        in_specs=[row, row, row,
                  pl.BlockSpec((None, 1, d), lambda i: (cond0 + (i * tm) // rows_per_cond, 0, 2)),
                  pl.BlockSpec((d, d), lambda i: (0, 0), pipeline_mode=pl.Buffered(1))],
        out_specs=row,
        out_shape=jax.ShapeDtypeStruct((m, d), F32),
        compiler_params=_cparams("parallel"),
        name="gated_out",
    )(o, z, x, mod3, w_out)


LANE_TILE = 512


def _slots_per_step(x4):
    nq, m, _, _ = x4.shape
    assert LANE_TILE % (nq * m) == 0 and S5_CHUNK % (LANE_TILE // (nq * m)) == 0
    return LANE_TILE // (nq * m)


def _slot_scratch(d):
    return [pltpu.VMEM((2, LANE_TILE, d), F32), pltpu.SemaphoreType.DMA((2,))]


def _slot_copies(x_hbm, buf, sem, l, to_hbm):
    nq, m, _, _ = x_hbm.shape
    n_x = buf.shape[1] // (nq * m)
    copies = []
    for xi in range(n_x):
        for k in range(m):
            hbm = x_hbm.at[:, k, l * n_x + xi, :]
            vmem = buf.at[l % 2, pl.ds((xi * m + k) * nq, nq), :]
            src, dst = (vmem, hbm) if to_hbm else (hbm, vmem)
            copies.append(pltpu.make_async_copy(src, dst, sem.at[l % 2]))
    return copies


def _fetch_slots(x_hbm, buf, sem):
    l = pl.program_id(0)

    @pl.when(l == 0)
    def _():
        for c in _slot_copies(x_hbm, buf, sem, l, False):
            c.start()

    @pl.when(l + 1 < pl.num_programs(0))
    def _():
        for c in _slot_copies(x_hbm, buf, sem, l + 1, False):
            c.start()

    for c in _slot_copies(x_hbm, buf, sem, l, False):
        c.wait()
    return buf[l % 2]


def _store_slots(res, o_hbm, buf, sem):
    l = pl.program_id(0)

    def wait(ll):
        for c in _slot_copies(o_hbm, buf, sem, ll, True):
            c.wait()

    @pl.when(l >= 2)
    def _():
        wait(l - 2)

    buf[l % 2] = res
    for c in _slot_copies(o_hbm, buf, sem, l, True):
        c.start()

    @pl.when(l == pl.num_programs(0) - 1)
    def _():
        @pl.when(l >= 1)
        def _():
            wait(l - 1)

        wait(l)


def _cond_rows(ref, n_rows, nq, n_cond):
    out = ref[n_cond - 1]
    if n_cond > 1:
        b = (lax.broadcasted_iota(jnp.int32, (n_rows, 1), 0) % nq) // (nq // n_cond)
        for i in range(n_cond - 2, -1, -1):
            out = jnp.where(b == i, ref[i], out)
    return out


def _inproj_t_kernel(x_hbm, g_ref, shift_ref, scale_ref, wu_ref, wz_ref, u_ref, z_ref, buf, sem, *, n_cond):
    nq = x_hbm.shape[0]
    x = _fetch_slots(x_hbm, buf, sem)
    n_rows = x.shape[0]
    h = _norm_mod(x, g_ref[...], _cond_rows(shift_ref, n_rows, nq, n_cond),
                  _cond_rows(scale_ref, n_rows, nq, n_cond))
    u_ref[...] = _dot(h, wu_ref[...]).T.astype(u_ref.dtype)
    z_ref[...] = _dot(h, wz_ref[...]).astype(z_ref.dtype)


def _inproj_t(x4, g, mod3, cond0, n_cond, w_in):
    nq, m, _, d = x4.shape
    n_tok = nq * m * S5_CHUNK
    _slots_per_step(x4)

    def mspec(col):
        return pl.BlockSpec((n_cond, 1, d), lambda l: (cond0 // n_cond, 0, col))

    def wspec(col):
        return pl.BlockSpec((d, d), lambda l: (0, col), pipeline_mode=pl.Buffered(1))

    return pl.pallas_call(
        functools.partial(_inproj_t_kernel, n_cond=n_cond),
        grid=(n_tok // LANE_TILE,),
        in_specs=[pl.BlockSpec(memory_space=pl.ANY),
                  pl.BlockSpec((1, d), lambda l: (0, 0)), mspec(0), mspec(1), wspec(0), wspec(1)],
        out_specs=[pl.BlockSpec((d, LANE_TILE), lambda l: (0, l)), pl.BlockSpec((LANE_TILE, d), lambda l: (l, 0))],
        out_shape=[jax.ShapeDtypeStruct((d, n_tok), BF16), jax.ShapeDtypeStruct((n_tok, d), BF16)],
        scratch_shapes=_slot_scratch(d),
        compiler_params=_cparams("arbitrary"),
        name="inproj_t",
    )(x4, g.reshape(1, d), mod3, mod3, w_in, w_in)


def _cmul(ar, ai, br, bi):
    return ar * br - ai * bi, ar * bi + ai * br


def _split_bf16(a):
    hi = a.astype(BF16)
    return hi, (a - hi.astype(F32)).astype(BF16)


def _s5_powers(ar, ai):
    width = ar.shape[-1]
    pw = [(jnp.ones_like(ar), jnp.zeros_like(ar))]
    for _ in range(S5_CHUNK):
        pw.append(_cmul(pw[-1][0], pw[-1][1], ar, ai))
    is_fwd = lax.broadcasted_iota(jnp.int32, (1, width), 1) < width // 2

    def pattern(fwd_ascending):
        out = []
        for part in range(2):
            blocks = []
            for j in range(S5_CHUNK):
                ef, eb = (j, S5_CHUNK - 1 - j) if fwd_ascending else (S5_CHUNK - 1 - j, j)
                blocks.append(jnp.broadcast_to(jnp.where(is_fwd, pw[ef][part], pw[eb][part]), (S5_CH, width)))
            out.append(jnp.concatenate(blocks, axis=0))
        return out

    return pattern(True), pattern(False), pw[S5_CHUNK]


S5_GROUPS_PER_STEP = 8


def _s5_kernel(xp_ref, xs_ref, *rest, nb_p, nb_s, n_seg_s):
    yp_ref, ys_ref = rest[9], rest[10]
    for gi in range(S5_GROUPS_PER_STEP):
        ch = pl.ds(gi * S5_CH, S5_CH)
        _s5_group(xp_ref.at[ch], xs_ref.at[ch], *[r.at[gi] for r in rest[:9]], yp_ref.at[ch], ys_ref.at[ch],
                  *[r.at[gi] for r in rest[11:]], nb_p=nb_p, nb_s=nb_s, n_seg_s=n_seg_s)


def _s5_group(xp_ref, xs_ref, h0_ref, are_ref, aim_ref, ldt_ref, bre_ref, bim_ref, cre_ref, cim_ref, dsk_ref,
              yp_ref, ys_ref, st_ref, w_ref, ws_ref, wc_ref, s_ref, f_ref, *, nb_p, nb_s, n_seg_s):
    p = S5_P
    kc = S5_CHUNK * S5_CH
    a_re = are_ref[...]
    a_im = aim_ref[...]
    dt = jnp.exp(ldt_ref[...])
    mag = jnp.exp(a_re * dt)
    ab_re = mag * jnp.cos(a_im * dt)
    ab_im = mag * jnp.sin(a_im * dt)
    den = a_re * a_re + a_im * a_im
    nr = ab_re - 1.0
    f_re = (nr * a_re + ab_im * a_im) / den
    f_im = (ab_im * a_re - nr * a_im) / den
    bbt_re, bbt_im = _cmul(f_re, f_im, bre_ref[...], bim_ref[...])

    (pg_r, pg_i), (ps_r, ps_i), (ac_r, ac_i) = _s5_powers(ab_re, ab_im)
    c_re = jnp.concatenate([cre_ref[...]] * S5_CHUNK, axis=0)
    c_im = jnp.concatenate([cim_ref[...]] * S5_CHUNK, axis=0)
    bt_re = jnp.concatenate([bbt_re] * S5_CHUNK, axis=0)
    bt_im = jnp.concatenate([bbt_im] * S5_CHUNK, axis=0)

    g_re, g_im = _cmul(pg_r, pg_i, c_re, c_im)
    fwd16 = lax.broadcasted_iota(jnp.int32, (S5_CH, 2 * p), 1) < p
    g_hi, g_lo = _split_bf16(jnp.concatenate([g_re, g_im], axis=1))
    r0 = []
    for d in range(2):
        msk = fwd16 if d == 0 else jnp.logical_not(fwd16)
        b_hi, b_lo = _split_bf16(
            jnp.concatenate([jnp.where(msk, bbt_re, 0.0), jnp.where(msk, -bbt_im, 0.0)], axis=1))
        r0.append(_dot_nt(b_hi, g_hi) + (_dot_nt(b_hi, g_lo) + _dot_nt(b_lo, g_hi)))
    lane = lax.broadcasted_iota(jnp.int32, (S5_CH, kc), 1)
    for s in range(S5_CHUNK):
        f = jnp.where(lane >= S5_CH * s, pltpu.roll(r0[0], S5_CH * s, 1), 0.0) if s else r0[0]
        b = jnp.where(lane < S5_CH * (s + 1), pltpu.roll(r0[1], (S5_CH * (s + 1)) % kc, 1), 0.0)
        w_ref[s * S5_CH:(s + 1) * S5_CH, :] = (f + b).astype(BF16)
    e_re, e_im = _cmul(ps_r, ps_i, bt_re, bt_im)
    ws_ref[:, 0:2 * p] = e_re.astype(BF16)
    ws_ref[:, 2 * p:4 * p] = e_im.astype(BF16)
    g1_re, g1_im = _cmul(g_re, g_im, ab_re, ab_im)
    wc_ref[:, 0:2 * p] = g1_re.astype(BF16)
    wc_ref[:, 2 * p:4 * p] = (-g1_im).astype(BF16)

    def run(xt_ref, yt_ref, nb, n_seg, hr, hm):
        rows = xt_ref.shape[1] // S5_CHUNK
        nbx = nb * n_seg
        m = rows // nbx
        xt = jnp.concatenate([xt_ref[:, s * rows:(s + 1) * rows] for s in range(S5_CHUNK)], axis=0)
        x = xt.T
        y = _dot(x, w_ref[...]) + dsk_ref[...] * x.astype(F32)
        s_all = _dot(x, ws_ref[...])
        s_ref[0, 0:rows, :] = s_all[:, 0:2 * p]
        s_ref[1, 0:rows, :] = s_all[:, 2 * p:4 * p]

        def step(ar, ai, hr, hm, plane_r, plane_i, rf, rb):
            isf = lax.broadcasted_iota(jnp.int32, hr.shape, 1) < p
            sr = jnp.where(isf, plane_r[rf, :], plane_r[rb, :])
            sm = jnp.where(isf, plane_i[rf, :], plane_i[rb, :])
            return ar * hr - ai * hm + sr, ar * hm + ai * hr + sm

        def scan(hr, hm, store):
            arb = jnp.broadcast_to(ac_r, (nbx, 2 * p))
            aib = jnp.broadcast_to(ac_i, (nbx, 2 * p))
            for k in range(m):
                rf = pl.ds(k * nbx, nbx)
                rb = pl.ds((m - 1 - k) * nbx, nbx)
                if store:
                    s_ref[2, rf, :] = hr
                    s_ref[3, rf, :] = hm
                    s_ref[4, rb, :] = hr
                    s_ref[5, rb, :] = hm
                hr, hm = step(arb, aib, hr, hm, s_ref.at[0], s_ref.at[1], rf, rb)
            return hr, hm

        if n_seg > 1:
            zeros = jnp.zeros((nbx, 2 * p), F32)
            f_ref[0], f_ref[1] = scan(zeros, zeros, False)
            sr_, si_ = ac_r, ac_i
            for _ in range(m - 1):
                sr_, si_ = _cmul(sr_, si_, ac_r, ac_i)
            sr_ = jnp.broadcast_to(sr_, (nb, 2 * p))
            si_ = jnp.broadcast_to(si_, (nb, 2 * p))
            for j in range(n_seg):
                rf = pl.ds(j, nb, stride=n_seg)
                rb = pl.ds(n_seg - 1 - j, nb, stride=n_seg)
                f_ref[2, rf, :] = hr
                f_ref[3, rf, :] = hm
                f_ref[4, rb, :] = hr
                f_ref[5, rb, :] = hm
                hr, hm = step(sr_, si_, hr, hm, f_ref.at[0], f_ref.at[1], rf, rb)
            isf = lax.broadcasted_iota(jnp.int32, (nbx, 2 * p), 1) < p
            hr = jnp.where(isf, f_ref[2], f_ref[4])
            hm = jnp.where(isf, f_ref[3], f_ref[5])
        hr, hm = scan(hr, hm, True)
        isf_rows = lax.broadcasted_iota(jnp.int32, (rows, 2 * p), 1) < p
        h_prev = jnp.concatenate([jnp.where(isf_rows, s_ref[2, 0:rows, :], s_ref[4, 0:rows, :]),
                                  jnp.where(isf_rows, s_ref[3, 0:rows, :], s_ref[5, 0:rows, :])], axis=1)
        yt = _gelu_tanh(y + _dot_nt(h_prev.astype(BF16), wc_ref[...])).astype(yt_ref.dtype).T
        for t in range(S5_CHUNK):
            yt_ref[:, t * rows:(t + 1) * rows] = yt[t * S5_CH:(t + 1) * S5_CH, :]
        return hr, hm

    zeros = jnp.zeros((nb_p, 2 * p), F32)
    hr, hm = run(xp_ref, yp_ref, nb_p, 1, zeros, zeros)
    st_ref[:, 0:2 * p] = hr
    st_ref[:, 2 * p:4 * p] = hm
    run(xs_ref, ys_ref, nb_s, n_seg_s, h0_ref[:, 0:2 * p], h0_ref[:, 2 * p:4 * p])


def _s5(utp, uts, h0, a_re, a_im, log_dt, bt_re, bt_im, c_re, c_im, d_skip, nb_p, nb_s, n_seg_s):
    d, lanes_p = utp.shape
    lanes_s = uts.shape[1]
    g = d // S5_CH
    rows_p = lanes_p // S5_CHUNK
    rows_s = lanes_s // S5_CHUNK
    kc = S5_CHUNK * S5_CH
    p = S5_P

    gps = S5_GROUPS_PER_STEP

    def gspec(shape):
        return pl.BlockSpec((gps,) + shape, lambda i: (i, 0, 0))

    def tspec(lanes):
        return pl.BlockSpec((gps * S5_CH, lanes), lambda i: (i, 0))

    return pl.pallas_call(
        functools.partial(_s5_kernel, nb_p=nb_p, nb_s=nb_s, n_seg_s=n_seg_s),
        grid=(g // gps,),
        in_specs=[tspec(lanes_p), tspec(lanes_s), gspec((nb_s, 4 * p)),
                  gspec((1, 2 * p)), gspec((1, 2 * p)), gspec((1, 2 * p)),
                  gspec((S5_CH, 2 * p)), gspec((S5_CH, 2 * p)), gspec((S5_CH, 2 * p)), gspec((S5_CH, 2 * p)),
                  gspec((1, kc))],
        out_specs=[tspec(lanes_p), tspec(lanes_s), gspec((nb_p, 4 * p))],
        out_shape=[jax.ShapeDtypeStruct((d, lanes_p), BF16),
                   jax.ShapeDtypeStruct((d, lanes_s), BF16),
                   jax.ShapeDtypeStruct((g, nb_p, 4 * p), F32)],
        scratch_shapes=[pltpu.VMEM((gps, kc, kc), BF16), pltpu.VMEM((gps, kc, 4 * p), BF16),
                        pltpu.VMEM((gps, kc, 4 * p), BF16),
                        pltpu.VMEM((gps, 6, max(rows_p, rows_s), 2 * p), F32),
                        pltpu.VMEM((gps, 6, nb_s * n_seg_s, 2 * p), F32)],
        compiler_params=_cparams("parallel"),
        name="s5",
    )(utp, uts, h0, a_re, a_im, log_dt, bt_re, bt_im, c_re, c_im,
      jnp.tile(d_skip.reshape(g, 1, S5_CH), (1, 1, S5_CHUNK)))


def _l1_tail_kernel(yt0_ref, ytn_ref, z_ref, wglu_ref, b_ref, x_hbm, gate_ref, fg_ref, wout_ref, o_hbm,
                    yb_ref, a_ref, xbuf, xsem, obuf, osem, *, tn, n_cond):
    d = wglu_ref.shape[0]
    nq = x_hbm.shape[0]
    l = pl.program_id(0)
    slot = l % 2
    _fetch_slots(x_hbm, xbuf, xsem)

    @pl.when(l == 0)
    def _():
        yb_ref[0] = yt0_ref[...].T

    for c in range(d // tn):
        cs = slice(c * tn, (c + 1) * tn)
        yb = yb_ref[slot]
        gl = _dot(yb, wglu_ref[:, cs]) + b_ref[:, cs]
        a = yb[:, cs].astype(F32) * _sigmoid(gl) * _silu(z_ref[:, cs].astype(F32))
        a_ref[:, cs] = a.astype(BF16)
        yb_ref[1 - slot, :, cs] = ytn_ref[cs, :].T

    x = xbuf[slot]
    xn = x + _cond_rows(gate_ref, x.shape[0], nq, n_cond) * _dot(a_ref[...], wout_ref[...])
    res = xn * lax.rsqrt(jnp.mean(xn * xn, axis=-1, keepdims=True) + EPS) * fg_ref[...]
    _store_slots(res, o_hbm, obuf, osem)


def _l1_tail(yt, z, x4, w_glu, b_glu, mod3, cond0, n_cond, final_g, w_out, tn):
    nq, m, _, d = x4.shape
    _slots_per_step(x4)
    tm = LANE_TILE
    ni = nq * m * S5_CHUNK // tm
    wspec = pl.BlockSpec((d, d), lambda i: (0, 0), pipeline_mode=pl.Buffered(1))
    vec = pl.BlockSpec((1, d), lambda i: (0, 0))
    return pl.pallas_call(
        functools.partial(_l1_tail_kernel, tn=tn, n_cond=n_cond),
        grid=(ni,),
        in_specs=[pl.BlockSpec((d, tm), lambda i: (0, 0), pipeline_mode=pl.Buffered(1)),
                  pl.BlockSpec((d, tm), lambda i: (0, jnp.minimum(i + 1, ni - 1))),
                  pl.BlockSpec((tm, d), lambda i: (i, 0)),
                  wspec, vec,
                  pl.BlockSpec(memory_space=pl.ANY),
                  pl.BlockSpec((n_cond, 1, d), lambda i: (cond0 // n_cond, 0, 2)),
                  vec, wspec],
        out_specs=pl.BlockSpec(memory_space=pl.ANY),
        out_shape=jax.ShapeDtypeStruct(x4.shape, F32),
        scratch_shapes=[pltpu.VMEM((2, tm, d), BF16), pltpu.VMEM((tm, d), BF16)] + 2 * _slot_scratch(d),
        compiler_params=_cparams("arbitrary"),
        name="l1_tail",
    )(yt, yt, z, w_glu, b_glu.reshape(1, d), x4, mod3, final_g.reshape(1, d), w_out)


def _both_dirs(a):
    _, g, r, p = a.shape
    return a.transpose(1, 2, 0, 3).reshape(g, r, 2 * p)


def kernel(x_prompt, x_sample, cache_l0_k, cache_l0_v, state_l1_s5, c, c_ctx, l0_norm_g, l0_w_ada, l0_b_ada, l0_w_in, l0_rpb, l0_w_out, l1_norm_g, l1_w_ada, l1_b_ada, l1_w_in, l1_a_re, l1_a_im, l1_log_dt, l1_b_re, l1_b_im, l1_c_re, l1_c_im, l1_d, l1_w_glu, l1_b_glu, l1_w_out, final_norm_g):
    bp, seq, d = x_prompt.shape
    bs, n_tok, _ = x_sample.shape
    g = d // S5_CH
    p = S5_P
    xp = x_prompt.reshape(bp * seq, d)
    xs = x_sample.reshape(bs * n_tok, d)

    ctx = bs
    cond8 = jnp.zeros((8, d), F32).at[0:bs].set(c).at[ctx].set(c_ctx)
    mods = _ada(cond8, (l0_w_ada, l1_w_ada), (l0_b_ada, l1_b_ada))
    mod0 = mods[0].reshape(8, 1, 3 * d)
    mod1 = mods[1].reshape(8, 1, 3 * d)

    w_in0 = l0_w_in.astype(BF16)
    w_out0 = l0_w_out.astype(BF16)
    w_in1 = l1_w_in.astype(BF16)
    w_glu = l1_w_glu.astype(BF16)
    w_out1 = l1_w_out.astype(BF16)

    qs, ks, vs, zs = _inproj(xs, l0_norm_g, mod0, 0, n_tok, w_in0,
                             ((0, BF16), (1, BF16), (2, BF16), (3, BF16)), 1024, 256)
    qp, kpb, vpb, zp, kp, vp = _inproj(xp, l0_norm_g, mod0, ctx, bp * seq, w_in0,
                                       ((0, BF16), (1, BF16), (2, BF16), (3, BF16)), 1024, 256, head_split=(1, 2))
    rpb_pad = jnp.zeros((N_HEADS, 16, LANES), F32).at[:, :RPB_R, :RPB_C].set(l0_rpb)
    os_ = _latent_na(qs, ks, vs, cache_l0_k, cache_l0_v, rpb_pad, n_tok)
    n_seg_s = max(1, n_tok // S5_CHUNK // S5_SEG)
    x1p = _ctx_layer(qp, kpb, vpb, zp, xp, mod0, ctx, w_out0, seq, 512).reshape(bp, seq // S5_CHUNK, S5_CHUNK, d)
    x1s = _gated_out(os_, zs, xs, mod0, 0, n_tok, w_out0, 512).reshape(bs * n_seg_s, -1, S5_CHUNK, d)

    utp, zp1 = _inproj_t(x1p, l1_norm_g, mod1, ctx, 1, w_in1)
    uts, zs1 = _inproj_t(x1s, l1_norm_g, mod1, 0, bs, w_in1)
    h0 = state_l1_s5.transpose(3, 0, 2, 1, 4).reshape(g, bs, 4 * p)
    log_dt = jnp.broadcast_to(l1_log_dt[:, :, None, None], (2, g, 1, p))
    ytp, yts, st = _s5(
        utp, uts, h0,
        _both_dirs(l1_a_re[:, :, None, :]), _both_dirs(l1_a_im[:, :, None, :]), _both_dirs(log_dt),
        _both_dirs(l1_b_re.transpose(0, 1, 3, 2)), _both_dirs(l1_b_im.transpose(0, 1, 3, 2)),
        _both_dirs(l1_c_re), _both_dirs(l1_c_im), l1_d, bp, bs, n_seg_s)
    y_prompt = _l1_tail(ytp, zp1, x1p, w_glu, l1_b_glu, mod1, ctx, 1, final_norm_g, w_out1, 512)
    y_sample = _l1_tail(yts, zs1, x1s, w_glu, l1_b_glu, mod1, 0, bs, final_norm_g, w_out1, 512)
    new_state = st.reshape(g, bp, 2, 2, p).transpose(1, 3, 2, 0, 4)
    return (y_prompt.reshape(bp, seq, d), y_sample.reshape(bs, n_tok, d),
            kp.reshape(bp, seq, N_HEADS, HEAD_DIM), vp.reshape(bp, seq, N_HEADS, HEAD_DIM), new_state)
```

```python
import functools

import jax
import jax.numpy as jnp
from jax import lax
from jax.experimental import pallas as pl
from jax.experimental.pallas import tpu as pltpu

F32 = jnp.float32
BF16 = jnp.bfloat16

EPS = 1e-6
N_HEADS = 16
HEAD_DIM = 128
GRID_W = 64
NA_ROWS = 8
NA_COLS = 16
RPB_R = 2 * NA_ROWS - 1
RPB_C = 2 * NA_COLS - 1
S5_CH = 16
S5_P = 64
S5_CHUNK = 16
S5_SEG = 8
NEG = -1e30
LANES = 128
VMEM_LIMIT = 56 * 1024 * 1024


def _cparams(*sem):
    return pltpu.CompilerParams(dimension_semantics=sem, vmem_limit_bytes=VMEM_LIMIT)


def _dot(a, b):
    return jnp.dot(a, b, preferred_element_type=F32)


def _dot_nt(a, b, precision=None):
    return lax.dot_general(a, b, (((1,), (1,)), ((), ())), preferred_element_type=F32,
                           precision=precision)


def _sigmoid(x):
    return 0.5 * jnp.tanh(0.5 * x) + 0.5


def _silu(x):
    return x * _sigmoid(x)


def _gelu_tanh(x):
    c = 0.7978845608028654
    return (0.5 * x) * (1.0 + jnp.tanh(x * (c + (c * 0.044715) * (x * x))))


def _ada_kernel(cond_ref, *refs, n_layers):
    w_refs, b_refs, o_ref = refs[:n_layers], refs[n_layers:2 * n_layers], refs[2 * n_layers]
    s = _silu(cond_ref[...]).astype(BF16)
    layer = pl.program_id(0)
    for i in range(n_layers):
        @pl.when(layer == i)
        def _(i=i):
            o_ref[...] = _dot(s, w_refs[i][...].astype(BF16)) + b_refs[i][...]


def _ada(cond8, w_adas, b_adas):
    n_layers = len(w_adas)
    d, n = w_adas[0].shape
    tn = 1024
    nj = n // tn

    def own(i):
        return lambda l, j: (0, jnp.where(l == i, j, jnp.where(l < i, 0, nj - 1)))

    return pl.pallas_call(
        functools.partial(_ada_kernel, n_layers=n_layers),
        grid=(n_layers, nj),
        in_specs=[pl.BlockSpec((8, d), lambda l, j: (0, 0))]
        + [pl.BlockSpec((d, tn), own(i)) for i in range(n_layers)]
        + [pl.BlockSpec((1, tn), own(i)) for i in range(n_layers)],
        out_specs=pl.BlockSpec((None, 8, tn), lambda l, j: (l, 0, j)),
        out_shape=jax.ShapeDtypeStruct((n_layers, 8, n), F32),
        compiler_params=_cparams("arbitrary", "arbitrary"),
        name="ada",
    )(cond8, *w_adas, *[b.reshape(1, n) for b in b_adas])


def _norm_mod(x, g, shift, scale):
    y = x * lax.rsqrt(jnp.mean(x * x, axis=-1, keepdims=True) + EPS) * g
    return (y * (1.0 + scale) + shift).astype(BF16)


def _inproj_kernel(x0_ref, xn_ref, g_ref, shift0_ref, scale0_ref, shiftn_ref, scalen_ref, *rest, out_split,
                   head_split, with_ada):
    n_split = max(out_split + head_split) + 1
    w_refs = rest[:n_split]
    rest = rest[n_split:]
    if with_ada:
        cond_ref, wada_ref, bada_ref = rest[:3]
        rest = rest[3:]
    o_refs = rest[:len(out_split)]
    hd_refs = rest[len(out_split):len(out_split) + len(head_split)]
    rest = rest[len(out_split) + len(head_split):]
    if with_ada:
        rest[0][...] = _dot(_silu(cond_ref[...]).astype(BF16), wada_ref[...].astype(BF16)) + bada_ref[...]
        rest = rest[1:]
    h_ref = rest[0]
    i = pl.program_id(0)
    j = pl.program_id(1)
    nj = pl.num_programs(1)
    slot = i % 2

    @pl.when((i == 0) & (j == 0))
    def _():
        h_ref[0] = _norm_mod(x0_ref[...], g_ref[...], shift0_ref[...], scale0_ref[...])

    if head_split:
        stage_ref, sem = rest[1:]
        tm, tn = stage_ref.shape[2:]
        t = i * nj + j

        def head_copies(tt):
            ii, jj = tt // nj, tt % nj
            return [pltpu.make_async_copy(stage_ref.at[tt % 2, n, :, pl.ds(hh * HEAD_DIM, HEAD_DIM)],
                                          hd.at[pl.ds(ii * tm, tm), jj * (tn // HEAD_DIM) + hh, :], sem.at[tt % 2])
                    for n, hd in enumerate(hd_refs) for hh in range(tn // HEAD_DIM)]

        @pl.when(t >= 2)
        def _():
            for c in head_copies(t - 2):
                c.wait()

    h = h_ref[slot]
    for s, w_ref in enumerate(w_refs):
        r = _dot(h, w_ref[...])
        for o_ref, o_s in zip(o_refs, out_split):
            if o_s == s:
                o_ref[...] = r.astype(o_ref.dtype)
        if s in head_split:
            stage_ref[t % 2, head_split.index(s)] = r
    rs = xn_ref.shape[0]
    h_ref[1 - slot, pl.ds(pl.multiple_of(j * rs, rs), rs), :] = _norm_mod(
        xn_ref[...], g_ref[...], shiftn_ref[...], scalen_ref[...])

    if head_split:
        for c in head_copies(t):
            c.start()

        @pl.when(t == pl.num_programs(0) * nj - 1)
        def _():
            @pl.when(t >= 1)
            def _():
                for c in head_copies(t - 1):
                    c.wait()

            for c in head_copies(t):
                c.wait()


def _inproj(x, g, mod3, cond0, rows_per_cond, w, outs, tm, tn, head_split=(), ada_next=None):
    m, d = x.shape
    out_split = tuple(s for s, _ in outs)
    out_dtypes = [dt for _, dt in outs]
    head_split = tuple(head_split)
    n_split = max(out_split + head_split) + 1
    ni = m // tm
    nj = d // tn

    def nxt(i):
        return jnp.minimum(i + 1, ni - 1)

    def cond_of(i):
        return cond0 + (i * tm) // rows_per_cond

    in_specs = [pl.BlockSpec((tm, d), lambda i, j: (0, 0), pipeline_mode=pl.Buffered(1)),
                pl.BlockSpec((tm // nj, d), lambda i, j: (nxt(i) * nj + j, 0)),
                pl.BlockSpec((1, d), lambda i, j: (0, 0)),
                pl.BlockSpec((None, 1, d), lambda i, j: (cond_of(0), 0, 0)),
                pl.BlockSpec((None, 1, d), lambda i, j: (cond_of(0), 0, 1)),
                pl.BlockSpec((None, 1, d), lambda i, j: (cond_of(nxt(i)), 0, 0)),
                pl.BlockSpec((None, 1, d), lambda i, j: (cond_of(nxt(i)), 0, 1))]
    for s in range(n_split):
        in_specs.append(pl.BlockSpec((d, tn), lambda i, j, s=s: (0, s * nj + j)))
    out_specs = ([pl.BlockSpec((tm, tn), lambda i, j: (i, j)) for _ in outs]
                 + [pl.BlockSpec(memory_space=pl.ANY)] * len(head_split))
    out_shape = ([jax.ShapeDtypeStruct((m, d), dt) for dt in out_dtypes]
                 + [jax.ShapeDtypeStruct((m, d // HEAD_DIM, HEAD_DIM), F32)] * len(head_split))
    operands = [x, x, g.reshape(1, d), mod3, mod3, mod3, mod3] + [w] * n_split
    if ada_next is not None:
        cond8, w_ada, b_ada = ada_next
        n_ada = w_ada.shape[1]
        ta = n_ada // (ni * nj)
        assert ta * ni * nj == n_ada and ta % LANES == 0
        in_specs += [pl.BlockSpec((8, d), lambda i, j: (0, 0)),
                     pl.BlockSpec((d, ta), lambda i, j: (0, i * nj + j)),
                     pl.BlockSpec((1, ta), lambda i, j: (0, i * nj + j))]
        out_specs.append(pl.BlockSpec((8, ta), lambda i, j: (0, i * nj + j)))
        out_shape.append(jax.ShapeDtypeStruct((8, n_ada), F32))
        operands += [cond8, w_ada, b_ada.reshape(1, n_ada)]
    scratch = [pltpu.VMEM((2, tm, d), BF16)]
    if head_split:
        scratch += [pltpu.VMEM((2, len(head_split), tm, tn), F32), pltpu.SemaphoreType.DMA((2,))]
    return pl.pallas_call(
        functools.partial(_inproj_kernel, out_split=out_split, head_split=head_split,
                          with_ada=ada_next is not None),
        grid=(ni, nj),
        in_specs=in_specs,
        out_specs=out_specs,
        out_shape=out_shape,
        scratch_shapes=scratch,
        compiler_params=_cparams("arbitrary", "arbitrary"),
        name="inproj",
    )(*operands)


def _softmax_pv(parts):
    m = parts[0][0].max(axis=-1, keepdims=True)
    for s, _ in parts[1:]:
        m = jnp.maximum(m, s.max(axis=-1, keepdims=True))
    l = None
    o = None
    for s, v in parts:
        e = jnp.exp(s - m)
        li = e.sum(axis=-1, keepdims=True)
        oi = _dot(e.astype(BF16), v)
        l = li if l is None else l + li
        o = oi if o is None else o + oi
    return o / l


def _ctx_layer_kernel(q_ref, k_ref, v_ref, z_ref, x_ref, gate_ref, w_ref, y_ref, a_ref, *, seq):
    scale = HEAD_DIM ** -0.5
    for b in range(q_ref.shape[0] // seq):
        rows = slice(b * seq, (b + 1) * seq)
        for h in range(N_HEADS):
            sl = slice(h * HEAD_DIM, (h + 1) * HEAD_DIM)
            s = _dot_nt(q_ref[rows, sl], k_ref[rows, sl]) * scale
            o = _softmax_pv([(s, v_ref[rows, sl])])
            a_ref[rows, sl] = (o * _silu(z_ref[rows, sl].astype(F32))).astype(BF16)
    y_ref[...] = x_ref[...] + gate_ref[...] * _dot(a_ref[...], w_ref[...])


def _ctx_layer(q, k, v, z, x, mod3, cond, w_out, seq, tm):
    m, d = x.shape
    row = pl.BlockSpec((tm, d), lambda i: (i, 0))
    return pl.pallas_call(
        functools.partial(_ctx_layer_kernel, seq=seq),
        grid=(m // tm,),
        in_specs=[row, row, row, row, row,
                  pl.BlockSpec((None, 1, d), lambda i: (cond, 0, 2)),
                  pl.BlockSpec((d, d), lambda i: (0, 0), pipeline_mode=pl.Buffered(1))],
        out_specs=row,
        out_shape=jax.ShapeDtypeStruct((m, d), F32),
        scratch_shapes=[pltpu.VMEM((tm, d), BF16)],
        compiler_params=_cparams("parallel"),
        name="ctx_layer",
    )(q, k, v, z, x, mod3, w_out)


def _na_window_start(r, rows):
    return min(max(r - NA_ROWS // 2, 0), rows - NA_ROWS)


def _na_kernel(q_ref, k_ref, v_ref, kc_hbm, vc_hbm, rpb_ref, wcast_ref, o_ref, wcast_bf_ref, cbuf, csem,
               *, rows, q_rows):
    wcast_bf_ref[...] = wcast_ref[...].astype(BF16)
    scale = HEAD_DIM ** -0.5
    w = GRID_W
    n_heads = pl.num_programs(1)
    t = pl.program_id(0) * n_heads + pl.program_id(1)

    def ctx_copies(tt):
        return [pltpu.make_async_copy(c.at[tt // n_heads, :, tt % n_heads, :], cbuf.at[tt % 2, n], csem.at[tt % 2])
                for n, c in enumerate((kc_hbm, vc_hbm))]

    @pl.when(t == 0)
    def _():
        for c in ctx_copies(t):
            c.start()

    @pl.when(t + 1 < pl.num_programs(0) * n_heads)
    def _():
        for c in ctx_copies(t + 1):
            c.start()
    qc = lax.broadcasted_iota(jnp.int32, (w, LANES), 0)
    lane = lax.broadcasted_iota(jnp.int32, (w, LANES), 1)
    c0 = jnp.clip(qc - NA_COLS // 2, 0, w - NA_COLS)
    ok_l = (lane >= c0) & (lane < c0 + NA_COLS)
    ok_r = (lane - w >= c0) & (lane - w < c0 + NA_COLS)
    neg_tile = jnp.full((w, LANES), NEG, F32)
    tile_l, tile_r = [], []
    for d in range(RPB_R):
        base = jnp.broadcast_to(rpb_ref[d:d + 1, :], (w, LANES))
        left = pltpu.roll(base, LANES - (NA_COLS - 1), 1, stride=1, stride_axis=0)
        right = pltpu.roll(base, w - (NA_COLS - 1), 1, stride=1, stride_axis=0)
        tile_l.append(jnp.where(ok_l, left, NEG))
        tile_r.append(jnp.where(ok_r, right, NEG))

    for c in ctx_copies(t):
        c.wait()
    kc = cbuf[t % 2, 0].astype(BF16)
    vc = cbuf[t % 2, 1].astype(BF16)
    n_groups = rows // q_rows
    for gi in range(n_groups):
        rs = list(range(gi * q_rows, (gi + 1) * q_rows))
        klo = min(_na_window_start(r, rows) for r in rs) // 2 * 2
        khi = -(-(max(_na_window_start(r, rows) for r in rs) + NA_ROWS) // 2) * 2
        bias_rows = []
        for r in rs:
            r0 = _na_window_start(r, rows)
            tiles = []
            for kr in range(klo, khi, 2):
                ok0 = r0 <= kr < r0 + NA_ROWS
                ok1 = r0 <= kr + 1 < r0 + NA_ROWS
                t0 = tile_l[kr - r + NA_ROWS - 1] if ok0 else neg_tile
                t1 = tile_r[kr + 1 - r + NA_ROWS - 1] if ok1 else neg_tile
                tiles.append(jnp.maximum(t0, t1) if (ok0 or ok1) else neg_tile)
            bias_rows.append(jnp.concatenate(tiles, axis=1))
        bias = jnp.concatenate(bias_rows, axis=0)
        q = q_ref[gi * q_rows * w:(gi + 1) * q_rows * w, :]
        kl = k_ref[klo * w:khi * w, :]
        vl = v_ref[klo * w:khi * w, :]
        s_loc = _dot_nt(q, kl) * scale + bias
        s_ctx = _dot_nt(q, kc) * scale
        o = _softmax_pv([(s_loc, vl), (s_ctx, vc)])
        o_ref[gi * q_rows * w:(gi + 1) * q_rows * w, :] = o.astype(o_ref.dtype)


def _latent_na(q, k, v, k_ctx, v_ctx, rpb_pad, n_tok, w_cast):
    m, d = q.shape
    nb = m // n_tok
    lc = k_ctx.shape[1]
    rows = n_tok // GRID_W
    tc = w_cast.shape[1] // (nb * N_HEADS)
    assert tc * nb * N_HEADS == w_cast.shape[1] and tc % LANES == 0
    spec = pl.BlockSpec((n_tok, HEAD_DIM), lambda b, h: (b, h))
    cspec = pl.BlockSpec(memory_space=pl.ANY)
    wspec = pl.BlockSpec((w_cast.shape[0], tc), lambda b, h: (0, b * N_HEADS + h))
    return pl.pallas_call(
        functools.partial(_na_kernel, rows=rows, q_rows=4),
        grid=(nb, N_HEADS),
        in_specs=[spec, spec, spec, cspec, cspec,
                  pl.BlockSpec((None, 16, LANES), lambda b, h: (h, 0, 0)), wspec],
        out_specs=[spec, wspec],
        out_shape=[jax.ShapeDtypeStruct((m, d), BF16), jax.ShapeDtypeStruct(w_cast.shape, BF16)],
        scratch_shapes=[pltpu.VMEM((2, 2, lc, HEAD_DIM), F32), pltpu.SemaphoreType.DMA((2,))],
        compiler_params=_cparams("arbitrary", "arbitrary"),
        name="latent_na",
    )(q, k, v, k_ctx, v_ctx, rpb_pad, w_cast)


def _gated_out_kernel(o_ref, z_ref, x_ref, gate_ref, w_ref, y_ref):
    a = o_ref[...].astype(F32) * _silu(z_ref[...].astype(F32))
    y_ref[...] = x_ref[...] + gate_ref[...] * _dot(a.astype(BF16), w_ref[...])


def _gated_out(o, z, x, mod3, cond0, rows_per_cond, w_out, tm):
    m, d = x.shape
    row = pl.BlockSpec((tm, d), lambda i: (i, 0))
    return pl.pallas_call(
        _gated_out_kernel,
        grid=(m // tm,),
        in_specs=[row, row, row,
                  pl.BlockSpec((None, 1, d), lambda i: (cond0 + (i * tm) // rows_per_cond, 0, 2)),
                  pl.BlockSpec((d, d), lambda i: (0, 0), pipeline_mode=pl.Buffered(1))],
        out_specs=row,
        out_shape=jax.ShapeDtypeStruct((m, d), F32),
        compiler_params=_cparams("parallel"),
        name="gated_out",
    )(o, z, x, mod3, w_out)


LANE_TILE = 512


def _slots_per_step(x4):
    nq, m, _, _ = x4.shape
    assert LANE_TILE % (nq * m) == 0 and S5_CHUNK % (LANE_TILE // (nq * m)) == 0
    return LANE_TILE // (nq * m)


def _slot_scratch(d):
    return [pltpu.VMEM((2, LANE_TILE, d), F32), pltpu.SemaphoreType.DMA((2,))]


def _slot_copies(x_hbm, buf, sem, l, to_hbm):
    nq, m, _, _ = x_hbm.shape
    n_x = buf.shape[1] // (nq * m)
    copies = []
    for xi in range(n_x):
        for k in range(m):
            hbm = x_hbm.at[:, k, l * n_x + xi, :]
            vmem = buf.at[l % 2, pl.ds((xi * m + k) * nq, nq), :]
            src, dst = (vmem, hbm) if to_hbm else (hbm, vmem)
            copies.append(pltpu.make_async_copy(src, dst, sem.at[l % 2]))
    return copies


def _fetch_slots(x_hbm, buf, sem):
    l = pl.program_id(0)

    @pl.when(l == 0)
    def _():
        for c in _slot_copies(x_hbm, buf, sem, l, False):
            c.start()

    @pl.when(l + 1 < pl.num_programs(0))
    def _():
        for c in _slot_copies(x_hbm, buf, sem, l + 1, False):
            c.start()

    for c in _slot_copies(x_hbm, buf, sem, l, False):
        c.wait()
    return buf[l % 2]


def _store_slots(res, o_hbm, buf, sem):
    l = pl.program_id(0)

    def wait(ll):
        for c in _slot_copies(o_hbm, buf, sem, ll, True):
            c.wait()

    @pl.when(l >= 2)
    def _():
        wait(l - 2)

    buf[l % 2] = res
    for c in _slot_copies(o_hbm, buf, sem, l, True):
        c.start()

    @pl.when(l == pl.num_programs(0) - 1)
    def _():
        @pl.when(l >= 1)
        def _():
            wait(l - 1)

        wait(l)


def _cond_rows(ref, n_rows, nq, n_cond):
    out = ref[n_cond - 1]
    if n_cond > 1:
        b = (lax.broadcasted_iota(jnp.int32, (n_rows, 1), 0) % nq) // (nq // n_cond)
        for i in range(n_cond - 2, -1, -1):
            out = jnp.where(b == i, ref[i], out)
    return out


def _inproj_t_kernel(x_hbm, g_ref, shift_ref, scale_ref, wu_ref, wz_ref, u_ref, z_ref, buf, sem, *, n_cond):
    nq = x_hbm.shape[0]
    x = _fetch_slots(x_hbm, buf, sem)
    n_rows = x.shape[0]
    h = _norm_mod(x, g_ref[...], _cond_rows(shift_ref, n_rows, nq, n_cond),
                  _cond_rows(scale_ref, n_rows, nq, n_cond))
    u_ref[...] = _dot(h, wu_ref[...]).T.astype(u_ref.dtype)
    z_ref[...] = _dot(h, wz_ref[...]).astype(z_ref.dtype)


def _inproj_t(x4, g, mod3, cond0, n_cond, w_in):
    nq, m, _, d = x4.shape
    n_tok = nq * m * S5_CHUNK
    _slots_per_step(x4)

    def mspec(col):
        return pl.BlockSpec((n_cond, 1, d), lambda l: (cond0 // n_cond, 0, col))

    def wspec(col):
        return pl.BlockSpec((d, d), lambda l: (0, col), pipeline_mode=pl.Buffered(1))

    return pl.pallas_call(
        functools.partial(_inproj_t_kernel, n_cond=n_cond),
        grid=(n_tok // LANE_TILE,),
        in_specs=[pl.BlockSpec(memory_space=pl.ANY),
                  pl.BlockSpec((1, d), lambda l: (0, 0)), mspec(0), mspec(1), wspec(0), wspec(1)],
        out_specs=[pl.BlockSpec((d, LANE_TILE), lambda l: (0, l)), pl.BlockSpec((LANE_TILE, d), lambda l: (l, 0))],
        out_shape=[jax.ShapeDtypeStruct((d, n_tok), BF16), jax.ShapeDtypeStruct((n_tok, d), BF16)],
        scratch_shapes=_slot_scratch(d),
        compiler_params=_cparams("arbitrary"),
        name="inproj_t",
    )(x4, g.reshape(1, d), mod3, mod3, w_in, w_in)


def _cmul(ar, ai, br, bi):
    return ar * br - ai * bi, ar * bi + ai * br


def _split_bf16(a):
    hi = a.astype(BF16)
    return hi, (a - hi.astype(F32)).astype(BF16)


def _s5_powers(ar, ai):
    width = ar.shape[-1]
    pw = [(jnp.ones_like(ar), jnp.zeros_like(ar))]
    for _ in range(S5_CHUNK):
        pw.append(_cmul(pw[-1][0], pw[-1][1], ar, ai))
    is_fwd = lax.broadcasted_iota(jnp.int32, (1, width), 1) < width // 2

    def pattern(fwd_ascending):
        out = []
        for part in range(2):
            blocks = []
            for j in range(S5_CHUNK):
                ef, eb = (j, S5_CHUNK - 1 - j) if fwd_ascending else (S5_CHUNK - 1 - j, j)
                blocks.append(jnp.broadcast_to(jnp.where(is_fwd, pw[ef][part], pw[eb][part]), (S5_CH, width)))
            out.append(jnp.concatenate(blocks, axis=0))
        return out

    return pattern(True), pattern(False), pw[S5_CHUNK]


S5_GROUPS_PER_STEP = 8


def _s5_kernel(xp_ref, xs_ref, *rest, nb_p, nb_s, n_seg_s, n_cast):
    group_in, cast_in = rest[:9], rest[9:9 + n_cast]
    yp_ref, ys_ref, st_ref = rest[9 + n_cast:12 + n_cast]
    cast_out = rest[12 + n_cast:12 + 2 * n_cast]
    scratch = rest[12 + 2 * n_cast:]
    for gi in range(S5_GROUPS_PER_STEP):
        ch = pl.ds(gi * S5_CH, S5_CH)
        _s5_group(xp_ref.at[ch], xs_ref.at[ch], *[r.at[gi] for r in group_in], yp_ref.at[ch], ys_ref.at[ch],
                  st_ref.at[gi], *[r.at[gi] for r in scratch], nb_p=nb_p, nb_s=nb_s, n_seg_s=n_seg_s)
    for src, dst in zip(cast_in, cast_out):
        dst[...] = src[...].astype(dst.dtype)


def _s5_group(xp_ref, xs_ref, h0_ref, are_ref, aim_ref, ldt_ref, bre_ref, bim_ref, cre_ref, cim_ref, dsk_ref,
              yp_ref, ys_ref, st_ref, w_ref, ws_ref, wc_ref, s_ref, f_ref, *, nb_p, nb_s, n_seg_s):
    p = S5_P
    kc = S5_CHUNK * S5_CH
    a_re = are_ref[...]
    a_im = aim_ref[...]
    dt = jnp.exp(ldt_ref[...])
    mag = jnp.exp(a_re * dt)
    ab_re = mag * jnp.cos(a_im * dt)
    ab_im = mag * jnp.sin(a_im * dt)
    den = a_re * a_re + a_im * a_im
    nr = ab_re - 1.0
    f_re = (nr * a_re + ab_im * a_im) / den
    f_im = (ab_im * a_re - nr * a_im) / den
    bbt_re, bbt_im = _cmul(f_re, f_im, bre_ref[...], bim_ref[...])

    (pg_r, pg_i), (ps_r, ps_i), (ac_r, ac_i) = _s5_powers(ab_re, ab_im)
    c_re = jnp.concatenate([cre_ref[...]] * S5_CHUNK, axis=0)
    c_im = jnp.concatenate([cim_ref[...]] * S5_CHUNK, axis=0)
    bt_re = jnp.concatenate([bbt_re] * S5_CHUNK, axis=0)
    bt_im = jnp.concatenate([bbt_im] * S5_CHUNK, axis=0)

    g_re, g_im = _cmul(pg_r, pg_i, c_re, c_im)
    fwd16 = lax.broadcasted_iota(jnp.int32, (S5_CH, 2 * p), 1) < p
    g_hi, g_lo = _split_bf16(jnp.concatenate([g_re, g_im], axis=1))
    r0 = []
    for d in range(2):
        msk = fwd16 if d == 0 else jnp.logical_not(fwd16)
        b_hi, b_lo = _split_bf16(
            jnp.concatenate([jnp.where(msk, bbt_re, 0.0), jnp.where(msk, -bbt_im, 0.0)], axis=1))
        r0.append(_dot_nt(b_hi, g_hi) + (_dot_nt(b_hi, g_lo) + _dot_nt(b_lo, g_hi)))
    lane = lax.broadcasted_iota(jnp.int32, (S5_CH, kc), 1)
    for s in range(S5_CHUNK):
        f = jnp.where(lane >= S5_CH * s, pltpu.roll(r0[0], S5_CH * s, 1), 0.0) if s else r0[0]
        b = jnp.where(lane < S5_CH * (s + 1), pltpu.roll(r0[1], (S5_CH * (s + 1)) % kc, 1), 0.0)
        w_ref[s * S5_CH:(s + 1) * S5_CH, :] = (f + b).astype(BF16)
    e_re, e_im = _cmul(ps_r, ps_i, bt_re, bt_im)
    ws_ref[:, 0:2 * p] = e_re.astype(BF16)
    ws_ref[:, 2 * p:4 * p] = e_im.astype(BF16)
    g1_re, g1_im = _cmul(g_re, g_im, ab_re, ab_im)
    wc_ref[:, 0:2 * p] = g1_re.astype(BF16)
    wc_ref[:, 2 * p:4 * p] = (-g1_im).astype(BF16)

    def run(xt_ref, yt_ref, nb, n_seg, hr, hm):
        rows = xt_ref.shape[1] // S5_CHUNK
        nbx = nb * n_seg
        m = rows // nbx
        xt = jnp.concatenate([xt_ref[:, s * rows:(s + 1) * rows] for s in range(S5_CHUNK)], axis=0)
        x = xt.T
        y = _dot(x, w_ref[...]) + dsk_ref[...] * x.astype(F32)
        s_all = _dot(x, ws_ref[...])
        s_ref[0, 0:rows, :] = s_all[:, 0:2 * p]
        s_ref[1, 0:rows, :] = s_all[:, 2 * p:4 * p]

        def step(ar, ai, hr, hm, plane_r, plane_i, rf, rb):
            isf = lax.broadcasted_iota(jnp.int32, hr.shape, 1) < p
            sr = jnp.where(isf, plane_r[rf, :], plane_r[rb, :])
            sm = jnp.where(isf, plane_i[rf, :], plane_i[rb, :])
            return ar * hr - ai * hm + sr, ar * hm + ai * hr + sm

        def scan(hr, hm, store):
            arb = jnp.broadcast_to(ac_r, (nbx, 2 * p))
            aib = jnp.broadcast_to(ac_i, (nbx, 2 * p))
            for k in range(m):
                rf = pl.ds(k * nbx, nbx)
                rb = pl.ds((m - 1 - k) * nbx, nbx)
                if store:
                    s_ref[2, rf, :] = hr
                    s_ref[3, rf, :] = hm
                    s_ref[4, rb, :] = hr
                    s_ref[5, rb, :] = hm
                hr, hm = step(arb, aib, hr, hm, s_ref.at[0], s_ref.at[1], rf, rb)
            return hr, hm

        if n_seg > 1:
            zeros = jnp.zeros((nbx, 2 * p), F32)
            f_ref[0], f_ref[1] = scan(zeros, zeros, False)
            sr_, si_ = ac_r, ac_i
            for _ in range(m - 1):
                sr_, si_ = _cmul(sr_, si_, ac_r, ac_i)
            sr_ = jnp.broadcast_to(sr_, (nb, 2 * p))
            si_ = jnp.broadcast_to(si_, (nb, 2 * p))
            for j in range(n_seg):
                rf = pl.ds(j, nb, stride=n_seg)
                rb = pl.ds(n_seg - 1 - j, nb, stride=n_seg)
                f_ref[2, rf, :] = hr
                f_ref[3, rf, :] = hm
                f_ref[4, rb, :] = hr
                f_ref[5, rb, :] = hm
                hr, hm = step(sr_, si_, hr, hm, f_ref.at[0], f_ref.at[1], rf, rb)
            isf = lax.broadcasted_iota(jnp.int32, (nbx, 2 * p), 1) < p
            hr = jnp.where(isf, f_ref[2], f_ref[4])
            hm = jnp.where(isf, f_ref[3], f_ref[5])
        hr, hm = scan(hr, hm, True)
        isf_rows = lax.broadcasted_iota(jnp.int32, (rows, 2 * p), 1) < p
        h_prev = jnp.concatenate([jnp.where(isf_rows, s_ref[2, 0:rows, :], s_ref[4, 0:rows, :]),
                                  jnp.where(isf_rows, s_ref[3, 0:rows, :], s_ref[5, 0:rows, :])], axis=1)
        yt = _gelu_tanh(y + _dot_nt(h_prev.astype(BF16), wc_ref[...])).astype(yt_ref.dtype).T
        for t in range(S5_CHUNK):
            yt_ref[:, t * rows:(t + 1) * rows] = yt[t * S5_CH:(t + 1) * S5_CH, :]
        return hr, hm

    zeros = jnp.zeros((nb_p, 2 * p), F32)
    hr, hm = run(xp_ref, yp_ref, nb_p, 1, zeros, zeros)
    st_ref[:, 0:2 * p] = hr
    st_ref[:, 2 * p:4 * p] = hm
    run(xs_ref, ys_ref, nb_s, n_seg_s, h0_ref[:, 0:2 * p], h0_ref[:, 2 * p:4 * p])


def _s5(utp, uts, h0, a_re, a_im, log_dt, bt_re, bt_im, c_re, c_im, d_skip, nb_p, nb_s, n_seg_s, casts):
    d, lanes_p = utp.shape
    lanes_s = uts.shape[1]
    g = d // S5_CH
    rows_p = lanes_p // S5_CHUNK
    rows_s = lanes_s // S5_CHUNK
    kc = S5_CHUNK * S5_CH
    p = S5_P

    gps = S5_GROUPS_PER_STEP

    def gspec(shape):
        return pl.BlockSpec((gps,) + shape, lambda i: (i, 0, 0))

    def tspec(lanes):
        return pl.BlockSpec((gps * S5_CH, lanes), lambda i: (i, 0))

    n_steps = g // gps
    cast_specs = []
    for wc in casts:
        tc = wc.shape[1] // n_steps
        assert tc * n_steps == wc.shape[1] and tc % LANES == 0
        cast_specs.append(pl.BlockSpec((wc.shape[0], tc), lambda i: (0, i)))
    return pl.pallas_call(
        functools.partial(_s5_kernel, nb_p=nb_p, nb_s=nb_s, n_seg_s=n_seg_s, n_cast=len(casts)),
        grid=(n_steps,),
        in_specs=[tspec(lanes_p), tspec(lanes_s), gspec((nb_s, 4 * p)),
                  gspec((1, 2 * p)), gspec((1, 2 * p)), gspec((1, 2 * p)),
                  gspec((S5_CH, 2 * p)), gspec((S5_CH, 2 * p)), gspec((S5_CH, 2 * p)), gspec((S5_CH, 2 * p)),
                  gspec((1, kc))] + cast_specs,
        out_specs=[tspec(lanes_p), tspec(lanes_s), gspec((nb_p, 4 * p))] + cast_specs,
        out_shape=[jax.ShapeDtypeStruct((d, lanes_p), BF16),
                   jax.ShapeDtypeStruct((d, lanes_s), BF16),
                   jax.ShapeDtypeStruct((g, nb_p, 4 * p), F32)]
        + [jax.ShapeDtypeStruct(wc.shape, BF16) for wc in casts],
        scratch_shapes=[pltpu.VMEM((gps, kc, kc), BF16), pltpu.VMEM((gps, kc, 4 * p), BF16),
                        pltpu.VMEM((gps, kc, 4 * p), BF16),
                        pltpu.VMEM((gps, 6, max(rows_p, rows_s), 2 * p), F32),
                        pltpu.VMEM((gps, 6, nb_s * n_seg_s, 2 * p), F32)],
        compiler_params=_cparams("parallel"),
        name="s5",
    )(utp, uts, h0, a_re, a_im, log_dt, bt_re, bt_im, c_re, c_im,
      jnp.tile(d_skip.reshape(g, 1, S5_CH), (1, 1, S5_CHUNK)), *casts)


def _l1_tail_kernel(yt0_ref, ytn_ref, z_ref, wglu_ref, b_ref, x_hbm, gate_ref, fg_ref, wout_ref, o_hbm,
                    yb_ref, a_ref, xbuf, xsem, obuf, osem, *, tn, n_cond):
    d = wglu_ref.shape[0]
    nq = x_hbm.shape[0]
    l = pl.program_id(0)
    slot = l % 2
    _fetch_slots(x_hbm, xbuf, xsem)

    @pl.when(l == 0)
    def _():
        yb_ref[0] = yt0_ref[...].T

    for c in range(d // tn):
        cs = slice(c * tn, (c + 1) * tn)
        yb = yb_ref[slot]
        gl = _dot(yb, wglu_ref[:, cs]) + b_ref[:, cs]
        a = yb[:, cs].astype(F32) * _sigmoid(gl) * _silu(z_ref[:, cs].astype(F32))
        a_ref[:, cs] = a.astype(BF16)
        yb_ref[1 - slot, :, cs] = ytn_ref[cs, :].T

    x = xbuf[slot]
    xn = x + _cond_rows(gate_ref, x.shape[0], nq, n_cond) * _dot(a_ref[...], wout_ref[...])
    res = xn * lax.rsqrt(jnp.mean(xn * xn, axis=-1, keepdims=True) + EPS) * fg_ref[...]
    _store_slots(res, o_hbm, obuf, osem)


def _l1_tail(yt, z, x4, w_glu, b_glu, mod3, cond0, n_cond, final_g, w_out, tn):
    nq, m, _, d = x4.shape
    _slots_per_step(x4)
    tm = LANE_TILE
    ni = nq * m * S5_CHUNK // tm
    wspec = pl.BlockSpec((d, d), lambda i: (0, 0), pipeline_mode=pl.Buffered(1))
    vec = pl.BlockSpec((1, d), lambda i: (0, 0))
    return pl.pallas_call(
        functools.partial(_l1_tail_kernel, tn=tn, n_cond=n_cond),
        grid=(ni,),
        in_specs=[pl.BlockSpec((d, tm), lambda i: (0, 0), pipeline_mode=pl.Buffered(1)),
                  pl.BlockSpec((d, tm), lambda i: (0, jnp.minimum(i + 1, ni - 1))),
                  pl.BlockSpec((tm, d), lambda i: (i, 0)),
                  wspec, vec,
                  pl.BlockSpec(memory_space=pl.ANY),
                  pl.BlockSpec((n_cond, 1, d), lambda i: (cond0 // n_cond, 0, 2)),
                  vec, wspec],
        out_specs=pl.BlockSpec(memory_space=pl.ANY),
        out_shape=jax.ShapeDtypeStruct(x4.shape, F32),
        scratch_shapes=[pltpu.VMEM((2, tm, d), BF16), pltpu.VMEM((tm, d), BF16)] + 2 * _slot_scratch(d),
        compiler_params=_cparams("arbitrary"),
        name="l1_tail",
    )(yt, yt, z, w_glu, b_glu.reshape(1, d), x4, mod3, final_g.reshape(1, d), w_out)


def _both_dirs(a):
    _, g, r, p = a.shape
    return a.transpose(1, 2, 0, 3).reshape(g, r, 2 * p)


def kernel(x_prompt, x_sample, cache_l0_k, cache_l0_v, state_l1_s5, c, c_ctx, l0_norm_g, l0_w_ada, l0_b_ada, l0_w_in, l0_rpb, l0_w_out, l1_norm_g, l1_w_ada, l1_b_ada, l1_w_in, l1_a_re, l1_a_im, l1_log_dt, l1_b_re, l1_b_im, l1_c_re, l1_c_im, l1_d, l1_w_glu, l1_b_glu, l1_w_out, final_norm_g):
    bp, seq, d = x_prompt.shape
    bs, n_tok, _ = x_sample.shape
    g = d // S5_CH
    p = S5_P
    xp = x_prompt.reshape(bp * seq, d)
    xs = x_sample.reshape(bs * n_tok, d)

    ctx = bs
    cond8 = jnp.zeros((8, d), F32).at[0:bs].set(c).at[ctx].set(c_ctx)
    mod0 = _ada(cond8, (l0_w_ada,), (l0_b_ada,))[0].reshape(8, 1, 3 * d)

    w_in0 = l0_w_in.astype(BF16)
    w_out0 = l0_w_out.astype(BF16)

    qs, ks, vs, zs, mod1 = _inproj(xs, l0_norm_g, mod0, 0, n_tok, w_in0,
                                   ((0, BF16), (1, BF16), (2, BF16), (3, BF16)), 1024, 256,
                                   ada_next=(cond8, l1_w_ada, l1_b_ada))
    mod1 = mod1.reshape(8, 1, 3 * d)
    qp, kpb, vpb, zp, kp, vp = _inproj(xp, l0_norm_g, mod0, ctx, bp * seq, w_in0,
                                       ((0, BF16), (1, BF16), (2, BF16), (3, BF16)), 1024, 256, head_split=(1, 2))
    rpb_pad = jnp.zeros((N_HEADS, 16, LANES), F32).at[:, :RPB_R, :RPB_C].set(l0_rpb)
    os_, w_in1 = _latent_na(qs, ks, vs, cache_l0_k, cache_l0_v, rpb_pad, n_tok, l1_w_in)
    n_seg_s = max(1, n_tok // S5_CHUNK // S5_SEG)
    x1p = _ctx_layer(qp, kpb, vpb, zp, xp, mod0, ctx, w_out0, seq, 512).reshape(bp, seq // S5_CHUNK, S5_CHUNK, d)
    x1s = _gated_out(os_, zs, xs, mod0, 0, n_tok, w_out0, 512).reshape(bs * n_seg_s, -1, S5_CHUNK, d)

    utp, zp1 = _inproj_t(x1p, l1_norm_g, mod1, ctx, 1, w_in1)
    uts, zs1 = _inproj_t(x1s, l1_norm_g, mod1, 0, bs, w_in1)
    h0 = state_l1_s5.transpose(3, 0, 2, 1, 4).reshape(g, bs, 4 * p)
    log_dt = jnp.broadcast_to(l1_log_dt[:, :, None, None], (2, g, 1, p))
    ytp, yts, st, w_glu, w_out1 = _s5(
        utp, uts, h0,
        _both_dirs(l1_a_re[:, :, None, :]), _both_dirs(l1_a_im[:, :, None, :]), _both_dirs(log_dt),
        _both_dirs(l1_b_re.transpose(0, 1, 3, 2)), _both_dirs(l1_b_im.transpose(0, 1, 3, 2)),
        _both_dirs(l1_c_re), _both_dirs(l1_c_im), l1_d, bp, bs, n_seg_s, (l1_w_glu, l1_w_out))
    y_prompt = _l1_tail(ytp, zp1, x1p, w_glu, l1_b_glu, mod1, ctx, 1, final_norm_g, w_out1, 512)
    y_sample = _l1_tail(yts, zs1, x1s, w_glu, l1_b_glu, mod1, 0, bs, final_norm_g, w_out1, 512)
    new_state = st.reshape(g, bp, 2, 2, p).transpose(1, 3, 2, 0, 4)
    return (y_prompt.reshape(bp, seq, d), y_sample.reshape(bs, n_tok, d),
            kp.reshape(bp, seq, N_HEADS, HEAD_DIM), vp.reshape(bp, seq, N_HEADS, HEAD_DIM), new_state)
```

```python
import functools

import jax
import jax.numpy as jnp
from jax import lax
from jax.experimental import pallas as pl
from jax.experimental.pallas import tpu as pltpu

F32 = jnp.float32
BF16 = jnp.bfloat16

EPS = 1e-6
N_HEADS = 16
HEAD_DIM = 128
GRID_W = 64
NA_ROWS = 8
NA_COLS = 16
RPB_R = 2 * NA_ROWS - 1
RPB_C = 2 * NA_COLS - 1
S5_CH = 16
S5_P = 64
S5_CHUNK = 16
S5_SEG = 8
NEG = -1e30
LANES = 128
VMEM_LIMIT = 56 * 1024 * 1024


def _cparams(*sem):
    return pltpu.CompilerParams(dimension_semantics=sem, vmem_limit_bytes=VMEM_LIMIT)


def _dot(a, b):
    return jnp.dot(a, b, preferred_element_type=F32)


def _dot_nt(a, b, precision=None):
    return lax.dot_general(a, b, (((1,), (1,)), ((), ())), preferred_element_type=F32,
                           precision=precision)


def _sigmoid(x):
    return 0.5 * jnp.tanh(0.5 * x) + 0.5


def _silu(x):
    return x * _sigmoid(x)


def _gelu_tanh(x):
    c = 0.7978845608028654
    return (0.5 * x) * (1.0 + jnp.tanh(x * (c + (c * 0.044715) * (x * x))))


def _ada_kernel(cond_ref, *refs, n_layers):
    w_refs, b_refs, o_ref = refs[:n_layers], refs[n_layers:2 * n_layers], refs[2 * n_layers]
    s = _silu(cond_ref[...]).astype(BF16)
    layer = pl.program_id(0)
    for i in range(n_layers):
        @pl.when(layer == i)
        def _(i=i):
            o_ref[...] = _dot(s, w_refs[i][...].astype(BF16)) + b_refs[i][...]


def _ada(cond8, w_adas, b_adas):
    n_layers = len(w_adas)
    d, n = w_adas[0].shape
    tn = 1024
    nj = n // tn

    def own(i):
        return lambda l, j: (0, jnp.where(l == i, j, jnp.where(l < i, 0, nj - 1)))

    return pl.pallas_call(
        functools.partial(_ada_kernel, n_layers=n_layers),
        grid=(n_layers, nj),
        in_specs=[pl.BlockSpec((8, d), lambda l, j: (0, 0))]
        + [pl.BlockSpec((d, tn), own(i)) for i in range(n_layers)]
        + [pl.BlockSpec((1, tn), own(i)) for i in range(n_layers)],
        out_specs=pl.BlockSpec((None, 8, tn), lambda l, j: (l, 0, j)),
        out_shape=jax.ShapeDtypeStruct((n_layers, 8, n), F32),
        compiler_params=_cparams("arbitrary", "arbitrary"),
        name="ada",
    )(cond8, *w_adas, *[b.reshape(1, n) for b in b_adas])


def _norm_mod(x, g, shift, scale):
    y = x * lax.rsqrt(jnp.mean(x * x, axis=-1, keepdims=True) + EPS) * g
    return (y * (1.0 + scale) + shift).astype(BF16)


def _inproj_kernel(x0_ref, xn_ref, g_ref, shift0_ref, scale0_ref, shiftn_ref, scalen_ref, *rest, out_split,
                   head_split, with_ada, n_cast):
    n_split = max(out_split + head_split) + 1
    w_refs = rest[:n_split]
    rest = rest[n_split:]
    if with_ada:
        cond_ref, wada_ref, bada_ref = rest[:3]
        rest = rest[3:]
    cast_in, rest = rest[:n_cast], rest[n_cast:]
    o_refs = rest[:len(out_split)]
    hd_refs = rest[len(out_split):len(out_split) + len(head_split)]
    rest = rest[len(out_split) + len(head_split):]
    if with_ada:
        rest[0][...] = _dot(_silu(cond_ref[...]).astype(BF16), wada_ref[...].astype(BF16)) + bada_ref[...]
        rest = rest[1:]
    for src, dst in zip(cast_in, rest[:n_cast]):
        dst[...] = src[...].astype(dst.dtype)
    rest = rest[n_cast:]
    h_ref = rest[0]
    i = pl.program_id(0)
    j = pl.program_id(1)
    nj = pl.num_programs(1)
    slot = i % 2

    @pl.when((i == 0) & (j == 0))
    def _():
        h_ref[0] = _norm_mod(x0_ref[...], g_ref[...], shift0_ref[...], scale0_ref[...])

    if head_split:
        stage_ref, sem = rest[1:]
        tm, tn = stage_ref.shape[2:]
        t = i * nj + j

        def head_copies(tt):
            ii, jj = tt // nj, tt % nj
            return [pltpu.make_async_copy(stage_ref.at[tt % 2, n, :, pl.ds(hh * HEAD_DIM, HEAD_DIM)],
                                          hd.at[pl.ds(ii * tm, tm), jj * (tn // HEAD_DIM) + hh, :], sem.at[tt % 2])
                    for n, hd in enumerate(hd_refs) for hh in range(tn // HEAD_DIM)]

        @pl.when(t >= 2)
        def _():
            for c in head_copies(t - 2):
                c.wait()

    h = h_ref[slot]
    for s, w_ref in enumerate(w_refs):
        r = _dot(h, w_ref[...])
        for o_ref, o_s in zip(o_refs, out_split):
            if o_s == s:
                o_ref[...] = r.astype(o_ref.dtype)
        if s in head_split:
            stage_ref[t % 2, head_split.index(s)] = r
    rs = xn_ref.shape[0]
    h_ref[1 - slot, pl.ds(pl.multiple_of(j * rs, rs), rs), :] = _norm_mod(
        xn_ref[...], g_ref[...], shiftn_ref[...], scalen_ref[...])

    if head_split:
        for c in head_copies(t):
            c.start()

        @pl.when(t == pl.num_programs(0) * nj - 1)
        def _():
            @pl.when(t >= 1)
            def _():
                for c in head_copies(t - 1):
                    c.wait()

            for c in head_copies(t):
                c.wait()


def _inproj(x, g, mod3, cond0, rows_per_cond, w, outs, tm, tn, head_split=(), ada_next=None, casts=()):
    m, d = x.shape
    out_split = tuple(s for s, _ in outs)
    out_dtypes = [dt for _, dt in outs]
    head_split = tuple(head_split)
    n_split = max(out_split + head_split) + 1
    ni = m // tm
    nj = d // tn

    def nxt(i):
        return jnp.minimum(i + 1, ni - 1)

    def cond_of(i):
        return cond0 + (i * tm) // rows_per_cond

    in_specs = [pl.BlockSpec((tm, d), lambda i, j: (0, 0), pipeline_mode=pl.Buffered(1)),
                pl.BlockSpec((tm // nj, d), lambda i, j: (nxt(i) * nj + j, 0)),
                pl.BlockSpec((1, d), lambda i, j: (0, 0)),
                pl.BlockSpec((None, 1, d), lambda i, j: (cond_of(0), 0, 0)),
                pl.BlockSpec((None, 1, d), lambda i, j: (cond_of(0), 0, 1)),
                pl.BlockSpec((None, 1, d), lambda i, j: (cond_of(nxt(i)), 0, 0)),
                pl.BlockSpec((None, 1, d), lambda i, j: (cond_of(nxt(i)), 0, 1))]
    for s in range(n_split):
        in_specs.append(pl.BlockSpec((d, tn), lambda i, j, s=s: (0, s * nj + j)))
    out_specs = ([pl.BlockSpec((tm, tn), lambda i, j: (i, j)) for _ in outs]
                 + [pl.BlockSpec(memory_space=pl.ANY)] * len(head_split))
    out_shape = ([jax.ShapeDtypeStruct((m, d), dt) for dt in out_dtypes]
                 + [jax.ShapeDtypeStruct((m, d // HEAD_DIM, HEAD_DIM), F32)] * len(head_split))
    operands = [x, x, g.reshape(1, d), mod3, mod3, mod3, mod3] + [w] * n_split
    if ada_next is not None:
        cond8, w_ada, b_ada = ada_next
        n_ada = w_ada.shape[1]
        ta = n_ada // (ni * nj)
        assert ta * ni * nj == n_ada and ta % LANES == 0
        in_specs += [pl.BlockSpec((8, d), lambda i, j: (0, 0)),
                     pl.BlockSpec((d, ta), lambda i, j: (0, i * nj + j)),
                     pl.BlockSpec((1, ta), lambda i, j: (0, i * nj + j))]
        out_specs.append(pl.BlockSpec((8, ta), lambda i, j: (0, i * nj + j)))
        out_shape.append(jax.ShapeDtypeStruct((8, n_ada), F32))
        operands += [cond8, w_ada, b_ada.reshape(1, n_ada)]
    for wc in casts:
        tc = wc.shape[1] // (ni * nj)
        assert tc * ni * nj == wc.shape[1] and tc % LANES == 0
        spec = pl.BlockSpec((wc.shape[0], tc), lambda i, j: (0, i * nj + j))
        in_specs.append(spec)
        out_specs.append(spec)
        out_shape.append(jax.ShapeDtypeStruct(wc.shape, BF16))
        operands.append(wc)
    scratch = [pltpu.VMEM((2, tm, d), BF16)]
    if head_split:
        scratch += [pltpu.VMEM((2, len(head_split), tm, tn), F32), pltpu.SemaphoreType.DMA((2,))]
    return pl.pallas_call(
        functools.partial(_inproj_kernel, out_split=out_split, head_split=head_split,
                          with_ada=ada_next is not None, n_cast=len(casts)),
        grid=(ni, nj),
        in_specs=in_specs,
        out_specs=out_specs,
        out_shape=out_shape,
        scratch_shapes=scratch,
        compiler_params=_cparams("arbitrary", "arbitrary"),
        name="inproj",
    )(*operands)


def _softmax_pv(parts):
    m = parts[0][0].max(axis=-1, keepdims=True)
    for s, _ in parts[1:]:
        m = jnp.maximum(m, s.max(axis=-1, keepdims=True))
    l = None
    o = None
    for s, v in parts:
        e = jnp.exp(s - m)
        li = e.sum(axis=-1, keepdims=True)
        oi = _dot(e.astype(BF16), v)
        l = li if l is None else l + li
        o = oi if o is None else o + oi
    return o / l


def _ctx_layer_kernel(q_ref, k_ref, v_ref, z_ref, x_ref, gate_ref, w_ref, y_ref, a_ref, *, seq):
    scale = HEAD_DIM ** -0.5
    for b in range(q_ref.shape[0] // seq):
        rows = slice(b * seq, (b + 1) * seq)
        for h in range(N_HEADS):
            sl = slice(h * HEAD_DIM, (h + 1) * HEAD_DIM)
            s = _dot_nt(q_ref[rows, sl], k_ref[rows, sl]) * scale
            o = _softmax_pv([(s, v_ref[rows, sl])])
            a_ref[rows, sl] = (o * _silu(z_ref[rows, sl].astype(F32))).astype(BF16)
    y_ref[...] = x_ref[...] + gate_ref[...] * _dot(a_ref[...], w_ref[...])


def _ctx_layer(q, k, v, z, x, mod3, cond, w_out, seq, tm):
    m, d = x.shape
    row = pl.BlockSpec((tm, d), lambda i: (i, 0))
    return pl.pallas_call(
        functools.partial(_ctx_layer_kernel, seq=seq),
        grid=(m // tm,),
        in_specs=[row, row, row, row, row,
                  pl.BlockSpec((None, 1, d), lambda i: (cond, 0, 2)),
                  pl.BlockSpec((d, d), lambda i: (0, 0), pipeline_mode=pl.Buffered(1))],
        out_specs=row,
        out_shape=jax.ShapeDtypeStruct((m, d), F32),
        scratch_shapes=[pltpu.VMEM((tm, d), BF16)],
        compiler_params=_cparams("parallel"),
        name="ctx_layer",
    )(q, k, v, z, x, mod3, w_out)


def _na_window_start(r, rows):
    return min(max(r - NA_ROWS // 2, 0), rows - NA_ROWS)


def _na_kernel(q_ref, k_ref, v_ref, kc_hbm, vc_hbm, rpb_ref, wcast_ref, o_ref, wcast_bf_ref, cbuf, csem,
               *, rows, q_rows):
    wcast_bf_ref[...] = wcast_ref[...].astype(BF16)
    scale = HEAD_DIM ** -0.5
    w = GRID_W
    n_heads = pl.num_programs(1)
    t = pl.program_id(0) * n_heads + pl.program_id(1)

    def ctx_copies(tt):
        return [pltpu.make_async_copy(c.at[tt // n_heads, :, tt % n_heads, :], cbuf.at[tt % 2, n], csem.at[tt % 2])
                for n, c in enumerate((kc_hbm, vc_hbm))]

    @pl.when(t == 0)
    def _():
        for c in ctx_copies(t):
            c.start()

    @pl.when(t + 1 < pl.num_programs(0) * n_heads)
    def _():
        for c in ctx_copies(t + 1):
            c.start()
    qc = lax.broadcasted_iota(jnp.int32, (w, LANES), 0)
    lane = lax.broadcasted_iota(jnp.int32, (w, LANES), 1)
    c0 = jnp.clip(qc - NA_COLS // 2, 0, w - NA_COLS)
    ok_l = (lane >= c0) & (lane < c0 + NA_COLS)
    ok_r = (lane - w >= c0) & (lane - w < c0 + NA_COLS)
    neg_tile = jnp.full((w, LANES), NEG, F32)
    tile_l, tile_r = [], []
    for d in range(RPB_R):
        base = jnp.broadcast_to(rpb_ref[d:d + 1, :], (w, LANES))
        left = pltpu.roll(base, LANES - (NA_COLS - 1), 1, stride=1, stride_axis=0)
        right = pltpu.roll(base, w - (NA_COLS - 1), 1, stride=1, stride_axis=0)
        tile_l.append(jnp.where(ok_l, left, NEG))
        tile_r.append(jnp.where(ok_r, right, NEG))

    for c in ctx_copies(t):
        c.wait()
    kc = cbuf[t % 2, 0].astype(BF16)
    vc = cbuf[t % 2, 1].astype(BF16)
    n_groups = rows // q_rows
    for gi in range(n_groups):
        rs = list(range(gi * q_rows, (gi + 1) * q_rows))
        klo = min(_na_window_start(r, rows) for r in rs) // 2 * 2
        khi = -(-(max(_na_window_start(r, rows) for r in rs) + NA_ROWS) // 2) * 2
        bias_rows = []
        for r in rs:
            r0 = _na_window_start(r, rows)
            tiles = []
            for kr in range(klo, khi, 2):
                ok0 = r0 <= kr < r0 + NA_ROWS
                ok1 = r0 <= kr + 1 < r0 + NA_ROWS
                t0 = tile_l[kr - r + NA_ROWS - 1] if ok0 else neg_tile
                t1 = tile_r[kr + 1 - r + NA_ROWS - 1] if ok1 else neg_tile
                tiles.append(jnp.maximum(t0, t1) if (ok0 or ok1) else neg_tile)
            bias_rows.append(jnp.concatenate(tiles, axis=1))
        bias = jnp.concatenate(bias_rows, axis=0)
        q = q_ref[gi * q_rows * w:(gi + 1) * q_rows * w, :]
        kl = k_ref[klo * w:khi * w, :]
        vl = v_ref[klo * w:khi * w, :]
        s_loc = _dot_nt(q, kl) * scale + bias
        s_ctx = _dot_nt(q, kc) * scale
        o = _softmax_pv([(s_loc, vl), (s_ctx, vc)])
        o_ref[gi * q_rows * w:(gi + 1) * q_rows * w, :] = o.astype(o_ref.dtype)


def _latent_na(q, k, v, k_ctx, v_ctx, rpb_pad, n_tok, w_cast):
    m, d = q.shape
    nb = m // n_tok
    lc = k_ctx.shape[1]
    rows = n_tok // GRID_W
    tc = w_cast.shape[1] // (nb * N_HEADS)
    assert tc * nb * N_HEADS == w_cast.shape[1] and tc % LANES == 0
    spec = pl.BlockSpec((n_tok, HEAD_DIM), lambda b, h: (b, h))
    cspec = pl.BlockSpec(memory_space=pl.ANY)
    wspec = pl.BlockSpec((w_cast.shape[0], tc), lambda b, h: (0, b * N_HEADS + h))
    return pl.pallas_call(
        functools.partial(_na_kernel, rows=rows, q_rows=4),
        grid=(nb, N_HEADS),
        in_specs=[spec, spec, spec, cspec, cspec,
                  pl.BlockSpec((None, 16, LANES), lambda b, h: (h, 0, 0)), wspec],
        out_specs=[spec, wspec],
        out_shape=[jax.ShapeDtypeStruct((m, d), BF16), jax.ShapeDtypeStruct(w_cast.shape, BF16)],
        scratch_shapes=[pltpu.VMEM((2, 2, lc, HEAD_DIM), F32), pltpu.SemaphoreType.DMA((2,))],
        compiler_params=_cparams("arbitrary", "arbitrary"),
        name="latent_na",
    )(q, k, v, k_ctx, v_ctx, rpb_pad, w_cast)


def _gated_out_kernel(o_ref, z_ref, x_ref, gate_ref, w_ref, y_ref):
    a = o_ref[...].astype(F32) * _silu(z_ref[...].astype(F32))
    y_ref[...] = x_ref[...] + gate_ref[...] * _dot(a.astype(BF16), w_ref[...])


def _gated_out(o, z, x, mod3, cond0, rows_per_cond, w_out, tm):
    m, d = x.shape
    row = pl.BlockSpec((tm, d), lambda i: (i, 0))
    return pl.pallas_call(
        _gated_out_kernel,
        grid=(m // tm,),
        in_specs=[row, row, row,
                  pl.BlockSpec((None, 1, d), lambda i: (cond0 + (i * tm) // rows_per_cond, 0, 2)),
                  pl.BlockSpec((d, d), lambda i: (0, 0), pipeline_mode=pl.Buffered(1))],
        out_specs=row,
        out_shape=jax.ShapeDtypeStruct((m, d), F32),
        compiler_params=_cparams("parallel"),
        name="gated_out",
    )(o, z, x, mod3, w_out)


LANE_TILE = 512


def _slots_per_step(x4):
    nq, m, _, _ = x4.shape
    assert LANE_TILE % (nq * m) == 0 and S5_CHUNK % (LANE_TILE // (nq * m)) == 0
    return LANE_TILE // (nq * m)


def _slot_scratch(d):
    return [pltpu.VMEM((2, LANE_TILE, d), F32), pltpu.SemaphoreType.DMA((2,))]


def _slot_copies(x_hbm, buf, sem, l, to_hbm):
    nq, m, _, _ = x_hbm.shape
    n_x = buf.shape[1] // (nq * m)
    copies = []
    for xi in range(n_x):
        for k in range(m):
            hbm = x_hbm.at[:, k, l * n_x + xi, :]
            vmem = buf.at[l % 2, pl.ds((xi * m + k) * nq, nq), :]
            src, dst = (vmem, hbm) if to_hbm else (hbm, vmem)
            copies.append(pltpu.make_async_copy(src, dst, sem.at[l % 2]))
    return copies


def _fetch_slots(x_hbm, buf, sem):
    l = pl.program_id(0)

    @pl.when(l == 0)
    def _():
        for c in _slot_copies(x_hbm, buf, sem, l, False):
            c.start()

    @pl.when(l + 1 < pl.num_programs(0))
    def _():
        for c in _slot_copies(x_hbm, buf, sem, l + 1, False):
            c.start()

    for c in _slot_copies(x_hbm, buf, sem, l, False):
        c.wait()
    return buf[l % 2]


def _store_slots(res, o_hbm, buf, sem):
    l = pl.program_id(0)

    def wait(ll):
        for c in _slot_copies(o_hbm, buf, sem, ll, True):
            c.wait()

    @pl.when(l >= 2)
    def _():
        wait(l - 2)

    buf[l % 2] = res
    for c in _slot_copies(o_hbm, buf, sem, l, True):
        c.start()

    @pl.when(l == pl.num_programs(0) - 1)
    def _():
        @pl.when(l >= 1)
        def _():
            wait(l - 1)

        wait(l)


def _cond_rows(ref, n_rows, nq, n_cond):
    out = ref[n_cond - 1]
    if n_cond > 1:
        b = (lax.broadcasted_iota(jnp.int32, (n_rows, 1), 0) % nq) // (nq // n_cond)
        for i in range(n_cond - 2, -1, -1):
            out = jnp.where(b == i, ref[i], out)
    return out


def _inproj_t_kernel(x_hbm, g_ref, shift_ref, scale_ref, wu_ref, wz_ref, u_ref, z_ref, buf, sem, *, n_cond):
    nq = x_hbm.shape[0]
    x = _fetch_slots(x_hbm, buf, sem)
    n_rows = x.shape[0]
    h = _norm_mod(x, g_ref[...], _cond_rows(shift_ref, n_rows, nq, n_cond),
                  _cond_rows(scale_ref, n_rows, nq, n_cond))
    u_ref[...] = _dot(h, wu_ref[...]).T.astype(u_ref.dtype)
    z_ref[...] = _dot(h, wz_ref[...]).astype(z_ref.dtype)


def _inproj_t(x4, g, mod3, cond0, n_cond, w_in):
    nq, m, _, d = x4.shape
    n_tok = nq * m * S5_CHUNK
    _slots_per_step(x4)

    def mspec(col):
        return pl.BlockSpec((n_cond, 1, d), lambda l: (cond0 // n_cond, 0, col))

    def wspec(col):
        return pl.BlockSpec((d, d), lambda l: (0, col), pipeline_mode=pl.Buffered(1))

    return pl.pallas_call(
        functools.partial(_inproj_t_kernel, n_cond=n_cond),
        grid=(n_tok // LANE_TILE,),
        in_specs=[pl.BlockSpec(memory_space=pl.ANY),
                  pl.BlockSpec((1, d), lambda l: (0, 0)), mspec(0), mspec(1), wspec(0), wspec(1)],
        out_specs=[pl.BlockSpec((d, LANE_TILE), lambda l: (0, l)), pl.BlockSpec((LANE_TILE, d), lambda l: (l, 0))],
        out_shape=[jax.ShapeDtypeStruct((d, n_tok), BF16), jax.ShapeDtypeStruct((n_tok, d), BF16)],
        scratch_shapes=_slot_scratch(d),
        compiler_params=_cparams("arbitrary"),
        name="inproj_t",
    )(x4, g.reshape(1, d), mod3, mod3, w_in, w_in)


def _cmul(ar, ai, br, bi):
    return ar * br - ai * bi, ar * bi + ai * br


def _split_bf16(a):
    hi = a.astype(BF16)
    return hi, (a - hi.astype(F32)).astype(BF16)


def _s5_powers(ar, ai):
    width = ar.shape[-1]
    pw = [(jnp.ones_like(ar), jnp.zeros_like(ar))]
    for _ in range(S5_CHUNK):
        pw.append(_cmul(pw[-1][0], pw[-1][1], ar, ai))
    is_fwd = lax.broadcasted_iota(jnp.int32, (1, width), 1) < width // 2

    def pattern(fwd_ascending):
        out = []
        for part in range(2):
            blocks = []
            for j in range(S5_CHUNK):
                ef, eb = (j, S5_CHUNK - 1 - j) if fwd_ascending else (S5_CHUNK - 1 - j, j)
                blocks.append(jnp.broadcast_to(jnp.where(is_fwd, pw[ef][part], pw[eb][part]), (S5_CH, width)))
            out.append(jnp.concatenate(blocks, axis=0))
        return out

    return pattern(True), pattern(False), pw[S5_CHUNK]


S5_GROUPS_PER_STEP = 8


def _s5_kernel(xp_ref, xs_ref, *rest, nb_p, nb_s, n_seg_s, n_cast):
    group_in, cast_in = rest[:9], rest[9:9 + n_cast]
    yp_ref, ys_ref, st_ref = rest[9 + n_cast:12 + n_cast]
    cast_out = rest[12 + n_cast:12 + 2 * n_cast]
    scratch = rest[12 + 2 * n_cast:]
    for gi in range(S5_GROUPS_PER_STEP):
        ch = pl.ds(gi * S5_CH, S5_CH)
        _s5_group(xp_ref.at[ch], xs_ref.at[ch], *[r.at[gi] for r in group_in], yp_ref.at[ch], ys_ref.at[ch],
                  st_ref.at[gi], *[r.at[gi] for r in scratch], nb_p=nb_p, nb_s=nb_s, n_seg_s=n_seg_s)
    for src, dst in zip(cast_in, cast_out):
        dst[...] = src[...].astype(dst.dtype)


def _s5_group(xp_ref, xs_ref, h0_ref, are_ref, aim_ref, ldt_ref, bre_ref, bim_ref, cre_ref, cim_ref, dsk_ref,
              yp_ref, ys_ref, st_ref, w_ref, ws_ref, wc_ref, s_ref, f_ref, *, nb_p, nb_s, n_seg_s):
    p = S5_P
    kc = S5_CHUNK * S5_CH
    a_re = are_ref[...]
    a_im = aim_ref[...]
    dt = jnp.exp(ldt_ref[...])
    mag = jnp.exp(a_re * dt)
    ab_re = mag * jnp.cos(a_im * dt)
    ab_im = mag * jnp.sin(a_im * dt)
    den = a_re * a_re + a_im * a_im
    nr = ab_re - 1.0
    f_re = (nr * a_re + ab_im * a_im) / den
    f_im = (ab_im * a_re - nr * a_im) / den
    bbt_re, bbt_im = _cmul(f_re, f_im, bre_ref[...], bim_ref[...])

    (pg_r, pg_i), (ps_r, ps_i), (ac_r, ac_i) = _s5_powers(ab_re, ab_im)
    c_re = jnp.concatenate([cre_ref[...]] * S5_CHUNK, axis=0)
    c_im = jnp.concatenate([cim_ref[...]] * S5_CHUNK, axis=0)
    bt_re = jnp.concatenate([bbt_re] * S5_CHUNK, axis=0)
    bt_im = jnp.concatenate([bbt_im] * S5_CHUNK, axis=0)

    g_re, g_im = _cmul(pg_r, pg_i, c_re, c_im)
    fwd16 = lax.broadcasted_iota(jnp.int32, (S5_CH, 2 * p), 1) < p
    g_hi, g_lo = _split_bf16(jnp.concatenate([g_re, g_im], axis=1))
    r0 = []
    for d in range(2):
        msk = fwd16 if d == 0 else jnp.logical_not(fwd16)
        b_hi, b_lo = _split_bf16(
            jnp.concatenate([jnp.where(msk, bbt_re, 0.0), jnp.where(msk, -bbt_im, 0.0)], axis=1))
        r0.append(_dot_nt(b_hi, g_hi) + (_dot_nt(b_hi, g_lo) + _dot_nt(b_lo, g_hi)))
    lane = lax.broadcasted_iota(jnp.int32, (S5_CH, kc), 1)
    for s in range(S5_CHUNK):
        f = jnp.where(lane >= S5_CH * s, pltpu.roll(r0[0], S5_CH * s, 1), 0.0) if s else r0[0]
        b = jnp.where(lane < S5_CH * (s + 1), pltpu.roll(r0[1], (S5_CH * (s + 1)) % kc, 1), 0.0)
        w_ref[s * S5_CH:(s + 1) * S5_CH, :] = (f + b).astype(BF16)
    e_re, e_im = _cmul(ps_r, ps_i, bt_re, bt_im)
    ws_ref[:, 0:2 * p] = e_re.astype(BF16)
    ws_ref[:, 2 * p:4 * p] = e_im.astype(BF16)
    g1_re, g1_im = _cmul(g_re, g_im, ab_re, ab_im)
    wc_ref[:, 0:2 * p] = g1_re.astype(BF16)
    wc_ref[:, 2 * p:4 * p] = (-g1_im).astype(BF16)

    def run(xt_ref, yt_ref, nb, n_seg, hr, hm):
        rows = xt_ref.shape[1] // S5_CHUNK
        nbx = nb * n_seg
        m = rows // nbx
        xt = jnp.concatenate([xt_ref[:, s * rows:(s + 1) * rows] for s in range(S5_CHUNK)], axis=0)
        x = xt.T
        y = _dot(x, w_ref[...]) + dsk_ref[...] * x.astype(F32)
        s_all = _dot(x, ws_ref[...])
        s_ref[0, 0:rows, :] = s_all[:, 0:2 * p]
        s_ref[1, 0:rows, :] = s_all[:, 2 * p:4 * p]

        def step(ar, ai, hr, hm, plane_r, plane_i, rf, rb):
            isf = lax.broadcasted_iota(jnp.int32, hr.shape, 1) < p
            sr = jnp.where(isf, plane_r[rf, :], plane_r[rb, :])
            sm = jnp.where(isf, plane_i[rf, :], plane_i[rb, :])
            return ar * hr - ai * hm + sr, ar * hm + ai * hr + sm

        def scan(hr, hm, store):
            arb = jnp.broadcast_to(ac_r, (nbx, 2 * p))
            aib = jnp.broadcast_to(ac_i, (nbx, 2 * p))
            for k in range(m):
                rf = pl.ds(k * nbx, nbx)
                rb = pl.ds((m - 1 - k) * nbx, nbx)
                if store:
                    s_ref[2, rf, :] = hr
                    s_ref[3, rf, :] = hm
                    s_ref[4, rb, :] = hr
                    s_ref[5, rb, :] = hm
                hr, hm = step(arb, aib, hr, hm, s_ref.at[0], s_ref.at[1], rf, rb)
            return hr, hm

        if n_seg > 1:
            zeros = jnp.zeros((nbx, 2 * p), F32)
            f_ref[0], f_ref[1] = scan(zeros, zeros, False)
            sr_, si_ = ac_r, ac_i
            for _ in range(m - 1):
                sr_, si_ = _cmul(sr_, si_, ac_r, ac_i)
            sr_ = jnp.broadcast_to(sr_, (nb, 2 * p))
            si_ = jnp.broadcast_to(si_, (nb, 2 * p))
            for j in range(n_seg):
                rf = pl.ds(j, nb, stride=n_seg)
                rb = pl.ds(n_seg - 1 - j, nb, stride=n_seg)
                f_ref[2, rf, :] = hr
                f_ref[3, rf, :] = hm
                f_ref[4, rb, :] = hr
                f_ref[5, rb, :] = hm
                hr, hm = step(sr_, si_, hr, hm, f_ref.at[0], f_ref.at[1], rf, rb)
            isf = lax.broadcasted_iota(jnp.int32, (nbx, 2 * p), 1) < p
            hr = jnp.where(isf, f_ref[2], f_ref[4])
            hm = jnp.where(isf, f_ref[3], f_ref[5])
        hr, hm = scan(hr, hm, True)
        isf_rows = lax.broadcasted_iota(jnp.int32, (rows, 2 * p), 1) < p
        h_prev = jnp.concatenate([jnp.where(isf_rows, s_ref[2, 0:rows, :], s_ref[4, 0:rows, :]),
                                  jnp.where(isf_rows, s_ref[3, 0:rows, :], s_ref[5, 0:rows, :])], axis=1)
        yt = _gelu_tanh(y + _dot_nt(h_prev.astype(BF16), wc_ref[...])).astype(yt_ref.dtype).T
        for t in range(S5_CHUNK):
            yt_ref[:, t * rows:(t + 1) * rows] = yt[t * S5_CH:(t + 1) * S5_CH, :]
        return hr, hm

    zeros = jnp.zeros((nb_p, 2 * p), F32)
    hr, hm = run(xp_ref, yp_ref, nb_p, 1, zeros, zeros)
    st_ref[:, 0:2 * p] = hr
    st_ref[:, 2 * p:4 * p] = hm
    run(xs_ref, ys_ref, nb_s, n_seg_s, h0_ref[:, 0:2 * p], h0_ref[:, 2 * p:4 * p])


def _s5(utp, uts, h0, a_re, a_im, log_dt, bt_re, bt_im, c_re, c_im, d_skip, nb_p, nb_s, n_seg_s, casts):
    d, lanes_p = utp.shape
    lanes_s = uts.shape[1]
    g = d // S5_CH
    rows_p = lanes_p // S5_CHUNK
    rows_s = lanes_s // S5_CHUNK
    kc = S5_CHUNK * S5_CH
    p = S5_P

    gps = S5_GROUPS_PER_STEP

    def gspec(shape):
        return pl.BlockSpec((gps,) + shape, lambda i: (i, 0, 0))

    def tspec(lanes):
        return pl.BlockSpec((gps * S5_CH, lanes), lambda i: (i, 0))

    n_steps = g // gps
    cast_specs = []
    for wc in casts:
        tc = wc.shape[1] // n_steps
        assert tc * n_steps == wc.shape[1] and tc % LANES == 0
        cast_specs.append(pl.BlockSpec((wc.shape[0], tc), lambda i: (0, i)))
    return pl.pallas_call(
        functools.partial(_s5_kernel, nb_p=nb_p, nb_s=nb_s, n_seg_s=n_seg_s, n_cast=len(casts)),
        grid=(n_steps,),
        in_specs=[tspec(lanes_p), tspec(lanes_s), gspec((nb_s, 4 * p)),
                  gspec((1, 2 * p)), gspec((1, 2 * p)), gspec((1, 2 * p)),
                  gspec((S5_CH, 2 * p)), gspec((S5_CH, 2 * p)), gspec((S5_CH, 2 * p)), gspec((S5_CH, 2 * p)),
                  gspec((1, kc))] + cast_specs,
        out_specs=[tspec(lanes_p), tspec(lanes_s), gspec((nb_p, 4 * p))] + cast_specs,
        out_shape=[jax.ShapeDtypeStruct((d, lanes_p), BF16),
                   jax.ShapeDtypeStruct((d, lanes_s), BF16),
                   jax.ShapeDtypeStruct((g, nb_p, 4 * p), F32)]
        + [jax.ShapeDtypeStruct(wc.shape, BF16) for wc in casts],
        scratch_shapes=[pltpu.VMEM((gps, kc, kc), BF16), pltpu.VMEM((gps, kc, 4 * p), BF16),
                        pltpu.VMEM((gps, kc, 4 * p), BF16),
                        pltpu.VMEM((gps, 6, max(rows_p, rows_s), 2 * p), F32),
                        pltpu.VMEM((gps, 6, nb_s * n_seg_s, 2 * p), F32)],
        compiler_params=_cparams("parallel"),
        name="s5",
    )(utp, uts, h0, a_re, a_im, log_dt, bt_re, bt_im, c_re, c_im,
      jnp.tile(d_skip.reshape(g, 1, S5_CH), (1, 1, S5_CHUNK)), *casts)


def _l1_tail_kernel(yt0_ref, ytn_ref, z_ref, wglu_ref, b_ref, x_hbm, gate_ref, fg_ref, wout_ref, o_hbm,
                    yb_ref, a_ref, xbuf, xsem, obuf, osem, *, tn, n_cond):
    d = wglu_ref.shape[0]
    nq = x_hbm.shape[0]
    l = pl.program_id(0)
    slot = l % 2
    _fetch_slots(x_hbm, xbuf, xsem)

    @pl.when(l == 0)
    def _():
        yb_ref[0] = yt0_ref[...].T

    for c in range(d // tn):
        cs = slice(c * tn, (c + 1) * tn)
        yb = yb_ref[slot]
        gl = _dot(yb, wglu_ref[:, cs]) + b_ref[:, cs]
        a = yb[:, cs].astype(F32) * _sigmoid(gl) * _silu(z_ref[:, cs].astype(F32))
        a_ref[:, cs] = a.astype(BF16)
        yb_ref[1 - slot, :, cs] = ytn_ref[cs, :].T

    x = xbuf[slot]
    xn = x + _cond_rows(gate_ref, x.shape[0], nq, n_cond) * _dot(a_ref[...], wout_ref[...])
    res = xn * lax.rsqrt(jnp.mean(xn * xn, axis=-1, keepdims=True) + EPS) * fg_ref[...]
    _store_slots(res, o_hbm, obuf, osem)


def _l1_tail(yt, z, x4, w_glu, b_glu, mod3, cond0, n_cond, final_g, w_out, tn):
    nq, m, _, d = x4.shape
    _slots_per_step(x4)
    tm = LANE_TILE
    ni = nq * m * S5_CHUNK // tm
    wspec = pl.BlockSpec((d, d), lambda i: (0, 0), pipeline_mode=pl.Buffered(1))
    vec = pl.BlockSpec((1, d), lambda i: (0, 0))
    return pl.pallas_call(
        functools.partial(_l1_tail_kernel, tn=tn, n_cond=n_cond),
        grid=(ni,),
        in_specs=[pl.BlockSpec((d, tm), lambda i: (0, 0), pipeline_mode=pl.Buffered(1)),
                  pl.BlockSpec((d, tm), lambda i: (0, jnp.minimum(i + 1, ni - 1))),
                  pl.BlockSpec((tm, d), lambda i: (i, 0)),
                  wspec, vec,
                  pl.BlockSpec(memory_space=pl.ANY),
                  pl.BlockSpec((n_cond, 1, d), lambda i: (cond0 // n_cond, 0, 2)),
                  vec, wspec],
        out_specs=pl.BlockSpec(memory_space=pl.ANY),
        out_shape=jax.ShapeDtypeStruct(x4.shape, F32),
        scratch_shapes=[pltpu.VMEM((2, tm, d), BF16), pltpu.VMEM((tm, d), BF16)] + 2 * _slot_scratch(d),
        compiler_params=_cparams("arbitrary"),
        name="l1_tail",
    )(yt, yt, z, w_glu, b_glu.reshape(1, d), x4, mod3, final_g.reshape(1, d), w_out)


def _both_dirs(a):
    _, g, r, p = a.shape
    return a.transpose(1, 2, 0, 3).reshape(g, r, 2 * p)


def kernel(x_prompt, x_sample, cache_l0_k, cache_l0_v, state_l1_s5, c, c_ctx, l0_norm_g, l0_w_ada, l0_b_ada, l0_w_in, l0_rpb, l0_w_out, l1_norm_g, l1_w_ada, l1_b_ada, l1_w_in, l1_a_re, l1_a_im, l1_log_dt, l1_b_re, l1_b_im, l1_c_re, l1_c_im, l1_d, l1_w_glu, l1_b_glu, l1_w_out, final_norm_g):
    bp, seq, d = x_prompt.shape
    bs, n_tok, _ = x_sample.shape
    g = d // S5_CH
    p = S5_P
    xp = x_prompt.reshape(bp * seq, d)
    xs = x_sample.reshape(bs * n_tok, d)

    ctx = bs
    cond8 = jnp.zeros((8, d), F32).at[0:bs].set(c).at[ctx].set(c_ctx)
    mod0 = _ada(cond8, (l0_w_ada,), (l0_b_ada,))[0].reshape(8, 1, 3 * d)

    w_in0 = l0_w_in.astype(BF16)

    qs, ks, vs, zs, mod1, w_out0 = _inproj(xs, l0_norm_g, mod0, 0, n_tok, w_in0,
                                           ((0, BF16), (1, BF16), (2, BF16), (3, BF16)), 1024, 256,
                                           ada_next=(cond8, l1_w_ada, l1_b_ada), casts=(l0_w_out,))
    mod1 = mod1.reshape(8, 1, 3 * d)
    qp, kpb, vpb, zp, kp, vp = _inproj(xp, l0_norm_g, mod0, ctx, bp * seq, w_in0,
                                       ((0, BF16), (1, BF16), (2, BF16), (3, BF16)), 1024, 256, head_split=(1, 2))
    rpb_pad = jnp.zeros((N_HEADS, 16, LANES), F32).at[:, :RPB_R, :RPB_C].set(l0_rpb)
    os_, w_in1 = _latent_na(qs, ks, vs, cache_l0_k, cache_l0_v, rpb_pad, n_tok, l1_w_in)
    n_seg_s = max(1, n_tok // S5_CHUNK // S5_SEG)
    x1p = _ctx_layer(qp, kpb, vpb, zp, xp, mod0, ctx, w_out0, seq, 512).reshape(bp, seq // S5_CHUNK, S5_CHUNK, d)
    x1s = _gated_out(os_, zs, xs, mod0, 0, n_tok, w_out0, 512).reshape(bs * n_seg_s, -1, S5_CHUNK, d)

    utp, zp1 = _inproj_t(x1p, l1_norm_g, mod1, ctx, 1, w_in1)
    uts, zs1 = _inproj_t(x1s, l1_norm_g, mod1, 0, bs, w_in1)
    h0 = state_l1_s5.transpose(3, 0, 2, 1, 4).reshape(g, bs, 4 * p)
    log_dt = jnp.broadcast_to(l1_log_dt[:, :, None, None], (2, g, 1, p))
    ytp, yts, st, w_glu, w_out1 = _s5(
        utp, uts, h0,
        _both_dirs(l1_a_re[:, :, None, :]), _both_dirs(l1_a_im[:, :, None, :]), _both_dirs(log_dt),
        _both_dirs(l1_b_re.transpose(0, 1, 3, 2)), _both_dirs(l1_b_im.transpose(0, 1, 3, 2)),
        _both_dirs(l1_c_re), _both_dirs(l1_c_im), l1_d, bp, bs, n_seg_s, (l1_w_glu, l1_w_out))
    y_prompt = _l1_tail(ytp, zp1, x1p, w_glu, l1_b_glu, mod1, ctx, 1, final_norm_g, w_out1, 512)
    y_sample = _l1_tail(yts, zs1, x1s, w_glu, l1_b_glu, mod1, 0, bs, final_norm_g, w_out1, 512)
    new_state = st.reshape(g, bp, 2, 2, p).transpose(1, 3, 2, 0, 4)
    return (y_prompt.reshape(bp, seq, d), y_sample.reshape(bs, n_tok, d),
            kp.reshape(bp, seq, N_HEADS, HEAD_DIM), vp.reshape(bp, seq, N_HEADS, HEAD_DIM), new_state)
```

```python
import functools

import jax
import jax.numpy as jnp
from jax import lax
from jax.experimental import pallas as pl
from jax.experimental.pallas import tpu as pltpu

F32 = jnp.float32
BF16 = jnp.bfloat16

EPS = 1e-6
N_HEADS = 16
HEAD_DIM = 128
GRID_W = 64
NA_ROWS = 8
NA_COLS = 16
RPB_R = 2 * NA_ROWS - 1
RPB_C = 2 * NA_COLS - 1
S5_CH = 16
S5_P = 64
S5_CHUNK = 16
S5_SEG = 8
NEG = -1e30
LANES = 128
VMEM_LIMIT = 56 * 1024 * 1024


def _cparams(*sem):
    return pltpu.CompilerParams(dimension_semantics=sem, vmem_limit_bytes=VMEM_LIMIT)


def _dot(a, b):
    return jnp.dot(a, b, preferred_element_type=F32)


def _dot_nt(a, b, precision=None):
    return lax.dot_general(a, b, (((1,), (1,)), ((), ())), preferred_element_type=F32,
                           precision=precision)


def _sigmoid(x):
    return 0.5 * jnp.tanh(0.5 * x) + 0.5


def _silu(x):
    return x * _sigmoid(x)


def _gelu_tanh(x):
    c = 0.7978845608028654
    return (0.5 * x) * (1.0 + jnp.tanh(x * (c + (c * 0.044715) * (x * x))))


def _ada_kernel(cond_ref, *refs, n_layers):
    w_refs, b_refs, o_ref = refs[:n_layers], refs[n_layers:2 * n_layers], refs[2 * n_layers]
    s = _silu(cond_ref[...]).astype(BF16)
    layer = pl.program_id(0)
    for i in range(n_layers):
        @pl.when(layer == i)
        def _(i=i):
            o_ref[...] = _dot(s, w_refs[i][...].astype(BF16)) + b_refs[i][...]


def _ada(cond8, w_adas, b_adas):
    n_layers = len(w_adas)
    d, n = w_adas[0].shape
    tn = 1024
    nj = n // tn

    def own(i):
        return lambda l, j: (0, jnp.where(l == i, j, jnp.where(l < i, 0, nj - 1)))

    return pl.pallas_call(
        functools.partial(_ada_kernel, n_layers=n_layers),
        grid=(n_layers, nj),
        in_specs=[pl.BlockSpec((8, d), lambda l, j: (0, 0))]
        + [pl.BlockSpec((d, tn), own(i)) for i in range(n_layers)]
        + [pl.BlockSpec((1, tn), own(i)) for i in range(n_layers)],
        out_specs=pl.BlockSpec((None, 8, tn), lambda l, j: (l, 0, j)),
        out_shape=jax.ShapeDtypeStruct((n_layers, 8, n), F32),
        compiler_params=_cparams("arbitrary", "arbitrary"),
        name="ada",
    )(cond8, *w_adas, *[b.reshape(1, n) for b in b_adas])


def _norm_mod(x, g, shift, scale):
    y = x * lax.rsqrt(jnp.mean(x * x, axis=-1, keepdims=True) + EPS) * g
    return (y * (1.0 + scale) + shift).astype(BF16)


def _inproj_kernel(x0_ref, xn_ref, g_ref, shift0_ref, scale0_ref, shiftn_ref, scalen_ref, *rest, out_split,
                   out_scale, head_split, with_ada, n_cast):
    n_split = max(out_split + head_split) + 1
    w_refs = rest[:n_split]
    rest = rest[n_split:]
    if with_ada:
        cond_ref, wada_ref, bada_ref = rest[:3]
        rest = rest[3:]
    cast_in, rest = rest[:n_cast], rest[n_cast:]
    o_refs = rest[:len(out_split)]
    hd_refs = rest[len(out_split):len(out_split) + len(head_split)]
    rest = rest[len(out_split) + len(head_split):]
    if with_ada:
        rest[0][...] = _dot(_silu(cond_ref[...]).astype(BF16), wada_ref[...].astype(BF16)) + bada_ref[...]
        rest = rest[1:]
    for src, dst in zip(cast_in, rest[:n_cast]):
        dst[...] = src[...].astype(dst.dtype)
    rest = rest[n_cast:]
    h_ref = rest[0]
    i = pl.program_id(0)
    j = pl.program_id(1)
    nj = pl.num_programs(1)
    slot = i % 2

    @pl.when((i == 0) & (j == 0))
    def _():
        h_ref[0] = _norm_mod(x0_ref[...], g_ref[...], shift0_ref[...], scale0_ref[...])

    if head_split:
        stage_ref, sem = rest[1:]
        tm, tn = stage_ref.shape[2:]
        t = i * nj + j

        def head_copies(tt):
            ii, jj = tt // nj, tt % nj
            return [pltpu.make_async_copy(stage_ref.at[tt % 2, n, :, pl.ds(hh * HEAD_DIM, HEAD_DIM)],
                                          hd.at[pl.ds(ii * tm, tm), jj * (tn // HEAD_DIM) + hh, :], sem.at[tt % 2])
                    for n, hd in enumerate(hd_refs) for hh in range(tn // HEAD_DIM)]

        @pl.when(t >= 2)
        def _():
            for c in head_copies(t - 2):
                c.wait()

    h = h_ref[slot]
    for s, w_ref in enumerate(w_refs):
        r = _dot(h, w_ref[...])
        for o_ref, o_s, o_c in zip(o_refs, out_split, out_scale):
            if o_s == s:
                o_ref[...] = (r if o_c == 1.0 else r * o_c).astype(o_ref.dtype)
        if s in head_split:
            stage_ref[t % 2, head_split.index(s)] = r
    rs = xn_ref.shape[0]
    h_ref[1 - slot, pl.ds(pl.multiple_of(j * rs, rs), rs), :] = _norm_mod(
        xn_ref[...], g_ref[...], shiftn_ref[...], scalen_ref[...])

    if head_split:
        for c in head_copies(t):
            c.start()

        @pl.when(t == pl.num_programs(0) * nj - 1)
        def _():
            @pl.when(t >= 1)
            def _():
                for c in head_copies(t - 1):
                    c.wait()

            for c in head_copies(t):
                c.wait()


def _inproj(x, g, mod3, cond0, rows_per_cond, w, outs, tm, tn, head_split=(), ada_next=None, casts=()):
    m, d = x.shape
    out_split = tuple(o[0] for o in outs)
    out_dtypes = [o[1] for o in outs]
    out_scale = tuple(float(o[2]) for o in outs)
    head_split = tuple(head_split)
    n_split = max(out_split + head_split) + 1
    ni = m // tm
    nj = d // tn

    def nxt(i):
        return jnp.minimum(i + 1, ni - 1)

    def cond_of(i):
        return cond0 + (i * tm) // rows_per_cond

    in_specs = [pl.BlockSpec((tm, d), lambda i, j: (0, 0), pipeline_mode=pl.Buffered(1)),
                pl.BlockSpec((tm // nj, d), lambda i, j: (nxt(i) * nj + j, 0)),
                pl.BlockSpec((1, d), lambda i, j: (0, 0)),
                pl.BlockSpec((None, 1, d), lambda i, j: (cond_of(0), 0, 0)),
                pl.BlockSpec((None, 1, d), lambda i, j: (cond_of(0), 0, 1)),
                pl.BlockSpec((None, 1, d), lambda i, j: (cond_of(nxt(i)), 0, 0)),
                pl.BlockSpec((None, 1, d), lambda i, j: (cond_of(nxt(i)), 0, 1))]
    for s in range(n_split):
        in_specs.append(pl.BlockSpec((d, tn), lambda i, j, s=s: (0, s * nj + j)))
    out_specs = ([pl.BlockSpec((tm, tn), lambda i, j: (i, j)) for _ in outs]
                 + [pl.BlockSpec(memory_space=pl.ANY)] * len(head_split))
    out_shape = ([jax.ShapeDtypeStruct((m, d), dt) for dt in out_dtypes]
                 + [jax.ShapeDtypeStruct((m, d // HEAD_DIM, HEAD_DIM), F32)] * len(head_split))
    operands = [x, x, g.reshape(1, d), mod3, mod3, mod3, mod3] + [w] * n_split
    if ada_next is not None:
        cond8, w_ada, b_ada = ada_next
        n_ada = w_ada.shape[1]
        ta = n_ada // (ni * nj)
        assert ta * ni * nj == n_ada and ta % LANES == 0
        in_specs += [pl.BlockSpec((8, d), lambda i, j: (0, 0)),
                     pl.BlockSpec((d, ta), lambda i, j: (0, i * nj + j)),
                     pl.BlockSpec((1, ta), lambda i, j: (0, i * nj + j))]
        out_specs.append(pl.BlockSpec((8, ta), lambda i, j: (0, i * nj + j)))
        out_shape.append(jax.ShapeDtypeStruct((8, n_ada), F32))
        operands += [cond8, w_ada, b_ada.reshape(1, n_ada)]
    for wc in casts:
        tc = wc.shape[1] // (ni * nj)
        assert tc * ni * nj == wc.shape[1] and tc % LANES == 0
        spec = pl.BlockSpec((wc.shape[0], tc), lambda i, j: (0, i * nj + j))
        in_specs.append(spec)
        out_specs.append(spec)
        out_shape.append(jax.ShapeDtypeStruct(wc.shape, BF16))
        operands.append(wc)
    scratch = [pltpu.VMEM((2, tm, d), BF16)]
    if head_split:
        scratch += [pltpu.VMEM((2, len(head_split), tm, tn), F32), pltpu.SemaphoreType.DMA((2,))]
    return pl.pallas_call(
        functools.partial(_inproj_kernel, out_split=out_split, out_scale=out_scale, head_split=head_split,
                          with_ada=ada_next is not None, n_cast=len(casts)),
        grid=(ni, nj),
        in_specs=in_specs,
        out_specs=out_specs,
        out_shape=out_shape,
        scratch_shapes=scratch,
        compiler_params=_cparams("arbitrary", "arbitrary"),
        name="inproj",
    )(*operands)


LOG2E = 1.4426950408889634
QK_SCALE = HEAD_DIM ** -0.5 * LOG2E


def _softmax_pv(parts):
    m = parts[0][0].max(axis=-1, keepdims=True)
    for s, _ in parts[1:]:
        m = jnp.maximum(m, s.max(axis=-1, keepdims=True))
    l = None
    o = None
    for s, v in parts:
        e = jnp.exp2(s - m)
        li = e.sum(axis=-1, keepdims=True)
        oi = _dot(e.astype(BF16), v)
        l = li if l is None else l + li
        o = oi if o is None else o + oi
    return o / l


def _ctx_layer_kernel(q_ref, k_ref, v_ref, z_ref, x_ref, gate_ref, w_ref, y_ref, a_ref, *, seq):
    for b in range(q_ref.shape[0] // seq):
        rows = slice(b * seq, (b + 1) * seq)
        for h in range(N_HEADS):
            sl = slice(h * HEAD_DIM, (h + 1) * HEAD_DIM)
            o = _softmax_pv([(_dot_nt(q_ref[rows, sl], k_ref[rows, sl]), v_ref[rows, sl])])
            a_ref[rows, sl] = (o * _silu(z_ref[rows, sl].astype(F32))).astype(BF16)
    y_ref[...] = x_ref[...] + gate_ref[...] * _dot(a_ref[...], w_ref[...])


def _ctx_layer(q, k, v, z, x, mod3, cond, w_out, seq, tm):
    m, d = x.shape
    row = pl.BlockSpec((tm, d), lambda i: (i, 0))
    return pl.pallas_call(
        functools.partial(_ctx_layer_kernel, seq=seq),
        grid=(m // tm,),
        in_specs=[row, row, row, row, row,
                  pl.BlockSpec((None, 1, d), lambda i: (cond, 0, 2)),
                  pl.BlockSpec((d, d), lambda i: (0, 0), pipeline_mode=pl.Buffered(1))],
        out_specs=row,
        out_shape=jax.ShapeDtypeStruct((m, d), F32),
        scratch_shapes=[pltpu.VMEM((tm, d), BF16)],
        compiler_params=_cparams("parallel"),
        name="ctx_layer",
    )(q, k, v, z, x, mod3, w_out)


def _na_window_start(r, rows):
    return min(max(r - NA_ROWS // 2, 0), rows - NA_ROWS)


def _na_kernel(q_ref, k_ref, v_ref, kc_hbm, vc_hbm, rpb_ref, wcast_ref, o_ref, wcast_bf_ref, cbuf, csem,
               *, rows, q_rows):
    wcast_bf_ref[...] = wcast_ref[...].astype(BF16)
    w = GRID_W
    n_heads = pl.num_programs(1)
    t = pl.program_id(0) * n_heads + pl.program_id(1)

    def ctx_copies(tt):
        return [pltpu.make_async_copy(c.at[tt // n_heads, :, tt % n_heads, :], cbuf.at[tt % 2, n], csem.at[tt % 2])
                for n, c in enumerate((kc_hbm, vc_hbm))]

    @pl.when(t == 0)
    def _():
        for c in ctx_copies(t):
            c.start()

    @pl.when(t + 1 < pl.num_programs(0) * n_heads)
    def _():
        for c in ctx_copies(t + 1):
            c.start()
    qc = lax.broadcasted_iota(jnp.int32, (w, LANES), 0)
    lane = lax.broadcasted_iota(jnp.int32, (w, LANES), 1)
    c0 = jnp.clip(qc - NA_COLS // 2, 0, w - NA_COLS)
    ok_l = (lane >= c0) & (lane < c0 + NA_COLS)
    ok_r = (lane - w >= c0) & (lane - w < c0 + NA_COLS)
    neg_tile = jnp.full((w, LANES), NEG, F32)
    tile_l, tile_r = [], []
    for d in range(RPB_R):
        base = jnp.broadcast_to(rpb_ref[d:d + 1, :] * LOG2E, (w, LANES))
        left = pltpu.roll(base, LANES - (NA_COLS - 1), 1, stride=1, stride_axis=0)
        right = pltpu.roll(base, w - (NA_COLS - 1), 1, stride=1, stride_axis=0)
        tile_l.append(jnp.where(ok_l, left, NEG))
        tile_r.append(jnp.where(ok_r, right, NEG))

    for c in ctx_copies(t):
        c.wait()
    kc = cbuf[t % 2, 0].astype(BF16)
    vc = cbuf[t % 2, 1].astype(BF16)
    n_groups = rows // q_rows
    for gi in range(n_groups):
        rs = list(range(gi * q_rows, (gi + 1) * q_rows))
        klo = min(_na_window_start(r, rows) for r in rs) // 2 * 2
        khi = -(-(max(_na_window_start(r, rows) for r in rs) + NA_ROWS) // 2) * 2
        bias_rows = []
        for r in rs:
            r0 = _na_window_start(r, rows)
            tiles = []
            for kr in range(klo, khi, 2):
                ok0 = r0 <= kr < r0 + NA_ROWS
                ok1 = r0 <= kr + 1 < r0 + NA_ROWS
                t0 = tile_l[kr - r + NA_ROWS - 1] if ok0 else neg_tile
                t1 = tile_r[kr + 1 - r + NA_ROWS - 1] if ok1 else neg_tile
                tiles.append(jnp.maximum(t0, t1) if (ok0 or ok1) else neg_tile)
            bias_rows.append(jnp.concatenate(tiles, axis=1))
        bias = jnp.concatenate(bias_rows, axis=0)
        q = q_ref[gi * q_rows * w:(gi + 1) * q_rows * w, :]
        kl = k_ref[klo * w:khi * w, :]
        vl = v_ref[klo * w:khi * w, :]
        s_loc = _dot_nt(q, kl) + bias
        s_ctx = _dot_nt(q, kc)
        o = _softmax_pv([(s_loc, vl), (s_ctx, vc)])
        o_ref[gi * q_rows * w:(gi + 1) * q_rows * w, :] = o.astype(o_ref.dtype)


def _latent_na(q, k, v, k_ctx, v_ctx, rpb_pad, n_tok, w_cast):
    m, d = q.shape
    nb = m // n_tok
    lc = k_ctx.shape[1]
    rows = n_tok // GRID_W
    tc = w_cast.shape[1] // (nb * N_HEADS)
    assert tc * nb * N_HEADS == w_cast.shape[1] and tc % LANES == 0
    spec = pl.BlockSpec((n_tok, HEAD_DIM), lambda b, h: (b, h))
    cspec = pl.BlockSpec(memory_space=pl.ANY)
    wspec = pl.BlockSpec((w_cast.shape[0], tc), lambda b, h: (0, b * N_HEADS + h))
    return pl.pallas_call(
        functools.partial(_na_kernel, rows=rows, q_rows=4),
        grid=(nb, N_HEADS),
        in_specs=[spec, spec, spec, cspec, cspec,
                  pl.BlockSpec((None, 16, LANES), lambda b, h: (h, 0, 0)), wspec],
        out_specs=[spec, wspec],
        out_shape=[jax.ShapeDtypeStruct((m, d), BF16), jax.ShapeDtypeStruct(w_cast.shape, BF16)],
        scratch_shapes=[pltpu.VMEM((2, 2, lc, HEAD_DIM), F32), pltpu.SemaphoreType.DMA((2,))],
        compiler_params=_cparams("arbitrary", "arbitrary"),
        name="latent_na",
    )(q, k, v, k_ctx, v_ctx, rpb_pad, w_cast)


def _gated_out_kernel(o_ref, z_ref, x_ref, gate_ref, w_ref, y_ref):
    a = o_ref[...].astype(F32) * _silu(z_ref[...].astype(F32))
    y_ref[...] = x_ref[...] + gate_ref[...] * _dot(a.astype(BF16), w_ref[...])


def _gated_out(o, z, x, mod3, cond0, rows_per_cond, w_out, tm):
    m, d = x.shape
    row = pl.BlockSpec((tm, d), lambda i: (i, 0))
    return pl.pallas_call(
        _gated_out_kernel,
        grid=(m // tm,),
        in_specs=[row, row, row,
                  pl.BlockSpec((None, 1, d), lambda i: (cond0 + (i * tm) // rows_per_cond, 0, 2)),
                  pl.BlockSpec((d, d), lambda i: (0, 0), pipeline_mode=pl.Buffered(1))],
        out_specs=row,
        out_shape=jax.ShapeDtypeStruct((m, d), F32),
        compiler_params=_cparams("parallel"),
        name="gated_out",
    )(o, z, x, mod3, w_out)


LANE_TILE = 512


def _slots_per_step(x4):
    nq, m, _, _ = x4.shape
    assert LANE_TILE % (nq * m) == 0 and S5_CHUNK % (LANE_TILE // (nq * m)) == 0
    return LANE_TILE // (nq * m)


def _slot_scratch(d):
    return [pltpu.VMEM((2, LANE_TILE, d), F32), pltpu.SemaphoreType.DMA((2,))]


def _slot_copies(x_hbm, buf, sem, l, to_hbm):
    nq, m, _, _ = x_hbm.shape
    n_x = buf.shape[1] // (nq * m)
    copies = []
    for xi in range(n_x):
        for k in range(m):
            hbm = x_hbm.at[:, k, l * n_x + xi, :]
            vmem = buf.at[l % 2, pl.ds((xi * m + k) * nq, nq), :]
            src, dst = (vmem, hbm) if to_hbm else (hbm, vmem)
            copies.append(pltpu.make_async_copy(src, dst, sem.at[l % 2]))
    return copies


def _fetch_slots(x_hbm, buf, sem):
    l = pl.program_id(0)

    @pl.when(l == 0)
    def _():
        for c in _slot_copies(x_hbm, buf, sem, l, False):
            c.start()

    @pl.when(l + 1 < pl.num_programs(0))
    def _():
        for c in _slot_copies(x_hbm, buf, sem, l + 1, False):
            c.start()

    for c in _slot_copies(x_hbm, buf, sem, l, False):
        c.wait()
    return buf[l % 2]


def _store_slots(res, o_hbm, buf, sem):
    l = pl.program_id(0)

    def wait(ll):
        for c in _slot_copies(o_hbm, buf, sem, ll, True):
            c.wait()

    @pl.when(l >= 2)
    def _():
        wait(l - 2)

    buf[l % 2] = res
    for c in _slot_copies(o_hbm, buf, sem, l, True):
        c.start()

    @pl.when(l == pl.num_programs(0) - 1)
    def _():
        @pl.when(l >= 1)
        def _():
            wait(l - 1)

        wait(l)


def _cond_rows(ref, n_rows, nq, n_cond):
    out = ref[n_cond - 1]
    if n_cond > 1:
        b = (lax.broadcasted_iota(jnp.int32, (n_rows, 1), 0) % nq) // (nq // n_cond)
        for i in range(n_cond - 2, -1, -1):
            out = jnp.where(b == i, ref[i], out)
    return out


def _inproj_t_kernel(x_hbm, g_ref, shift_ref, scale_ref, wu_ref, wz_ref, u_ref, z_ref, buf, sem, *, n_cond):
    nq = x_hbm.shape[0]
    x = _fetch_slots(x_hbm, buf, sem)
    n_rows = x.shape[0]
    h = _norm_mod(x, g_ref[...], _cond_rows(shift_ref, n_rows, nq, n_cond),
                  _cond_rows(scale_ref, n_rows, nq, n_cond))
    u_ref[...] = _dot(h, wu_ref[...]).T.astype(u_ref.dtype)
    z_ref[...] = _dot(h, wz_ref[...]).astype(z_ref.dtype)


def _inproj_t(x4, g, mod3, cond0, n_cond, w_in):
    nq, m, _, d = x4.shape
    n_tok = nq * m * S5_CHUNK
    _slots_per_step(x4)

    def mspec(col):
        return pl.BlockSpec((n_cond, 1, d), lambda l: (cond0 // n_cond, 0, col))

    def wspec(col):
        return pl.BlockSpec((d, d), lambda l: (0, col), pipeline_mode=pl.Buffered(1))

    return pl.pallas_call(
        functools.partial(_inproj_t_kernel, n_cond=n_cond),
        grid=(n_tok // LANE_TILE,),
        in_specs=[pl.BlockSpec(memory_space=pl.ANY),
                  pl.BlockSpec((1, d), lambda l: (0, 0)), mspec(0), mspec(1), wspec(0), wspec(1)],
        out_specs=[pl.BlockSpec((d, LANE_TILE), lambda l: (0, l)), pl.BlockSpec((LANE_TILE, d), lambda l: (l, 0))],
        out_shape=[jax.ShapeDtypeStruct((d, n_tok), BF16), jax.ShapeDtypeStruct((n_tok, d), BF16)],
        scratch_shapes=_slot_scratch(d),
        compiler_params=_cparams("arbitrary"),
        name="inproj_t",
    )(x4, g.reshape(1, d), mod3, mod3, w_in, w_in)


def _cmul(ar, ai, br, bi):
    return ar * br - ai * bi, ar * bi + ai * br


def _split_bf16(a):
    hi = a.astype(BF16)
    return hi, (a - hi.astype(F32)).astype(BF16)


def _s5_powers(ar, ai):
    width = ar.shape[-1]
    pw = [(jnp.ones_like(ar), jnp.zeros_like(ar))]
    for _ in range(S5_CHUNK):
        pw.append(_cmul(pw[-1][0], pw[-1][1], ar, ai))
    is_fwd = lax.broadcasted_iota(jnp.int32, (1, width), 1) < width // 2

    def pattern(fwd_ascending):
        out = []
        for part in range(2):
            blocks = []
            for j in range(S5_CHUNK):
                ef, eb = (j, S5_CHUNK - 1 - j) if fwd_ascending else (S5_CHUNK - 1 - j, j)
                blocks.append(jnp.broadcast_to(jnp.where(is_fwd, pw[ef][part], pw[eb][part]), (S5_CH, width)))
            out.append(jnp.concatenate(blocks, axis=0))
        return out

    return pattern(True), pattern(False), pw[S5_CHUNK]


S5_GROUPS_PER_STEP = 8


def _s5_kernel(xp_ref, xs_ref, *rest, nb_p, nb_s, n_seg_s, n_cast):
    group_in, cast_in = rest[:9], rest[9:9 + n_cast]
    yp_ref, ys_ref, st_ref = rest[9 + n_cast:12 + n_cast]
    cast_out = rest[12 + n_cast:12 + 2 * n_cast]
    scratch = rest[12 + 2 * n_cast:]
    for gi in range(S5_GROUPS_PER_STEP):
        ch = pl.ds(gi * S5_CH, S5_CH)
        _s5_group(xp_ref.at[ch], xs_ref.at[ch], *[r.at[gi] for r in group_in], yp_ref.at[ch], ys_ref.at[ch],
                  st_ref.at[gi], *[r.at[gi] for r in scratch], nb_p=nb_p, nb_s=nb_s, n_seg_s=n_seg_s)
    for src, dst in zip(cast_in, cast_out):
        dst[...] = src[...].astype(dst.dtype)


def _s5_group(xp_ref, xs_ref, h0_ref, are_ref, aim_ref, ldt_ref, bre_ref, bim_ref, cre_ref, cim_ref, dsk_ref,
              yp_ref, ys_ref, st_ref, w_ref, ws_ref, wc_ref, s_ref, f_ref, *, nb_p, nb_s, n_seg_s):
    p = S5_P
    kc = S5_CHUNK * S5_CH
    a_re = are_ref[...]
    a_im = aim_ref[...]
    dt = jnp.exp(ldt_ref[...])
    mag = jnp.exp(a_re * dt)
    ab_re = mag * jnp.cos(a_im * dt)
    ab_im = mag * jnp.sin(a_im * dt)
    den = a_re * a_re + a_im * a_im
    nr = ab_re - 1.0
    f_re = (nr * a_re + ab_im * a_im) / den
    f_im = (ab_im * a_re - nr * a_im) / den
    bbt_re, bbt_im = _cmul(f_re, f_im, bre_ref[...], bim_ref[...])

    (pg_r, pg_i), (ps_r, ps_i), (ac_r, ac_i) = _s5_powers(ab_re, ab_im)
    c_re = jnp.concatenate([cre_ref[...]] * S5_CHUNK, axis=0)
    c_im = jnp.concatenate([cim_ref[...]] * S5_CHUNK, axis=0)
    bt_re = jnp.concatenate([bbt_re] * S5_CHUNK, axis=0)
    bt_im = jnp.concatenate([bbt_im] * S5_CHUNK, axis=0)

    g_re, g_im = _cmul(pg_r, pg_i, c_re, c_im)
    fwd16 = lax.broadcasted_iota(jnp.int32, (S5_CH, 2 * p), 1) < p
    g_hi, g_lo = _split_bf16(jnp.concatenate([g_re, g_im], axis=1))
    r0 = []
    for d in range(2):
        msk = fwd16 if d == 0 else jnp.logical_not(fwd16)
        b_hi, b_lo = _split_bf16(
            jnp.concatenate([jnp.where(msk, bbt_re, 0.0), jnp.where(msk, -bbt_im, 0.0)], axis=1))
        r0.append(_dot_nt(b_hi, g_hi) + (_dot_nt(b_hi, g_lo) + _dot_nt(b_lo, g_hi)))
    lane = lax.broadcasted_iota(jnp.int32, (S5_CH, kc), 1)
    for s in range(S5_CHUNK):
        f = jnp.where(lane >= S5_CH * s, pltpu.roll(r0[0], S5_CH * s, 1), 0.0) if s else r0[0]
        b = jnp.where(lane < S5_CH * (s + 1), pltpu.roll(r0[1], (S5_CH * (s + 1)) % kc, 1), 0.0)
        w_ref[s * S5_CH:(s + 1) * S5_CH, :] = (f + b).astype(BF16)
    e_re, e_im = _cmul(ps_r, ps_i, bt_re, bt_im)
    ws_ref[:, 0:2 * p] = e_re.astype(BF16)
    ws_ref[:, 2 * p:4 * p] = e_im.astype(BF16)
    g1_re, g1_im = _cmul(g_re, g_im, ab_re, ab_im)
    wc_ref[:, 0:2 * p] = g1_re.astype(BF16)
    wc_ref[:, 2 * p:4 * p] = (-g1_im).astype(BF16)

    def run(xt_ref, yt_ref, nb, n_seg, hr, hm):
        rows = xt_ref.shape[1] // S5_CHUNK
        nbx = nb * n_seg
        m = rows // nbx
        xt = jnp.concatenate([xt_ref[:, s * rows:(s + 1) * rows] for s in range(S5_CHUNK)], axis=0)
        x = xt.T
        y = _dot(x, w_ref[...]) + dsk_ref[...] * x.astype(F32)
        s_all = _dot(x, ws_ref[...])
        s_ref[0, 0:rows, :] = s_all[:, 0:2 * p]
        s_ref[1, 0:rows, :] = s_all[:, 2 * p:4 * p]

        def step(ar, ai, hr, hm, plane_r, plane_i, rf, rb):
            isf = lax.broadcasted_iota(jnp.int32, hr.shape, 1) < p
            sr = jnp.where(isf, plane_r[rf, :], plane_r[rb, :])
            sm = jnp.where(isf, plane_i[rf, :], plane_i[rb, :])
            return ar * hr - ai * hm + sr, ar * hm + ai * hr + sm

        def scan(hr, hm, store):
            arb = jnp.broadcast_to(ac_r, (nbx, 2 * p))
            aib = jnp.broadcast_to(ac_i, (nbx, 2 * p))
            for k in range(m):
                rf = pl.ds(k * nbx, nbx)
                rb = pl.ds((m - 1 - k) * nbx, nbx)
                if store:
                    s_ref[2, rf, :] = hr
                    s_ref[3, rf, :] = hm
                    s_ref[4, rb, :] = hr
                    s_ref[5, rb, :] = hm
                hr, hm = step(arb, aib, hr, hm, s_ref.at[0], s_ref.at[1], rf, rb)
            return hr, hm

        if n_seg > 1:
            zeros = jnp.zeros((nbx, 2 * p), F32)
            f_ref[0], f_ref[1] = scan(zeros, zeros, False)
            sr_, si_ = ac_r, ac_i
            for _ in range(m - 1):
                sr_, si_ = _cmul(sr_, si_, ac_r, ac_i)
            sr_ = jnp.broadcast_to(sr_, (nb, 2 * p))
            si_ = jnp.broadcast_to(si_, (nb, 2 * p))
            for j in range(n_seg):
                rf = pl.ds(j, nb, stride=n_seg)
                rb = pl.ds(n_seg - 1 - j, nb, stride=n_seg)
                f_ref[2, rf, :] = hr
                f_ref[3, rf, :] = hm
                f_ref[4, rb, :] = hr
                f_ref[5, rb, :] = hm
                hr, hm = step(sr_, si_, hr, hm, f_ref.at[0], f_ref.at[1], rf, rb)
            isf = lax.broadcasted_iota(jnp.int32, (nbx, 2 * p), 1) < p
            hr = jnp.where(isf, f_ref[2], f_ref[4])
            hm = jnp.where(isf, f_ref[3], f_ref[5])
        hr, hm = scan(hr, hm, True)
        isf_rows = lax.broadcasted_iota(jnp.int32, (rows, 2 * p), 1) < p
        h_prev = jnp.concatenate([jnp.where(isf_rows, s_ref[2, 0:rows, :], s_ref[4, 0:rows, :]),
                                  jnp.where(isf_rows, s_ref[3, 0:rows, :], s_ref[5, 0:rows, :])], axis=1)
        yt = _gelu_tanh(y + _dot_nt(h_prev.astype(BF16), wc_ref[...])).astype(yt_ref.dtype).T
        for t in range(S5_CHUNK):
            yt_ref[:, t * rows:(t + 1) * rows] = yt[t * S5_CH:(t + 1) * S5_CH, :]
        return hr, hm

    zeros = jnp.zeros((nb_p, 2 * p), F32)
    hr, hm = run(xp_ref, yp_ref, nb_p, 1, zeros, zeros)
    st_ref[:, 0:2 * p] = hr
    st_ref[:, 2 * p:4 * p] = hm
    run(xs_ref, ys_ref, nb_s, n_seg_s, h0_ref[:, 0:2 * p], h0_ref[:, 2 * p:4 * p])


def _s5(utp, uts, h0, a_re, a_im, log_dt, bt_re, bt_im, c_re, c_im, d_skip, nb_p, nb_s, n_seg_s, casts):
    d, lanes_p = utp.shape
    lanes_s = uts.shape[1]
    g = d // S5_CH
    rows_p = lanes_p // S5_CHUNK
    rows_s = lanes_s // S5_CHUNK
    kc = S5_CHUNK * S5_CH
    p = S5_P

    gps = S5_GROUPS_PER_STEP

    def gspec(shape):
        return pl.BlockSpec((gps,) + shape, lambda i: (i, 0, 0))

    def tspec(lanes):
        return pl.BlockSpec((gps * S5_CH, lanes), lambda i: (i, 0))

    n_steps = g // gps
    cast_specs = []
    for wc in casts:
        tc = wc.shape[1] // n_steps
        assert tc * n_steps == wc.shape[1] and tc % LANES == 0
        cast_specs.append(pl.BlockSpec((wc.shape[0], tc), lambda i: (0, i)))
    return pl.pallas_call(
        functools.partial(_s5_kernel, nb_p=nb_p, nb_s=nb_s, n_seg_s=n_seg_s, n_cast=len(casts)),
        grid=(n_steps,),
        in_specs=[tspec(lanes_p), tspec(lanes_s), gspec((nb_s, 4 * p)),
                  gspec((1, 2 * p)), gspec((1, 2 * p)), gspec((1, 2 * p)),
                  gspec((S5_CH, 2 * p)), gspec((S5_CH, 2 * p)), gspec((S5_CH, 2 * p)), gspec((S5_CH, 2 * p)),
                  gspec((1, kc))] + cast_specs,
        out_specs=[tspec(lanes_p), tspec(lanes_s), gspec((nb_p, 4 * p))] + cast_specs,
        out_shape=[jax.ShapeDtypeStruct((d, lanes_p), BF16),
                   jax.ShapeDtypeStruct((d, lanes_s), BF16),
                   jax.ShapeDtypeStruct((g, nb_p, 4 * p), F32)]
        + [jax.ShapeDtypeStruct(wc.shape, BF16) for wc in casts],
        scratch_shapes=[pltpu.VMEM((gps, kc, kc), BF16), pltpu.VMEM((gps, kc, 4 * p), BF16),
                        pltpu.VMEM((gps, kc, 4 * p), BF16),
                        pltpu.VMEM((gps, 6, max(rows_p, rows_s), 2 * p), F32),
                        pltpu.VMEM((gps, 6, nb_s * n_seg_s, 2 * p), F32)],
        compiler_params=_cparams("parallel"),
        name="s5",
    )(utp, uts, h0, a_re, a_im, log_dt, bt_re, bt_im, c_re, c_im,
      jnp.tile(d_skip.reshape(g, 1, S5_CH), (1, 1, S5_CHUNK)), *casts)


def _l1_tail_kernel(yt0_ref, ytn_ref, z_ref, wglu_ref, b_ref, x_hbm, gate_ref, fg_ref, wout_ref, o_hbm,
                    yb_ref, a_ref, xbuf, xsem, obuf, osem, *, tn, n_cond):
    d = wglu_ref.shape[0]
    nq = x_hbm.shape[0]
    l = pl.program_id(0)
    slot = l % 2
    _fetch_slots(x_hbm, xbuf, xsem)

    @pl.when(l == 0)
    def _():
        yb_ref[0] = yt0_ref[...].T

    for c in range(d // tn):
        cs = slice(c * tn, (c + 1) * tn)
        yb = yb_ref[slot]
        gl = _dot(yb, wglu_ref[:, cs]) + b_ref[:, cs]
        a = yb[:, cs].astype(F32) * _sigmoid(gl) * _silu(z_ref[:, cs].astype(F32))
        a_ref[:, cs] = a.astype(BF16)
        yb_ref[1 - slot, :, cs] = ytn_ref[cs, :].T

    x = xbuf[slot]
    xn = x + _cond_rows(gate_ref, x.shape[0], nq, n_cond) * _dot(a_ref[...], wout_ref[...])
    res = xn * lax.rsqrt(jnp.mean(xn * xn, axis=-1, keepdims=True) + EPS) * fg_ref[...]
    _store_slots(res, o_hbm, obuf, osem)


def _l1_tail(yt, z, x4, w_glu, b_glu, mod3, cond0, n_cond, final_g, w_out, tn):
    nq, m, _, d = x4.shape
    _slots_per_step(x4)
    tm = LANE_TILE
    ni = nq * m * S5_CHUNK // tm
    wspec = pl.BlockSpec((d, d), lambda i: (0, 0), pipeline_mode=pl.Buffered(1))
    vec = pl.BlockSpec((1, d), lambda i: (0, 0))
    return pl.pallas_call(
        functools.partial(_l1_tail_kernel, tn=tn, n_cond=n_cond),
        grid=(ni,),
        in_specs=[pl.BlockSpec((d, tm), lambda i: (0, 0), pipeline_mode=pl.Buffered(1)),
                  pl.BlockSpec((d, tm), lambda i: (0, jnp.minimum(i + 1, ni - 1))),
                  pl.BlockSpec((tm, d), lambda i: (i, 0)),
                  wspec, vec,
                  pl.BlockSpec(memory_space=pl.ANY),
                  pl.BlockSpec((n_cond, 1, d), lambda i: (cond0 // n_cond, 0, 2)),
                  vec, wspec],
        out_specs=pl.BlockSpec(memory_space=pl.ANY),
        out_shape=jax.ShapeDtypeStruct(x4.shape, F32),
        scratch_shapes=[pltpu.VMEM((2, tm, d), BF16), pltpu.VMEM((tm, d), BF16)] + 2 * _slot_scratch(d),
        compiler_params=_cparams("arbitrary"),
        name="l1_tail",
    )(yt, yt, z, w_glu, b_glu.reshape(1, d), x4, mod3, final_g.reshape(1, d), w_out)


def _both_dirs(a):
    _, g, r, p = a.shape
    return a.transpose(1, 2, 0, 3).reshape(g, r, 2 * p)


def kernel(x_prompt, x_sample, cache_l0_k, cache_l0_v, state_l1_s5, c, c_ctx, l0_norm_g, l0_w_ada, l0_b_ada, l0_w_in, l0_rpb, l0_w_out, l1_norm_g, l1_w_ada, l1_b_ada, l1_w_in, l1_a_re, l1_a_im, l1_log_dt, l1_b_re, l1_b_im, l1_c_re, l1_c_im, l1_d, l1_w_glu, l1_b_glu, l1_w_out, final_norm_g):
    bp, seq, d = x_prompt.shape
    bs, n_tok, _ = x_sample.shape
    g = d // S5_CH
    p = S5_P
    xp = x_prompt.reshape(bp * seq, d)
    xs = x_sample.reshape(bs * n_tok, d)

    ctx = bs
    cond8 = jnp.zeros((8, d), F32).at[0:bs].set(c).at[ctx].set(c_ctx)
    mod0 = _ada(cond8, (l0_w_ada,), (l0_b_ada,))[0].reshape(8, 1, 3 * d)

    w_in0 = l0_w_in.astype(BF16)

    qkvz = ((0, BF16, QK_SCALE), (1, BF16, 1.0), (2, BF16, 1.0), (3, BF16, 1.0))
    qs, ks, vs, zs, mod1, w_out0 = _inproj(xs, l0_norm_g, mod0, 0, n_tok, w_in0, qkvz, 1024, 256,
                                           ada_next=(cond8, l1_w_ada, l1_b_ada), casts=(l0_w_out,))
    mod1 = mod1.reshape(8, 1, 3 * d)
    qp, kpb, vpb, zp, kp, vp = _inproj(xp, l0_norm_g, mod0, ctx, bp * seq, w_in0, qkvz, 1024, 256,
                                       head_split=(1, 2))
    rpb_pad = jnp.zeros((N_HEADS, 16, LANES), F32).at[:, :RPB_R, :RPB_C].set(l0_rpb)
    os_, w_in1 = _latent_na(qs, ks, vs, cache_l0_k, cache_l0_v, rpb_pad, n_tok, l1_w_in)
    n_seg_s = max(1, n_tok // S5_CHUNK // S5_SEG)
    x1p = _ctx_layer(qp, kpb, vpb, zp, xp, mod0, ctx, w_out0, seq, 512).reshape(bp, seq // S5_CHUNK, S5_CHUNK, d)
    x1s = _gated_out(os_, zs, xs, mod0, 0, n_tok, w_out0, 512).reshape(bs * n_seg_s, -1, S5_CHUNK, d)

    utp, zp1 = _inproj_t(x1p, l1_norm_g, mod1, ctx, 1, w_in1)
    uts, zs1 = _inproj_t(x1s, l1_norm_g, mod1, 0, bs, w_in1)
    h0 = state_l1_s5.transpose(3, 0, 2, 1, 4).reshape(g, bs, 4 * p)
    log_dt = jnp.broadcast_to(l1_log_dt[:, :, None, None], (2, g, 1, p))
    ytp, yts, st, w_glu, w_out1 = _s5(
        utp, uts, h0,
        _both_dirs(l1_a_re[:, :, None, :]), _both_dirs(l1_a_im[:, :, None, :]), _both_dirs(log_dt),
        _both_dirs(l1_b_re.transpose(0, 1, 3, 2)), _both_dirs(l1_b_im.transpose(0, 1, 3, 2)),
        _both_dirs(l1_c_re), _both_dirs(l1_c_im), l1_d, bp, bs, n_seg_s, (l1_w_glu, l1_w_out))
    y_prompt = _l1_tail(ytp, zp1, x1p, w_glu, l1_b_glu, mod1, ctx, 1, final_norm_g, w_out1, 512)
    y_sample = _l1_tail(yts, zs1, x1s, w_glu, l1_b_glu, mod1, 0, bs, final_norm_g, w_out1, 512)
    new_state = st.reshape(g, bp, 2, 2, p).transpose(1, 3, 2, 0, 4)
    return (y_prompt.reshape(bp, seq, d), y_sample.reshape(bs, n_tok, d),
            kp.reshape(bp, seq, N_HEADS, HEAD_DIM), vp.reshape(bp, seq, N_HEADS, HEAD_DIM), new_state)
```

```python
import functools

import jax
import jax.numpy as jnp
from jax import lax
from jax.experimental import pallas as pl
from jax.experimental.pallas import tpu as pltpu

F32 = jnp.float32
BF16 = jnp.bfloat16

EPS = 1e-6
N_HEADS = 16
HEAD_DIM = 128
GRID_W = 64
NA_ROWS = 8
NA_COLS = 16
RPB_R = 2 * NA_ROWS - 1
RPB_C = 2 * NA_COLS - 1
S5_CH = 16
S5_P = 64
S5_CHUNK = 16
S5_SEG = 8
NEG = -1e30
LANES = 128
VMEM_LIMIT = 56 * 1024 * 1024


def _cparams(*sem):
    return pltpu.CompilerParams(dimension_semantics=sem, vmem_limit_bytes=VMEM_LIMIT)


def _dot(a, b):
    return jnp.dot(a, b, preferred_element_type=F32)


def _dot_nt(a, b, precision=None):
    return lax.dot_general(a, b, (((1,), (1,)), ((), ())), preferred_element_type=F32,
                           precision=precision)


def _sigmoid(x):
    return 0.5 * jnp.tanh(0.5 * x) + 0.5


def _silu(x):
    return x * _sigmoid(x)


def _gelu_tanh(x):
    c = 0.7978845608028654
    return (0.5 * x) * (1.0 + jnp.tanh(x * (c + (c * 0.044715) * (x * x))))


def _ada_kernel(cond_ref, *refs, n_layers):
    w_refs, b_refs, o_ref = refs[:n_layers], refs[n_layers:2 * n_layers], refs[2 * n_layers]
    s = _silu(cond_ref[...]).astype(BF16)
    layer = pl.program_id(0)
    for i in range(n_layers):
        @pl.when(layer == i)
        def _(i=i):
            o_ref[...] = _dot(s, w_refs[i][...].astype(BF16)) + b_refs[i][...]


def _ada(cond8, w_adas, b_adas):
    n_layers = len(w_adas)
    d, n = w_adas[0].shape
    tn = 1024
    nj = n // tn

    def own(i):
        return lambda l, j: (0, jnp.where(l == i, j, jnp.where(l < i, 0, nj - 1)))

    return pl.pallas_call(
        functools.partial(_ada_kernel, n_layers=n_layers),
        grid=(n_layers, nj),
        in_specs=[pl.BlockSpec((8, d), lambda l, j: (0, 0))]
        + [pl.BlockSpec((d, tn), own(i)) for i in range(n_layers)]
        + [pl.BlockSpec((1, tn), own(i)) for i in range(n_layers)],
        out_specs=pl.BlockSpec((None, 8, tn), lambda l, j: (l, 0, j)),
        out_shape=jax.ShapeDtypeStruct((n_layers, 8, n), F32),
        compiler_params=_cparams("arbitrary", "arbitrary"),
        name="ada",
    )(cond8, *w_adas, *[b.reshape(1, n) for b in b_adas])


def _norm_mod(x, g, shift, scale):
    y = x * lax.rsqrt(jnp.mean(x * x, axis=-1, keepdims=True) + EPS) * g
    return (y * (1.0 + scale) + shift).astype(BF16)


def _inproj_kernel(x0_ref, xn_ref, g_ref, shift0_ref, scale0_ref, shiftn_ref, scalen_ref, *rest, out_split,
                   out_scale, head_split, with_ada, n_cast):
    n_split = max(out_split + head_split) + 1
    w_refs = rest[:n_split]
    rest = rest[n_split:]
    if with_ada:
        cond_ref, wada_ref, bada_ref = rest[:3]
        rest = rest[3:]
    cast_in, rest = rest[:n_cast], rest[n_cast:]
    o_refs = rest[:len(out_split)]
    hd_refs = rest[len(out_split):len(out_split) + len(head_split)]
    rest = rest[len(out_split) + len(head_split):]
    if with_ada:
        rest[0][...] = _dot(_silu(cond_ref[...]).astype(BF16), wada_ref[...].astype(BF16)) + bada_ref[...]
        rest = rest[1:]
    for src, dst in zip(cast_in, rest[:n_cast]):
        dst[...] = src[...].astype(dst.dtype)
    rest = rest[n_cast:]
    h_ref = rest[0]
    i = pl.program_id(0)
    j = pl.program_id(1)
    nj = pl.num_programs(1)
    slot = i % 2

    @pl.when((i == 0) & (j == 0))
    def _():
        h_ref[0] = _norm_mod(x0_ref[...], g_ref[...], shift0_ref[...], scale0_ref[...])

    if head_split:
        stage_ref, sem = rest[1:]
        tm, tn = stage_ref.shape[2:]
        t = i * nj + j

        def head_copies(tt):
            ii, jj = tt // nj, tt % nj
            return [pltpu.make_async_copy(stage_ref.at[tt % 2, n, :, pl.ds(hh * HEAD_DIM, HEAD_DIM)],
                                          hd.at[pl.ds(ii * tm, tm), jj * (tn // HEAD_DIM) + hh, :], sem.at[tt % 2])
                    for n, hd in enumerate(hd_refs) for hh in range(tn // HEAD_DIM)]

        @pl.when(t >= 2)
        def _():
            for c in head_copies(t - 2):
                c.wait()

    h = h_ref[slot]
    for s, w_ref in enumerate(w_refs):
        r = _dot(h, w_ref[...])
        for o_ref, o_s, o_c in zip(o_refs, out_split, out_scale):
            if o_s == s:
                o_ref[...] = (r if o_c == 1.0 else r * o_c).astype(o_ref.dtype)
        if s in head_split:
            stage_ref[t % 2, head_split.index(s)] = r
    rs = xn_ref.shape[0]
    h_ref[1 - slot, pl.ds(pl.multiple_of(j * rs, rs), rs), :] = _norm_mod(
        xn_ref[...], g_ref[...], shiftn_ref[...], scalen_ref[...])

    if head_split:
        for c in head_copies(t):
            c.start()

        @pl.when(t == pl.num_programs(0) * nj - 1)
        def _():
            @pl.when(t >= 1)
            def _():
                for c in head_copies(t - 1):
                    c.wait()

            for c in head_copies(t):
                c.wait()


def _inproj(x, g, mod3, cond0, rows_per_cond, w, outs, tm, tn, head_split=(), ada_next=None, casts=()):
    m, d = x.shape
    out_split = tuple(o[0] for o in outs)
    out_dtypes = [o[1] for o in outs]
    out_scale = tuple(float(o[2]) for o in outs)
    head_split = tuple(head_split)
    n_split = max(out_split + head_split) + 1
    ni = m // tm
    nj = d // tn

    def nxt(i):
        return jnp.minimum(i + 1, ni - 1)

    def cond_of(i):
        return cond0 + (i * tm) // rows_per_cond

    in_specs = [pl.BlockSpec((tm, d), lambda i, j: (0, 0), pipeline_mode=pl.Buffered(1)),
                pl.BlockSpec((tm // nj, d), lambda i, j: (nxt(i) * nj + j, 0)),
                pl.BlockSpec((1, d), lambda i, j: (0, 0)),
                pl.BlockSpec((None, 1, d), lambda i, j: (cond_of(0), 0, 0)),
                pl.BlockSpec((None, 1, d), lambda i, j: (cond_of(0), 0, 1)),
                pl.BlockSpec((None, 1, d), lambda i, j: (cond_of(nxt(i)), 0, 0)),
                pl.BlockSpec((None, 1, d), lambda i, j: (cond_of(nxt(i)), 0, 1))]
    for s in range(n_split):
        in_specs.append(pl.BlockSpec((d, tn), lambda i, j, s=s: (0, s * nj + j)))
    out_specs = ([pl.BlockSpec((tm, tn), lambda i, j: (i, j)) for _ in outs]
                 + [pl.BlockSpec(memory_space=pl.ANY)] * len(head_split))
    out_shape = ([jax.ShapeDtypeStruct((m, d), dt) for dt in out_dtypes]
                 + [jax.ShapeDtypeStruct((m, d // HEAD_DIM, HEAD_DIM), F32)] * len(head_split))
    operands = [x, x, g.reshape(1, d), mod3, mod3, mod3, mod3] + [w] * n_split
    if ada_next is not None:
        cond8, w_ada, b_ada = ada_next
        n_ada = w_ada.shape[1]
        ta = n_ada // (ni * nj)
        assert ta * ni * nj == n_ada and ta % LANES == 0
        in_specs += [pl.BlockSpec((8, d), lambda i, j: (0, 0)),
                     pl.BlockSpec((d, ta), lambda i, j: (0, i * nj + j)),
                     pl.BlockSpec((1, ta), lambda i, j: (0, i * nj + j))]
        out_specs.append(pl.BlockSpec((8, ta), lambda i, j: (0, i * nj + j)))
        out_shape.append(jax.ShapeDtypeStruct((8, n_ada), F32))
        operands += [cond8, w_ada, b_ada.reshape(1, n_ada)]
    for wc in casts:
        tc = wc.shape[1] // (ni * nj)
        assert tc * ni * nj == wc.shape[1] and tc % LANES == 0
        spec = pl.BlockSpec((wc.shape[0], tc), lambda i, j: (0, i * nj + j))
        in_specs.append(spec)
        out_specs.append(spec)
        out_shape.append(jax.ShapeDtypeStruct(wc.shape, BF16))
        operands.append(wc)
    scratch = [pltpu.VMEM((2, tm, d), BF16)]
    if head_split:
        scratch += [pltpu.VMEM((2, len(head_split), tm, tn), F32), pltpu.SemaphoreType.DMA((2,))]
    return pl.pallas_call(
        functools.partial(_inproj_kernel, out_split=out_split, out_scale=out_scale, head_split=head_split,
                          with_ada=ada_next is not None, n_cast=len(casts)),
        grid=(ni, nj),
        in_specs=in_specs,
        out_specs=out_specs,
        out_shape=out_shape,
        scratch_shapes=scratch,
        compiler_params=_cparams("arbitrary", "arbitrary"),
        name="inproj",
    )(*operands)


LOG2E = 1.4426950408889634
QK_SCALE = HEAD_DIM ** -0.5 * LOG2E


def _softmax_pv(parts):
    m = parts[0][0].max(axis=-1, keepdims=True)
    for s, _ in parts[1:]:
        m = jnp.maximum(m, s.max(axis=-1, keepdims=True))
    l = None
    o = None
    for s, v in parts:
        e = jnp.exp2(s - m)
        li = e.sum(axis=-1, keepdims=True)
        oi = _dot(e.astype(BF16), v)
        l = li if l is None else l + li
        o = oi if o is None else o + oi
    return o / l


def _ctx_layer_kernel(q_ref, k_ref, v_ref, z_ref, x_ref, gate_ref, w_ref, y_ref, a_ref, *, seq):
    for b in range(q_ref.shape[0] // seq):
        rows = slice(b * seq, (b + 1) * seq)
        for h in range(N_HEADS):
            sl = slice(h * HEAD_DIM, (h + 1) * HEAD_DIM)
            o = _softmax_pv([(_dot_nt(q_ref[rows, sl], k_ref[rows, sl]), v_ref[rows, sl])])
            a_ref[rows, sl] = (o * _silu(z_ref[rows, sl].astype(F32))).astype(BF16)
    y_ref[...] = x_ref[...] + gate_ref[...] * _dot(a_ref[...], w_ref[...])


def _ctx_layer(q, k, v, z, x, mod3, cond, w_out, seq, tm):
    m, d = x.shape
    row = pl.BlockSpec((tm, d), lambda i: (i, 0))
    return pl.pallas_call(
        functools.partial(_ctx_layer_kernel, seq=seq),
        grid=(m // tm,),
        in_specs=[row, row, row, row, row,
                  pl.BlockSpec((None, 1, d), lambda i: (cond, 0, 2)),
                  pl.BlockSpec((d, d), lambda i: (0, 0), pipeline_mode=pl.Buffered(1))],
        out_specs=row,
        out_shape=jax.ShapeDtypeStruct((m, d), F32),
        scratch_shapes=[pltpu.VMEM((tm, d), BF16)],
        compiler_params=_cparams("parallel"),
        name="ctx_layer",
    )(q, k, v, z, x, mod3, w_out)


def _na_window_start(r, rows):
    return min(max(r - NA_ROWS // 2, 0), rows - NA_ROWS)


NA_HEADS_PER_STEP = 4


def _na_kernel(q_ref, k_ref, v_ref, kc_hbm, vc_hbm, rpb_ref, wcast_ref, o_ref, wcast_bf_ref, cbuf, csem,
               *, rows, q_rows):
    wcast_bf_ref[...] = wcast_ref[...].astype(BF16)
    w = GRID_W
    hps = NA_HEADS_PER_STEP
    n_hblk = pl.num_programs(1)
    t = pl.program_id(0) * n_hblk + pl.program_id(1)

    def ctx_copies(tt):
        return [pltpu.make_async_copy(c.at[tt // n_hblk, :, (tt % n_hblk) * hps + hh, :], cbuf.at[tt % 2, hh, n],
                                      csem.at[tt % 2])
                for hh in range(hps) for n, c in enumerate((kc_hbm, vc_hbm))]

    @pl.when(t == 0)
    def _():
        for c in ctx_copies(t):
            c.start()

    @pl.when(t + 1 < pl.num_programs(0) * n_hblk)
    def _():
        for c in ctx_copies(t + 1):
            c.start()

    qc = lax.broadcasted_iota(jnp.int32, (w, LANES), 0)
    lane = lax.broadcasted_iota(jnp.int32, (w, LANES), 1)
    c0 = jnp.clip(qc - NA_COLS // 2, 0, w - NA_COLS)
    ok_l = (lane >= c0) & (lane < c0 + NA_COLS)
    ok_r = (lane - w >= c0) & (lane - w < c0 + NA_COLS)
    neg_tile = jnp.full((w, LANES), NEG, F32)
    for c in ctx_copies(t):
        c.wait()
    for hh in range(hps):
        _na_head(q_ref, k_ref, v_ref, cbuf.at[t % 2, hh], rpb_ref.at[hh], o_ref,
                 slice(hh * HEAD_DIM, (hh + 1) * HEAD_DIM), ok_l, ok_r, neg_tile, rows, q_rows)


def _na_head(q_ref, k_ref, v_ref, ctx_ref, rpb_ref, o_ref, sl, ok_l, ok_r, neg_tile, rows, q_rows):
    w = GRID_W
    tile_l, tile_r = [], []
    for d in range(RPB_R):
        base = jnp.broadcast_to(rpb_ref[d:d + 1, :] * LOG2E, (w, LANES))
        left = pltpu.roll(base, LANES - (NA_COLS - 1), 1, stride=1, stride_axis=0)
        right = pltpu.roll(base, w - (NA_COLS - 1), 1, stride=1, stride_axis=0)
        tile_l.append(jnp.where(ok_l, left, NEG))
        tile_r.append(jnp.where(ok_r, right, NEG))

    kc = ctx_ref[0].astype(BF16)
    vc = ctx_ref[1].astype(BF16)
    n_groups = rows // q_rows
    for gi in range(n_groups):
        rs = list(range(gi * q_rows, (gi + 1) * q_rows))
        klo = min(_na_window_start(r, rows) for r in rs) // 2 * 2
        khi = -(-(max(_na_window_start(r, rows) for r in rs) + NA_ROWS) // 2) * 2
        bias_rows = []
        for r in rs:
            r0 = _na_window_start(r, rows)
            tiles = []
            for kr in range(klo, khi, 2):
                ok0 = r0 <= kr < r0 + NA_ROWS
                ok1 = r0 <= kr + 1 < r0 + NA_ROWS
                t0 = tile_l[kr - r + NA_ROWS - 1] if ok0 else neg_tile
                t1 = tile_r[kr + 1 - r + NA_ROWS - 1] if ok1 else neg_tile
                tiles.append(jnp.maximum(t0, t1) if (ok0 or ok1) else neg_tile)
            bias_rows.append(jnp.concatenate(tiles, axis=1))
        bias = jnp.concatenate(bias_rows, axis=0)
        q = q_ref[gi * q_rows * w:(gi + 1) * q_rows * w, sl]
        kl = k_ref[klo * w:khi * w, sl]
        vl = v_ref[klo * w:khi * w, sl]
        s_loc = _dot_nt(q, kl) + bias
        s_ctx = _dot_nt(q, kc)
        o = _softmax_pv([(s_loc, vl), (s_ctx, vc)])
        o_ref[gi * q_rows * w:(gi + 1) * q_rows * w, sl] = o.astype(o_ref.dtype)


def _latent_na(q, k, v, k_ctx, v_ctx, rpb_pad, n_tok, w_cast):
    m, d = q.shape
    nb = m // n_tok
    lc = k_ctx.shape[1]
    rows = n_tok // GRID_W
    hps = NA_HEADS_PER_STEP
    n_hblk = N_HEADS // hps
    tc = w_cast.shape[1] // (nb * n_hblk)
    assert tc * nb * n_hblk == w_cast.shape[1] and tc % LANES == 0
    spec = pl.BlockSpec((n_tok, hps * HEAD_DIM), lambda b, h: (b, h))
    cspec = pl.BlockSpec(memory_space=pl.ANY)
    wspec = pl.BlockSpec((w_cast.shape[0], tc), lambda b, h: (0, b * n_hblk + h))
    return pl.pallas_call(
        functools.partial(_na_kernel, rows=rows, q_rows=4),
        grid=(nb, n_hblk),
        in_specs=[spec, spec, spec, cspec, cspec,
                  pl.BlockSpec((hps, 16, LANES), lambda b, h: (h, 0, 0)), wspec],
        out_specs=[spec, wspec],
        out_shape=[jax.ShapeDtypeStruct((m, d), BF16), jax.ShapeDtypeStruct(w_cast.shape, BF16)],
        scratch_shapes=[pltpu.VMEM((2, hps, 2, lc, HEAD_DIM), F32), pltpu.SemaphoreType.DMA((2,))],
        compiler_params=_cparams("arbitrary", "arbitrary"),
        name="latent_na",
    )(q, k, v, k_ctx, v_ctx, rpb_pad, w_cast)


def _gated_out_kernel(o_ref, z_ref, x_ref, gate_ref, w_ref, y_ref):
    a = o_ref[...].astype(F32) * _silu(z_ref[...].astype(F32))
    y_ref[...] = x_ref[...] + gate_ref[...] * _dot(a.astype(BF16), w_ref[...])


def _gated_out(o, z, x, mod3, cond0, rows_per_cond, w_out, tm):
    m, d = x.shape
    row = pl.BlockSpec((tm, d), lambda i: (i, 0))
    return pl.pallas_call(
        _gated_out_kernel,
        grid=(m // tm,),
        in_specs=[row, row, row,
                  pl.BlockSpec((None, 1, d), lambda i: (cond0 + (i * tm) // rows_per_cond, 0, 2)),
                  pl.BlockSpec((d, d), lambda i: (0, 0), pipeline_mode=pl.Buffered(1))],
        out_specs=row,
        out_shape=jax.ShapeDtypeStruct((m, d), F32),
        compiler_params=_cparams("parallel"),
        name="gated_out",
    )(o, z, x, mod3, w_out)


LANE_TILE = 512


def _slots_per_step(x4):
    nq, m, _, _ = x4.shape
    assert LANE_TILE % (nq * m) == 0 and S5_CHUNK % (LANE_TILE // (nq * m)) == 0
    return LANE_TILE // (nq * m)


def _slot_scratch(d):
    return [pltpu.VMEM((2, LANE_TILE, d), F32), pltpu.SemaphoreType.DMA((2,))]


def _slot_copies(x_hbm, buf, sem, l, to_hbm):
    nq, m, _, _ = x_hbm.shape
    n_x = buf.shape[1] // (nq * m)
    copies = []
    for xi in range(n_x):
        for k in range(m):
            hbm = x_hbm.at[:, k, l * n_x + xi, :]
            vmem = buf.at[l % 2, pl.ds((xi * m + k) * nq, nq), :]
            src, dst = (vmem, hbm) if to_hbm else (hbm, vmem)
            copies.append(pltpu.make_async_copy(src, dst, sem.at[l % 2]))
    return copies


def _fetch_slots(x_hbm, buf, sem):
    l = pl.program_id(0)

    @pl.when(l == 0)
    def _():
        for c in _slot_copies(x_hbm, buf, sem, l, False):
            c.start()

    @pl.when(l + 1 < pl.num_programs(0))
    def _():
        for c in _slot_copies(x_hbm, buf, sem, l + 1, False):
            c.start()

    for c in _slot_copies(x_hbm, buf, sem, l, False):
        c.wait()
    return buf[l % 2]


def _store_slots(res, o_hbm, buf, sem):
    l = pl.program_id(0)

    def wait(ll):
        for c in _slot_copies(o_hbm, buf, sem, ll, True):
            c.wait()

    @pl.when(l >= 2)
    def _():
        wait(l - 2)

    buf[l % 2] = res
    for c in _slot_copies(o_hbm, buf, sem, l, True):
        c.start()

    @pl.when(l == pl.num_programs(0) - 1)
    def _():
        @pl.when(l >= 1)
        def _():
            wait(l - 1)

        wait(l)


def _cond_rows(ref, n_rows, nq, n_cond):
    out = ref[n_cond - 1]
    if n_cond > 1:
        b = (lax.broadcasted_iota(jnp.int32, (n_rows, 1), 0) % nq) // (nq // n_cond)
        for i in range(n_cond - 2, -1, -1):
            out = jnp.where(b == i, ref[i], out)
    return out


def _inproj_t_kernel(x_hbm, g_ref, shift_ref, scale_ref, wu_ref, wz_ref, u_ref, z_ref, buf, sem, *, n_cond):
    nq = x_hbm.shape[0]
    x = _fetch_slots(x_hbm, buf, sem)
    n_rows = x.shape[0]
    h = _norm_mod(x, g_ref[...], _cond_rows(shift_ref, n_rows, nq, n_cond),
                  _cond_rows(scale_ref, n_rows, nq, n_cond))
    u_ref[...] = _dot(h, wu_ref[...]).T.astype(u_ref.dtype)
    z_ref[...] = _dot(h, wz_ref[...]).astype(z_ref.dtype)


def _inproj_t(x4, g, mod3, cond0, n_cond, w_in):
    nq, m, _, d = x4.shape
    n_tok = nq * m * S5_CHUNK
    _slots_per_step(x4)

    def mspec(col):
        return pl.BlockSpec((n_cond, 1, d), lambda l: (cond0 // n_cond, 0, col))

    def wspec(col):
        return pl.BlockSpec((d, d), lambda l: (0, col), pipeline_mode=pl.Buffered(1))

    return pl.pallas_call(
        functools.partial(_inproj_t_kernel, n_cond=n_cond),
        grid=(n_tok // LANE_TILE,),
        in_specs=[pl.BlockSpec(memory_space=pl.ANY),
                  pl.BlockSpec((1, d), lambda l: (0, 0)), mspec(0), mspec(1), wspec(0), wspec(1)],
        out_specs=[pl.BlockSpec((d, LANE_TILE), lambda l: (0, l)), pl.BlockSpec((LANE_TILE, d), lambda l: (l, 0))],
        out_shape=[jax.ShapeDtypeStruct((d, n_tok), BF16), jax.ShapeDtypeStruct((n_tok, d), BF16)],
        scratch_shapes=_slot_scratch(d),
        compiler_params=_cparams("arbitrary"),
        name="inproj_t",
    )(x4, g.reshape(1, d), mod3, mod3, w_in, w_in)


def _cmul(ar, ai, br, bi):
    return ar * br - ai * bi, ar * bi + ai * br


def _split_bf16(a):
    hi = a.astype(BF16)
    return hi, (a - hi.astype(F32)).astype(BF16)


def _s5_powers(ar, ai):
    width = ar.shape[-1]
    pw = [(jnp.ones_like(ar), jnp.zeros_like(ar))]
    for _ in range(S5_CHUNK):
        pw.append(_cmul(pw[-1][0], pw[-1][1], ar, ai))
    is_fwd = lax.broadcasted_iota(jnp.int32, (1, width), 1) < width // 2

    def pattern(fwd_ascending):
        out = []
        for part in range(2):
            blocks = []
            for j in range(S5_CHUNK):
                ef, eb = (j, S5_CHUNK - 1 - j) if fwd_ascending else (S5_CHUNK - 1 - j, j)
                blocks.append(jnp.broadcast_to(jnp.where(is_fwd, pw[ef][part], pw[eb][part]), (S5_CH, width)))
            out.append(jnp.concatenate(blocks, axis=0))
        return out

    return pattern(True), pattern(False), pw[S5_CHUNK]


S5_GROUPS_PER_STEP = 8


def _s5_kernel(xp_ref, xs_ref, *rest, nb_p, nb_s, n_seg_s, n_cast):
    group_in, cast_in = rest[:9], rest[9:9 + n_cast]
    yp_ref, ys_ref, st_ref = rest[9 + n_cast:12 + n_cast]
    cast_out = rest[12 + n_cast:12 + 2 * n_cast]
    scratch = rest[12 + 2 * n_cast:]
    for gi in range(S5_GROUPS_PER_STEP):
        ch = pl.ds(gi * S5_CH, S5_CH)
        _s5_group(xp_ref.at[ch], xs_ref.at[ch], *[r.at[gi] for r in group_in], yp_ref.at[ch], ys_ref.at[ch],
                  st_ref.at[gi], *[r.at[gi] for r in scratch], nb_p=nb_p, nb_s=nb_s, n_seg_s=n_seg_s)
    for src, dst in zip(cast_in, cast_out):
        dst[...] = src[...].astype(dst.dtype)


def _s5_group(xp_ref, xs_ref, h0_ref, are_ref, aim_ref, ldt_ref, bre_ref, bim_ref, cre_ref, cim_ref, dsk_ref,
              yp_ref, ys_ref, st_ref, w_ref, ws_ref, wc_ref, s_ref, f_ref, *, nb_p, nb_s, n_seg_s):
    p = S5_P
    kc = S5_CHUNK * S5_CH
    a_re = are_ref[...]
    a_im = aim_ref[...]
    dt = jnp.exp(ldt_ref[...])
    mag = jnp.exp(a_re * dt)
    ab_re = mag * jnp.cos(a_im * dt)
    ab_im = mag * jnp.sin(a_im * dt)
    den = a_re * a_re + a_im * a_im
    nr = ab_re - 1.0
    f_re = (nr * a_re + ab_im * a_im) / den
    f_im = (ab_im * a_re - nr * a_im) / den
    bbt_re, bbt_im = _cmul(f_re, f_im, bre_ref[...], bim_ref[...])

    (pg_r, pg_i), (ps_r, ps_i), (ac_r, ac_i) = _s5_powers(ab_re, ab_im)
    c_re = jnp.concatenate([cre_ref[...]] * S5_CHUNK, axis=0)
    c_im = jnp.concatenate([cim_ref[...]] * S5_CHUNK, axis=0)
    bt_re = jnp.concatenate([bbt_re] * S5_CHUNK, axis=0)
    bt_im = jnp.concatenate([bbt_im] * S5_CHUNK, axis=0)

    g_re, g_im = _cmul(pg_r, pg_i, c_re, c_im)
    fwd16 = lax.broadcasted_iota(jnp.int32, (S5_CH, 2 * p), 1) < p
    g_hi, g_lo = _split_bf16(jnp.concatenate([g_re, g_im], axis=1))
    r0 = []
    for d in range(2):
        msk = fwd16 if d == 0 else jnp.logical_not(fwd16)
        b_hi, b_lo = _split_bf16(
            jnp.concatenate([jnp.where(msk, bbt_re, 0.0), jnp.where(msk, -bbt_im, 0.0)], axis=1))
        r0.append(_dot_nt(b_hi, g_hi) + (_dot_nt(b_hi, g_lo) + _dot_nt(b_lo, g_hi)))
    lane = lax.broadcasted_iota(jnp.int32, (S5_CH, kc), 1)
    for s in range(S5_CHUNK):
        f = jnp.where(lane >= S5_CH * s, pltpu.roll(r0[0], S5_CH * s, 1), 0.0) if s else r0[0]
        b = jnp.where(lane < S5_CH * (s + 1), pltpu.roll(r0[1], (S5_CH * (s + 1)) % kc, 1), 0.0)
        w_ref[s * S5_CH:(s + 1) * S5_CH, :] = (f + b).astype(BF16)
    e_re, e_im = _cmul(ps_r, ps_i, bt_re, bt_im)
    ws_ref[:, 0:2 * p] = e_re.astype(BF16)
    ws_ref[:, 2 * p:4 * p] = e_im.astype(BF16)
    g1_re, g1_im = _cmul(g_re, g_im, ab_re, ab_im)
    wc_ref[:, 0:2 * p] = g1_re.astype(BF16)
    wc_ref[:, 2 * p:4 * p] = (-g1_im).astype(BF16)

    def run(xt_ref, yt_ref, nb, n_seg, hr, hm):
        rows = xt_ref.shape[1] // S5_CHUNK
        nbx = nb * n_seg
        m = rows // nbx
        xt = jnp.concatenate([xt_ref[:, s * rows:(s + 1) * rows] for s in range(S5_CHUNK)], axis=0)
        x = xt.T
        y = _dot(x, w_ref[...]) + dsk_ref[...] * x.astype(F32)
        s_all = _dot(x, ws_ref[...])
        s_ref[0, 0:rows, :] = s_all[:, 0:2 * p]
        s_ref[1, 0:rows, :] = s_all[:, 2 * p:4 * p]

        def step(ar, ai, hr, hm, plane_r, plane_i, rf, rb):
            isf = lax.broadcasted_iota(jnp.int32, hr.shape, 1) < p
            sr = jnp.where(isf, plane_r[rf, :], plane_r[rb, :])
            sm = jnp.where(isf, plane_i[rf, :], plane_i[rb, :])
            return ar * hr - ai * hm + sr, ar * hm + ai * hr + sm

        def scan(hr, hm, store):
            arb = jnp.broadcast_to(ac_r, (nbx, 2 * p))
            aib = jnp.broadcast_to(ac_i, (nbx, 2 * p))
            for k in range(m):
                rf = pl.ds(k * nbx, nbx)
                rb = pl.ds((m - 1 - k) * nbx, nbx)
                if store:
                    s_ref[2, rf, :] = hr
                    s_ref[3, rf, :] = hm
                    s_ref[4, rb, :] = hr
                    s_ref[5, rb, :] = hm
                hr, hm = step(arb, aib, hr, hm, s_ref.at[0], s_ref.at[1], rf, rb)
            return hr, hm

        if n_seg > 1:
            zeros = jnp.zeros((nbx, 2 * p), F32)
            f_ref[0], f_ref[1] = scan(zeros, zeros, False)
            sr_, si_ = ac_r, ac_i
            for _ in range(m - 1):
                sr_, si_ = _cmul(sr_, si_, ac_r, ac_i)
            sr_ = jnp.broadcast_to(sr_, (nb, 2 * p))
            si_ = jnp.broadcast_to(si_, (nb, 2 * p))
            for j in range(n_seg):
                rf = pl.ds(j, nb, stride=n_seg)
                rb = pl.ds(n_seg - 1 - j, nb, stride=n_seg)
                f_ref[2, rf, :] = hr
                f_ref[3, rf, :] = hm
                f_ref[4, rb, :] = hr
                f_ref[5, rb, :] = hm
                hr, hm = step(sr_, si_, hr, hm, f_ref.at[0], f_ref.at[1], rf, rb)
            isf = lax.broadcasted_iota(jnp.int32, (nbx, 2 * p), 1) < p
            hr = jnp.where(isf, f_ref[2], f_ref[4])
            hm = jnp.where(isf, f_ref[3], f_ref[5])
        hr, hm = scan(hr, hm, True)
        isf_rows = lax.broadcasted_iota(jnp.int32, (rows, 2 * p), 1) < p
        h_prev = jnp.concatenate([jnp.where(isf_rows, s_ref[2, 0:rows, :], s_ref[4, 0:rows, :]),
                                  jnp.where(isf_rows, s_ref[3, 0:rows, :], s_ref[5, 0:rows, :])], axis=1)
        yt = _gelu_tanh(y + _dot_nt(h_prev.astype(BF16), wc_ref[...])).astype(yt_ref.dtype).T
        for t in range(S5_CHUNK):
            yt_ref[:, t * rows:(t + 1) * rows] = yt[t * S5_CH:(t + 1) * S5_CH, :]
        return hr, hm

    zeros = jnp.zeros((nb_p, 2 * p), F32)
    hr, hm = run(xp_ref, yp_ref, nb_p, 1, zeros, zeros)
    st_ref[:, 0:2 * p] = hr
    st_ref[:, 2 * p:4 * p] = hm
    run(xs_ref, ys_ref, nb_s, n_seg_s, h0_ref[:, 0:2 * p], h0_ref[:, 2 * p:4 * p])


def _s5(utp, uts, h0, a_re, a_im, log_dt, bt_re, bt_im, c_re, c_im, d_skip, nb_p, nb_s, n_seg_s, casts):
    d, lanes_p = utp.shape
    lanes_s = uts.shape[1]
    g = d // S5_CH
    rows_p = lanes_p // S5_CHUNK
    rows_s = lanes_s // S5_CHUNK
    kc = S5_CHUNK * S5_CH
    p = S5_P

    gps = S5_GROUPS_PER_STEP

    def gspec(shape):
        return pl.BlockSpec((gps,) + shape, lambda i: (i, 0, 0))

    def tspec(lanes):
        return pl.BlockSpec((gps * S5_CH, lanes), lambda i: (i, 0))

    n_steps = g // gps
    cast_specs = []
    for wc in casts:
        tc = wc.shape[1] // n_steps
        assert tc * n_steps == wc.shape[1] and tc % LANES == 0
        cast_specs.append(pl.BlockSpec((wc.shape[0], tc), lambda i: (0, i)))
    return pl.pallas_call(
        functools.partial(_s5_kernel, nb_p=nb_p, nb_s=nb_s, n_seg_s=n_seg_s, n_cast=len(casts)),
        grid=(n_steps,),
        in_specs=[tspec(lanes_p), tspec(lanes_s), gspec((nb_s, 4 * p)),
                  gspec((1, 2 * p)), gspec((1, 2 * p)), gspec((1, 2 * p)),
                  gspec((S5_CH, 2 * p)), gspec((S5_CH, 2 * p)), gspec((S5_CH, 2 * p)), gspec((S5_CH, 2 * p)),
                  gspec((1, kc))] + cast_specs,
        out_specs=[tspec(lanes_p), tspec(lanes_s), gspec((nb_p, 4 * p))] + cast_specs,
        out_shape=[jax.ShapeDtypeStruct((d, lanes_p), BF16),
                   jax.ShapeDtypeStruct((d, lanes_s), BF16),
                   jax.ShapeDtypeStruct((g, nb_p, 4 * p), F32)]
        + [jax.ShapeDtypeStruct(wc.shape, BF16) for wc in casts],
        scratch_shapes=[pltpu.VMEM((gps, kc, kc), BF16), pltpu.VMEM((gps, kc, 4 * p), BF16),
                        pltpu.VMEM((gps, kc, 4 * p), BF16),
                        pltpu.VMEM((gps, 6, max(rows_p, rows_s), 2 * p), F32),
                        pltpu.VMEM((gps, 6, nb_s * n_seg_s, 2 * p), F32)],
        compiler_params=_cparams("parallel"),
        name="s5",
    )(utp, uts, h0, a_re, a_im, log_dt, bt_re, bt_im, c_re, c_im,
      jnp.tile(d_skip.reshape(g, 1, S5_CH), (1, 1, S5_CHUNK)), *casts)


def _l1_tail_kernel(yt0_ref, ytn_ref, z_ref, wglu_ref, b_ref, x_hbm, gate_ref, fg_ref, wout_ref, o_hbm,
                    yb_ref, a_ref, xbuf, xsem, obuf, osem, *, tn, n_cond):
    d = wglu_ref.shape[0]
    nq = x_hbm.shape[0]
    l = pl.program_id(0)
    slot = l % 2
    _fetch_slots(x_hbm, xbuf, xsem)

    @pl.when(l == 0)
    def _():
        yb_ref[0] = yt0_ref[...].T

    for c in range(d // tn):
        cs = slice(c * tn, (c + 1) * tn)
        yb = yb_ref[slot]
        gl = _dot(yb, wglu_ref[:, cs]) + b_ref[:, cs]
        a = yb[:, cs].astype(F32) * _sigmoid(gl) * _silu(z_ref[:, cs].astype(F32))
        a_ref[:, cs] = a.astype(BF16)
        yb_ref[1 - slot, :, cs] = ytn_ref[cs, :].T

    x = xbuf[slot]
    xn = x + _cond_rows(gate_ref, x.shape[0], nq, n_cond) * _dot(a_ref[...], wout_ref[...])
    res = xn * lax.rsqrt(jnp.mean(xn * xn, axis=-1, keepdims=True) + EPS) * fg_ref[...]
    _store_slots(res, o_hbm, obuf, osem)


def _l1_tail(yt, z, x4, w_glu, b_glu, mod3, cond0, n_cond, final_g, w_out, tn):
    nq, m, _, d = x4.shape
    _slots_per_step(x4)
    tm = LANE_TILE
    ni = nq * m * S5_CHUNK // tm
    wspec = pl.BlockSpec((d, d), lambda i: (0, 0), pipeline_mode=pl.Buffered(1))
    vec = pl.BlockSpec((1, d), lambda i: (0, 0))
    return pl.pallas_call(
        functools.partial(_l1_tail_kernel, tn=tn, n_cond=n_cond),
        grid=(ni,),
        in_specs=[pl.BlockSpec((d, tm), lambda i: (0, 0), pipeline_mode=pl.Buffered(1)),
                  pl.BlockSpec((d, tm), lambda i: (0, jnp.minimum(i + 1, ni - 1))),
                  pl.BlockSpec((tm, d), lambda i: (i, 0)),
                  wspec, vec,
                  pl.BlockSpec(memory_space=pl.ANY),
                  pl.BlockSpec((n_cond, 1, d), lambda i: (cond0 // n_cond, 0, 2)),
                  vec, wspec],
        out_specs=pl.BlockSpec(memory_space=pl.ANY),
        out_shape=jax.ShapeDtypeStruct(x4.shape, F32),
        scratch_shapes=[pltpu.VMEM((2, tm, d), BF16), pltpu.VMEM((tm, d), BF16)] + 2 * _slot_scratch(d),
        compiler_params=_cparams("arbitrary"),
        name="l1_tail",
    )(yt, yt, z, w_glu, b_glu.reshape(1, d), x4, mod3, final_g.reshape(1, d), w_out)


def _both_dirs(a):
    _, g, r, p = a.shape
    return a.transpose(1, 2, 0, 3).reshape(g, r, 2 * p)


def kernel(x_prompt, x_sample, cache_l0_k, cache_l0_v, state_l1_s5, c, c_ctx, l0_norm_g, l0_w_ada, l0_b_ada, l0_w_in, l0_rpb, l0_w_out, l1_norm_g, l1_w_ada, l1_b_ada, l1_w_in, l1_a_re, l1_a_im, l1_log_dt, l1_b_re, l1_b_im, l1_c_re, l1_c_im, l1_d, l1_w_glu, l1_b_glu, l1_w_out, final_norm_g):
    bp, seq, d = x_prompt.shape
    bs, n_tok, _ = x_sample.shape
    g = d // S5_CH
    p = S5_P
    xp = x_prompt.reshape(bp * seq, d)
    xs = x_sample.reshape(bs * n_tok, d)

    ctx = bs
    cond8 = jnp.zeros((8, d), F32).at[0:bs].set(c).at[ctx].set(c_ctx)
    mod0 = _ada(cond8, (l0_w_ada,), (l0_b_ada,))[0].reshape(8, 1, 3 * d)

    w_in0 = l0_w_in.astype(BF16)

    qkvz = ((0, BF16, QK_SCALE), (1, BF16, 1.0), (2, BF16, 1.0), (3, BF16, 1.0))
    qs, ks, vs, zs, mod1, w_out0 = _inproj(xs, l0_norm_g, mod0, 0, n_tok, w_in0, qkvz, 1024, 256,
                                           ada_next=(cond8, l1_w_ada, l1_b_ada), casts=(l0_w_out,))
    mod1 = mod1.reshape(8, 1, 3 * d)
    qp, kpb, vpb, zp, kp, vp = _inproj(xp, l0_norm_g, mod0, ctx, bp * seq, w_in0, qkvz, 1024, 256,
                                       head_split=(1, 2))
    rpb_pad = jnp.zeros((N_HEADS, 16, LANES), F32).at[:, :RPB_R, :RPB_C].set(l0_rpb)
    os_, w_in1 = _latent_na(qs, ks, vs, cache_l0_k, cache_l0_v, rpb_pad, n_tok, l1_w_in)
    n_seg_s = max(1, n_tok // S5_CHUNK // S5_SEG)
    x1p = _ctx_layer(qp, kpb, vpb, zp, xp, mod0, ctx, w_out0, seq, 512).reshape(bp, seq // S5_CHUNK, S5_CHUNK, d)
    x1s = _gated_out(os_, zs, xs, mod0, 0, n_tok, w_out0, 512).reshape(bs * n_seg_s, -1, S5_CHUNK, d)

    utp, zp1 = _inproj_t(x1p, l1_norm_g, mod1, ctx, 1, w_in1)
    uts, zs1 = _inproj_t(x1s, l1_norm_g, mod1, 0, bs, w_in1)
    h0 = state_l1_s5.transpose(3, 0, 2, 1, 4).reshape(g, bs, 4 * p)
    log_dt = jnp.broadcast_to(l1_log_dt[:, :, None, None], (2, g, 1, p))
    ytp, yts, st, w_glu, w_out1 = _s5(
        utp, uts, h0,
        _both_dirs(l1_a_re[:, :, None, :]), _both_dirs(l1_a_im[:, :, None, :]), _both_dirs(log_dt),
        _both_dirs(l1_b_re.transpose(0, 1, 3, 2)), _both_dirs(l1_b_im.transpose(0, 1, 3, 2)),
        _both_dirs(l1_c_re), _both_dirs(l1_c_im), l1_d, bp, bs, n_seg_s, (l1_w_glu, l1_w_out))
    y_prompt = _l1_tail(ytp, zp1, x1p, w_glu, l1_b_glu, mod1, ctx, 1, final_norm_g, w_out1, 512)
    y_sample = _l1_tail(yts, zs1, x1s, w_glu, l1_b_glu, mod1, 0, bs, final_norm_g, w_out1, 512)
    new_state = st.reshape(g, bp, 2, 2, p).transpose(1, 3, 2, 0, 4)
    return (y_prompt.reshape(bp, seq, d), y_sample.reshape(bs, n_tok, d),
            kp.reshape(bp, seq, N_HEADS, HEAD_DIM), vp.reshape(bp, seq, N_HEADS, HEAD_DIM), new_state)
```

```python
import functools

import jax
import jax.numpy as jnp
from jax import lax
from jax.experimental import pallas as pl
from jax.experimental.pallas import tpu as pltpu

F32 = jnp.float32
BF16 = jnp.bfloat16

EPS = 1e-6
N_HEADS = 16
HEAD_DIM = 128
GRID_W = 64
NA_ROWS = 8
NA_COLS = 16
RPB_R = 2 * NA_ROWS - 1
RPB_C = 2 * NA_COLS - 1
S5_CH = 16
S5_P = 64
S5_CHUNK = 16
S5_SEG = 8
NEG = -1e30
LANES = 128
VMEM_LIMIT = 56 * 1024 * 1024


def _cparams(*sem):
    return pltpu.CompilerParams(dimension_semantics=sem, vmem_limit_bytes=VMEM_LIMIT)


def _dot(a, b):
    return jnp.dot(a, b, preferred_element_type=F32)


def _dot_nt(a, b, precision=None):
    return lax.dot_general(a, b, (((1,), (1,)), ((), ())), preferred_element_type=F32,
                           precision=precision)


def _sigmoid(x):
    return 0.5 * jnp.tanh(0.5 * x) + 0.5


def _silu(x):
    return x * _sigmoid(x)


def _gelu_tanh(x):
    c = 0.7978845608028654
    return (0.5 * x) * (1.0 + jnp.tanh(x * (c + (c * 0.044715) * (x * x))))


def _ada_kernel(cond_ref, *refs, n_layers):
    w_refs, b_refs, o_ref = refs[:n_layers], refs[n_layers:2 * n_layers], refs[2 * n_layers]
    s = _silu(cond_ref[...]).astype(BF16)
    layer = pl.program_id(0)
    for i in range(n_layers):
        @pl.when(layer == i)
        def _(i=i):
            o_ref[...] = _dot(s, w_refs[i][...].astype(BF16)) + b_refs[i][...]


def _ada(cond8, w_adas, b_adas):
    n_layers = len(w_adas)
    d, n = w_adas[0].shape
    tn = 1024
    nj = n // tn

    def own(i):
        return lambda l, j: (0, jnp.where(l == i, j, jnp.where(l < i, 0, nj - 1)))

    return pl.pallas_call(
        functools.partial(_ada_kernel, n_layers=n_layers),
        grid=(n_layers, nj),
        in_specs=[pl.BlockSpec((8, d), lambda l, j: (0, 0))]
        + [pl.BlockSpec((d, tn), own(i)) for i in range(n_layers)]
        + [pl.BlockSpec((1, tn), own(i)) for i in range(n_layers)],
        out_specs=pl.BlockSpec((None, 8, tn), lambda l, j: (l, 0, j)),
        out_shape=jax.ShapeDtypeStruct((n_layers, 8, n), F32),
        compiler_params=_cparams("arbitrary", "arbitrary"),
        name="ada",
    )(cond8, *w_adas, *[b.reshape(1, n) for b in b_adas])


def _norm_mod(x, g, shift, scale):
    y = x * lax.rsqrt(jnp.mean(x * x, axis=-1, keepdims=True) + EPS) * g
    return (y * (1.0 + scale) + shift).astype(BF16)


def _inproj_kernel(x0_ref, xn_ref, g_ref, shift0_ref, scale0_ref, shiftn_ref, scalen_ref, *rest, out_split,
                   out_scale, head_split, n_cast):
    n_split = max(out_split + head_split) + 1
    w_refs = rest[:n_split]
    rest = rest[n_split:]
    cast_in, rest = rest[:n_cast], rest[n_cast:]
    o_refs = rest[:len(out_split)]
    hd_refs = rest[len(out_split):len(out_split) + len(head_split)]
    rest = rest[len(out_split) + len(head_split):]
    for src, dst in zip(cast_in, rest[:n_cast]):
        dst[...] = src[...].astype(dst.dtype)
    rest = rest[n_cast:]
    h_ref = rest[0]
    i = pl.program_id(0)
    j = pl.program_id(1)
    nj = pl.num_programs(1)
    slot = i % 2

    @pl.when((i == 0) & (j == 0))
    def _():
        h_ref[0] = _norm_mod(x0_ref[...], g_ref[...], shift0_ref[...], scale0_ref[...])

    if head_split:
        stage_ref, sem = rest[1:]
        tm, tn = stage_ref.shape[2:]
        t = i * nj + j

        def head_copies(tt):
            ii, jj = tt // nj, tt % nj
            return [pltpu.make_async_copy(stage_ref.at[tt % 2, n, :, pl.ds(hh * HEAD_DIM, HEAD_DIM)],
                                          hd.at[pl.ds(ii * tm, tm), jj * (tn // HEAD_DIM) + hh, :], sem.at[tt % 2])
                    for n, hd in enumerate(hd_refs) for hh in range(tn // HEAD_DIM)]

        @pl.when(t >= 2)
        def _():
            for c in head_copies(t - 2):
                c.wait()

    h = h_ref[slot]
    for s, w_ref in enumerate(w_refs):
        r = _dot(h, w_ref[...])
        for o_ref, o_s, o_c in zip(o_refs, out_split, out_scale):
            if o_s == s:
                o_ref[...] = (r if o_c == 1.0 else r * o_c).astype(o_ref.dtype)
        if s in head_split:
            stage_ref[t % 2, head_split.index(s)] = r
    rs = xn_ref.shape[0]
    h_ref[1 - slot, pl.ds(pl.multiple_of(j * rs, rs), rs), :] = _norm_mod(
        xn_ref[...], g_ref[...], shiftn_ref[...], scalen_ref[...])

    if head_split:
        for c in head_copies(t):
            c.start()

        @pl.when(t == pl.num_programs(0) * nj - 1)
        def _():
            @pl.when(t >= 1)
            def _():
                for c in head_copies(t - 1):
                    c.wait()

            for c in head_copies(t):
                c.wait()


def _inproj(x, g, mod3, cond0, rows_per_cond, w, outs, tm, tn, head_split=(), casts=()):
    m, d = x.shape
    out_split = tuple(o[0] for o in outs)
    out_dtypes = [o[1] for o in outs]
    out_scale = tuple(float(o[2]) for o in outs)
    head_split = tuple(head_split)
    n_split = max(out_split + head_split) + 1
    ni = m // tm
    nj = d // tn

    def nxt(i):
        return jnp.minimum(i + 1, ni - 1)

    def cond_of(i):
        return cond0 + (i * tm) // rows_per_cond

    in_specs = [pl.BlockSpec((tm, d), lambda i, j: (0, 0), pipeline_mode=pl.Buffered(1)),
                pl.BlockSpec((tm // nj, d), lambda i, j: (nxt(i) * nj + j, 0)),
                pl.BlockSpec((1, d), lambda i, j: (0, 0)),
                pl.BlockSpec((None, 1, d), lambda i, j: (cond_of(0), 0, 0)),
                pl.BlockSpec((None, 1, d), lambda i, j: (cond_of(0), 0, 1)),
                pl.BlockSpec((None, 1, d), lambda i, j: (cond_of(nxt(i)), 0, 0)),
                pl.BlockSpec((None, 1, d), lambda i, j: (cond_of(nxt(i)), 0, 1))]
    for s in range(n_split):
        in_specs.append(pl.BlockSpec((d, tn), lambda i, j, s=s: (0, s * nj + j)))
    out_specs = ([pl.BlockSpec((tm, tn), lambda i, j: (i, j)) for _ in outs]
                 + [pl.BlockSpec(memory_space=pl.ANY)] * len(head_split))
    out_shape = ([jax.ShapeDtypeStruct((m, d), dt) for dt in out_dtypes]
                 + [jax.ShapeDtypeStruct((m, d // HEAD_DIM, HEAD_DIM), F32)] * len(head_split))
    operands = [x, x, g.reshape(1, d), mod3, mod3, mod3, mod3] + [w] * n_split
    for wc in casts:
        tc = wc.shape[1] // (ni * nj)
        assert tc * ni * nj == wc.shape[1] and tc % LANES == 0
        spec = pl.BlockSpec((wc.shape[0], tc), lambda i, j: (0, i * nj + j))
        in_specs.append(spec)
        out_specs.append(spec)
        out_shape.append(jax.ShapeDtypeStruct(wc.shape, BF16))
        operands.append(wc)
    scratch = [pltpu.VMEM((2, tm, d), BF16)]
    if head_split:
        scratch += [pltpu.VMEM((2, len(head_split), tm, tn), F32), pltpu.SemaphoreType.DMA((2,))]
    return pl.pallas_call(
        functools.partial(_inproj_kernel, out_split=out_split, out_scale=out_scale, head_split=head_split,
                          n_cast=len(casts)),
        grid=(ni, nj),
        in_specs=in_specs,
        out_specs=out_specs,
        out_shape=out_shape,
        scratch_shapes=scratch,
        compiler_params=_cparams("arbitrary", "arbitrary"),
        name="inproj",
    )(*operands)


LOG2E = 1.4426950408889634
QK_SCALE = HEAD_DIM ** -0.5 * LOG2E


def _softmax_pv(parts):
    m = parts[0][0].max(axis=-1, keepdims=True)
    for s, _ in parts[1:]:
        m = jnp.maximum(m, s.max(axis=-1, keepdims=True))
    l = None
    o = None
    for s, v in parts:
        e = jnp.exp2(s - m)
        li = e.sum(axis=-1, keepdims=True)
        oi = _dot(e.astype(BF16), v)
        l = li if l is None else l + li
        o = oi if o is None else o + oi
    return o / l


def _ctx_layer_kernel(q_ref, k_ref, v_ref, z_ref, x_ref, gate_ref, w_ref, y_ref, a_ref, *, seq):
    for b in range(q_ref.shape[0] // seq):
        rows = slice(b * seq, (b + 1) * seq)
        for h in range(N_HEADS):
            sl = slice(h * HEAD_DIM, (h + 1) * HEAD_DIM)
            o = _softmax_pv([(_dot_nt(q_ref[rows, sl], k_ref[rows, sl]), v_ref[rows, sl])])
            a_ref[rows, sl] = (o * _silu(z_ref[rows, sl].astype(F32))).astype(BF16)
    y_ref[...] = x_ref[...] + gate_ref[...] * _dot(a_ref[...], w_ref[...])


def _ctx_layer(q, k, v, z, x, mod3, cond, w_out, seq, tm):
    m, d = x.shape
    row = pl.BlockSpec((tm, d), lambda i: (i, 0))
    return pl.pallas_call(
        functools.partial(_ctx_layer_kernel, seq=seq),
        grid=(m // tm,),
        in_specs=[row, row, row, row, row,
                  pl.BlockSpec((None, 1, d), lambda i: (cond, 0, 2)),
                  pl.BlockSpec((d, d), lambda i: (0, 0), pipeline_mode=pl.Buffered(1))],
        out_specs=row,
        out_shape=jax.ShapeDtypeStruct((m, d), F32),
        scratch_shapes=[pltpu.VMEM((tm, d), BF16)],
        compiler_params=_cparams("parallel"),
        name="ctx_layer",
    )(q, k, v, z, x, mod3, w_out)


def _na_window_start(r, rows):
    return min(max(r - NA_ROWS // 2, 0), rows - NA_ROWS)


NA_HEADS_PER_STEP = 4


def _na_kernel(q_ref, k_ref, v_ref, kc_hbm, vc_hbm, rpb_ref, wcast_ref, cond_ref, wada_ref, bada_ref,
               o_ref, wcast_bf_ref, mod_ref, cbuf, csem, *, rows, q_rows):
    wcast_bf_ref[...] = wcast_ref[...].astype(BF16)
    mod_ref[...] = _dot(_silu(cond_ref[...]).astype(BF16), wada_ref[...].astype(BF16)) + bada_ref[...]
    w = GRID_W
    hps = NA_HEADS_PER_STEP
    n_hblk = pl.num_programs(1)
    t = pl.program_id(0) * n_hblk + pl.program_id(1)

    def ctx_copies(tt):
        return [pltpu.make_async_copy(c.at[tt // n_hblk, :, (tt % n_hblk) * hps + hh, :], cbuf.at[tt % 2, hh, n],
                                      csem.at[tt % 2])
                for hh in range(hps) for n, c in enumerate((kc_hbm, vc_hbm))]

    @pl.when(t == 0)
    def _():
        for c in ctx_copies(t):
            c.start()

    @pl.when(t + 1 < pl.num_programs(0) * n_hblk)
    def _():
        for c in ctx_copies(t + 1):
            c.start()

    qc = lax.broadcasted_iota(jnp.int32, (w, LANES), 0)
    lane = lax.broadcasted_iota(jnp.int32, (w, LANES), 1)
    c0 = jnp.clip(qc - NA_COLS // 2, 0, w - NA_COLS)
    ok_l = (lane >= c0) & (lane < c0 + NA_COLS)
    ok_r = (lane - w >= c0) & (lane - w < c0 + NA_COLS)
    neg_tile = jnp.full((w, LANES), NEG, F32)
    for c in ctx_copies(t):
        c.wait()
    for hh in range(hps):
        _na_head(q_ref, k_ref, v_ref, cbuf.at[t % 2, hh], rpb_ref.at[hh], o_ref,
                 slice(hh * HEAD_DIM, (hh + 1) * HEAD_DIM), ok_l, ok_r, neg_tile, rows, q_rows)


def _na_head(q_ref, k_ref, v_ref, ctx_ref, rpb_ref, o_ref, sl, ok_l, ok_r, neg_tile, rows, q_rows):
    w = GRID_W
    tile_l, tile_r = [], []
    for d in range(RPB_R):
        base = jnp.broadcast_to(rpb_ref[d:d + 1, :] * LOG2E, (w, LANES))
        left = pltpu.roll(base, LANES - (NA_COLS - 1), 1, stride=1, stride_axis=0)
        right = pltpu.roll(base, w - (NA_COLS - 1), 1, stride=1, stride_axis=0)
        tile_l.append(jnp.where(ok_l, left, NEG))
        tile_r.append(jnp.where(ok_r, right, NEG))

    kc = ctx_ref[0].astype(BF16)
    vc = ctx_ref[1].astype(BF16)
    n_groups = rows // q_rows
    for gi in range(n_groups):
        rs = list(range(gi * q_rows, (gi + 1) * q_rows))
        klo = min(_na_window_start(r, rows) for r in rs) // 2 * 2
        khi = -(-(max(_na_window_start(r, rows) for r in rs) + NA_ROWS) // 2) * 2
        bias_rows = []
        for r in rs:
            r0 = _na_window_start(r, rows)
            tiles = []
            for kr in range(klo, khi, 2):
                ok0 = r0 <= kr < r0 + NA_ROWS
                ok1 = r0 <= kr + 1 < r0 + NA_ROWS
                t0 = tile_l[kr - r + NA_ROWS - 1] if ok0 else neg_tile
                t1 = tile_r[kr + 1 - r + NA_ROWS - 1] if ok1 else neg_tile
                tiles.append(jnp.maximum(t0, t1) if (ok0 or ok1) else neg_tile)
            bias_rows.append(jnp.concatenate(tiles, axis=1))
        bias = jnp.concatenate(bias_rows, axis=0)
        q = q_ref[gi * q_rows * w:(gi + 1) * q_rows * w, sl]
        kl = k_ref[klo * w:khi * w, sl]
        vl = v_ref[klo * w:khi * w, sl]
        s_loc = _dot_nt(q, kl) + bias
        s_ctx = _dot_nt(q, kc)
        o = _softmax_pv([(s_loc, vl), (s_ctx, vc)])
        o_ref[gi * q_rows * w:(gi + 1) * q_rows * w, sl] = o.astype(o_ref.dtype)


def _latent_na(q, k, v, k_ctx, v_ctx, rpb_pad, n_tok, w_cast, ada_next):
    m, d = q.shape
    nb = m // n_tok
    lc = k_ctx.shape[1]
    rows = n_tok // GRID_W
    hps = NA_HEADS_PER_STEP
    n_hblk = N_HEADS // hps
    cond8, w_ada, b_ada = ada_next
    n_ada = w_ada.shape[1]
    tc = w_cast.shape[1] // (nb * n_hblk)
    ta = n_ada // (nb * n_hblk)
    assert tc * nb * n_hblk == w_cast.shape[1] and tc % LANES == 0
    assert ta * nb * n_hblk == n_ada and ta % LANES == 0
    spec = pl.BlockSpec((n_tok, hps * HEAD_DIM), lambda b, h: (b, h))
    cspec = pl.BlockSpec(memory_space=pl.ANY)

    def step_cols(r, t):
        return pl.BlockSpec((r, t), lambda b, h: (0, b * n_hblk + h))

    return pl.pallas_call(
        functools.partial(_na_kernel, rows=rows, q_rows=4),
        grid=(nb, n_hblk),
        in_specs=[spec, spec, spec, cspec, cspec,
                  pl.BlockSpec((hps, 16, LANES), lambda b, h: (h, 0, 0)), step_cols(w_cast.shape[0], tc),
                  pl.BlockSpec((8, d), lambda b, h: (0, 0)), step_cols(d, ta), step_cols(1, ta)],
        out_specs=[spec, step_cols(w_cast.shape[0], tc), step_cols(8, ta)],
        out_shape=[jax.ShapeDtypeStruct((m, d), BF16), jax.ShapeDtypeStruct(w_cast.shape, BF16),
                   jax.ShapeDtypeStruct((8, n_ada), F32)],
        scratch_shapes=[pltpu.VMEM((2, hps, 2, lc, HEAD_DIM), F32), pltpu.SemaphoreType.DMA((2,))],
        compiler_params=_cparams("arbitrary", "arbitrary"),
        name="latent_na",
    )(q, k, v, k_ctx, v_ctx, rpb_pad, w_cast, cond8, w_ada, b_ada.reshape(1, n_ada))


def _gated_out_kernel(o_ref, z_ref, x_ref, gate_ref, w_ref, y_ref):
    a = o_ref[...].astype(F32) * _silu(z_ref[...].astype(F32))
    y_ref[...] = x_ref[...] + gate_ref[...] * _dot(a.astype(BF16), w_ref[...])


def _gated_out(o, z, x, mod3, cond0, rows_per_cond, w_out, tm):
    m, d = x.shape
    row = pl.BlockSpec((tm, d), lambda i: (i, 0))
    return pl.pallas_call(
        _gated_out_kernel,
        grid=(m // tm,),
        in_specs=[row, row, row,
                  pl.BlockSpec((None, 1, d), lambda i: (cond0 + (i * tm) // rows_per_cond, 0, 2)),
                  pl.BlockSpec((d, d), lambda i: (0, 0), pipeline_mode=pl.Buffered(1))],
        out_specs=row,
        out_shape=jax.ShapeDtypeStruct((m, d), F32),
        compiler_params=_cparams("parallel"),
        name="gated_out",
    )(o, z, x, mod3, w_out)


LANE_TILE = 512


def _slots_per_step(x4):
    nq, m, _, _ = x4.shape
    assert LANE_TILE % (nq * m) == 0 and S5_CHUNK % (LANE_TILE // (nq * m)) == 0
    return LANE_TILE // (nq * m)


def _slot_scratch(d):
    return [pltpu.VMEM((2, LANE_TILE, d), F32), pltpu.SemaphoreType.DMA((2,))]


def _slot_copies(x_hbm, buf, sem, l, to_hbm):
    nq, m, _, _ = x_hbm.shape
    n_x = buf.shape[1] // (nq * m)
    copies = []
    for xi in range(n_x):
        for k in range(m):
            hbm = x_hbm.at[:, k, l * n_x + xi, :]
            vmem = buf.at[l % 2, pl.ds((xi * m + k) * nq, nq), :]
            src, dst = (vmem, hbm) if to_hbm else (hbm, vmem)
            copies.append(pltpu.make_async_copy(src, dst, sem.at[l % 2]))
    return copies


def _fetch_slots(x_hbm, buf, sem, l=None, n=None):
    l = pl.program_id(0) if l is None else l
    n = pl.num_programs(0) if n is None else n

    @pl.when(l == 0)
    def _():
        for c in _slot_copies(x_hbm, buf, sem, l, False):
            c.start()

    @pl.when(l + 1 < n)
    def _():
        for c in _slot_copies(x_hbm, buf, sem, l + 1, False):
            c.start()

    for c in _slot_copies(x_hbm, buf, sem, l, False):
        c.wait()
    return buf[l % 2]


def _store_slots(res, o_hbm, buf, sem):
    l = pl.program_id(0)

    def wait(ll):
        for c in _slot_copies(o_hbm, buf, sem, ll, True):
            c.wait()

    @pl.when(l >= 2)
    def _():
        wait(l - 2)

    buf[l % 2] = res
    for c in _slot_copies(o_hbm, buf, sem, l, True):
        c.start()

    @pl.when(l == pl.num_programs(0) - 1)
    def _():
        @pl.when(l >= 1)
        def _():
            wait(l - 1)

        wait(l)


def _cond_rows(ref, n_rows, nq, n_cond):
    out = ref[n_cond - 1]
    if n_cond > 1:
        b = (lax.broadcasted_iota(jnp.int32, (n_rows, 1), 0) % nq) // (nq // n_cond)
        for i in range(n_cond - 2, -1, -1):
            out = jnp.where(b == i, ref[i], out)
    return out


def _inproj_t_kernel(*refs, n_conds, n_steps):
    n_in = len(n_conds)
    x_hbms = refs[:n_in]
    g_ref = refs[n_in]
    mod_refs = refs[n_in + 1:3 * n_in + 1]
    wu_ref, wz_ref, u_ref, z_ref, buf, sem = refs[3 * n_in + 1:]
    l = pl.program_id(0)
    first = 0
    for k in range(n_in):
        @pl.when((l >= first) & (l < first + n_steps[k]))
        def _(k=k, first=first):
            nq = x_hbms[k].shape[0]
            x = _fetch_slots(x_hbms[k], buf, sem, l - first, n_steps[k])
            n_rows = x.shape[0]
            h = _norm_mod(x, g_ref[...], _cond_rows(mod_refs[2 * k], n_rows, nq, n_conds[k]),
                          _cond_rows(mod_refs[2 * k + 1], n_rows, nq, n_conds[k]))
            u_ref[...] = _dot(h, wu_ref[...]).T.astype(u_ref.dtype)
            z_ref[...] = _dot(h, wz_ref[...]).astype(z_ref.dtype)

        first += n_steps[k]


def _inproj_t(x4s, g, mod3, cond0s, n_conds, w_in):
    d = x4s[0].shape[-1]
    n_steps = []
    for x4 in x4s:
        _slots_per_step(x4)
        n_steps.append(x4.shape[0] * x4.shape[1] * S5_CHUNK // LANE_TILE)
    n_tok = sum(n_steps) * LANE_TILE
    mod_specs = []
    for c0, nc in zip(cond0s, n_conds):
        for col in range(2):
            mod_specs.append(pl.BlockSpec((nc, 1, d), lambda l, c0=c0, nc=nc, col=col: (c0 // nc, 0, col)))

    def wspec(col):
        return pl.BlockSpec((d, d), lambda l: (0, col), pipeline_mode=pl.Buffered(1))

    return pl.pallas_call(
        functools.partial(_inproj_t_kernel, n_conds=tuple(n_conds), n_steps=tuple(n_steps)),
        grid=(sum(n_steps),),
        in_specs=[pl.BlockSpec(memory_space=pl.ANY)] * len(x4s)
        + [pl.BlockSpec((1, d), lambda l: (0, 0))] + mod_specs + [wspec(0), wspec(1)],
        out_specs=[pl.BlockSpec((d, LANE_TILE), lambda l: (0, l)), pl.BlockSpec((LANE_TILE, d), lambda l: (l, 0))],
        out_shape=[jax.ShapeDtypeStruct((d, n_tok), BF16), jax.ShapeDtypeStruct((n_tok, d), BF16)],
        scratch_shapes=_slot_scratch(d),
        compiler_params=_cparams("arbitrary"),
        name="inproj_t",
    )(*x4s, g.reshape(1, d), *([mod3] * (2 * len(x4s))), w_in, w_in)


def _cmul(ar, ai, br, bi):
    return ar * br - ai * bi, ar * bi + ai * br


def _split_bf16(a):
    hi = a.astype(BF16)
    return hi, (a - hi.astype(F32)).astype(BF16)


def _s5_powers(ar, ai):
    width = ar.shape[-1]
    pw = [(jnp.ones_like(ar), jnp.zeros_like(ar))]
    for _ in range(S5_CHUNK):
        pw.append(_cmul(pw[-1][0], pw[-1][1], ar, ai))
    is_fwd = lax.broadcasted_iota(jnp.int32, (1, width), 1) < width // 2

    def pattern(fwd_ascending):
        out = []
        for part in range(2):
            blocks = []
            for j in range(S5_CHUNK):
                ef, eb = (j, S5_CHUNK - 1 - j) if fwd_ascending else (S5_CHUNK - 1 - j, j)
                blocks.append(jnp.broadcast_to(jnp.where(is_fwd, pw[ef][part], pw[eb][part]), (S5_CH, width)))
            out.append(jnp.concatenate(blocks, axis=0))
        return out

    return pattern(True), pattern(False), pw[S5_CHUNK]


S5_GROUPS_PER_STEP = 8


def _s5_kernel(xp_ref, xs_ref, *rest, nb_p, nb_s, n_seg_s, n_cast):
    group_in, cast_in = rest[:9], rest[9:9 + n_cast]
    yp_ref, ys_ref, st_ref = rest[9 + n_cast:12 + n_cast]
    cast_out = rest[12 + n_cast:12 + 2 * n_cast]
    scratch = rest[12 + 2 * n_cast:]
    for gi in range(S5_GROUPS_PER_STEP):
        ch = pl.ds(gi * S5_CH, S5_CH)
        _s5_group(xp_ref.at[ch], xs_ref.at[ch], *[r.at[gi] for r in group_in], yp_ref.at[ch], ys_ref.at[ch],
                  st_ref.at[gi], *[r.at[gi] for r in scratch], nb_p=nb_p, nb_s=nb_s, n_seg_s=n_seg_s)
    for src, dst in zip(cast_in, cast_out):
        dst[...] = src[...].astype(dst.dtype)


def _s5_group(xp_ref, xs_ref, h0_ref, are_ref, aim_ref, ldt_ref, bre_ref, bim_ref, cre_ref, cim_ref, dsk_ref,
              yp_ref, ys_ref, st_ref, w_ref, ws_ref, wc_ref, s_ref, f_ref, *, nb_p, nb_s, n_seg_s):
    p = S5_P
    kc = S5_CHUNK * S5_CH
    a_re = are_ref[...]
    a_im = aim_ref[...]
    dt = jnp.exp(ldt_ref[...])
    mag = jnp.exp(a_re * dt)
    ab_re = mag * jnp.cos(a_im * dt)
    ab_im = mag * jnp.sin(a_im * dt)
    den = a_re * a_re + a_im * a_im
    nr = ab_re - 1.0
    f_re = (nr * a_re + ab_im * a_im) / den
    f_im = (ab_im * a_re - nr * a_im) / den
    bbt_re, bbt_im = _cmul(f_re, f_im, bre_ref[...], bim_ref[...])

    (pg_r, pg_i), (ps_r, ps_i), (ac_r, ac_i) = _s5_powers(ab_re, ab_im)
    c_re = jnp.concatenate([cre_ref[...]] * S5_CHUNK, axis=0)
    c_im = jnp.concatenate([cim_ref[...]] * S5_CHUNK, axis=0)
    bt_re = jnp.concatenate([bbt_re] * S5_CHUNK, axis=0)
    bt_im = jnp.concatenate([bbt_im] * S5_CHUNK, axis=0)

    g_re, g_im = _cmul(pg_r, pg_i, c_re, c_im)
    fwd16 = lax.broadcasted_iota(jnp.int32, (S5_CH, 2 * p), 1) < p
    g_hi, g_lo = _split_bf16(jnp.concatenate([g_re, g_im], axis=1))
    r0 = []
    for d in range(2):
        msk = fwd16 if d == 0 else jnp.logical_not(fwd16)
        b_hi, b_lo = _split_bf16(
            jnp.concatenate([jnp.where(msk, bbt_re, 0.0), jnp.where(msk, -bbt_im, 0.0)], axis=1))
        r0.append(_dot_nt(b_hi, g_hi) + (_dot_nt(b_hi, g_lo) + _dot_nt(b_lo, g_hi)))
    lane = lax.broadcasted_iota(jnp.int32, (S5_CH, kc), 1)
    for s in range(S5_CHUNK):
        f = jnp.where(lane >= S5_CH * s, pltpu.roll(r0[0], S5_CH * s, 1), 0.0) if s else r0[0]
        b = jnp.where(lane < S5_CH * (s + 1), pltpu.roll(r0[1], (S5_CH * (s + 1)) % kc, 1), 0.0)
        w_ref[s * S5_CH:(s + 1) * S5_CH, :] = (f + b).astype(BF16)
    e_re, e_im = _cmul(ps_r, ps_i, bt_re, bt_im)
    ws_ref[:, 0:2 * p] = e_re.astype(BF16)
    ws_ref[:, 2 * p:4 * p] = e_im.astype(BF16)
    g1_re, g1_im = _cmul(g_re, g_im, ab_re, ab_im)
    wc_ref[:, 0:2 * p] = g1_re.astype(BF16)
    wc_ref[:, 2 * p:4 * p] = (-g1_im).astype(BF16)

    def run(xt_ref, yt_ref, nb, n_seg, hr, hm):
        rows = xt_ref.shape[1] // S5_CHUNK
        nbx = nb * n_seg
        m = rows // nbx
        xt = jnp.concatenate([xt_ref[:, s * rows:(s + 1) * rows] for s in range(S5_CHUNK)], axis=0)
        x = xt.T
        y = _dot(x, w_ref[...]) + dsk_ref[...] * x.astype(F32)
        s_all = _dot(x, ws_ref[...])
        s_ref[0, 0:rows, :] = s_all[:, 0:2 * p]
        s_ref[1, 0:rows, :] = s_all[:, 2 * p:4 * p]

        def step(ar, ai, hr, hm, plane_r, plane_i, rf, rb):
            isf = lax.broadcasted_iota(jnp.int32, hr.shape, 1) < p
            sr = jnp.where(isf, plane_r[rf, :], plane_r[rb, :])
            sm = jnp.where(isf, plane_i[rf, :], plane_i[rb, :])
            return ar * hr - ai * hm + sr, ar * hm + ai * hr + sm

        def scan(hr, hm, store):
            arb = jnp.broadcast_to(ac_r, (nbx, 2 * p))
            aib = jnp.broadcast_to(ac_i, (nbx, 2 * p))
            for k in range(m):
                rf = pl.ds(k * nbx, nbx)
                rb = pl.ds((m - 1 - k) * nbx, nbx)
                if store:
                    s_ref[2, rf, :] = hr
                    s_ref[3, rf, :] = hm
                    s_ref[4, rb, :] = hr
                    s_ref[5, rb, :] = hm
                hr, hm = step(arb, aib, hr, hm, s_ref.at[0], s_ref.at[1], rf, rb)
            return hr, hm

        if n_seg > 1:
            zeros = jnp.zeros((nbx, 2 * p), F32)
            f_ref[0], f_ref[1] = scan(zeros, zeros, False)
            sr_, si_ = ac_r, ac_i
            for _ in range(m - 1):
                sr_, si_ = _cmul(sr_, si_, ac_r, ac_i)
            sr_ = jnp.broadcast_to(sr_, (nb, 2 * p))
            si_ = jnp.broadcast_to(si_, (nb, 2 * p))
            for j in range(n_seg):
                rf = pl.ds(j, nb, stride=n_seg)
                rb = pl.ds(n_seg - 1 - j, nb, stride=n_seg)
                f_ref[2, rf, :] = hr
                f_ref[3, rf, :] = hm
                f_ref[4, rb, :] = hr
                f_ref[5, rb, :] = hm
                hr, hm = step(sr_, si_, hr, hm, f_ref.at[0], f_ref.at[1], rf, rb)
            isf = lax.broadcasted_iota(jnp.int32, (nbx, 2 * p), 1) < p
            hr = jnp.where(isf, f_ref[2], f_ref[4])
            hm = jnp.where(isf, f_ref[3], f_ref[5])
        hr, hm = scan(hr, hm, True)
        isf_rows = lax.broadcasted_iota(jnp.int32, (rows, 2 * p), 1) < p
        h_prev = jnp.concatenate([jnp.where(isf_rows, s_ref[2, 0:rows, :], s_ref[4, 0:rows, :]),
                                  jnp.where(isf_rows, s_ref[3, 0:rows, :], s_ref[5, 0:rows, :])], axis=1)
        yt = _gelu_tanh(y + _dot_nt(h_prev.astype(BF16), wc_ref[...])).astype(yt_ref.dtype).T
        for t in range(S5_CHUNK):
            yt_ref[:, t * rows:(t + 1) * rows] = yt[t * S5_CH:(t + 1) * S5_CH, :]
        return hr, hm

    zeros = jnp.zeros((nb_p, 2 * p), F32)
    hr, hm = run(xp_ref, yp_ref, nb_p, 1, zeros, zeros)
    st_ref[:, 0:2 * p] = hr
    st_ref[:, 2 * p:4 * p] = hm
    run(xs_ref, ys_ref, nb_s, n_seg_s, h0_ref[:, 0:2 * p], h0_ref[:, 2 * p:4 * p])


def _s5(ut, lanes_p, h0, a_re, a_im, log_dt, bt_re, bt_im, c_re, c_im, d_skip, nb_p, nb_s, n_seg_s, casts):
    d, lanes = ut.shape
    lanes_s = lanes - lanes_p
    assert lanes_p % lanes_s == 0
    g = d // S5_CH
    rows_p = lanes_p // S5_CHUNK
    rows_s = lanes_s // S5_CHUNK
    kc = S5_CHUNK * S5_CH
    p = S5_P

    gps = S5_GROUPS_PER_STEP

    def gspec(shape):
        return pl.BlockSpec((gps,) + shape, lambda i: (i, 0, 0))

    def tspec(width, first=0):
        return pl.BlockSpec((gps * S5_CH, width), lambda i: (i, first // width))

    n_steps = g // gps
    cast_specs = []
    for wc in casts:
        tc = wc.shape[1] // n_steps
        assert tc * n_steps == wc.shape[1] and tc % LANES == 0
        cast_specs.append(pl.BlockSpec((wc.shape[0], tc), lambda i: (0, i)))
    return pl.pallas_call(
        functools.partial(_s5_kernel, nb_p=nb_p, nb_s=nb_s, n_seg_s=n_seg_s, n_cast=len(casts)),
        grid=(n_steps,),
        in_specs=[tspec(lanes_p), tspec(lanes_s, lanes_p), gspec((nb_s, 4 * p)),
                  gspec((1, 2 * p)), gspec((1, 2 * p)), gspec((1, 2 * p)),
                  gspec((S5_CH, 2 * p)), gspec((S5_CH, 2 * p)), gspec((S5_CH, 2 * p)), gspec((S5_CH, 2 * p)),
                  gspec((1, kc))] + cast_specs,
        out_specs=[tspec(lanes_p), tspec(lanes_s), gspec((nb_p, 4 * p))] + cast_specs,
        out_shape=[jax.ShapeDtypeStruct((d, lanes_p), BF16),
                   jax.ShapeDtypeStruct((d, lanes_s), BF16),
                   jax.ShapeDtypeStruct((g, nb_p, 4 * p), F32)]
        + [jax.ShapeDtypeStruct(wc.shape, BF16) for wc in casts],
        scratch_shapes=[pltpu.VMEM((gps, kc, kc), BF16), pltpu.VMEM((gps, kc, 4 * p), BF16),
                        pltpu.VMEM((gps, kc, 4 * p), BF16),
                        pltpu.VMEM((gps, 6, max(rows_p, rows_s), 2 * p), F32),
                        pltpu.VMEM((gps, 6, nb_s * n_seg_s, 2 * p), F32)],
        compiler_params=_cparams("parallel"),
        name="s5",
    )(ut, ut, h0, a_re, a_im, log_dt, bt_re, bt_im, c_re, c_im,
      jnp.tile(d_skip.reshape(g, 1, S5_CH), (1, 1, S5_CHUNK)), *casts)


def _l1_tail_kernel(yt0_ref, ytn_ref, z_ref, wglu_ref, b_ref, x_hbm, gate_ref, fg_ref, wout_ref, o_hbm,
                    yb_ref, a_ref, xbuf, xsem, obuf, osem, *, tn, n_cond):
    d = wglu_ref.shape[0]
    nq = x_hbm.shape[0]
    l = pl.program_id(0)
    slot = l % 2
    _fetch_slots(x_hbm, xbuf, xsem)

    @pl.when(l == 0)
    def _():
        yb_ref[0] = yt0_ref[...].T

    for c in range(d // tn):
        cs = slice(c * tn, (c + 1) * tn)
        yb = yb_ref[slot]
        gl = _dot(yb, wglu_ref[:, cs]) + b_ref[:, cs]
        a = yb[:, cs].astype(F32) * _sigmoid(gl) * _silu(z_ref[:, cs].astype(F32))
        a_ref[:, cs] = a.astype(BF16)
        yb_ref[1 - slot, :, cs] = ytn_ref[cs, :].T

    x = xbuf[slot]
    xn = x + _cond_rows(gate_ref, x.shape[0], nq, n_cond) * _dot(a_ref[...], wout_ref[...])
    res = xn * lax.rsqrt(jnp.mean(xn * xn, axis=-1, keepdims=True) + EPS) * fg_ref[...]
    _store_slots(res, o_hbm, obuf, osem)


def _l1_tail(yt, z, z_first, x4, w_glu, b_glu, mod3, cond0, n_cond, final_g, w_out, tn):
    nq, m, _, d = x4.shape
    _slots_per_step(x4)
    tm = LANE_TILE
    ni = nq * m * S5_CHUNK // tm
    wspec = pl.BlockSpec((d, d), lambda i: (0, 0), pipeline_mode=pl.Buffered(1))
    vec = pl.BlockSpec((1, d), lambda i: (0, 0))
    return pl.pallas_call(
        functools.partial(_l1_tail_kernel, tn=tn, n_cond=n_cond),
        grid=(ni,),
        in_specs=[pl.BlockSpec((d, tm), lambda i: (0, 0), pipeline_mode=pl.Buffered(1)),
                  pl.BlockSpec((d, tm), lambda i: (0, jnp.minimum(i + 1, ni - 1))),
                  pl.BlockSpec((tm, d), lambda i: (i + z_first // tm, 0)),
                  wspec, vec,
                  pl.BlockSpec(memory_space=pl.ANY),
                  pl.BlockSpec((n_cond, 1, d), lambda i: (cond0 // n_cond, 0, 2)),
                  vec, wspec],
        out_specs=pl.BlockSpec(memory_space=pl.ANY),
        out_shape=jax.ShapeDtypeStruct(x4.shape, F32),
        scratch_shapes=[pltpu.VMEM((2, tm, d), BF16), pltpu.VMEM((tm, d), BF16)] + 2 * _slot_scratch(d),
        compiler_params=_cparams("arbitrary"),
        name="l1_tail",
    )(yt, yt, z, w_glu, b_glu.reshape(1, d), x4, mod3, final_g.reshape(1, d), w_out)


def _both_dirs(a):
    _, g, r, p = a.shape
    return a.transpose(1, 2, 0, 3).reshape(g, r, 2 * p)


def kernel(x_prompt, x_sample, cache_l0_k, cache_l0_v, state_l1_s5, c, c_ctx, l0_norm_g, l0_w_ada, l0_b_ada, l0_w_in, l0_rpb, l0_w_out, l1_norm_g, l1_w_ada, l1_b_ada, l1_w_in, l1_a_re, l1_a_im, l1_log_dt, l1_b_re, l1_b_im, l1_c_re, l1_c_im, l1_d, l1_w_glu, l1_b_glu, l1_w_out, final_norm_g):
    bp, seq, d = x_prompt.shape
    bs, n_tok, _ = x_sample.shape
    g = d // S5_CH
    p = S5_P
    xp = x_prompt.reshape(bp * seq, d)
    xs = x_sample.reshape(bs * n_tok, d)

    ctx = bs
    cond8 = jnp.zeros((8, d), F32).at[0:bs].set(c).at[ctx].set(c_ctx)
    mod0 = _ada(cond8, (l0_w_ada,), (l0_b_ada,))[0].reshape(8, 1, 3 * d)

    w_in0 = l0_w_in.astype(BF16)

    qkvz = ((0, BF16, QK_SCALE), (1, BF16, 1.0), (2, BF16, 1.0), (3, BF16, 1.0))
    qs, ks, vs, zs, w_out0 = _inproj(xs, l0_norm_g, mod0, 0, n_tok, w_in0, qkvz, 1024, 256, casts=(l0_w_out,))
    qp, kpb, vpb, zp, kp, vp = _inproj(xp, l0_norm_g, mod0, ctx, bp * seq, w_in0, qkvz, 1024, 256,
                                       head_split=(1, 2))
    rpb_pad = jnp.zeros((N_HEADS, 16, LANES), F32).at[:, :RPB_R, :RPB_C].set(l0_rpb)
    os_, w_in1, mod1 = _latent_na(qs, ks, vs, cache_l0_k, cache_l0_v, rpb_pad, n_tok, l1_w_in,
                                  (cond8, l1_w_ada, l1_b_ada))
    mod1 = mod1.reshape(8, 1, 3 * d)
    n_seg_s = max(1, n_tok // S5_CHUNK // S5_SEG)
    x1p = _ctx_layer(qp, kpb, vpb, zp, xp, mod0, ctx, w_out0, seq, 512).reshape(bp, seq // S5_CHUNK, S5_CHUNK, d)
    x1s = _gated_out(os_, zs, xs, mod0, 0, n_tok, w_out0, 512).reshape(bs * n_seg_s, -1, S5_CHUNK, d)

    ut, z1 = _inproj_t((x1p, x1s), l1_norm_g, mod1, (ctx, 0), (1, bs), w_in1)
    h0 = state_l1_s5.transpose(3, 0, 2, 1, 4).reshape(g, bs, 4 * p)
    log_dt = jnp.broadcast_to(l1_log_dt[:, :, None, None], (2, g, 1, p))
    ytp, yts, st, w_glu, w_out1 = _s5(
        ut, bp * seq, h0,
        _both_dirs(l1_a_re[:, :, None, :]), _both_dirs(l1_a_im[:, :, None, :]), _both_dirs(log_dt),
        _both_dirs(l1_b_re.transpose(0, 1, 3, 2)), _both_dirs(l1_b_im.transpose(0, 1, 3, 2)),
        _both_dirs(l1_c_re), _both_dirs(l1_c_im), l1_d, bp, bs, n_seg_s, (l1_w_glu, l1_w_out))
    y_prompt = _l1_tail(ytp, z1, 0, x1p, w_glu, l1_b_glu, mod1, ctx, 1, final_norm_g, w_out1, 512)
    y_sample = _l1_tail(yts, z1, bp * seq, x1s, w_glu, l1_b_glu, mod1, 0, bs, final_norm_g, w_out1, 512)
    new_state = st.reshape(g, bp, 2, 2, p).transpose(1, 3, 2, 0, 4)
    return (y_prompt.reshape(bp, seq, d), y_sample.reshape(bs, n_tok, d),
            kp.reshape(bp, seq, N_HEADS, HEAD_DIM), vp.reshape(bp, seq, N_HEADS, HEAD_DIM), new_state)
```

```python
import functools

import jax
import jax.numpy as jnp
from jax import lax
from jax.experimental import pallas as pl
from jax.experimental.pallas import tpu as pltpu

F32 = jnp.float32
BF16 = jnp.bfloat16

EPS = 1e-6
N_HEADS = 16
HEAD_DIM = 128
GRID_W = 64
NA_ROWS = 8
NA_COLS = 16
RPB_R = 2 * NA_ROWS - 1
RPB_C = 2 * NA_COLS - 1
S5_CH = 16
S5_P = 64
S5_CHUNK = 16
S5_SEG = 8
NEG = -1e30
LANES = 128
VMEM_LIMIT = 56 * 1024 * 1024


def _cparams(*sem):
    return pltpu.CompilerParams(dimension_semantics=sem, vmem_limit_bytes=VMEM_LIMIT)


def _dot(a, b):
    return jnp.dot(a, b, preferred_element_type=F32)


def _dot_nt(a, b, precision=None):
    return lax.dot_general(a, b, (((1,), (1,)), ((), ())), preferred_element_type=F32,
                           precision=precision)


def _sigmoid(x):
    return 0.5 * jnp.tanh(0.5 * x) + 0.5


def _silu(x):
    return x * _sigmoid(x)


def _gelu_tanh(x):
    c = 0.7978845608028654
    return (0.5 * x) * (1.0 + jnp.tanh(x * (c + (c * 0.044715) * (x * x))))


def _ada_kernel(cond_ref, *refs, n_layers):
    w_refs, b_refs, o_ref = refs[:n_layers], refs[n_layers:2 * n_layers], refs[2 * n_layers]
    s = _silu(cond_ref[...]).astype(BF16)
    layer = pl.program_id(0)
    for i in range(n_layers):
        @pl.when(layer == i)
        def _(i=i):
            o_ref[...] = _dot(s, w_refs[i][...].astype(BF16)) + b_refs[i][...]


def _ada(cond8, w_adas, b_adas):
    n_layers = len(w_adas)
    d, n = w_adas[0].shape
    tn = 1024
    nj = n // tn

    def own(i):
        return lambda l, j: (0, jnp.where(l == i, j, jnp.where(l < i, 0, nj - 1)))

    return pl.pallas_call(
        functools.partial(_ada_kernel, n_layers=n_layers),
        grid=(n_layers, nj),
        in_specs=[pl.BlockSpec((8, d), lambda l, j: (0, 0))]
        + [pl.BlockSpec((d, tn), own(i)) for i in range(n_layers)]
        + [pl.BlockSpec((1, tn), own(i)) for i in range(n_layers)],
        out_specs=pl.BlockSpec((None, 8, tn), lambda l, j: (l, 0, j)),
        out_shape=jax.ShapeDtypeStruct((n_layers, 8, n), F32),
        compiler_params=_cparams("arbitrary", "arbitrary"),
        name="ada",
    )(cond8, *w_adas, *[b.reshape(1, n) for b in b_adas])


def _norm_mod(x, g, shift, scale):
    y = x * lax.rsqrt(jnp.mean(x * x, axis=-1, keepdims=True) + EPS) * g
    return (y * (1.0 + scale) + shift).astype(BF16)


def _inproj_kernel(x0_ref, xn_ref, g_ref, shift0_ref, scale0_ref, shiftn_ref, scalen_ref, *rest, out_split,
                   out_scale, head_split, n_cast):
    n_split = max(out_split + head_split) + 1
    w_refs = rest[:n_split]
    rest = rest[n_split:]
    cast_in, rest = rest[:n_cast], rest[n_cast:]
    o_refs = rest[:len(out_split)]
    hd_refs = rest[len(out_split):len(out_split) + len(head_split)]
    rest = rest[len(out_split) + len(head_split):]
    for src, dst in zip(cast_in, rest[:n_cast]):
        dst[...] = src[...].astype(dst.dtype)
    rest = rest[n_cast:]
    h_ref = rest[0]
    i = pl.program_id(0)
    j = pl.program_id(1)
    nj = pl.num_programs(1)
    slot = i % 2

    @pl.when((i == 0) & (j == 0))
    def _():
        h_ref[0] = _norm_mod(x0_ref[...], g_ref[...], shift0_ref[...], scale0_ref[...])

    if head_split:
        stage_ref, sem = rest[1:]
        tm, tn = stage_ref.shape[2:]
        t = i * nj + j

        def head_copies(tt):
            ii, jj = tt // nj, tt % nj
            return [pltpu.make_async_copy(stage_ref.at[tt % 2, n, :, pl.ds(hh * HEAD_DIM, HEAD_DIM)],
                                          hd.at[pl.ds(ii * tm, tm), jj * (tn // HEAD_DIM) + hh, :], sem.at[tt % 2])
                    for n, hd in enumerate(hd_refs) for hh in range(tn // HEAD_DIM)]

        @pl.when(t >= 2)
        def _():
            for c in head_copies(t - 2):
                c.wait()

    h = h_ref[slot]
    for s, w_ref in enumerate(w_refs):
        r = _dot(h, w_ref[...])
        for o_ref, o_s, o_c in zip(o_refs, out_split, out_scale):
            if o_s == s:
                o_ref[...] = (r if o_c == 1.0 else r * o_c).astype(o_ref.dtype)
        if s in head_split:
            stage_ref[t % 2, head_split.index(s)] = r
    rs = xn_ref.shape[0]
    h_ref[1 - slot, pl.ds(pl.multiple_of(j * rs, rs), rs), :] = _norm_mod(
        xn_ref[...], g_ref[...], shiftn_ref[...], scalen_ref[...])

    if head_split:
        for c in head_copies(t):
            c.start()

        @pl.when(t == pl.num_programs(0) * nj - 1)
        def _():
            @pl.when(t >= 1)
            def _():
                for c in head_copies(t - 1):
                    c.wait()

            for c in head_copies(t):
                c.wait()


def _inproj(x, g, mod3, cond0, rows_per_cond, w, outs, tm, tn, head_split=(), casts=()):
    m, d = x.shape
    out_split = tuple(o[0] for o in outs)
    out_dtypes = [o[1] for o in outs]
    out_scale = tuple(float(o[2]) for o in outs)
    head_split = tuple(head_split)
    n_split = max(out_split + head_split) + 1
    ni = m // tm
    nj = d // tn

    def nxt(i):
        return jnp.minimum(i + 1, ni - 1)

    def cond_of(i):
        return cond0 + (i * tm) // rows_per_cond

    in_specs = [pl.BlockSpec((tm, d), lambda i, j: (0, 0), pipeline_mode=pl.Buffered(1)),
                pl.BlockSpec((tm // nj, d), lambda i, j: (nxt(i) * nj + j, 0)),
                pl.BlockSpec((1, d), lambda i, j: (0, 0)),
                pl.BlockSpec((None, 1, d), lambda i, j: (cond_of(0), 0, 0)),
                pl.BlockSpec((None, 1, d), lambda i, j: (cond_of(0), 0, 1)),
                pl.BlockSpec((None, 1, d), lambda i, j: (cond_of(nxt(i)), 0, 0)),
                pl.BlockSpec((None, 1, d), lambda i, j: (cond_of(nxt(i)), 0, 1))]
    for s in range(n_split):
        in_specs.append(pl.BlockSpec((d, tn), lambda i, j, s=s: (0, s * nj + j)))
    out_specs = ([pl.BlockSpec((tm, tn), lambda i, j: (i, j)) for _ in outs]
                 + [pl.BlockSpec(memory_space=pl.ANY)] * len(head_split))
    out_shape = ([jax.ShapeDtypeStruct((m, d), dt) for dt in out_dtypes]
                 + [jax.ShapeDtypeStruct((m, d // HEAD_DIM, HEAD_DIM), F32)] * len(head_split))
    operands = [x, x, g.reshape(1, d), mod3, mod3, mod3, mod3] + [w] * n_split
    for wc in casts:
        tc = wc.shape[1] // (ni * nj)
        assert tc * ni * nj == wc.shape[1] and tc % LANES == 0
        spec = pl.BlockSpec((wc.shape[0], tc), lambda i, j: (0, i * nj + j))
        in_specs.append(spec)
        out_specs.append(spec)
        out_shape.append(jax.ShapeDtypeStruct(wc.shape, BF16))
        operands.append(wc)
    scratch = [pltpu.VMEM((2, tm, d), BF16)]
    if head_split:
        scratch += [pltpu.VMEM((2, len(head_split), tm, tn), F32), pltpu.SemaphoreType.DMA((2,))]
    return pl.pallas_call(
        functools.partial(_inproj_kernel, out_split=out_split, out_scale=out_scale, head_split=head_split,
                          n_cast=len(casts)),
        grid=(ni, nj),
        in_specs=in_specs,
        out_specs=out_specs,
        out_shape=out_shape,
        scratch_shapes=scratch,
        compiler_params=_cparams("arbitrary", "arbitrary"),
        name="inproj",
    )(*operands)


LOG2E = 1.4426950408889634
QK_SCALE = HEAD_DIM ** -0.5 * LOG2E


def _softmax_pv(parts):
    m = parts[0][0].max(axis=-1, keepdims=True)
    for s, _ in parts[1:]:
        m = jnp.maximum(m, s.max(axis=-1, keepdims=True))
    l = None
    o = None
    for s, v in parts:
        e = jnp.exp2(s - m)
        li = e.sum(axis=-1, keepdims=True)
        oi = _dot(e.astype(BF16), v)
        l = li if l is None else l + li
        o = oi if o is None else o + oi
    return o / l


def _ctx_layer_kernel(q_ref, k_ref, v_ref, z_ref, x_ref, gate_ref, w_ref, y_ref, a_ref, *, seq):
    for b in range(q_ref.shape[0] // seq):
        rows = slice(b * seq, (b + 1) * seq)
        for h in range(N_HEADS):
            sl = slice(h * HEAD_DIM, (h + 1) * HEAD_DIM)
            o = _softmax_pv([(_dot_nt(q_ref[rows, sl], k_ref[rows, sl]), v_ref[rows, sl])])
            a_ref[rows, sl] = (o * _silu(z_ref[rows, sl].astype(F32))).astype(BF16)
    y_ref[...] = x_ref[...] + gate_ref[...] * _dot(a_ref[...], w_ref[...])


def _ctx_layer(q, k, v, z, x, mod3, cond, w_out, seq, tm):
    m, d = x.shape
    row = pl.BlockSpec((tm, d), lambda i: (i, 0))
    return pl.pallas_call(
        functools.partial(_ctx_layer_kernel, seq=seq),
        grid=(m // tm,),
        in_specs=[row, row, row, row, row,
                  pl.BlockSpec((None, 1, d), lambda i: (cond, 0, 2)),
                  pl.BlockSpec((d, d), lambda i: (0, 0), pipeline_mode=pl.Buffered(1))],
        out_specs=row,
        out_shape=jax.ShapeDtypeStruct((m, d), F32),
        scratch_shapes=[pltpu.VMEM((tm, d), BF16)],
        compiler_params=_cparams("parallel"),
        name="ctx_layer",
    )(q, k, v, z, x, mod3, w_out)


def _na_window_start(r, rows):
    return min(max(r - NA_ROWS // 2, 0), rows - NA_ROWS)


NA_HEADS_PER_STEP = 4


def _na_kernel(q_ref, k_ref, v_ref, kc_hbm, vc_hbm, rpb_ref, wcast_ref, cond_ref, wada_ref, bada_ref,
               o_ref, wcast_bf_ref, mod_ref, cbuf, csem, *, rows, q_rows):
    wcast_bf_ref[...] = wcast_ref[...].astype(BF16)
    mod_ref[...] = _dot(_silu(cond_ref[...]).astype(BF16), wada_ref[...].astype(BF16)) + bada_ref[...]
    w = GRID_W
    hps = NA_HEADS_PER_STEP
    n_hblk = pl.num_programs(1)
    t = pl.program_id(0) * n_hblk + pl.program_id(1)

    def ctx_copies(tt):
        return [pltpu.make_async_copy(c.at[tt // n_hblk, :, (tt % n_hblk) * hps + hh, :], cbuf.at[tt % 2, hh, n],
                                      csem.at[tt % 2])
                for hh in range(hps) for n, c in enumerate((kc_hbm, vc_hbm))]

    @pl.when(t == 0)
    def _():
        for c in ctx_copies(t):
            c.start()

    @pl.when(t + 1 < pl.num_programs(0) * n_hblk)
    def _():
        for c in ctx_copies(t + 1):
            c.start()

    qc = lax.broadcasted_iota(jnp.int32, (w, LANES), 0)
    lane = lax.broadcasted_iota(jnp.int32, (w, LANES), 1)
    c0 = jnp.clip(qc - NA_COLS // 2, 0, w - NA_COLS)
    ok_l = (lane >= c0) & (lane < c0 + NA_COLS)
    ok_r = (lane - w >= c0) & (lane - w < c0 + NA_COLS)
    neg_tile = jnp.full((w, LANES), NEG, F32)
    for c in ctx_copies(t):
        c.wait()
    for hh in range(hps):
        _na_head(q_ref, k_ref, v_ref, cbuf.at[t % 2, hh], rpb_ref.at[hh], o_ref,
                 slice(hh * HEAD_DIM, (hh + 1) * HEAD_DIM), ok_l, ok_r, neg_tile, rows, q_rows)


def _na_head(q_ref, k_ref, v_ref, ctx_ref, rpb_ref, o_ref, sl, ok_l, ok_r, neg_tile, rows, q_rows):
    w = GRID_W
    tile_l, tile_r = [], []
    for d in range(RPB_R):
        base = jnp.broadcast_to(rpb_ref[d:d + 1, :] * LOG2E, (w, LANES))
        left = pltpu.roll(base, LANES - (NA_COLS - 1), 1, stride=1, stride_axis=0)
        right = pltpu.roll(base, w - (NA_COLS - 1), 1, stride=1, stride_axis=0)
        tile_l.append(jnp.where(ok_l, left, NEG))
        tile_r.append(jnp.where(ok_r, right, NEG))

    kc = ctx_ref[0].astype(BF16)
    vc = ctx_ref[1].astype(BF16)
    n_groups = rows // q_rows
    for gi in range(n_groups):
        rs = list(range(gi * q_rows, (gi + 1) * q_rows))
        klo = min(_na_window_start(r, rows) for r in rs) // 2 * 2
        khi = -(-(max(_na_window_start(r, rows) for r in rs) + NA_ROWS) // 2) * 2
        bias_rows = []
        for r in rs:
            r0 = _na_window_start(r, rows)
            tiles = []
            for kr in range(klo, khi, 2):
                ok0 = r0 <= kr < r0 + NA_ROWS
                ok1 = r0 <= kr + 1 < r0 + NA_ROWS
                t0 = tile_l[kr - r + NA_ROWS - 1] if ok0 else neg_tile
                t1 = tile_r[kr + 1 - r + NA_ROWS - 1] if ok1 else neg_tile
                tiles.append(jnp.maximum(t0, t1) if (ok0 or ok1) else neg_tile)
            bias_rows.append(jnp.concatenate(tiles, axis=1))
        bias = jnp.concatenate(bias_rows, axis=0)
        q = q_ref[gi * q_rows * w:(gi + 1) * q_rows * w, sl]
        kl = k_ref[klo * w:khi * w, sl]
        vl = v_ref[klo * w:khi * w, sl]
        s_loc = _dot_nt(q, kl) + bias
        s_ctx = _dot_nt(q, kc)
        o = _softmax_pv([(s_loc, vl), (s_ctx, vc)])
        o_ref[gi * q_rows * w:(gi + 1) * q_rows * w, sl] = o.astype(o_ref.dtype)


def _latent_na(q, k, v, k_ctx, v_ctx, rpb_pad, n_tok, w_cast, ada_next):
    m, d = q.shape
    nb = m // n_tok
    lc = k_ctx.shape[1]
    rows = n_tok // GRID_W
    hps = NA_HEADS_PER_STEP
    n_hblk = N_HEADS // hps
    cond8, w_ada, b_ada = ada_next
    n_ada = w_ada.shape[1]
    tc = w_cast.shape[1] // (nb * n_hblk)
    ta = n_ada // (nb * n_hblk)
    assert tc * nb * n_hblk == w_cast.shape[1] and tc % LANES == 0
    assert ta * nb * n_hblk == n_ada and ta % LANES == 0
    spec = pl.BlockSpec((n_tok, hps * HEAD_DIM), lambda b, h: (b, h))
    cspec = pl.BlockSpec(memory_space=pl.ANY)

    def step_cols(r, t):
        return pl.BlockSpec((r, t), lambda b, h: (0, b * n_hblk + h))

    return pl.pallas_call(
        functools.partial(_na_kernel, rows=rows, q_rows=4),
        grid=(nb, n_hblk),
        in_specs=[spec, spec, spec, cspec, cspec,
                  pl.BlockSpec((hps, 16, LANES), lambda b, h: (h, 0, 0)), step_cols(w_cast.shape[0], tc),
                  pl.BlockSpec((8, d), lambda b, h: (0, 0)), step_cols(d, ta), step_cols(1, ta)],
        out_specs=[spec, step_cols(w_cast.shape[0], tc), step_cols(8, ta)],
        out_shape=[jax.ShapeDtypeStruct((m, d), BF16), jax.ShapeDtypeStruct(w_cast.shape, BF16),
                   jax.ShapeDtypeStruct((8, n_ada), F32)],
        scratch_shapes=[pltpu.VMEM((2, hps, 2, lc, HEAD_DIM), F32), pltpu.SemaphoreType.DMA((2,))],
        compiler_params=_cparams("arbitrary", "arbitrary"),
        name="latent_na",
    )(q, k, v, k_ctx, v_ctx, rpb_pad, w_cast, cond8, w_ada, b_ada.reshape(1, n_ada))


def _gated_out_kernel(o_ref, z_ref, x_ref, gate_ref, w_ref, y_ref):
    a = o_ref[...].astype(F32) * _silu(z_ref[...].astype(F32))
    y_ref[...] = x_ref[...] + gate_ref[...] * _dot(a.astype(BF16), w_ref[...])


def _gated_out(o, z, x, mod3, cond0, rows_per_cond, w_out, tm):
    m, d = x.shape
    row = pl.BlockSpec((tm, d), lambda i: (i, 0))
    return pl.pallas_call(
        _gated_out_kernel,
        grid=(m // tm,),
        in_specs=[row, row, row,
                  pl.BlockSpec((None, 1, d), lambda i: (cond0 + (i * tm) // rows_per_cond, 0, 2)),
                  pl.BlockSpec((d, d), lambda i: (0, 0), pipeline_mode=pl.Buffered(1))],
        out_specs=row,
        out_shape=jax.ShapeDtypeStruct((m, d), F32),
        compiler_params=_cparams("parallel"),
        name="gated_out",
    )(o, z, x, mod3, w_out)


LANE_TILE = 512


def _slots_per_step(x4):
    nq, m, _, _ = x4.shape
    assert LANE_TILE % (nq * m) == 0 and S5_CHUNK % (LANE_TILE // (nq * m)) == 0
    return LANE_TILE // (nq * m)


def _slot_scratch(d):
    return [pltpu.VMEM((2, LANE_TILE, d), F32), pltpu.SemaphoreType.DMA((2,))]


def _slot_copies(x_hbm, buf, sem, l, to_hbm):
    nq, m, _, _ = x_hbm.shape
    n_x = buf.shape[1] // (nq * m)
    copies = []
    for xi in range(n_x):
        for k in range(m):
            hbm = x_hbm.at[:, k, l * n_x + xi, :]
            vmem = buf.at[l % 2, pl.ds((xi * m + k) * nq, nq), :]
            src, dst = (vmem, hbm) if to_hbm else (hbm, vmem)
            copies.append(pltpu.make_async_copy(src, dst, sem.at[l % 2]))
    return copies


def _fetch_slots(x_hbm, buf, sem, l=None, n=None):
    l = pl.program_id(0) if l is None else l
    n = pl.num_programs(0) if n is None else n

    @pl.when(l == 0)
    def _():
        for c in _slot_copies(x_hbm, buf, sem, l, False):
            c.start()

    @pl.when(l + 1 < n)
    def _():
        for c in _slot_copies(x_hbm, buf, sem, l + 1, False):
            c.start()

    for c in _slot_copies(x_hbm, buf, sem, l, False):
        c.wait()
    return buf[l % 2]


def _store_slots(res, o_hbm, buf, sem, l=None, n=None):
    l = pl.program_id(0) if l is None else l
    n = pl.num_programs(0) if n is None else n

    def wait(ll):
        for c in _slot_copies(o_hbm, buf, sem, ll, True):
            c.wait()

    @pl.when(l >= 2)
    def _():
        wait(l - 2)

    buf[l % 2] = res
    for c in _slot_copies(o_hbm, buf, sem, l, True):
        c.start()

    @pl.when(l == n - 1)
    def _():
        @pl.when(l >= 1)
        def _():
            wait(l - 1)

        wait(l)


def _cond_rows(ref, n_rows, nq, n_cond):
    out = ref[n_cond - 1]
    if n_cond > 1:
        b = (lax.broadcasted_iota(jnp.int32, (n_rows, 1), 0) % nq) // (nq // n_cond)
        for i in range(n_cond - 2, -1, -1):
            out = jnp.where(b == i, ref[i], out)
    return out


def _inproj_t_kernel(*refs, n_conds, n_steps):
    n_in = len(n_conds)
    x_hbms = refs[:n_in]
    g_ref = refs[n_in]
    mod_refs = refs[n_in + 1:3 * n_in + 1]
    wu_ref, wz_ref, u_ref, z_ref, buf, sem = refs[3 * n_in + 1:]
    l = pl.program_id(0)
    first = 0
    for k in range(n_in):
        @pl.when((l >= first) & (l < first + n_steps[k]))
        def _(k=k, first=first):
            nq = x_hbms[k].shape[0]
            x = _fetch_slots(x_hbms[k], buf, sem, l - first, n_steps[k])
            n_rows = x.shape[0]
            h = _norm_mod(x, g_ref[...], _cond_rows(mod_refs[2 * k], n_rows, nq, n_conds[k]),
                          _cond_rows(mod_refs[2 * k + 1], n_rows, nq, n_conds[k]))
            u_ref[...] = _dot(h, wu_ref[...]).T.astype(u_ref.dtype)
            z_ref[...] = _dot(h, wz_ref[...]).astype(z_ref.dtype)

        first += n_steps[k]


def _inproj_t(x4s, g, mod3, cond0s, n_conds, w_in):
    d = x4s[0].shape[-1]
    n_steps = []
    for x4 in x4s:
        _slots_per_step(x4)
        n_steps.append(x4.shape[0] * x4.shape[1] * S5_CHUNK // LANE_TILE)
    n_tok = sum(n_steps) * LANE_TILE
    mod_specs = []
    for c0, nc in zip(cond0s, n_conds):
        for col in range(2):
            mod_specs.append(pl.BlockSpec((nc, 1, d), lambda l, c0=c0, nc=nc, col=col: (c0 // nc, 0, col)))

    def wspec(col):
        return pl.BlockSpec((d, d), lambda l: (0, col), pipeline_mode=pl.Buffered(1))

    return pl.pallas_call(
        functools.partial(_inproj_t_kernel, n_conds=tuple(n_conds), n_steps=tuple(n_steps)),
        grid=(sum(n_steps),),
        in_specs=[pl.BlockSpec(memory_space=pl.ANY)] * len(x4s)
        + [pl.BlockSpec((1, d), lambda l: (0, 0))] + mod_specs + [wspec(0), wspec(1)],
        out_specs=[pl.BlockSpec((d, LANE_TILE), lambda l: (0, l)), pl.BlockSpec((LANE_TILE, d), lambda l: (l, 0))],
        out_shape=[jax.ShapeDtypeStruct((d, n_tok), BF16), jax.ShapeDtypeStruct((n_tok, d), BF16)],
        scratch_shapes=_slot_scratch(d),
        compiler_params=_cparams("arbitrary"),
        name="inproj_t",
    )(*x4s, g.reshape(1, d), *([mod3] * (2 * len(x4s))), w_in, w_in)


def _cmul(ar, ai, br, bi):
    return ar * br - ai * bi, ar * bi + ai * br


def _split_bf16(a):
    hi = a.astype(BF16)
    return hi, (a - hi.astype(F32)).astype(BF16)


def _s5_powers(ar, ai):
    width = ar.shape[-1]
    pw = [(jnp.ones_like(ar), jnp.zeros_like(ar))]
    for _ in range(S5_CHUNK):
        pw.append(_cmul(pw[-1][0], pw[-1][1], ar, ai))
    is_fwd = lax.broadcasted_iota(jnp.int32, (1, width), 1) < width // 2

    def pattern(fwd_ascending):
        out = []
        for part in range(2):
            blocks = []
            for j in range(S5_CHUNK):
                ef, eb = (j, S5_CHUNK - 1 - j) if fwd_ascending else (S5_CHUNK - 1 - j, j)
                blocks.append(jnp.broadcast_to(jnp.where(is_fwd, pw[ef][part], pw[eb][part]), (S5_CH, width)))
            out.append(jnp.concatenate(blocks, axis=0))
        return out

    return pattern(True), pattern(False), pw[S5_CHUNK]


S5_GROUPS_PER_STEP = 8


def _s5_kernel(xp_ref, xs_ref, *rest, nb_p, nb_s, n_seg_s, n_cast):
    group_in, cast_in = rest[:9], rest[9:9 + n_cast]
    yt_ref, st_ref = rest[9 + n_cast:11 + n_cast]
    cast_out = rest[11 + n_cast:11 + 2 * n_cast]
    scratch = rest[11 + 2 * n_cast:]
    lanes_p, lanes_s = xp_ref.shape[1], xs_ref.shape[1]
    for gi in range(S5_GROUPS_PER_STEP):
        ch = pl.ds(gi * S5_CH, S5_CH)
        _s5_group(xp_ref.at[ch], xs_ref.at[ch], *[r.at[gi] for r in group_in],
                  yt_ref.at[ch, pl.ds(0, lanes_p)], yt_ref.at[ch, pl.ds(lanes_p, lanes_s)],
                  st_ref.at[gi], *[r.at[gi] for r in scratch], nb_p=nb_p, nb_s=nb_s, n_seg_s=n_seg_s)
    for src, dst in zip(cast_in, cast_out):
        dst[...] = src[...].astype(dst.dtype)


def _s5_group(xp_ref, xs_ref, h0_ref, are_ref, aim_ref, ldt_ref, bre_ref, bim_ref, cre_ref, cim_ref, dsk_ref,
              yp_ref, ys_ref, st_ref, w_ref, ws_ref, wc_ref, s_ref, f_ref, *, nb_p, nb_s, n_seg_s):
    p = S5_P
    kc = S5_CHUNK * S5_CH
    a_re = are_ref[...]
    a_im = aim_ref[...]
    dt = jnp.exp(ldt_ref[...])
    mag = jnp.exp(a_re * dt)
    ab_re = mag * jnp.cos(a_im * dt)
    ab_im = mag * jnp.sin(a_im * dt)
    den = a_re * a_re + a_im * a_im
    nr = ab_re - 1.0
    f_re = (nr * a_re + ab_im * a_im) / den
    f_im = (ab_im * a_re - nr * a_im) / den
    bbt_re, bbt_im = _cmul(f_re, f_im, bre_ref[...], bim_ref[...])

    (pg_r, pg_i), (ps_r, ps_i), (ac_r, ac_i) = _s5_powers(ab_re, ab_im)
    c_re = jnp.concatenate([cre_ref[...]] * S5_CHUNK, axis=0)
    c_im = jnp.concatenate([cim_ref[...]] * S5_CHUNK, axis=0)
    bt_re = jnp.concatenate([bbt_re] * S5_CHUNK, axis=0)
    bt_im = jnp.concatenate([bbt_im] * S5_CHUNK, axis=0)

    g_re, g_im = _cmul(pg_r, pg_i, c_re, c_im)
    fwd16 = lax.broadcasted_iota(jnp.int32, (S5_CH, 2 * p), 1) < p
    g_hi, g_lo = _split_bf16(jnp.concatenate([g_re, g_im], axis=1))
    r0 = []
    for d in range(2):
        msk = fwd16 if d == 0 else jnp.logical_not(fwd16)
        b_hi, b_lo = _split_bf16(
            jnp.concatenate([jnp.where(msk, bbt_re, 0.0), jnp.where(msk, -bbt_im, 0.0)], axis=1))
        r0.append(_dot_nt(b_hi, g_hi) + (_dot_nt(b_hi, g_lo) + _dot_nt(b_lo, g_hi)))
    lane = lax.broadcasted_iota(jnp.int32, (S5_CH, kc), 1)
    for s in range(S5_CHUNK):
        f = jnp.where(lane >= S5_CH * s, pltpu.roll(r0[0], S5_CH * s, 1), 0.0) if s else r0[0]
        b = jnp.where(lane < S5_CH * (s + 1), pltpu.roll(r0[1], (S5_CH * (s + 1)) % kc, 1), 0.0)
        w_ref[s * S5_CH:(s + 1) * S5_CH, :] = (f + b).astype(BF16)
    e_re, e_im = _cmul(ps_r, ps_i, bt_re, bt_im)
    ws_ref[:, 0:2 * p] = e_re.astype(BF16)
    ws_ref[:, 2 * p:4 * p] = e_im.astype(BF16)
    g1_re, g1_im = _cmul(g_re, g_im, ab_re, ab_im)
    wc_ref[:, 0:2 * p] = g1_re.astype(BF16)
    wc_ref[:, 2 * p:4 * p] = (-g1_im).astype(BF16)

    def run(xt_ref, yt_ref, nb, n_seg, hr, hm):
        rows = xt_ref.shape[1] // S5_CHUNK
        nbx = nb * n_seg
        m = rows // nbx
        xt = jnp.concatenate([xt_ref[:, s * rows:(s + 1) * rows] for s in range(S5_CHUNK)], axis=0)
        x = xt.T
        y = _dot(x, w_ref[...]) + dsk_ref[...] * x.astype(F32)
        s_all = _dot(x, ws_ref[...])
        s_ref[0, 0:rows, :] = s_all[:, 0:2 * p]
        s_ref[1, 0:rows, :] = s_all[:, 2 * p:4 * p]

        def step(ar, ai, hr, hm, plane_r, plane_i, rf, rb):
            isf = lax.broadcasted_iota(jnp.int32, hr.shape, 1) < p
            sr = jnp.where(isf, plane_r[rf, :], plane_r[rb, :])
            sm = jnp.where(isf, plane_i[rf, :], plane_i[rb, :])
            return ar * hr - ai * hm + sr, ar * hm + ai * hr + sm

        def scan(hr, hm, store):
            arb = jnp.broadcast_to(ac_r, (nbx, 2 * p))
            aib = jnp.broadcast_to(ac_i, (nbx, 2 * p))
            for k in range(m):
                rf = pl.ds(k * nbx, nbx)
                rb = pl.ds((m - 1 - k) * nbx, nbx)
                if store:
                    s_ref[2, rf, :] = hr
                    s_ref[3, rf, :] = hm
                    s_ref[4, rb, :] = hr
                    s_ref[5, rb, :] = hm
                hr, hm = step(arb, aib, hr, hm, s_ref.at[0], s_ref.at[1], rf, rb)
            return hr, hm

        if n_seg > 1:
            zeros = jnp.zeros((nbx, 2 * p), F32)
            f_ref[0], f_ref[1] = scan(zeros, zeros, False)
            sr_, si_ = ac_r, ac_i
            for _ in range(m - 1):
                sr_, si_ = _cmul(sr_, si_, ac_r, ac_i)
            sr_ = jnp.broadcast_to(sr_, (nb, 2 * p))
            si_ = jnp.broadcast_to(si_, (nb, 2 * p))
            for j in range(n_seg):
                rf = pl.ds(j, nb, stride=n_seg)
                rb = pl.ds(n_seg - 1 - j, nb, stride=n_seg)
                f_ref[2, rf, :] = hr
                f_ref[3, rf, :] = hm
                f_ref[4, rb, :] = hr
                f_ref[5, rb, :] = hm
                hr, hm = step(sr_, si_, hr, hm, f_ref.at[0], f_ref.at[1], rf, rb)
            isf = lax.broadcasted_iota(jnp.int32, (nbx, 2 * p), 1) < p
            hr = jnp.where(isf, f_ref[2], f_ref[4])
            hm = jnp.where(isf, f_ref[3], f_ref[5])
        hr, hm = scan(hr, hm, True)
        isf_rows = lax.broadcasted_iota(jnp.int32, (rows, 2 * p), 1) < p
        h_prev = jnp.concatenate([jnp.where(isf_rows, s_ref[2, 0:rows, :], s_ref[4, 0:rows, :]),
                                  jnp.where(isf_rows, s_ref[3, 0:rows, :], s_ref[5, 0:rows, :])], axis=1)
        yt = _gelu_tanh(y + _dot_nt(h_prev.astype(BF16), wc_ref[...])).astype(yt_ref.dtype).T
        for t in range(S5_CHUNK):
            yt_ref[:, t * rows:(t + 1) * rows] = yt[t * S5_CH:(t + 1) * S5_CH, :]
        return hr, hm

    zeros = jnp.zeros((nb_p, 2 * p), F32)
    hr, hm = run(xp_ref, yp_ref, nb_p, 1, zeros, zeros)
    st_ref[:, 0:2 * p] = hr
    st_ref[:, 2 * p:4 * p] = hm
    run(xs_ref, ys_ref, nb_s, n_seg_s, h0_ref[:, 0:2 * p], h0_ref[:, 2 * p:4 * p])


def _s5(ut, lanes_p, h0, a_re, a_im, log_dt, bt_re, bt_im, c_re, c_im, d_skip, nb_p, nb_s, n_seg_s, casts):
    d, lanes = ut.shape
    lanes_s = lanes - lanes_p
    assert lanes_p % lanes_s == 0
    g = d // S5_CH
    rows_p = lanes_p // S5_CHUNK
    rows_s = lanes_s // S5_CHUNK
    kc = S5_CHUNK * S5_CH
    p = S5_P

    gps = S5_GROUPS_PER_STEP

    def gspec(shape):
        return pl.BlockSpec((gps,) + shape, lambda i: (i, 0, 0))

    def tspec(width, first=0):
        return pl.BlockSpec((gps * S5_CH, width), lambda i: (i, first // width))

    n_steps = g // gps
    cast_specs = []
    for wc in casts:
        tc = wc.shape[1] // n_steps
        assert tc * n_steps == wc.shape[1] and tc % LANES == 0
        cast_specs.append(pl.BlockSpec((wc.shape[0], tc), lambda i: (0, i)))
    return pl.pallas_call(
        functools.partial(_s5_kernel, nb_p=nb_p, nb_s=nb_s, n_seg_s=n_seg_s, n_cast=len(casts)),
        grid=(n_steps,),
        in_specs=[tspec(lanes_p), tspec(lanes_s, lanes_p), gspec((nb_s, 4 * p)),
                  gspec((1, 2 * p)), gspec((1, 2 * p)), gspec((1, 2 * p)),
                  gspec((S5_CH, 2 * p)), gspec((S5_CH, 2 * p)), gspec((S5_CH, 2 * p)), gspec((S5_CH, 2 * p)),
                  gspec((1, kc))] + cast_specs,
        out_specs=[tspec(lanes), gspec((nb_p, 4 * p))] + cast_specs,
        out_shape=[jax.ShapeDtypeStruct((d, lanes), BF16),
                   jax.ShapeDtypeStruct((g, nb_p, 4 * p), F32)]
        + [jax.ShapeDtypeStruct(wc.shape, BF16) for wc in casts],
        scratch_shapes=[pltpu.VMEM((gps, kc, kc), BF16), pltpu.VMEM((gps, kc, 4 * p), BF16),
                        pltpu.VMEM((gps, kc, 4 * p), BF16),
                        pltpu.VMEM((gps, 6, max(rows_p, rows_s), 2 * p), F32),
                        pltpu.VMEM((gps, 6, nb_s * n_seg_s, 2 * p), F32)],
        compiler_params=_cparams("parallel"),
        name="s5",
    )(ut, ut, h0, a_re, a_im, log_dt, bt_re, bt_im, c_re, c_im,
      jnp.tile(d_skip.reshape(g, 1, S5_CH), (1, 1, S5_CHUNK)), *casts)


def _l1_tail_kernel(*refs, tn, n_conds, n_steps):
    n_in = len(n_conds)
    yt0_ref, ytn_ref, z_ref, wglu_ref, b_ref = refs[:5]
    x_hbms = refs[5:5 + n_in]
    gate_refs = refs[5 + n_in:5 + 2 * n_in]
    fg_ref, wout_ref = refs[5 + 2 * n_in:7 + 2 * n_in]
    o_hbms = refs[7 + 2 * n_in:7 + 3 * n_in]
    yb_ref, a_ref, xbuf, xsem, obuf, osem = refs[7 + 3 * n_in:]
    d = wglu_ref.shape[0]
    l = pl.program_id(0)
    slot = l % 2
    firsts = [sum(n_steps[:k]) for k in range(n_in)]

    def in_set(k):
        return (l >= firsts[k]) & (l < firsts[k] + n_steps[k])

    for k in range(n_in):
        @pl.when(in_set(k))
        def _(k=k):
            _fetch_slots(x_hbms[k], xbuf, xsem, l - firsts[k], n_steps[k])

    @pl.when(l == 0)
    def _():
        yb_ref[0] = yt0_ref[...].T

    for c in range(d // tn):
        cs = slice(c * tn, (c + 1) * tn)
        yb = yb_ref[slot]
        gl = _dot(yb, wglu_ref[:, cs]) + b_ref[:, cs]
        a = yb[:, cs].astype(F32) * _sigmoid(gl) * _silu(z_ref[:, cs].astype(F32))
        a_ref[:, cs] = a.astype(BF16)
        yb_ref[1 - slot, :, cs] = ytn_ref[cs, :].T

    out = _dot(a_ref[...], wout_ref[...])
    for k in range(n_in):
        @pl.when(in_set(k))
        def _(k=k):
            ll = l - firsts[k]
            x = xbuf[ll % 2]
            xn = x + _cond_rows(gate_refs[k], x.shape[0], x_hbms[k].shape[0], n_conds[k]) * out
            res = xn * lax.rsqrt(jnp.mean(xn * xn, axis=-1, keepdims=True) + EPS) * fg_ref[...]
            _store_slots(res, o_hbms[k], obuf, osem, ll, n_steps[k])


def _l1_tail(yt, z, x4s, w_glu, b_glu, mod3, cond0s, n_conds, final_g, w_out, tn):
    d = x4s[0].shape[-1]
    n_steps = []
    for x4 in x4s:
        _slots_per_step(x4)
        n_steps.append(x4.shape[0] * x4.shape[1] * S5_CHUNK // LANE_TILE)
    tm = LANE_TILE
    ni = sum(n_steps)
    n_in = len(x4s)
    wspec = pl.BlockSpec((d, d), lambda i: (0, 0), pipeline_mode=pl.Buffered(1))
    vec = pl.BlockSpec((1, d), lambda i: (0, 0))
    gate_specs = [pl.BlockSpec((nc, 1, d), lambda i, c0=c0, nc=nc: (c0 // nc, 0, 2))
                  for c0, nc in zip(cond0s, n_conds)]
    return pl.pallas_call(
        functools.partial(_l1_tail_kernel, tn=tn, n_conds=tuple(n_conds), n_steps=tuple(n_steps)),
        grid=(ni,),
        in_specs=[pl.BlockSpec((d, tm), lambda i: (0, 0), pipeline_mode=pl.Buffered(1)),
                  pl.BlockSpec((d, tm), lambda i: (0, jnp.minimum(i + 1, ni - 1))),
                  pl.BlockSpec((tm, d), lambda i: (i, 0)),
                  wspec, vec] + [pl.BlockSpec(memory_space=pl.ANY)] * n_in + gate_specs + [vec, wspec],
        out_specs=[pl.BlockSpec(memory_space=pl.ANY)] * n_in,
        out_shape=[jax.ShapeDtypeStruct(x4.shape, F32) for x4 in x4s],
        scratch_shapes=[pltpu.VMEM((2, tm, d), BF16), pltpu.VMEM((tm, d), BF16)] + 2 * _slot_scratch(d),
        compiler_params=_cparams("arbitrary"),
        name="l1_tail",
    )(yt, yt, z, w_glu, b_glu.reshape(1, d), *x4s, *([mod3] * n_in), final_g.reshape(1, d), w_out)


def _both_dirs(a):
    _, g, r, p = a.shape
    return a.transpose(1, 2, 0, 3).reshape(g, r, 2 * p)


def kernel(x_prompt, x_sample, cache_l0_k, cache_l0_v, state_l1_s5, c, c_ctx, l0_norm_g, l0_w_ada, l0_b_ada, l0_w_in, l0_rpb, l0_w_out, l1_norm_g, l1_w_ada, l1_b_ada, l1_w_in, l1_a_re, l1_a_im, l1_log_dt, l1_b_re, l1_b_im, l1_c_re, l1_c_im, l1_d, l1_w_glu, l1_b_glu, l1_w_out, final_norm_g):
    bp, seq, d = x_prompt.shape
    bs, n_tok, _ = x_sample.shape
    g = d // S5_CH
    p = S5_P
    xp = x_prompt.reshape(bp * seq, d)
    xs = x_sample.reshape(bs * n_tok, d)

    ctx = bs
    cond8 = jnp.zeros((8, d), F32).at[0:bs].set(c).at[ctx].set(c_ctx)
    mod0 = _ada(cond8, (l0_w_ada,), (l0_b_ada,))[0].reshape(8, 1, 3 * d)

    w_in0 = l0_w_in.astype(BF16)

    qkvz = ((0, BF16, QK_SCALE), (1, BF16, 1.0), (2, BF16, 1.0), (3, BF16, 1.0))
    qs, ks, vs, zs, w_out0 = _inproj(xs, l0_norm_g, mod0, 0, n_tok, w_in0, qkvz, 1024, 256, casts=(l0_w_out,))
    qp, kpb, vpb, zp, kp, vp = _inproj(xp, l0_norm_g, mod0, ctx, bp * seq, w_in0, qkvz, 1024, 256,
                                       head_split=(1, 2))
    rpb_pad = jnp.zeros((N_HEADS, 16, LANES), F32).at[:, :RPB_R, :RPB_C].set(l0_rpb)
    os_, w_in1, mod1 = _latent_na(qs, ks, vs, cache_l0_k, cache_l0_v, rpb_pad, n_tok, l1_w_in,
                                  (cond8, l1_w_ada, l1_b_ada))
    mod1 = mod1.reshape(8, 1, 3 * d)
    n_seg_s = max(1, n_tok // S5_CHUNK // S5_SEG)
    x1p = _ctx_layer(qp, kpb, vpb, zp, xp, mod0, ctx, w_out0, seq, 512).reshape(bp, seq // S5_CHUNK, S5_CHUNK, d)
    x1s = _gated_out(os_, zs, xs, mod0, 0, n_tok, w_out0, 512).reshape(bs * n_seg_s, -1, S5_CHUNK, d)

    ut, z1 = _inproj_t((x1p, x1s), l1_norm_g, mod1, (ctx, 0), (1, bs), w_in1)
    h0 = state_l1_s5.transpose(3, 0, 2, 1, 4).reshape(g, bs, 4 * p)
    log_dt = jnp.broadcast_to(l1_log_dt[:, :, None, None], (2, g, 1, p))
    yt, st, w_glu, w_out1 = _s5(
        ut, bp * seq, h0,
        _both_dirs(l1_a_re[:, :, None, :]), _both_dirs(l1_a_im[:, :, None, :]), _both_dirs(log_dt),
        _both_dirs(l1_b_re.transpose(0, 1, 3, 2)), _both_dirs(l1_b_im.transpose(0, 1, 3, 2)),
        _both_dirs(l1_c_re), _both_dirs(l1_c_im), l1_d, bp, bs, n_seg_s, (l1_w_glu, l1_w_out))
    y_prompt, y_sample = _l1_tail(yt, z1, (x1p, x1s), w_glu, l1_b_glu, mod1, (ctx, 0), (1, bs), final_norm_g,
                                  w_out1, 512)
    new_state = st.reshape(g, bp, 2, 2, p).transpose(1, 3, 2, 0, 4)
    return (y_prompt.reshape(bp, seq, d), y_sample.reshape(bs, n_tok, d),
            kp.reshape(bp, seq, N_HEADS, HEAD_DIM), vp.reshape(bp, seq, N_HEADS, HEAD_DIM), new_state)
```

```python
import functools

import jax
import jax.numpy as jnp
from jax import lax
from jax.experimental import pallas as pl
from jax.experimental.pallas import tpu as pltpu

F32 = jnp.float32
BF16 = jnp.bfloat16

EPS = 1e-6
N_HEADS = 16
HEAD_DIM = 128
GRID_W = 64
NA_ROWS = 8
NA_COLS = 16
RPB_R = 2 * NA_ROWS - 1
RPB_C = 2 * NA_COLS - 1
S5_CH = 16
S5_P = 64
S5_CHUNK = 16
S5_SEG = 8
NEG = -1e30
LANES = 128
VMEM_LIMIT = 56 * 1024 * 1024


def _cparams(*sem):
    return pltpu.CompilerParams(dimension_semantics=sem, vmem_limit_bytes=VMEM_LIMIT)


def _dot(a, b):
    return jnp.dot(a, b, preferred_element_type=F32)


def _dot_nt(a, b, precision=None):
    return lax.dot_general(a, b, (((1,), (1,)), ((), ())), preferred_element_type=F32,
                           precision=precision)


def _sigmoid(x):
    return 0.5 * jnp.tanh(0.5 * x) + 0.5


def _silu(x):
    return x * _sigmoid(x)


def _gelu_tanh(x):
    c = 0.7978845608028654
    return (0.5 * x) * (1.0 + jnp.tanh(x * (c + (c * 0.044715) * (x * x))))


def _ada_kernel(cond_ref, *refs, n_layers):
    w_refs, b_refs, o_ref = refs[:n_layers], refs[n_layers:2 * n_layers], refs[2 * n_layers]
    s = _silu(cond_ref[...]).astype(BF16)
    layer = pl.program_id(0)
    for i in range(n_layers):
        @pl.when(layer == i)
        def _(i=i):
            o_ref[...] = _dot(s, w_refs[i][...].astype(BF16)) + b_refs[i][...]


def _ada(cond8, w_adas, b_adas):
    n_layers = len(w_adas)
    d, n = w_adas[0].shape
    tn = 1024
    nj = n // tn

    def own(i):
        return lambda l, j: (0, jnp.where(l == i, j, jnp.where(l < i, 0, nj - 1)))

    return pl.pallas_call(
        functools.partial(_ada_kernel, n_layers=n_layers),
        grid=(n_layers, nj),
        in_specs=[pl.BlockSpec((8, d), lambda l, j: (0, 0))]
        + [pl.BlockSpec((d, tn), own(i)) for i in range(n_layers)]
        + [pl.BlockSpec((1, tn), own(i)) for i in range(n_layers)],
        out_specs=pl.BlockSpec((None, 8, tn), lambda l, j: (l, 0, j)),
        out_shape=jax.ShapeDtypeStruct((n_layers, 8, n), F32),
        compiler_params=_cparams("arbitrary", "arbitrary"),
        name="ada",
    )(cond8, *w_adas, *[b.reshape(1, n) for b in b_adas])


def _norm_mod(x, g, shift, scale):
    y = x * lax.rsqrt(jnp.mean(x * x, axis=-1, keepdims=True) + EPS) * g
    return (y * (1.0 + scale) + shift).astype(BF16)


def _inproj_kernel(x0_ref, xn_ref, g_ref, shift0_ref, scale0_ref, shiftn_ref, scalen_ref, *rest, out_split,
                   out_scale, head_split, n_cast):
    n_split = max(out_split + head_split) + 1
    w_refs = rest[:n_split]
    rest = rest[n_split:]
    cast_in, rest = rest[:n_cast], rest[n_cast:]
    o_refs = rest[:len(out_split)]
    hd_refs = rest[len(out_split):len(out_split) + len(head_split)]
    rest = rest[len(out_split) + len(head_split):]
    for src, dst in zip(cast_in, rest[:n_cast]):
        dst[...] = src[...].astype(dst.dtype)
    rest = rest[n_cast:]
    h_ref = rest[0]
    i = pl.program_id(0)
    j = pl.program_id(1)
    nj = pl.num_programs(1)
    slot = i % 2

    @pl.when((i == 0) & (j == 0))
    def _():
        h_ref[0] = _norm_mod(x0_ref[...], g_ref[...], shift0_ref[...], scale0_ref[...])

    if head_split:
        stage_ref, sem = rest[1:]
        tm, tn = stage_ref.shape[2:]
        t = i * nj + j

        def head_copies(tt):
            ii, jj = tt // nj, tt % nj
            return [pltpu.make_async_copy(stage_ref.at[tt % 2, n, :, pl.ds(hh * HEAD_DIM, HEAD_DIM)],
                                          hd.at[pl.ds(ii * tm, tm), jj * (tn // HEAD_DIM) + hh, :], sem.at[tt % 2])
                    for n, hd in enumerate(hd_refs) for hh in range(tn // HEAD_DIM)]

        @pl.when(t >= 2)
        def _():
            for c in head_copies(t - 2):
                c.wait()

    h = h_ref[slot]
    for s, w_ref in enumerate(w_refs):
        r = _dot(h, w_ref[...])
        for o_ref, o_s, o_c in zip(o_refs, out_split, out_scale):
            if o_s == s:
                o_ref[...] = (r if o_c == 1.0 else r * o_c).astype(o_ref.dtype)
        if s in head_split:
            stage_ref[t % 2, head_split.index(s)] = r
    rs = xn_ref.shape[0]
    h_ref[1 - slot, pl.ds(pl.multiple_of(j * rs, rs), rs), :] = _norm_mod(
        xn_ref[...], g_ref[...], shiftn_ref[...], scalen_ref[...])

    if head_split:
        for c in head_copies(t):
            c.start()

        @pl.when(t == pl.num_programs(0) * nj - 1)
        def _():
            @pl.when(t >= 1)
            def _():
                for c in head_copies(t - 1):
                    c.wait()

            for c in head_copies(t):
                c.wait()


def _inproj(x, g, mod3, cond0, rows_per_cond, w, outs, tm, tn, head_split=(), casts=()):
    m, d = x.shape
    out_split = tuple(o[0] for o in outs)
    out_dtypes = [o[1] for o in outs]
    out_scale = tuple(float(o[2]) for o in outs)
    head_split = tuple(head_split)
    n_split = max(out_split + head_split) + 1
    ni = m // tm
    nj = d // tn

    def nxt(i):
        return jnp.minimum(i + 1, ni - 1)

    def cond_of(i):
        return cond0 + (i * tm) // rows_per_cond

    in_specs = [pl.BlockSpec((tm, d), lambda i, j: (0, 0), pipeline_mode=pl.Buffered(1)),
                pl.BlockSpec((tm // nj, d), lambda i, j: (nxt(i) * nj + j, 0)),
                pl.BlockSpec((1, d), lambda i, j: (0, 0)),
                pl.BlockSpec((None, 1, d), lambda i, j: (cond_of(0), 0, 0)),
                pl.BlockSpec((None, 1, d), lambda i, j: (cond_of(0), 0, 1)),
                pl.BlockSpec((None, 1, d), lambda i, j: (cond_of(nxt(i)), 0, 0)),
                pl.BlockSpec((None, 1, d), lambda i, j: (cond_of(nxt(i)), 0, 1))]
    for s in range(n_split):
        in_specs.append(pl.BlockSpec((d, tn), lambda i, j, s=s: (0, s * nj + j)))
    out_specs = ([pl.BlockSpec((tm, tn), lambda i, j: (i, j)) for _ in outs]
                 + [pl.BlockSpec(memory_space=pl.ANY)] * len(head_split))
    out_shape = ([jax.ShapeDtypeStruct((m, d), dt) for dt in out_dtypes]
                 + [jax.ShapeDtypeStruct((m, d // HEAD_DIM, HEAD_DIM), F32)] * len(head_split))
    operands = [x, x, g.reshape(1, d), mod3, mod3, mod3, mod3] + [w] * n_split
    for wc in casts:
        tc = wc.shape[1] // (ni * nj)
        assert tc * ni * nj == wc.shape[1] and tc % LANES == 0
        spec = pl.BlockSpec((wc.shape[0], tc), lambda i, j: (0, i * nj + j))
        in_specs.append(spec)
        out_specs.append(spec)
        out_shape.append(jax.ShapeDtypeStruct(wc.shape, BF16))
        operands.append(wc)
    scratch = [pltpu.VMEM((2, tm, d), BF16)]
    if head_split:
        scratch += [pltpu.VMEM((2, len(head_split), tm, tn), F32), pltpu.SemaphoreType.DMA((2,))]
    return pl.pallas_call(
        functools.partial(_inproj_kernel, out_split=out_split, out_scale=out_scale, head_split=head_split,
                          n_cast=len(casts)),
        grid=(ni, nj),
        in_specs=in_specs,
        out_specs=out_specs,
        out_shape=out_shape,
        scratch_shapes=scratch,
        compiler_params=_cparams("arbitrary", "arbitrary"),
        name="inproj",
    )(*operands)


LOG2E = 1.4426950408889634
QK_SCALE = HEAD_DIM ** -0.5 * LOG2E


def _softmax_pv(parts):
    m = parts[0][0].max(axis=-1, keepdims=True)
    for s, _ in parts[1:]:
        m = jnp.maximum(m, s.max(axis=-1, keepdims=True))
    l = None
    o = None
    for s, v in parts:
        e = jnp.exp2(s - m)
        li = e.sum(axis=-1, keepdims=True)
        oi = _dot(e.astype(BF16), v)
        l = li if l is None else l + li
        o = oi if o is None else o + oi
    return o / l


def _ctx_layer_kernel(q_ref, k_ref, v_ref, z_ref, x_ref, gate_ref, w_ref, y_ref, a_ref, *, seq):
    for b in range(q_ref.shape[0] // seq):
        rows = slice(b * seq, (b + 1) * seq)
        for h in range(N_HEADS):
            sl = slice(h * HEAD_DIM, (h + 1) * HEAD_DIM)
            o = _softmax_pv([(_dot_nt(q_ref[rows, sl], k_ref[rows, sl]), v_ref[rows, sl])])
            a_ref[rows, sl] = (o * _silu(z_ref[rows, sl].astype(F32))).astype(BF16)
    y_ref[...] = x_ref[...] + gate_ref[...] * _dot(a_ref[...], w_ref[...])


def _ctx_layer(q, k, v, z, x, mod3, cond, w_out, seq, tm):
    m, d = x.shape
    row = pl.BlockSpec((tm, d), lambda i: (i, 0))
    return pl.pallas_call(
        functools.partial(_ctx_layer_kernel, seq=seq),
        grid=(m // tm,),
        in_specs=[row, row, row, row, row,
                  pl.BlockSpec((None, 1, d), lambda i: (cond, 0, 2)),
                  pl.BlockSpec((d, d), lambda i: (0, 0), pipeline_mode=pl.Buffered(1))],
        out_specs=row,
        out_shape=jax.ShapeDtypeStruct((m, d), F32),
        scratch_shapes=[pltpu.VMEM((tm, d), BF16)],
        compiler_params=_cparams("parallel"),
        name="ctx_layer",
    )(q, k, v, z, x, mod3, w_out)


def _na_window_start(r, rows):
    return min(max(r - NA_ROWS // 2, 0), rows - NA_ROWS)


NA_HEADS_PER_STEP = 4


def _na_kernel(q_ref, k_ref, v_ref, kc_hbm, vc_hbm, rpb_ref, wcast_ref, cond_ref, wada_ref, bada_ref,
               o_ref, wcast_bf_ref, mod_ref, cbuf, csem, *, rows, q_rows):
    wcast_bf_ref[...] = wcast_ref[...].astype(BF16)
    mod_ref[...] = _dot(_silu(cond_ref[...]).astype(BF16), wada_ref[...].astype(BF16)) + bada_ref[...]
    w = GRID_W
    hps = NA_HEADS_PER_STEP
    n_hblk = pl.num_programs(1)
    t = pl.program_id(0) * n_hblk + pl.program_id(1)

    def ctx_copies(tt):
        return [pltpu.make_async_copy(c.at[tt // n_hblk, :, (tt % n_hblk) * hps + hh, :], cbuf.at[tt % 2, hh, n],
                                      csem.at[tt % 2])
                for hh in range(hps) for n, c in enumerate((kc_hbm, vc_hbm))]

    @pl.when(t == 0)
    def _():
        for c in ctx_copies(t):
            c.start()

    @pl.when(t + 1 < pl.num_programs(0) * n_hblk)
    def _():
        for c in ctx_copies(t + 1):
            c.start()

    qc = lax.broadcasted_iota(jnp.int32, (w, LANES), 0)
    lane = lax.broadcasted_iota(jnp.int32, (w, LANES), 1)
    c0 = jnp.clip(qc - NA_COLS // 2, 0, w - NA_COLS)
    ok_l = (lane >= c0) & (lane < c0 + NA_COLS)
    ok_r = (lane - w >= c0) & (lane - w < c0 + NA_COLS)
    neg_tile = jnp.full((w, LANES), NEG, F32)
    for c in ctx_copies(t):
        c.wait()
    for hh in range(hps):
        _na_head(q_ref, k_ref, v_ref, cbuf.at[t % 2, hh], rpb_ref.at[hh], o_ref,
                 slice(hh * HEAD_DIM, (hh + 1) * HEAD_DIM), ok_l, ok_r, neg_tile, rows, q_rows)


def _na_head(q_ref, k_ref, v_ref, ctx_ref, rpb_ref, o_ref, sl, ok_l, ok_r, neg_tile, rows, q_rows):
    w = GRID_W
    tile_l, tile_r = [], []
    for d in range(RPB_R):
        base = jnp.broadcast_to(rpb_ref[d:d + 1, :] * LOG2E, (w, LANES))
        left = pltpu.roll(base, LANES - (NA_COLS - 1), 1, stride=1, stride_axis=0)
        right = pltpu.roll(base, w - (NA_COLS - 1), 1, stride=1, stride_axis=0)
        tile_l.append(jnp.where(ok_l, left, NEG))
        tile_r.append(jnp.where(ok_r, right, NEG))

    kc = ctx_ref[0].astype(BF16)
    vc = ctx_ref[1].astype(BF16)
    n_groups = rows // q_rows
    for gi in range(n_groups):
        rs = list(range(gi * q_rows, (gi + 1) * q_rows))
        klo = min(_na_window_start(r, rows) for r in rs) // 2 * 2
        khi = -(-(max(_na_window_start(r, rows) for r in rs) + NA_ROWS) // 2) * 2
        bias_rows = []
        for r in rs:
            r0 = _na_window_start(r, rows)
            tiles = []
            for kr in range(klo, khi, 2):
                ok0 = r0 <= kr < r0 + NA_ROWS
                ok1 = r0 <= kr + 1 < r0 + NA_ROWS
                t0 = tile_l[kr - r + NA_ROWS - 1] if ok0 else neg_tile
                t1 = tile_r[kr + 1 - r + NA_ROWS - 1] if ok1 else neg_tile
                tiles.append(jnp.maximum(t0, t1) if (ok0 or ok1) else neg_tile)
            bias_rows.append(jnp.concatenate(tiles, axis=1))
        bias = jnp.concatenate(bias_rows, axis=0)
        q = q_ref[gi * q_rows * w:(gi + 1) * q_rows * w, sl]
        kl = k_ref[klo * w:khi * w, sl]
        vl = v_ref[klo * w:khi * w, sl]
        s_loc = _dot_nt(q, kl) + bias
        s_ctx = _dot_nt(q, kc)
        o = _softmax_pv([(s_loc, vl), (s_ctx, vc)])
        o_ref[gi * q_rows * w:(gi + 1) * q_rows * w, sl] = o.astype(o_ref.dtype)


def _latent_na(q, k, v, k_ctx, v_ctx, rpb_pad, n_tok, w_cast, ada_next):
    m, d = q.shape
    nb = m // n_tok
    lc = k_ctx.shape[1]
    rows = n_tok // GRID_W
    hps = NA_HEADS_PER_STEP
    n_hblk = N_HEADS // hps
    cond8, w_ada, b_ada = ada_next
    n_ada = w_ada.shape[1]
    tc = w_cast.shape[1] // (nb * n_hblk)
    ta = n_ada // (nb * n_hblk)
    assert tc * nb * n_hblk == w_cast.shape[1] and tc % LANES == 0
    assert ta * nb * n_hblk == n_ada and ta % LANES == 0
    spec = pl.BlockSpec((n_tok, hps * HEAD_DIM), lambda b, h: (b, h))
    cspec = pl.BlockSpec(memory_space=pl.ANY)

    def step_cols(r, t):
        return pl.BlockSpec((r, t), lambda b, h: (0, b * n_hblk + h))

    return pl.pallas_call(
        functools.partial(_na_kernel, rows=rows, q_rows=8),
        grid=(nb, n_hblk),
        in_specs=[spec, spec, spec, cspec, cspec,
                  pl.BlockSpec((hps, 16, LANES), lambda b, h: (h, 0, 0)), step_cols(w_cast.shape[0], tc),
                  pl.BlockSpec((8, d), lambda b, h: (0, 0)), step_cols(d, ta), step_cols(1, ta)],
        out_specs=[spec, step_cols(w_cast.shape[0], tc), step_cols(8, ta)],
        out_shape=[jax.ShapeDtypeStruct((m, d), BF16), jax.ShapeDtypeStruct(w_cast.shape, BF16),
                   jax.ShapeDtypeStruct((8, n_ada), F32)],
        scratch_shapes=[pltpu.VMEM((2, hps, 2, lc, HEAD_DIM), F32), pltpu.SemaphoreType.DMA((2,))],
        compiler_params=_cparams("arbitrary", "arbitrary"),
        name="latent_na",
    )(q, k, v, k_ctx, v_ctx, rpb_pad, w_cast, cond8, w_ada, b_ada.reshape(1, n_ada))


def _gated_out_kernel(o_ref, z_ref, x_ref, gate_ref, w_ref, y_ref):
    a = o_ref[...].astype(F32) * _silu(z_ref[...].astype(F32))
    y_ref[...] = x_ref[...] + gate_ref[...] * _dot(a.astype(BF16), w_ref[...])


def _gated_out(o, z, x, mod3, cond0, rows_per_cond, w_out, tm):
    m, d = x.shape
    row = pl.BlockSpec((tm, d), lambda i: (i, 0))
    return pl.pallas_call(
        _gated_out_kernel,
        grid=(m // tm,),
        in_specs=[row, row, row,
                  pl.BlockSpec((None, 1, d), lambda i: (cond0 + (i * tm) // rows_per_cond, 0, 2)),
                  pl.BlockSpec((d, d), lambda i: (0, 0), pipeline_mode=pl.Buffered(1))],
        out_specs=row,
        out_shape=jax.ShapeDtypeStruct((m, d), F32),
        compiler_params=_cparams("parallel"),
        name="gated_out",
    )(o, z, x, mod3, w_out)


LANE_TILE = 512


def _slots_per_step(x4):
    nq, m, _, _ = x4.shape
    assert LANE_TILE % (nq * m) == 0 and S5_CHUNK % (LANE_TILE // (nq * m)) == 0
    return LANE_TILE // (nq * m)


def _slot_scratch(d):
    return [pltpu.VMEM((2, LANE_TILE, d), F32), pltpu.SemaphoreType.DMA((2,))]


def _slot_copies(x_hbm, buf, sem, l, to_hbm):
    nq, m, _, _ = x_hbm.shape
    n_x = buf.shape[1] // (nq * m)
    copies = []
    for xi in range(n_x):
        for k in range(m):
            hbm = x_hbm.at[:, k, l * n_x + xi, :]
            vmem = buf.at[l % 2, pl.ds((xi * m + k) * nq, nq), :]
            src, dst = (vmem, hbm) if to_hbm else (hbm, vmem)
            copies.append(pltpu.make_async_copy(src, dst, sem.at[l % 2]))
    return copies


def _fetch_slots(x_hbm, buf, sem, l=None, n=None):
    l = pl.program_id(0) if l is None else l
    n = pl.num_programs(0) if n is None else n

    @pl.when(l == 0)
    def _():
        for c in _slot_copies(x_hbm, buf, sem, l, False):
            c.start()

    @pl.when(l + 1 < n)
    def _():
        for c in _slot_copies(x_hbm, buf, sem, l + 1, False):
            c.start()

    for c in _slot_copies(x_hbm, buf, sem, l, False):
        c.wait()
    return buf[l % 2]


def _store_slots(res, o_hbm, buf, sem):
    l = pl.program_id(0)

    def wait(ll):
        for c in _slot_copies(o_hbm, buf, sem, ll, True):
            c.wait()

    @pl.when(l >= 2)
    def _():
        wait(l - 2)

    buf[l % 2] = res
    for c in _slot_copies(o_hbm, buf, sem, l, True):
        c.start()

    @pl.when(l == pl.num_programs(0) - 1)
    def _():
        @pl.when(l >= 1)
        def _():
            wait(l - 1)

        wait(l)


def _cond_rows(ref, n_rows, nq, n_cond):
    out = ref[n_cond - 1]
    if n_cond > 1:
        b = (lax.broadcasted_iota(jnp.int32, (n_rows, 1), 0) % nq) // (nq // n_cond)
        for i in range(n_cond - 2, -1, -1):
            out = jnp.where(b == i, ref[i], out)
    return out


def _inproj_t_kernel(*refs, n_conds, n_steps):
    n_in = len(n_conds)
    x_hbms = refs[:n_in]
    g_ref = refs[n_in]
    mod_refs = refs[n_in + 1:3 * n_in + 1]
    wu_ref, wz_ref, u_ref, z_ref, buf, sem = refs[3 * n_in + 1:]
    l = pl.program_id(0)
    first = 0
    for k in range(n_in):
        @pl.when((l >= first) & (l < first + n_steps[k]))
        def _(k=k, first=first):
            nq = x_hbms[k].shape[0]
            x = _fetch_slots(x_hbms[k], buf, sem, l - first, n_steps[k])
            n_rows = x.shape[0]
            h = _norm_mod(x, g_ref[...], _cond_rows(mod_refs[2 * k], n_rows, nq, n_conds[k]),
                          _cond_rows(mod_refs[2 * k + 1], n_rows, nq, n_conds[k]))
            u_ref[...] = _dot(h, wu_ref[...]).T.astype(u_ref.dtype)
            z_ref[...] = _dot(h, wz_ref[...]).astype(z_ref.dtype)

        first += n_steps[k]


def _inproj_t(x4s, g, mod3, cond0s, n_conds, w_in):
    d = x4s[0].shape[-1]
    n_steps = []
    for x4 in x4s:
        _slots_per_step(x4)
        n_steps.append(x4.shape[0] * x4.shape[1] * S5_CHUNK // LANE_TILE)
    n_tok = sum(n_steps) * LANE_TILE
    mod_specs = []
    for c0, nc in zip(cond0s, n_conds):
        for col in range(2):
            mod_specs.append(pl.BlockSpec((nc, 1, d), lambda l, c0=c0, nc=nc, col=col: (c0 // nc, 0, col)))

    def wspec(col):
        return pl.BlockSpec((d, d), lambda l: (0, col), pipeline_mode=pl.Buffered(1))

    return pl.pallas_call(
        functools.partial(_inproj_t_kernel, n_conds=tuple(n_conds), n_steps=tuple(n_steps)),
        grid=(sum(n_steps),),
        in_specs=[pl.BlockSpec(memory_space=pl.ANY)] * len(x4s)
        + [pl.BlockSpec((1, d), lambda l: (0, 0))] + mod_specs + [wspec(0), wspec(1)],
        out_specs=[pl.BlockSpec((d, LANE_TILE), lambda l: (0, l)), pl.BlockSpec((LANE_TILE, d), lambda l: (l, 0))],
        out_shape=[jax.ShapeDtypeStruct((d, n_tok), BF16), jax.ShapeDtypeStruct((n_tok, d), BF16)],
        scratch_shapes=_slot_scratch(d),
        compiler_params=_cparams("arbitrary"),
        name="inproj_t",
    )(*x4s, g.reshape(1, d), *([mod3] * (2 * len(x4s))), w_in, w_in)


def _cmul(ar, ai, br, bi):
    return ar * br - ai * bi, ar * bi + ai * br


def _split_bf16(a):
    hi = a.astype(BF16)
    return hi, (a - hi.astype(F32)).astype(BF16)


def _s5_powers(ar, ai):
    width = ar.shape[-1]
    pw = [(jnp.ones_like(ar), jnp.zeros_like(ar))]
    for _ in range(S5_CHUNK):
        pw.append(_cmul(pw[-1][0], pw[-1][1], ar, ai))
    is_fwd = lax.broadcasted_iota(jnp.int32, (1, width), 1) < width // 2

    def pattern(fwd_ascending):
        out = []
        for part in range(2):
            blocks = []
            for j in range(S5_CHUNK):
                ef, eb = (j, S5_CHUNK - 1 - j) if fwd_ascending else (S5_CHUNK - 1 - j, j)
                blocks.append(jnp.broadcast_to(jnp.where(is_fwd, pw[ef][part], pw[eb][part]), (S5_CH, width)))
            out.append(jnp.concatenate(blocks, axis=0))
        return out

    return pattern(True), pattern(False), pw[S5_CHUNK]


S5_GROUPS_PER_STEP = 8


def _s5_kernel(xp_ref, xs_ref, *rest, nb_p, nb_s, n_seg_s, n_cast):
    group_in, cast_in = rest[:9], rest[9:9 + n_cast]
    yp_ref, ys_ref, st_ref = rest[9 + n_cast:12 + n_cast]
    cast_out = rest[12 + n_cast:12 + 2 * n_cast]
    scratch = rest[12 + 2 * n_cast:]
    for gi in range(S5_GROUPS_PER_STEP):
        ch = pl.ds(gi * S5_CH, S5_CH)
        _s5_group(xp_ref.at[ch], xs_ref.at[ch], *[r.at[gi] for r in group_in], yp_ref.at[ch], ys_ref.at[ch],
                  st_ref.at[gi], *[r.at[gi] for r in scratch], nb_p=nb_p, nb_s=nb_s, n_seg_s=n_seg_s)
    for src, dst in zip(cast_in, cast_out):
        dst[...] = src[...].astype(dst.dtype)


def _s5_group(xp_ref, xs_ref, h0_ref, are_ref, aim_ref, ldt_ref, bre_ref, bim_ref, cre_ref, cim_ref, dsk_ref,
              yp_ref, ys_ref, st_ref, w_ref, ws_ref, wc_ref, s_ref, f_ref, *, nb_p, nb_s, n_seg_s):
    p = S5_P
    kc = S5_CHUNK * S5_CH
    a_re = are_ref[...]
    a_im = aim_ref[...]
    dt = jnp.exp(ldt_ref[...])
    mag = jnp.exp(a_re * dt)
    ab_re = mag * jnp.cos(a_im * dt)
    ab_im = mag * jnp.sin(a_im * dt)
    den = a_re * a_re + a_im * a_im
    nr = ab_re - 1.0
    f_re = (nr * a_re + ab_im * a_im) / den
    f_im = (ab_im * a_re - nr * a_im) / den
    bbt_re, bbt_im = _cmul(f_re, f_im, bre_ref[...], bim_ref[...])

    (pg_r, pg_i), (ps_r, ps_i), (ac_r, ac_i) = _s5_powers(ab_re, ab_im)
    c_re = jnp.concatenate([cre_ref[...]] * S5_CHUNK, axis=0)
    c_im = jnp.concatenate([cim_ref[...]] * S5_CHUNK, axis=0)
    bt_re = jnp.concatenate([bbt_re] * S5_CHUNK, axis=0)
    bt_im = jnp.concatenate([bbt_im] * S5_CHUNK, axis=0)

    g_re, g_im = _cmul(pg_r, pg_i, c_re, c_im)
    fwd16 = lax.broadcasted_iota(jnp.int32, (S5_CH, 2 * p), 1) < p
    g_hi, g_lo = _split_bf16(jnp.concatenate([g_re, g_im], axis=1))
    r0 = []
    for d in range(2):
        msk = fwd16 if d == 0 else jnp.logical_not(fwd16)
        b_hi, b_lo = _split_bf16(
            jnp.concatenate([jnp.where(msk, bbt_re, 0.0), jnp.where(msk, -bbt_im, 0.0)], axis=1))
        r0.append(_dot_nt(b_hi, g_hi) + (_dot_nt(b_hi, g_lo) + _dot_nt(b_lo, g_hi)))
    lane = lax.broadcasted_iota(jnp.int32, (S5_CH, kc), 1)
    for s in range(S5_CHUNK):
        f = jnp.where(lane >= S5_CH * s, pltpu.roll(r0[0], S5_CH * s, 1), 0.0) if s else r0[0]
        b = jnp.where(lane < S5_CH * (s + 1), pltpu.roll(r0[1], (S5_CH * (s + 1)) % kc, 1), 0.0)
        w_ref[s * S5_CH:(s + 1) * S5_CH, :] = (f + b).astype(BF16)
    e_re, e_im = _cmul(ps_r, ps_i, bt_re, bt_im)
    ws_ref[:, 0:2 * p] = e_re.astype(BF16)
    ws_ref[:, 2 * p:4 * p] = e_im.astype(BF16)
    g1_re, g1_im = _cmul(g_re, g_im, ab_re, ab_im)
    wc_ref[:, 0:2 * p] = g1_re.astype(BF16)
    wc_ref[:, 2 * p:4 * p] = (-g1_im).astype(BF16)

    def run(xt_ref, yt_ref, nb, n_seg, hr, hm):
        rows = xt_ref.shape[1] // S5_CHUNK
        nbx = nb * n_seg
        m = rows // nbx
        xt = jnp.concatenate([xt_ref[:, s * rows:(s + 1) * rows] for s in range(S5_CHUNK)], axis=0)
        x = xt.T
        y = _dot(x, w_ref[...]) + dsk_ref[...] * x.astype(F32)
        s_all = _dot(x, ws_ref[...])
        s_ref[0, 0:rows, :] = s_all[:, 0:2 * p]
        s_ref[1, 0:rows, :] = s_all[:, 2 * p:4 * p]

        def step(ar, ai, hr, hm, plane_r, plane_i, rf, rb):
            isf = lax.broadcasted_iota(jnp.int32, hr.shape, 1) < p
            sr = jnp.where(isf, plane_r[rf, :], plane_r[rb, :])
            sm = jnp.where(isf, plane_i[rf, :], plane_i[rb, :])
            return ar * hr - ai * hm + sr, ar * hm + ai * hr + sm

        def scan(hr, hm, store):
            arb = jnp.broadcast_to(ac_r, (nbx, 2 * p))
            aib = jnp.broadcast_to(ac_i, (nbx, 2 * p))
            for k in range(m):
                rf = pl.ds(k * nbx, nbx)
                rb = pl.ds((m - 1 - k) * nbx, nbx)
                if store:
                    s_ref[2, rf, :] = hr
                    s_ref[3, rf, :] = hm
                    s_ref[4, rb, :] = hr
                    s_ref[5, rb, :] = hm
                hr, hm = step(arb, aib, hr, hm, s_ref.at[0], s_ref.at[1], rf, rb)
            return hr, hm

        if n_seg > 1:
            zeros = jnp.zeros((nbx, 2 * p), F32)
            f_ref[0], f_ref[1] = scan(zeros, zeros, False)
            sr_, si_ = ac_r, ac_i
            for _ in range(m - 1):
                sr_, si_ = _cmul(sr_, si_, ac_r, ac_i)
            sr_ = jnp.broadcast_to(sr_, (nb, 2 * p))
            si_ = jnp.broadcast_to(si_, (nb, 2 * p))
            for j in range(n_seg):
                rf = pl.ds(j, nb, stride=n_seg)
                rb = pl.ds(n_seg - 1 - j, nb, stride=n_seg)
                f_ref[2, rf, :] = hr
                f_ref[3, rf, :] = hm
                f_ref[4, rb, :] = hr
                f_ref[5, rb, :] = hm
                hr, hm = step(sr_, si_, hr, hm, f_ref.at[0], f_ref.at[1], rf, rb)
            isf = lax.broadcasted_iota(jnp.int32, (nbx, 2 * p), 1) < p
            hr = jnp.where(isf, f_ref[2], f_ref[4])
            hm = jnp.where(isf, f_ref[3], f_ref[5])
        hr, hm = scan(hr, hm, True)
        isf_rows = lax.broadcasted_iota(jnp.int32, (rows, 2 * p), 1) < p
        h_prev = jnp.concatenate([jnp.where(isf_rows, s_ref[2, 0:rows, :], s_ref[4, 0:rows, :]),
                                  jnp.where(isf_rows, s_ref[3, 0:rows, :], s_ref[5, 0:rows, :])], axis=1)
        yt = _gelu_tanh(y + _dot_nt(h_prev.astype(BF16), wc_ref[...])).astype(yt_ref.dtype).T
        for t in range(S5_CHUNK):
            yt_ref[:, t * rows:(t + 1) * rows] = yt[t * S5_CH:(t + 1) * S5_CH, :]
        return hr, hm

    zeros = jnp.zeros((nb_p, 2 * p), F32)
    hr, hm = run(xp_ref, yp_ref, nb_p, 1, zeros, zeros)
    st_ref[:, 0:2 * p] = hr
    st_ref[:, 2 * p:4 * p] = hm
    run(xs_ref, ys_ref, nb_s, n_seg_s, h0_ref[:, 0:2 * p], h0_ref[:, 2 * p:4 * p])


def _s5(ut, lanes_p, h0, a_re, a_im, log_dt, bt_re, bt_im, c_re, c_im, d_skip, nb_p, nb_s, n_seg_s, casts):
    d, lanes = ut.shape
    lanes_s = lanes - lanes_p
    assert lanes_p % lanes_s == 0
    g = d // S5_CH
    rows_p = lanes_p // S5_CHUNK
    rows_s = lanes_s // S5_CHUNK
    kc = S5_CHUNK * S5_CH
    p = S5_P

    gps = S5_GROUPS_PER_STEP

    def gspec(shape):
        return pl.BlockSpec((gps,) + shape, lambda i: (i, 0, 0))

    def tspec(width, first=0):
        return pl.BlockSpec((gps * S5_CH, width), lambda i: (i, first // width))

    n_steps = g // gps
    cast_specs = []
    for wc in casts:
        tc = wc.shape[1] // n_steps
        assert tc * n_steps == wc.shape[1] and tc % LANES == 0
        cast_specs.append(pl.BlockSpec((wc.shape[0], tc), lambda i: (0, i)))
    return pl.pallas_call(
        functools.partial(_s5_kernel, nb_p=nb_p, nb_s=nb_s, n_seg_s=n_seg_s, n_cast=len(casts)),
        grid=(n_steps,),
        in_specs=[tspec(lanes_p), tspec(lanes_s, lanes_p), gspec((nb_s, 4 * p)),
                  gspec((1, 2 * p)), gspec((1, 2 * p)), gspec((1, 2 * p)),
                  gspec((S5_CH, 2 * p)), gspec((S5_CH, 2 * p)), gspec((S5_CH, 2 * p)), gspec((S5_CH, 2 * p)),
                  gspec((1, kc))] + cast_specs,
        out_specs=[tspec(lanes_p), tspec(lanes_s), gspec((nb_p, 4 * p))] + cast_specs,
        out_shape=[jax.ShapeDtypeStruct((d, lanes_p), BF16),
                   jax.ShapeDtypeStruct((d, lanes_s), BF16),
                   jax.ShapeDtypeStruct((g, nb_p, 4 * p), F32)]
        + [jax.ShapeDtypeStruct(wc.shape, BF16) for wc in casts],
        scratch_shapes=[pltpu.VMEM((gps, kc, kc), BF16), pltpu.VMEM((gps, kc, 4 * p), BF16),
                        pltpu.VMEM((gps, kc, 4 * p), BF16),
                        pltpu.VMEM((gps, 6, max(rows_p, rows_s), 2 * p), F32),
                        pltpu.VMEM((gps, 6, nb_s * n_seg_s, 2 * p), F32)],
        compiler_params=_cparams("parallel"),
        name="s5",
    )(ut, ut, h0, a_re, a_im, log_dt, bt_re, bt_im, c_re, c_im,
      jnp.tile(d_skip.reshape(g, 1, S5_CH), (1, 1, S5_CHUNK)), *casts)


def _l1_tail_kernel(yt0_ref, ytn_ref, z_ref, wglu_ref, b_ref, x_hbm, gate_ref, fg_ref, wout_ref, o_hbm,
                    yb_ref, a_ref, xbuf, xsem, obuf, osem, *, tn, n_cond):
    d = wglu_ref.shape[0]
    nq = x_hbm.shape[0]
    l = pl.program_id(0)
    slot = l % 2
    _fetch_slots(x_hbm, xbuf, xsem)

    @pl.when(l == 0)
    def _():
        yb_ref[0] = yt0_ref[...].T

    for c in range(d // tn):
        cs = slice(c * tn, (c + 1) * tn)
        yb = yb_ref[slot]
        gl = _dot(yb, wglu_ref[:, cs]) + b_ref[:, cs]
        a = yb[:, cs].astype(F32) * _sigmoid(gl) * _silu(z_ref[:, cs].astype(F32))
        a_ref[:, cs] = a.astype(BF16)
        yb_ref[1 - slot, :, cs] = ytn_ref[cs, :].T

    x = xbuf[slot]
    xn = x + _cond_rows(gate_ref, x.shape[0], nq, n_cond) * _dot(a_ref[...], wout_ref[...])
    res = xn * lax.rsqrt(jnp.mean(xn * xn, axis=-1, keepdims=True) + EPS) * fg_ref[...]
    _store_slots(res, o_hbm, obuf, osem)


def _l1_tail(yt, z, z_first, x4, w_glu, b_glu, mod3, cond0, n_cond, final_g, w_out, tn):
    nq, m, _, d = x4.shape
    _slots_per_step(x4)
    tm = LANE_TILE
    ni = nq * m * S5_CHUNK // tm
    wspec = pl.BlockSpec((d, d), lambda i: (0, 0), pipeline_mode=pl.Buffered(1))
    vec = pl.BlockSpec((1, d), lambda i: (0, 0))
    return pl.pallas_call(
        functools.partial(_l1_tail_kernel, tn=tn, n_cond=n_cond),
        grid=(ni,),
        in_specs=[pl.BlockSpec((d, tm), lambda i: (0, 0), pipeline_mode=pl.Buffered(1)),
                  pl.BlockSpec((d, tm), lambda i: (0, jnp.minimum(i + 1, ni - 1))),
                  pl.BlockSpec((tm, d), lambda i: (i + z_first // tm, 0)),
                  wspec, vec,
                  pl.BlockSpec(memory_space=pl.ANY),
                  pl.BlockSpec((n_cond, 1, d), lambda i: (cond0 // n_cond, 0, 2)),
                  vec, wspec],
        out_specs=pl.BlockSpec(memory_space=pl.ANY),
        out_shape=jax.ShapeDtypeStruct(x4.shape, F32),
        scratch_shapes=[pltpu.VMEM((2, tm, d), BF16), pltpu.VMEM((tm, d), BF16)] + 2 * _slot_scratch(d),
        compiler_params=_cparams("arbitrary"),
        name="l1_tail",
    )(yt, yt, z, w_glu, b_glu.reshape(1, d), x4, mod3, final_g.reshape(1, d), w_out)


def _both_dirs(a):
    _, g, r, p = a.shape
    return a.transpose(1, 2, 0, 3).reshape(g, r, 2 * p)


def kernel(x_prompt, x_sample, cache_l0_k, cache_l0_v, state_l1_s5, c, c_ctx, l0_norm_g, l0_w_ada, l0_b_ada, l0_w_in, l0_rpb, l0_w_out, l1_norm_g, l1_w_ada, l1_b_ada, l1_w_in, l1_a_re, l1_a_im, l1_log_dt, l1_b_re, l1_b_im, l1_c_re, l1_c_im, l1_d, l1_w_glu, l1_b_glu, l1_w_out, final_norm_g):
    bp, seq, d = x_prompt.shape
    bs, n_tok, _ = x_sample.shape
    g = d // S5_CH
    p = S5_P
    xp = x_prompt.reshape(bp * seq, d)
    xs = x_sample.reshape(bs * n_tok, d)

    ctx = bs
    cond8 = jnp.zeros((8, d), F32).at[0:bs].set(c).at[ctx].set(c_ctx)
    mod0 = _ada(cond8, (l0_w_ada,), (l0_b_ada,))[0].reshape(8, 1, 3 * d)

    w_in0 = l0_w_in.astype(BF16)

    qkvz = ((0, BF16, QK_SCALE), (1, BF16, 1.0), (2, BF16, 1.0), (3, BF16, 1.0))
    qs, ks, vs, zs, w_out0 = _inproj(xs, l0_norm_g, mod0, 0, n_tok, w_in0, qkvz, 1024, 256, casts=(l0_w_out,))
    qp, kpb, vpb, zp, kp, vp = _inproj(xp, l0_norm_g, mod0, ctx, bp * seq, w_in0, qkvz, 1024, 256,
                                       head_split=(1, 2))
    rpb_pad = jnp.zeros((N_HEADS, 16, LANES), F32).at[:, :RPB_R, :RPB_C].set(l0_rpb)
    os_, w_in1, mod1 = _latent_na(qs, ks, vs, cache_l0_k, cache_l0_v, rpb_pad, n_tok, l1_w_in,
                                  (cond8, l1_w_ada, l1_b_ada))
    mod1 = mod1.reshape(8, 1, 3 * d)
    n_seg_s = max(1, n_tok // S5_CHUNK // S5_SEG)
    x1p = _ctx_layer(qp, kpb, vpb, zp, xp, mod0, ctx, w_out0, seq, 512).reshape(bp, seq // S5_CHUNK, S5_CHUNK, d)
    x1s = _gated_out(os_, zs, xs, mod0, 0, n_tok, w_out0, 512).reshape(bs * n_seg_s, -1, S5_CHUNK, d)

    ut, z1 = _inproj_t((x1p, x1s), l1_norm_g, mod1, (ctx, 0), (1, bs), w_in1)
    h0 = state_l1_s5.transpose(3, 0, 2, 1, 4).reshape(g, bs, 4 * p)
    log_dt = jnp.broadcast_to(l1_log_dt[:, :, None, None], (2, g, 1, p))
    ytp, yts, st, w_glu, w_out1 = _s5(
        ut, bp * seq, h0,
        _both_dirs(l1_a_re[:, :, None, :]), _both_dirs(l1_a_im[:, :, None, :]), _both_dirs(log_dt),
        _both_dirs(l1_b_re.transpose(0, 1, 3, 2)), _both_dirs(l1_b_im.transpose(0, 1, 3, 2)),
        _both_dirs(l1_c_re), _both_dirs(l1_c_im), l1_d, bp, bs, n_seg_s, (l1_w_glu, l1_w_out))
    y_prompt = _l1_tail(ytp, z1, 0, x1p, w_glu, l1_b_glu, mod1, ctx, 1, final_norm_g, w_out1, 512)
    y_sample = _l1_tail(yts, z1, bp * seq, x1s, w_glu, l1_b_glu, mod1, 0, bs, final_norm_g, w_out1, 512)
    new_state = st.reshape(g, bp, 2, 2, p).transpose(1, 3, 2, 0, 4)
    return (y_prompt.reshape(bp, seq, d), y_sample.reshape(bs, n_tok, d),
            kp.reshape(bp, seq, N_HEADS, HEAD_DIM), vp.reshape(bp, seq, N_HEADS, HEAD_DIM), new_state)
```
